```python
import math
import jax, jax.numpy as jnp
from jax import lax
import numpy as np

D_MODEL = 1024
BATCH = 2
SEQ = 8192
DEPTH = 1

A_HEADS = 8
A_HEAD_DIM = 64
A_WIDTH = A_HEADS * A_HEAD_DIM
DILATED_PATTERNS = ((128, 1), (512, 4), (2048, 16))
BAND_BLOCK = 128
REL_BUCKETS = 32
REL_MAX_DIST = 2048
M_HEADS = 8
M_NOPE = 64
M_ROPE = 32
M_V = 64
M_Q_LORA = 768
M_KV_LORA = 256
M_WIDTH = M_HEADS * M_V
ROPE_THETA = 10000.0
Q_BLOCK = 128
D_FF = -(-8 * D_MODEL // (3 * 256)) * 256
N_MOD = 6
EPS = 1e-6
NEG_INF = -1e30
IN_SIZES = (3 * A_WIDTH, M_Q_LORA, M_KV_LORA, M_ROPE, D_MODEL, D_MODEL)
IN_WIDTH = 3 * A_WIDTH + M_Q_LORA + M_KV_LORA + M_ROPE + 2 * D_MODEL
IN_SPLIT_POINTS = (3 * A_WIDTH,
                   3 * A_WIDTH + M_Q_LORA,
                   3 * A_WIDTH + M_Q_LORA + M_KV_LORA,
                   3 * A_WIDTH + M_Q_LORA + M_KV_LORA + M_ROPE,
                   3 * A_WIDTH + M_Q_LORA + M_KV_LORA + M_ROPE + D_MODEL)

kernel_name = "hybrid_dilated_mla_gated_block"


def _rmsnorm(x, g):
    xf = x.astype(jnp.float32)
    inv = lax.rsqrt(jnp.mean(xf * xf, axis=-1, keepdims=True) + EPS)
    return (xf * inv).astype(x.dtype) * g


def _t5_bucket(dist):
    max_exact = REL_BUCKETS // 2
    d = jnp.maximum(dist, 1).astype(jnp.float32)
    log_b = max_exact + (jnp.log(d / max_exact) / math.log(REL_MAX_DIST / max_exact)
                         * (REL_BUCKETS - max_exact)).astype(jnp.int32)
    log_b = jnp.minimum(log_b, REL_BUCKETS - 1)
    return jnp.where(dist < max_exact, dist, log_b)


def _dilated_pattern(q, k, v, rel_bias, window, dilation):
    B, H, S, Dh = q.shape
    n_back = window // dilation
    blk = BAND_BLOCK
    L = S // dilation
    pad_end = (-L) % blk
    Lp = L + pad_end
    nb = Lp // blk

    def to_sub(t):
        t = t.reshape(B, H, L, dilation, Dh).transpose(0, 1, 3, 2, 4)
        return jnp.pad(t, ((0, 0), (0, 0), (0, 0), (0, pad_end), (0, 0)))

    def band(t):
        tp = jnp.pad(t, ((0, 0), (0, 0), (0, 0), (blk, 0), (0, 0)))
        prev = tp[:, :, :, :Lp].reshape(B, H, dilation, nb, blk, Dh)
        cur = t.reshape(B, H, dilation, nb, blk, Dh)
        return jnp.concatenate([prev, cur], axis=4)

    qb = to_sub(q).reshape(B, H, dilation, nb, blk, Dh)
    kb = band(to_sub(k))
    vb = band(to_sub(v))
    s = jnp.einsum('bhrnqd,bhrnkd->bhrnqk', qb, kb).astype(jnp.float32) * (Dh ** -0.5)
    sub_dist = (jnp.arange(blk)[:, None] + blk) - jnp.arange(2 * blk)[None, :]
    in_band = (sub_dist >= 0) & (sub_dist <= n_back)
    bucket = _t5_bucket(jnp.clip(sub_dist, 0, n_back) * dilation)
    bias = rel_bias[:, bucket].astype(jnp.float32)
    key_idx = jnp.arange(nb)[:, None] * blk + jnp.arange(2 * blk)[None, :] - blk
    mask = in_band[None] & (key_idx >= 0)[:, None, :]
    s = jnp.where(mask, s + bias[None, :, None, None], NEG_INF)
    lse = jax.nn.logsumexp(s, axis=-1)
    p = jnp.exp(s - lse[..., None])
    o = jnp.einsum('bhrnqk,bhrnkd->bhrnqd', p.astype(v.dtype), vb)
    o = o.reshape(B, H, dilation, Lp, Dh)[:, :, :, :L].transpose(0, 1, 3, 2, 4).reshape(B, H, S, Dh)
    lse = lse.reshape(B, H, dilation, Lp)[:, :, :, :L].transpose(0, 1, 3, 2).reshape(B, H, S)
    return o, lse


def _dilated_attention(a_qkv, rel_bias):
    B, S, _ = a_qkv.shape
    qkv = a_qkv.reshape(B, S, 3, A_HEADS, A_HEAD_DIM).transpose(2, 0, 3, 1, 4)
    q, k, v = qkv[0], qkv[1], qkv[2]
    outs, lses = [], []
    for window, dilation in DILATED_PATTERNS:
        o, l = _dilated_pattern(q, k, v, rel_bias, window, dilation)
        outs.append(o)
        lses.append(l)
    w = jax.nn.softmax(jnp.stack(lses), axis=0)
    o = jnp.einsum('gbhs,gbhsd->bhsd', w.astype(v.dtype), jnp.stack(outs))
    return o.transpose(0, 2, 1, 3).reshape(B, S, A_WIDTH)


def _rope(x, pos):
    half = x.shape[-1] // 2
    freqs = ROPE_THETA ** (-jnp.arange(half, dtype=jnp.float32) / half)
    ang = pos.astype(jnp.float32)[..., None] * freqs
    ang = ang.reshape(ang.shape[:2] + (1,) * (x.ndim - 3) + (half,))
    cos, sin = jnp.cos(ang), jnp.sin(ang)
    x1 = x[..., :half].astype(jnp.float32)
    x2 = x[..., half:].astype(jnp.float32)
    return jnp.concatenate([x1 * cos - x2 * sin, x1 * sin + x2 * cos], axis=-1).astype(x.dtype)


def _mla(c_q, c_kv, k_r, positions, g_q_lora, w_uq, g_kv_lora, w_ukv):
    B, S, _ = c_q.shape
    q = jnp.einsum('bsl,lhd->bshd', _rmsnorm(c_q, g_q_lora), w_uq)
    kv = jnp.einsum('bsl,lhd->bshd', _rmsnorm(c_kv, g_kv_lora), w_ukv)
    q_nope, q_rope = q[..., :M_NOPE], _rope(q[..., M_NOPE:], positions)
    k_nope, v = kv[..., :M_NOPE], kv[..., M_NOPE:]
    k_rope = _rope(k_r, positions)
    scale = (M_NOPE + M_ROPE) ** -0.5
    nq = S // Q_BLOCK
    qn_b = jnp.moveaxis(q_nope.reshape(B, nq, Q_BLOCK, M_HEADS, M_NOPE), 1, 0)
    qr_b = jnp.moveaxis(q_rope.reshape(B, nq, Q_BLOCK, M_HEADS, M_ROPE), 1, 0)
    kpos = jnp.arange(S)

    def block(args):
        qn, qr, i = args
        s = (jnp.einsum('bqhd,bkhd->bhqk', qn, k_nope)
             + jnp.einsum('bqhr,bkr->bhqk', qr, k_rope)).astype(jnp.float32) * scale
        qpos = i * Q_BLOCK + jnp.arange(Q_BLOCK)
        s = jnp.where(kpos[None, :] <= qpos[:, None], s, NEG_INF)
        p = jax.nn.softmax(s, axis=-1)
        return jnp.einsum('bhqk,bkhd->bqhd', p.astype(v.dtype), v)

    o = lax.map(block, (qn_b, qr_b, jnp.arange(nq)))
    return jnp.moveaxis(o, 0, 1).reshape(B, S, M_WIDTH)


def _hybrid_mixer(h, positions, rel_bias, w_in, g_q_lora, w_uq, g_kv_lora, w_ukv,
                  w_up_a, w_up_b, w_o):
    proj = h @ w_in
    a_qkv, c_q, c_kv, k_r, gate_a, gate_b = jnp.split(proj, IN_SPLIT_POINTS, axis=-1)
    y_a = _dilated_attention(a_qkv, rel_bias) @ w_up_a
    y_b = _mla(c_q, c_kv, k_r, positions, g_q_lora, w_uq, g_kv_lora, w_ukv) @ w_up_b
    merged = jax.nn.sigmoid(gate_a) * y_a + jax.nn.sigmoid(gate_b) * y_b
    return merged @ w_o


def _swiglu(h, w_gate, w_up, w_down):
    return (jax.nn.silu(h @ w_gate) * (h @ w_up)) @ w_down


def setup_inputs(seed: int = 0) -> dict:
    key = jax.random.key(seed)
    ks = jax.random.split(key, 20)
    f32 = jnp.float32

    def nrm(k, shape, fan_in, mult=1.0):
        return jax.random.normal(k, shape, f32) * (mult * fan_in ** -0.5)

    def gain(k, shape):
        return 1.0 + 0.05 * jax.random.normal(k, shape, f32)

    x = jax.random.normal(ks[0], (BATCH, SEQ, D_MODEL), f32)
    c = jax.random.normal(ks[1], (BATCH, D_MODEL), f32)
    offsets = jax.random.randint(ks[2], (BATCH, 1), 0, 4096, dtype=jnp.int32)
    positions = offsets + jnp.arange(SEQ, dtype=jnp.int32)[None, :]
    return {
        "x": x,
        "c": c,
        "positions": positions,
        "rel_bias": 0.5 * jax.random.normal(ks[3], (A_HEADS, REL_BUCKETS), f32),
        "w_ada": nrm(ks[4], (DEPTH, D_MODEL, N_MOD * D_MODEL), D_MODEL, 0.5),
        "b_ada": 0.02 * jax.random.normal(ks[5], (DEPTH, N_MOD * D_MODEL), f32),
        "g_mix": gain(ks[6], (DEPTH, D_MODEL)),
        "w_in": nrm(ks[7], (DEPTH, D_MODEL, IN_WIDTH), D_MODEL),
        "g_q_lora": gain(ks[8], (DEPTH, M_Q_LORA)),
        "w_uq": nrm(ks[9], (DEPTH, M_Q_LORA, M_HEADS, M_NOPE + M_ROPE), M_Q_LORA),
        "g_kv_lora": gain(ks[10], (DEPTH, M_KV_LORA)),
        "w_ukv": nrm(ks[11], (DEPTH, M_KV_LORA, M_HEADS, M_NOPE + M_V), M_KV_LORA),
        "w_up_a": nrm(ks[12], (DEPTH, A_WIDTH, D_MODEL), A_WIDTH),
        "w_up_b": nrm(ks[13], (DEPTH, M_WIDTH, D_MODEL), M_WIDTH),
        "w_o": nrm(ks[14], (DEPTH, D_MODEL, D_MODEL), D_MODEL),
        "g_ffn": gain(ks[15], (DEPTH, D_MODEL)),
        "w_gate": nrm(ks[16], (DEPTH, D_MODEL, D_FF), D_MODEL),
        "w_up": nrm(ks[17], (DEPTH, D_MODEL, D_FF), D_MODEL),
        "w_down": nrm(ks[18], (DEPTH, D_FF, D_MODEL), D_FF),
        "g_final": gain(ks[19], (D_MODEL,)),
    }


def reference(x, c, positions, rel_bias, w_ada, b_ada, g_mix, w_in, g_q_lora, w_uq,
              g_kv_lora, w_ukv, w_up_a, w_up_b, w_o, g_ffn, w_gate, w_up, w_down, g_final):
    cond = jax.nn.silu(c)
    for layer in range(DEPTH):
        mod = (cond @ w_ada[layer] + b_ada[layer])[:, None, :]
        sh1, sc1, gt1, sh2, sc2, gt2 = jnp.split(mod, N_MOD, axis=-1)
        h = _rmsnorm(x, g_mix[layer]) * (1.0 + sc1) + sh1
        x = x + gt1 * _hybrid_mixer(h, positions, rel_bias, w_in[layer], g_q_lora[layer],
                                    w_uq[layer], g_kv_lora[layer], w_ukv[layer],
                                    w_up_a[layer], w_up_b[layer], w_o[layer])
        h = _rmsnorm(x, g_ffn[layer]) * (1.0 + sc2) + sh2
        x = x + gt2 * _swiglu(h, w_gate[layer], w_up[layer], w_down[layer])
    return _rmsnorm(x, g_final)
```

```python
import functools
import math

import jax
import jax.numpy as jnp
import numpy as np
from jax import lax
from jax.experimental import pallas as pl
from jax.experimental.pallas import tpu as pltpu

D_MODEL = 1024
A_HEADS = 8
A_HEAD_DIM = 64
A_WIDTH = A_HEADS * A_HEAD_DIM
DILATED_PATTERNS = ((128, 1), (512, 4), (2048, 16))
BAND_BLOCK = 128
REL_BUCKETS = 32
REL_MAX_DIST = 2048
M_HEADS = 8
M_NOPE = 64
M_ROPE = 32
M_V = 64
M_Q_LORA = 768
M_KV_LORA = 256
M_WIDTH = M_HEADS * M_V
ROPE_THETA = 10000.0
D_FF = -(-8 * D_MODEL // (3 * 256)) * 256
N_MOD = 6
EPS = 1e-6
NEG_INF = -1e30

LANES = 128
SUBLANES = 8
V7X_VMEM_BYTES = 64 * 1024 * 1024
VMEM_LIMIT_BYTES = V7X_VMEM_BYTES - 8 * 1024 * 1024

M_HEAD_PAD = LANES
M_PAIRS = M_HEADS // 2
ROPE_HALF = M_ROPE // 2
ROPE_LO = M_NOPE
ROPE_MID = M_NOPE + ROPE_HALF
ROPE_HI = M_NOPE + M_ROPE

ROW_TILE = 512
MLA_TILE = 512

F32 = jnp.float32
BF16 = jnp.bfloat16


def _params(n_axes):
    return pltpu.CompilerParams(
        dimension_semantics=("arbitrary",) * n_axes,
        vmem_limit_bytes=VMEM_LIMIT_BYTES,
    )


def _resident(shape):
    zeros = (0,) * len(shape)
    return pl.BlockSpec(shape, lambda *_: zeros, pipeline_mode=pl.Buffered(1))


def _bdot(a, b):
    return jnp.dot(a, b, preferred_element_type=F32)


def _rms(x):
    return x * lax.rsqrt(jnp.mean(x * x, axis=-1, keepdims=True) + EPS)


def _sigmoid(x):
    return 1.0 / (1.0 + jnp.exp(-x))


def _mod_kernel(c_ref, w_ref, b_ref, o_ref):
    c = c_ref[...]
    cond = c * _sigmoid(c)
    o_ref[...] = (
        jnp.dot(cond, w_ref[...], preferred_element_type=F32, precision=lax.Precision.HIGHEST)
        + b_ref[...]
    )


def _modulation(c, w_ada, b_ada):
    batch = c.shape[0]
    rows = -(-batch // SUBLANES) * SUBLANES
    c_pad = jnp.pad(c, ((0, rows - batch), (0, 0)))
    out = pl.pallas_call(
        _mod_kernel,
        grid=(N_MOD,),
        in_specs=[
            pl.BlockSpec((rows, D_MODEL), lambda j: (0, 0)),
            pl.BlockSpec((D_MODEL, D_MODEL), lambda j: (0, j)),
            pl.BlockSpec((1, D_MODEL), lambda j: (0, j)),
        ],
        out_specs=pl.BlockSpec((rows, D_MODEL), lambda j: (0, j)),
        out_shape=jax.ShapeDtypeStruct((rows, N_MOD * D_MODEL), F32),
        compiler_params=_params(1),
        name="adaln_mod",
    )(c_pad, w_ada, b_ada.reshape(1, N_MOD * D_MODEL))
    return out[:batch].reshape(batch, N_MOD, D_MODEL)


def _rope_lanes(x, cos, sin_lo, sin_hi):
    return x * cos + pltpu.roll(x, LANES - ROPE_HALF, 1) * sin_lo + pltpu.roll(x, ROPE_HALF, 1) * sin_hi


def _input_kernel(x_ref, mod_ref, g_ref, pos_ref, freq_ref, wa_ref, wg_ref, wcq_ref, wckvr_ref,
                  gq_ref, wuq_ref, gkv_ref, wuk_ref, wuv_ref,
                  a_ref, gate_ref, q_ref, k_ref, v_ref):
    x = x_ref[0]
    shift = mod_ref[0, 0:1, :]
    scale = mod_ref[0, 1:2, :]
    h = (_rms(x) * g_ref[...]) * (1.0 + scale) + shift
    hb = h.astype(BF16)

    a_ref[0] = _bdot(hb, wa_ref[...]).astype(BF16)
    gate_ref[0] = _bdot(hb, wg_ref[...]).astype(BF16)

    ang = pos_ref[0].astype(F32) * freq_ref[...]
    cos = jnp.cos(ang)
    sin = jnp.sin(ang)
    lane = lax.broadcasted_iota(jnp.int32, ang.shape, 1)
    sin_lo = jnp.where((lane >= ROPE_LO) & (lane < ROPE_MID), -sin, 0.0)
    sin_hi = jnp.where((lane >= ROPE_MID) & (lane < ROPE_HI), sin, 0.0)

    c_q = _bdot(hb, wcq_ref[...])
    q_all = _bdot((_rms(c_q) * gq_ref[...]).astype(BF16), wuq_ref[...])
    q_scale = (M_NOPE + M_ROPE) ** -0.5
    for hd in range(M_HEADS):
        q_h = q_all[:, hd * M_HEAD_PAD:(hd + 1) * M_HEAD_PAD]
        q_ref[0, hd] = (_rope_lanes(q_h, cos, sin_lo, sin_hi) * q_scale).astype(BF16)

    ckvr = _bdot(hb, wckvr_ref[...])
    c_kv = (_rms(ckvr[:, :M_KV_LORA]) * gkv_ref[...]).astype(BF16)
    k_rope = _rope_lanes(ckvr[:, M_KV_LORA:], cos, sin_lo, sin_hi)
    k_all = _bdot(c_kv, wuk_ref[...])
    for hd in range(M_HEADS):
        k_ref[0, hd] = (k_all[:, hd * M_HEAD_PAD:(hd + 1) * M_HEAD_PAD] + k_rope).astype(BF16)
    v_all = _bdot(c_kv, wuv_ref[...])
    for pr in range(M_PAIRS):
        v_ref[0, pr] = v_all[:, pr * LANES:(pr + 1) * LANES].astype(BF16)


def _input_stage(x, mod, g_mix, positions, w_in, g_q_lora, w_uq, g_kv_lora, w_ukv):
    batch, seq, _ = x.shape
    tm = ROW_TILE
    s0 = 3 * A_WIDTH
    s1 = s0 + M_Q_LORA
    s2 = s1 + M_KV_LORA
    s3 = s2 + M_ROPE
    w_a = jnp.concatenate([w_in[:, :A_WIDTH] * (A_HEAD_DIM ** -0.5), w_in[:, A_WIDTH:s0]], axis=1).astype(BF16)
    w_cq = w_in[:, s0:s1].astype(BF16)
    zeros = functools.partial(jnp.zeros, dtype=w_in.dtype)
    w_ckvr = jnp.concatenate(
        [w_in[:, s1:s2], zeros((D_MODEL, ROPE_LO)), w_in[:, s2:s3], zeros((D_MODEL, LANES - ROPE_HI))],
        axis=1).astype(BF16)
    w_g = w_in[:, s3:].astype(BF16)
    w_uq_p = jnp.pad(w_uq, ((0, 0), (0, 0), (0, M_HEAD_PAD - M_NOPE - M_ROPE)))
    w_uq_p = w_uq_p.reshape(M_Q_LORA, M_HEADS * M_HEAD_PAD).astype(BF16)
    w_uk_p = jnp.pad(w_ukv[:, :, :M_NOPE], ((0, 0), (0, 0), (0, M_HEAD_PAD - M_NOPE)))
    w_uk_p = w_uk_p.reshape(M_KV_LORA, M_HEADS * M_HEAD_PAD).astype(BF16)
    w_uv = w_ukv[:, :, M_NOPE:].reshape(M_KV_LORA, M_WIDTH).astype(BF16)

    freqs = ROPE_THETA ** (-jnp.arange(ROPE_HALF, dtype=F32) / ROPE_HALF)
    freq_row = jnp.zeros((1, LANES), F32)
    freq_row = freq_row.at[0, ROPE_LO:ROPE_MID].set(freqs).at[0, ROPE_MID:ROPE_HI].set(freqs)

    row3 = lambda b, i: (b, i, 0)
    head4 = lambda b, i: (b, 0, i, 0)
    return pl.pallas_call(
        _input_kernel,
        grid=(batch, seq // tm),
        in_specs=[
            pl.BlockSpec((1, tm, D_MODEL), row3),
            pl.BlockSpec((1, N_MOD, D_MODEL), lambda b, i: (b, 0, 0)),
            _resident((1, D_MODEL)),
            pl.BlockSpec((1, tm, 1), row3),
            _resident((1, LANES)),
            _resident(w_a.shape), _resident(w_g.shape), _resident(w_cq.shape), _resident(w_ckvr.shape),
            _resident((1, M_Q_LORA)), _resident(w_uq_p.shape),
            _resident((1, M_KV_LORA)), _resident(w_uk_p.shape), _resident(w_uv.shape),
        ],
        out_specs=[
            pl.BlockSpec((1, tm, 3 * A_WIDTH), row3),
            pl.BlockSpec((1, tm, 2 * D_MODEL), row3),
            pl.BlockSpec((1, M_HEADS, tm, M_HEAD_PAD), head4),
            pl.BlockSpec((1, M_HEADS, tm, M_HEAD_PAD), head4),
            pl.BlockSpec((1, M_PAIRS, tm, LANES), head4),
        ],
        out_shape=[
            jax.ShapeDtypeStruct((batch, seq, 3 * A_WIDTH), BF16),
            jax.ShapeDtypeStruct((batch, seq, 2 * D_MODEL), BF16),
            jax.ShapeDtypeStruct((batch, M_HEADS, seq, M_HEAD_PAD), BF16),
            jax.ShapeDtypeStruct((batch, M_HEADS, seq, M_HEAD_PAD), BF16),
            jax.ShapeDtypeStruct((batch, M_PAIRS, seq, LANES), BF16),
        ],
        compiler_params=_params(2),
        name="input_stage",
    )(x, mod, g_mix.reshape(1, D_MODEL), positions.reshape(batch, seq, 1), freq_row,
      w_a, w_g, w_cq, w_ckvr, g_q_lora.reshape(1, M_Q_LORA), w_uq_p,
      g_kv_lora.reshape(1, M_KV_LORA), w_uk_p, w_uv)


def _mla_kernel(q_ref, k_ref, v_ref, o_ref, m_scr, l_scr, acc_scr):
    t = MLA_TILE
    qi = pl.program_id(2)
    row = lax.broadcasted_iota(jnp.int32, (t, t), 0)
    col = lax.broadcasted_iota(jnp.int32, (t, t), 1)
    causal = row >= col
    contract_last = (((1,), (1,)), ((), ()))

    for hh in range(2):
        q = q_ref[0, hh]
        m_scr[hh] = jnp.full((t, 1), NEG_INF, F32)
        l_scr[hh] = jnp.zeros((t, 1), F32)
        acc_scr[hh] = jnp.zeros((t, LANES), F32)

        def step(ki, diagonal, hh=hh, q=q):
            start = pl.multiple_of(ki * t, t)
            k = k_ref[0, hh, pl.ds(start, t), :]
            v = v_ref[0, 0, pl.ds(start, t), :]
            s = lax.dot_general(q, k, contract_last, preferred_element_type=F32)
            if diagonal:
                s = jnp.where(causal, s, NEG_INF)
            m_prev = m_scr[hh]
            m_new = jnp.maximum(m_prev, jnp.max(s, axis=-1, keepdims=True))
            alpha = jnp.exp(m_prev - m_new)
            p = jnp.exp(s - m_new)
            l_scr[hh] = alpha * l_scr[hh] + jnp.sum(p, axis=-1, keepdims=True)
            acc_scr[hh] = alpha * acc_scr[hh] + _bdot(p.astype(BF16), v)
            m_scr[hh] = m_new

        def body(ki, carry, step=step):
            step(ki, False)
            return carry

        lax.fori_loop(0, qi, body, 0)
        step(qi, True)

    lane = lax.broadcasted_iota(jnp.int32, (t, LANES), 1)
    o0 = acc_scr[0] * (1.0 / l_scr[0])
    o1 = acc_scr[1] * (1.0 / l_scr[1])
    o_ref[0] = jnp.where(lane < M_V, o0, o1).astype(BF16)


def _mla_attention(q, k, v):
    batch, _, seq, _ = q.shape
    t = MLA_TILE
    return pl.pallas_call(
        _mla_kernel,
        grid=(batch, M_PAIRS, seq // t),
        in_specs=[
            pl.BlockSpec((1, 2, t, M_HEAD_PAD), lambda b, p, i: (b, p, i, 0)),
            pl.BlockSpec((1, 2, seq, M_HEAD_PAD), lambda b, p, i: (b, p, 0, 0)),
            pl.BlockSpec((1, 1, seq, LANES), lambda b, p, i: (b, p, 0, 0)),
        ],
        out_specs=pl.BlockSpec((1, t, LANES), lambda b, p, i: (b, i, p)),
        out_shape=jax.ShapeDtypeStruct((batch, seq, M_WIDTH), BF16),
        scratch_shapes=[
            pltpu.VMEM((2, t, 1), F32),
            pltpu.VMEM((2, t, 1), F32),
            pltpu.VMEM((2, t, LANES), F32),
        ],
        compiler_params=_params(3),
        name="mla_attention",
    )(q, k, v)


def _t5_bucket_table(dilation, n_back):
    blk = BAND_BLOCK
    sub_dist = (np.arange(blk)[:, None] + blk) - np.arange(2 * blk)[None, :]
    dist = np.clip(sub_dist, 0, n_back) * dilation
    max_exact = REL_BUCKETS // 2
    d = np.maximum(dist, 1).astype(np.float32)
    ratio = np.log(d / np.float32(max_exact)) / np.float32(math.log(REL_MAX_DIST / max_exact))
    log_b = max_exact + (ratio * np.float32(REL_BUCKETS - max_exact)).astype(np.int32)
    log_b = np.minimum(log_b, REL_BUCKETS - 1)
    return np.where(dist < max_exact, dist, log_b).astype(np.int32)


def _dilated_kernel(rb_ref, bucket_ref, q_ref, kp_ref, kc_ref, vp_ref, vc_ref, o_ref, lse_ref, bias_scr):
    blk = BAND_BLOCK
    n = pl.program_id(2)
    first_step = (pl.program_id(0) == 0) & (pl.program_id(1) == 0) & (n == 0)

    @pl.when(first_step)
    def _build_bias():
        row = lax.broadcasted_iota(jnp.int32, (blk, 2 * blk), 0)
        col = lax.broadcasted_iota(jnp.int32, (blk, 2 * blk), 1)
        sub_dist = row + blk - col
        in_band = (sub_dist >= 0) & (sub_dist <= blk)
        bucket = bucket_ref[...]
        for hd in range(A_HEADS):
            bias = jnp.zeros((blk, 2 * blk), F32)
            for bk in range(REL_BUCKETS):
                bias = jnp.where(bucket == bk, rb_ref[hd, bk], bias)
            bias_scr[hd] = jnp.where(in_band, bias, NEG_INF)

    has_prev = n > 0
    contract_last = (((1,), (1,)), ((), ()))
    for hd in range(A_HEADS):
        cols = slice(hd * A_HEAD_DIM, (hd + 1) * A_HEAD_DIM)
        q = q_ref[0, :, cols]
        s_prev = lax.dot_general(q, kp_ref[0, :, cols], contract_last, preferred_element_type=F32)
        s_cur = lax.dot_general(q, kc_ref[0, :, cols], contract_last, preferred_element_type=F32)
        s_prev = s_prev + jnp.where(has_prev, bias_scr[hd, :, :blk], NEG_INF)
        s_cur = s_cur + bias_scr[hd, :, blk:]
        m = jnp.maximum(jnp.max(s_prev, axis=-1, keepdims=True), jnp.max(s_cur, axis=-1, keepdims=True))
        p_prev = jnp.exp(s_prev - m)
        p_cur = jnp.exp(s_cur - m)
        denom = jnp.sum(p_prev, axis=-1, keepdims=True) + jnp.sum(p_cur, axis=-1, keepdims=True)
        o = _bdot(p_prev.astype(BF16), vp_ref[0, :, cols]) + _bdot(p_cur.astype(BF16), vc_ref[0, :, cols])
        o_ref[0, :, cols] = (o * (1.0 / denom)).astype(BF16)
        lse_ref[0, :, cols] = jnp.broadcast_to(m + jnp.log(denom), (blk, A_HEAD_DIM))


def _dilated_pattern(a_qkv, rel_bias, window, dilation):
    batch, seq, width = a_qkv.shape
    blk = BAND_BLOCK
    assert window // dilation == blk, "band of exactly one block behind the query"
    sub_len = seq // dilation
    nb = sub_len // blk
    a_view = a_qkv.reshape(batch, sub_len, dilation * width)
    bucket = jnp.asarray(_t5_bucket_table(dilation, window // dilation))

    def part(which, prev):
        def index(b, r, n):
            return (b, jnp.maximum(n - 1, 0) if prev else n, r * 3 + which)
        return pl.BlockSpec((1, blk, A_WIDTH), index)

    out_spec = pl.BlockSpec((1, blk, A_WIDTH), lambda b, r, n: (b, n, r))
    o, lse = pl.pallas_call(
        _dilated_kernel,
        grid=(batch, dilation, nb),
        in_specs=[
            pl.BlockSpec(memory_space=pltpu.SMEM),
            pl.BlockSpec((blk, 2 * blk), lambda b, r, n: (0, 0)),
            part(0, False), part(1, True), part(1, False), part(2, True), part(2, False),
        ],
        out_specs=[out_spec, out_spec],
        out_shape=[
            jax.ShapeDtypeStruct((batch, sub_len, dilation * A_WIDTH), BF16),
            jax.ShapeDtypeStruct((batch, sub_len, dilation * A_WIDTH), F32),
        ],
        scratch_shapes=[pltpu.VMEM((A_HEADS, blk, 2 * blk), F32)],
        compiler_params=_params(3),
        name=f"dilated_d{dilation}",
    )(rel_bias, bucket, a_view, a_view, a_view, a_view, a_view)
    return o.reshape(batch, seq, A_WIDTH), lse.reshape(batch, seq, A_WIDTH)


def _merge_kernel(x_ref, mod_ref, o1_ref, o2_ref, o3_ref, l1_ref, l2_ref, l3_ref, ob_ref, gate_ref,
                  wa_ref, wb_ref, wo_ref, out_ref):
    l1, l2, l3 = l1_ref[0], l2_ref[0], l3_ref[0]
    top = jnp.maximum(jnp.maximum(l1, l2), l3)
    e1, e2, e3 = jnp.exp(l1 - top), jnp.exp(l2 - top), jnp.exp(l3 - top)
    mix = e1 * o1_ref[0].astype(F32) + e2 * o2_ref[0].astype(F32) + e3 * o3_ref[0].astype(F32)
    o_a = mix * (1.0 / (e1 + e2 + e3))
    y_a = _bdot(o_a.astype(BF16), wa_ref[...])
    y_b = _bdot(ob_ref[0], wb_ref[...])
    gates = gate_ref[0].astype(F32)
    merged = _sigmoid(gates[:, :D_MODEL]) * y_a + _sigmoid(gates[:, D_MODEL:]) * y_b
    mixed = _bdot(merged.astype(BF16), wo_ref[...])
    out_ref[0] = x_ref[0] + mod_ref[0, 2:3, :] * mixed


def _merge_stage(x, mod, outs_a, lses_a, o_b, gates, w_up_a, w_up_b, w_o):
    batch, seq, _ = x.shape
    tm = ROW_TILE
    row3 = lambda b, i: (b, i, 0)
    half = pl.BlockSpec((1, tm, A_WIDTH), row3)
    return pl.pallas_call(
        _merge_kernel,
        grid=(batch, seq // tm),
        in_specs=[
            pl.BlockSpec((1, tm, D_MODEL), row3),
            pl.BlockSpec((1, N_MOD, D_MODEL), lambda b, i: (b, 0, 0)),
            half, half, half, half, half, half, half,
            pl.BlockSpec((1, tm, 2 * D_MODEL), row3),
            _resident((A_WIDTH, D_MODEL)), _resident((M_WIDTH, D_MODEL)), _resident((D_MODEL, D_MODEL)),
        ],
        out_specs=pl.BlockSpec((1, tm, D_MODEL), row3),
        out_shape=jax.ShapeDtypeStruct((batch, seq, D_MODEL), F32),
        compiler_params=_params(2),
        name="merge_stage",
    )(x, mod, *outs_a, *lses_a, o_b, gates,
      w_up_a.astype(BF16), w_up_b.astype(BF16), w_o.astype(BF16))


def _ffn_kernel(x_ref, mod_ref, g_ref, gf_ref, wg_ref, wu_ref, wd_ref, out_ref):
    x = x_ref[0]
    h = (_rms(x) * g_ref[...]) * (1.0 + mod_ref[0, 4:5, :]) + mod_ref[0, 3:4, :]
    hb = h.astype(BF16)
    gate = _bdot(hb, wg_ref[...])
    up = _bdot(hb, wu_ref[...])
    act = (gate * _sigmoid(gate) * up).astype(BF16)
    y = x + mod_ref[0, 5:6, :] * _bdot(act, wd_ref[...])
    out_ref[0] = _rms(y) * gf_ref[...]


def _ffn_stage(x, mod, g_ffn, g_final, w_gate, w_up, w_down):
    batch, seq, _ = x.shape
    tm = ROW_TILE
    row3 = lambda b, i: (b, i, 0)
    return pl.pallas_call(
        _ffn_kernel,
        grid=(batch, seq // tm),
        in_specs=[
            pl.BlockSpec((1, tm, D_MODEL), row3),
            pl.BlockSpec((1, N_MOD, D_MODEL), lambda b, i: (b, 0, 0)),
            _resident((1, D_MODEL)), _resident((1, D_MODEL)),
            _resident((D_MODEL, D_FF)), _resident((D_MODEL, D_FF)), _resident((D_FF, D_MODEL)),
        ],
        out_specs=pl.BlockSpec((1, tm, D_MODEL), row3),
        out_shape=jax.ShapeDtypeStruct((batch, seq, D_MODEL), F32),
        compiler_params=_params(2),
        name="ffn_stage",
    )(x, mod, g_ffn.reshape(1, D_MODEL), g_final.reshape(1, D_MODEL),
      w_gate.astype(BF16), w_up.astype(BF16), w_down.astype(BF16))


def kernel(x, c, positions, rel_bias, w_ada, b_ada, g_mix, w_in, g_q_lora, w_uq, g_kv_lora, w_ukv,
           w_up_a, w_up_b, w_o, g_ffn, w_gate, w_up, w_down, g_final):
    assert w_ada.shape[0] == 1, "single-layer trunk"
    mod = _modulation(c, w_ada[0], b_ada[0])
    a_qkv, gates, q, k, v = _input_stage(x, mod, g_mix[0], positions, w_in[0], g_q_lora[0], w_uq[0],
                                         g_kv_lora[0], w_ukv[0])
    o_b = _mla_attention(q, k, v)
    outs_a, lses_a = [], []
    for window, dilation in DILATED_PATTERNS:
        o, lse = _dilated_pattern(a_qkv, rel_bias, window, dilation)
        outs_a.append(o)
        lses_a.append(lse)
    x1 = _merge_stage(x, mod, outs_a, lses_a, o_b, gates, w_up_a[0], w_up_b[0], w_o[0])
    return _ffn_stage(x1, mod, g_ffn[0], g_final, w_gate[0], w_up[0], w_down[0])
```

```python
import functools
import math

import jax
import jax.numpy as jnp
import numpy as np
from jax import lax
from jax.experimental import pallas as pl
from jax.experimental.pallas import tpu as pltpu

D_MODEL = 1024
A_HEADS = 8
A_HEAD_DIM = 64
A_WIDTH = A_HEADS * A_HEAD_DIM
DILATED_PATTERNS = ((128, 1), (512, 4), (2048, 16))
BAND_BLOCK = 128
REL_BUCKETS = 32
REL_MAX_DIST = 2048
M_HEADS = 8
M_NOPE = 64
M_ROPE = 32
M_V = 64
M_Q_LORA = 768
M_KV_LORA = 256
M_WIDTH = M_HEADS * M_V
ROPE_THETA = 10000.0
D_FF = -(-8 * D_MODEL // (3 * 256)) * 256
N_MOD = 6
EPS = 1e-6
NEG_INF = -1e30

LANES = 128
SUBLANES = 8
V7X_VMEM_BYTES = 64 * 1024 * 1024
VMEM_LIMIT_BYTES = V7X_VMEM_BYTES - 8 * 1024 * 1024

M_HEAD_PAD = LANES
M_PAIRS = M_HEADS // 2
ROPE_HALF = M_ROPE // 2
ROPE_LO = M_NOPE
ROPE_MID = M_NOPE + ROPE_HALF
ROPE_HI = M_NOPE + M_ROPE

ROW_TILE = 512
MLA_TILE = 512

F32 = jnp.float32
BF16 = jnp.bfloat16


def _params(n_axes):
    return pltpu.CompilerParams(
        dimension_semantics=("arbitrary",) * n_axes,
        vmem_limit_bytes=VMEM_LIMIT_BYTES,
    )


def _resident(shape):
    zeros = (0,) * len(shape)
    return pl.BlockSpec(shape, lambda *_: zeros, pipeline_mode=pl.Buffered(1))


def _bdot(a, b):
    return jnp.dot(a, b, preferred_element_type=F32)


def _rms(x):
    return x * lax.rsqrt(jnp.mean(x * x, axis=-1, keepdims=True) + EPS)


def _sigmoid(x):
    return 1.0 / (1.0 + jnp.exp(-x))


def _mod_kernel(c_ref, w_ref, b_ref, o_ref):
    c = c_ref[...]
    cond = c * _sigmoid(c)
    o_ref[...] = (
        jnp.dot(cond, w_ref[...], preferred_element_type=F32, precision=lax.Precision.HIGHEST)
        + b_ref[...]
    )


def _modulation(c, w_ada, b_ada):
    batch = c.shape[0]
    rows = -(-batch // SUBLANES) * SUBLANES
    c_pad = jnp.pad(c, ((0, rows - batch), (0, 0)))
    out = pl.pallas_call(
        _mod_kernel,
        grid=(N_MOD,),
        in_specs=[
            pl.BlockSpec((rows, D_MODEL), lambda j: (0, 0)),
            pl.BlockSpec((D_MODEL, D_MODEL), lambda j: (0, j)),
            pl.BlockSpec((1, D_MODEL), lambda j: (0, j)),
        ],
        out_specs=pl.BlockSpec((rows, D_MODEL), lambda j: (0, j)),
        out_shape=jax.ShapeDtypeStruct((rows, N_MOD * D_MODEL), F32),
        compiler_params=_params(1),
        name="adaln_mod",
    )(c_pad, w_ada, b_ada.reshape(1, N_MOD * D_MODEL))
    return out[:batch].reshape(batch, N_MOD, D_MODEL)


def _rope_lanes(x, cos, sin_lo, sin_hi):
    return x * cos + pltpu.roll(x, LANES - ROPE_HALF, 1) * sin_lo + pltpu.roll(x, ROPE_HALF, 1) * sin_hi


def _input_kernel(x_ref, mod_ref, g_ref, pos_ref, freq_ref, wa_ref, wg_ref, wcq_ref, wckvr_ref,
                  gq_ref, wuq_ref, gkv_ref, wuk_ref, wuv_ref,
                  a_ref, gate_ref, q_ref, k_ref, v_ref):
    x = x_ref[0]
    shift = mod_ref[0, 0:1, :]
    scale = mod_ref[0, 1:2, :]
    h = (_rms(x) * g_ref[...]) * (1.0 + scale) + shift
    hb = h.astype(BF16)

    a_ref[0] = _bdot(hb, wa_ref[...]).astype(BF16)
    gate_ref[0] = _bdot(hb, wg_ref[...]).astype(BF16)

    ang = pos_ref[0].astype(F32) * freq_ref[...]
    cos = jnp.cos(ang)
    sin = jnp.sin(ang)
    lane = lax.broadcasted_iota(jnp.int32, ang.shape, 1)
    sin_lo = jnp.where((lane >= ROPE_LO) & (lane < ROPE_MID), -sin, 0.0)
    sin_hi = jnp.where((lane >= ROPE_MID) & (lane < ROPE_HI), sin, 0.0)

    c_q = _bdot(hb, wcq_ref[...])
    q_all = _bdot((_rms(c_q) * gq_ref[...]).astype(BF16), wuq_ref[...])
    q_scale = (M_NOPE + M_ROPE) ** -0.5
    for hd in range(M_HEADS):
        q_h = q_all[:, hd * M_HEAD_PAD:(hd + 1) * M_HEAD_PAD]
        q_ref[0, hd] = (_rope_lanes(q_h, cos, sin_lo, sin_hi) * q_scale).astype(BF16)

    ckvr = _bdot(hb, wckvr_ref[...])
    c_kv = (_rms(ckvr[:, :M_KV_LORA]) * gkv_ref[...]).astype(BF16)
    k_rope = _rope_lanes(ckvr[:, M_KV_LORA:], cos, sin_lo, sin_hi)
    k_all = _bdot(c_kv, wuk_ref[...])
    for hd in range(M_HEADS):
        k_ref[0, hd] = (k_all[:, hd * M_HEAD_PAD:(hd + 1) * M_HEAD_PAD] + k_rope).astype(BF16)
    v_all = _bdot(c_kv, wuv_ref[...])
    for pr in range(M_PAIRS):
        v_ref[0, pr, 0] = v_all[:, pr * LANES:(pr + 1) * LANES].T.astype(BF16)


def _input_stage(x, mod, g_mix, positions, w_in, g_q_lora, w_uq, g_kv_lora, w_ukv):
    batch, seq, _ = x.shape
    tm = MLA_TILE
    s0 = 3 * A_WIDTH
    s1 = s0 + M_Q_LORA
    s2 = s1 + M_KV_LORA
    s3 = s2 + M_ROPE
    w_a = jnp.concatenate([w_in[:, :A_WIDTH] * (A_HEAD_DIM ** -0.5), w_in[:, A_WIDTH:s0]], axis=1).astype(BF16)
    w_cq = w_in[:, s0:s1].astype(BF16)
    zeros = functools.partial(jnp.zeros, dtype=w_in.dtype)
    w_ckvr = jnp.concatenate(
        [w_in[:, s1:s2], zeros((D_MODEL, ROPE_LO)), w_in[:, s2:s3], zeros((D_MODEL, LANES - ROPE_HI))],
        axis=1).astype(BF16)
    w_g = w_in[:, s3:].astype(BF16)
    w_uq_p = jnp.pad(w_uq, ((0, 0), (0, 0), (0, M_HEAD_PAD - M_NOPE - M_ROPE)))
    w_uq_p = w_uq_p.reshape(M_Q_LORA, M_HEADS * M_HEAD_PAD).astype(BF16)
    w_uk_p = jnp.pad(w_ukv[:, :, :M_NOPE], ((0, 0), (0, 0), (0, M_HEAD_PAD - M_NOPE)))
    w_uk_p = w_uk_p.reshape(M_KV_LORA, M_HEADS * M_HEAD_PAD).astype(BF16)
    w_uv = w_ukv[:, :, M_NOPE:].reshape(M_KV_LORA, M_WIDTH).astype(BF16)

    freqs = ROPE_THETA ** (-jnp.arange(ROPE_HALF, dtype=F32) / ROPE_HALF)
    freq_row = jnp.zeros((1, LANES), F32)
    freq_row = freq_row.at[0, ROPE_LO:ROPE_MID].set(freqs).at[0, ROPE_MID:ROPE_HI].set(freqs)

    row3 = lambda b, i: (b, i, 0)
    head4 = lambda b, i: (b, 0, i, 0)
    return pl.pallas_call(
        _input_kernel,
        grid=(batch, seq // tm),
        in_specs=[
            pl.BlockSpec((1, tm, D_MODEL), row3),
            pl.BlockSpec((1, N_MOD, D_MODEL), lambda b, i: (b, 0, 0)),
            _resident((1, D_MODEL)),
            pl.BlockSpec((1, tm, 1), row3),
            _resident((1, LANES)),
            _resident(w_a.shape), _resident(w_g.shape), _resident(w_cq.shape), _resident(w_ckvr.shape),
            _resident((1, M_Q_LORA)), _resident(w_uq_p.shape),
            _resident((1, M_KV_LORA)), _resident(w_uk_p.shape), _resident(w_uv.shape),
        ],
        out_specs=[
            pl.BlockSpec((1, tm, 3 * A_WIDTH), row3),
            pl.BlockSpec((1, tm, 2 * D_MODEL), row3),
            pl.BlockSpec((1, M_HEADS, tm, M_HEAD_PAD), head4),
            pl.BlockSpec((1, M_HEADS, tm, M_HEAD_PAD), head4),
            pl.BlockSpec((1, M_PAIRS, 1, LANES, tm), lambda b, i: (b, 0, i, 0, 0)),
        ],
        out_shape=[
            jax.ShapeDtypeStruct((batch, seq, 3 * A_WIDTH), BF16),
            jax.ShapeDtypeStruct((batch, seq, 2 * D_MODEL), BF16),
            jax.ShapeDtypeStruct((batch, M_HEADS, seq, M_HEAD_PAD), BF16),
            jax.ShapeDtypeStruct((batch, M_HEADS, seq, M_HEAD_PAD), BF16),
            jax.ShapeDtypeStruct((batch, M_PAIRS, seq // tm, LANES, tm), BF16),
        ],
        compiler_params=_params(2),
        name="input_stage",
    )(x, mod, g_mix.reshape(1, D_MODEL), positions.reshape(batch, seq, 1), freq_row,
      w_a, w_g, w_cq, w_ckvr, g_q_lora.reshape(1, M_Q_LORA), w_uq_p,
      g_kv_lora.reshape(1, M_KV_LORA), w_uk_p, w_uv)


def _mla_kernel(q_ref, k_ref, vt_ref, o_ref, m_scr, l_scr, acc_scr):
    t = MLA_TILE
    qi = pl.program_id(2)
    key_pos = lax.broadcasted_iota(jnp.int32, (t, t), 0)
    query_pos = lax.broadcasted_iota(jnp.int32, (t, t), 1)
    causal = key_pos <= query_pos
    contract_last = (((1,), (1,)), ((), ()))

    m_scr[...] = jnp.full(m_scr.shape, NEG_INF, F32)
    l_scr[...] = jnp.zeros(l_scr.shape, F32)
    acc_scr[...] = jnp.zeros(acc_scr.shape, F32)

    def step(ki, diagonal):
        start = pl.multiple_of(ki * t, t)
        vt = vt_ref[0, 0, ki]
        for hh in range(2):
            k = k_ref[0, hh, pl.ds(start, t), :]
            s = lax.dot_general(k, q_ref[0, hh], contract_last, preferred_element_type=F32)
            if diagonal:
                s = jnp.where(causal, s, NEG_INF)
            m_prev = m_scr[hh]
            m_new = jnp.maximum(m_prev, jnp.max(s, axis=0, keepdims=True))
            alpha = jnp.exp(m_prev - m_new)
            p = jnp.exp(s - m_new)
            l_scr[hh] = alpha * l_scr[hh] + jnp.sum(p, axis=0, keepdims=True)
            acc_scr[hh] = alpha * acc_scr[hh] + _bdot(vt, p.astype(BF16))
            m_scr[hh] = m_new

    def body(ki, carry):
        step(ki, False)
        return carry

    lax.fori_loop(0, qi, body, 0)
    step(qi, True)

    v_row = lax.broadcasted_iota(jnp.int32, (LANES, t), 0)
    o0 = acc_scr[0] * (1.0 / l_scr[0])
    o1 = acc_scr[1] * (1.0 / l_scr[1])
    o_ref[0] = jnp.where(v_row < M_V, o0, o1).T.astype(BF16)


def _mla_attention(q, k, vt):
    batch, _, seq, _ = q.shape
    t = MLA_TILE
    return pl.pallas_call(
        _mla_kernel,
        grid=(batch, M_PAIRS, seq // t),
        in_specs=[
            pl.BlockSpec((1, 2, t, M_HEAD_PAD), lambda b, p, i: (b, p, i, 0)),
            pl.BlockSpec((1, 2, seq, M_HEAD_PAD), lambda b, p, i: (b, p, 0, 0)),
            pl.BlockSpec((1, 1, seq // t, LANES, t), lambda b, p, i: (b, p, 0, 0, 0)),
        ],
        out_specs=pl.BlockSpec((1, t, LANES), lambda b, p, i: (b, i, p)),
        out_shape=jax.ShapeDtypeStruct((batch, seq, M_WIDTH), BF16),
        scratch_shapes=[
            pltpu.VMEM((2, 1, t), F32),
            pltpu.VMEM((2, 1, t), F32),
            pltpu.VMEM((2, LANES, t), F32),
        ],
        compiler_params=_params(3),
        name="mla_attention",
    )(q, k, vt)


def _t5_bucket_table(dilation, n_back):
    blk = BAND_BLOCK
    sub_dist = (np.arange(blk)[:, None] + blk) - np.arange(2 * blk)[None, :]
    dist = np.clip(sub_dist, 0, n_back) * dilation
    max_exact = REL_BUCKETS // 2
    d = np.maximum(dist, 1).astype(np.float32)
    ratio = np.log(d / np.float32(max_exact)) / np.float32(math.log(REL_MAX_DIST / max_exact))
    log_b = max_exact + (ratio * np.float32(REL_BUCKETS - max_exact)).astype(np.int32)
    log_b = np.minimum(log_b, REL_BUCKETS - 1)
    return np.where(dist < max_exact, dist, log_b).astype(np.int32)


def _dilated_kernel(rb_ref, bucket_ref, q_ref, kp_ref, kc_ref, vp_ref, vc_ref, o_ref, lse_ref, bias_scr):
    blk = BAND_BLOCK
    n = pl.program_id(2)
    first_step = (pl.program_id(0) == 0) & (pl.program_id(1) == 0) & (n == 0)

    @pl.when(first_step)
    def _build_bias():
        row = lax.broadcasted_iota(jnp.int32, (blk, 2 * blk), 0)
        col = lax.broadcasted_iota(jnp.int32, (blk, 2 * blk), 1)
        sub_dist = row + blk - col
        in_band = (sub_dist >= 0) & (sub_dist <= blk)
        bucket = bucket_ref[...]
        for hd in range(A_HEADS):
            bias = jnp.zeros((blk, 2 * blk), F32)
            for bk in range(REL_BUCKETS):
                bias = jnp.where(bucket == bk, rb_ref[hd, bk], bias)
            bias_scr[hd] = jnp.where(in_band, bias, NEG_INF)

    has_prev = n > 0
    contract_last = (((1,), (1,)), ((), ()))
    for hd in range(A_HEADS):
        cols = slice(hd * A_HEAD_DIM, (hd + 1) * A_HEAD_DIM)
        q = q_ref[0, :, cols]
        s_prev = lax.dot_general(q, kp_ref[0, :, cols], contract_last, preferred_element_type=F32)
        s_cur = lax.dot_general(q, kc_ref[0, :, cols], contract_last, preferred_element_type=F32)
        s_prev = s_prev + jnp.where(has_prev, bias_scr[hd, :, :blk], NEG_INF)
        s_cur = s_cur + bias_scr[hd, :, blk:]
        m = jnp.maximum(jnp.max(s_prev, axis=-1, keepdims=True), jnp.max(s_cur, axis=-1, keepdims=True))
        p_prev = jnp.exp(s_prev - m)
        p_cur = jnp.exp(s_cur - m)
        denom = jnp.sum(p_prev, axis=-1, keepdims=True) + jnp.sum(p_cur, axis=-1, keepdims=True)
        o = _bdot(p_prev.astype(BF16), vp_ref[0, :, cols]) + _bdot(p_cur.astype(BF16), vc_ref[0, :, cols])
        o_ref[0, :, cols] = (o * (1.0 / denom)).astype(BF16)
        lse_ref[0, :, cols] = jnp.broadcast_to(m + jnp.log(denom), (blk, A_HEAD_DIM))


def _dilated_pattern(a_qkv, rel_bias, window, dilation):
    batch, seq, width = a_qkv.shape
    blk = BAND_BLOCK
    assert window // dilation == blk, "band of exactly one block behind the query"
    sub_len = seq // dilation
    nb = sub_len // blk
    a_view = a_qkv.reshape(batch, sub_len, dilation * width)
    bucket = jnp.asarray(_t5_bucket_table(dilation, window // dilation))

    def part(which, prev):
        def index(b, r, n):
            return (b, jnp.maximum(n - 1, 0) if prev else n, r * 3 + which)
        return pl.BlockSpec((1, blk, A_WIDTH), index)

    out_spec = pl.BlockSpec((1, blk, A_WIDTH), lambda b, r, n: (b, n, r))
    o, lse = pl.pallas_call(
        _dilated_kernel,
        grid=(batch, dilation, nb),
        in_specs=[
            pl.BlockSpec(memory_space=pltpu.SMEM),
            pl.BlockSpec((blk, 2 * blk), lambda b, r, n: (0, 0)),
            part(0, False), part(1, True), part(1, False), part(2, True), part(2, False),
        ],
        out_specs=[out_spec, out_spec],
        out_shape=[
            jax.ShapeDtypeStruct((batch, sub_len, dilation * A_WIDTH), BF16),
            jax.ShapeDtypeStruct((batch, sub_len, dilation * A_WIDTH), F32),
        ],
        scratch_shapes=[pltpu.VMEM((A_HEADS, blk, 2 * blk), F32)],
        compiler_params=_params(3),
        name=f"dilated_d{dilation}",
    )(rel_bias, bucket, a_view, a_view, a_view, a_view, a_view)
    return o.reshape(batch, seq, A_WIDTH), lse.reshape(batch, seq, A_WIDTH)


def _merge_kernel(x_ref, mod_ref, o1_ref, o2_ref, o3_ref, l1_ref, l2_ref, l3_ref, ob_ref, gate_ref,
                  wa_ref, wb_ref, wo_ref, out_ref):
    l1, l2, l3 = l1_ref[0], l2_ref[0], l3_ref[0]
    top = jnp.maximum(jnp.maximum(l1, l2), l3)
    e1, e2, e3 = jnp.exp(l1 - top), jnp.exp(l2 - top), jnp.exp(l3 - top)
    mix = e1 * o1_ref[0].astype(F32) + e2 * o2_ref[0].astype(F32) + e3 * o3_ref[0].astype(F32)
    o_a = mix * (1.0 / (e1 + e2 + e3))
    y_a = _bdot(o_a.astype(BF16), wa_ref[...])
    y_b = _bdot(ob_ref[0], wb_ref[...])
    gates = gate_ref[0].astype(F32)
    merged = _sigmoid(gates[:, :D_MODEL]) * y_a + _sigmoid(gates[:, D_MODEL:]) * y_b
    mixed = _bdot(merged.astype(BF16), wo_ref[...])
    out_ref[0] = x_ref[0] + mod_ref[0, 2:3, :] * mixed


def _merge_stage(x, mod, outs_a, lses_a, o_b, gates, w_up_a, w_up_b, w_o):
    batch, seq, _ = x.shape
    tm = ROW_TILE
    row3 = lambda b, i: (b, i, 0)
    half = pl.BlockSpec((1, tm, A_WIDTH), row3)
    return pl.pallas_call(
        _merge_kernel,
        grid=(batch, seq // tm),
        in_specs=[
            pl.BlockSpec((1, tm, D_MODEL), row3),
            pl.BlockSpec((1, N_MOD, D_MODEL), lambda b, i: (b, 0, 0)),
            half, half, half, half, half, half, half,
            pl.BlockSpec((1, tm, 2 * D_MODEL), row3),
            _resident((A_WIDTH, D_MODEL)), _resident((M_WIDTH, D_MODEL)), _resident((D_MODEL, D_MODEL)),
        ],
        out_specs=pl.BlockSpec((1, tm, D_MODEL), row3),
        out_shape=jax.ShapeDtypeStruct((batch, seq, D_MODEL), F32),
        compiler_params=_params(2),
        name="merge_stage",
    )(x, mod, *outs_a, *lses_a, o_b, gates,
      w_up_a.astype(BF16), w_up_b.astype(BF16), w_o.astype(BF16))


def _ffn_kernel(x_ref, mod_ref, g_ref, gf_ref, wg_ref, wu_ref, wd_ref, out_ref):
    x = x_ref[0]
    h = (_rms(x) * g_ref[...]) * (1.0 + mod_ref[0, 4:5, :]) + mod_ref[0, 3:4, :]
    hb = h.astype(BF16)
    gate = _bdot(hb, wg_ref[...])
    up = _bdot(hb, wu_ref[...])
    act = (gate * _sigmoid(gate) * up).astype(BF16)
    y = x + mod_ref[0, 5:6, :] * _bdot(act, wd_ref[...])
    out_ref[0] = _rms(y) * gf_ref[...]


def _ffn_stage(x, mod, g_ffn, g_final, w_gate, w_up, w_down):
    batch, seq, _ = x.shape
    tm = ROW_TILE
    row3 = lambda b, i: (b, i, 0)
    return pl.pallas_call(
        _ffn_kernel,
        grid=(batch, seq // tm),
        in_specs=[
            pl.BlockSpec((1, tm, D_MODEL), row3),
            pl.BlockSpec((1, N_MOD, D_MODEL), lambda b, i: (b, 0, 0)),
            _resident((1, D_MODEL)), _resident((1, D_MODEL)),
            _resident((D_MODEL, D_FF)), _resident((D_MODEL, D_FF)), _resident((D_FF, D_MODEL)),
        ],
        out_specs=pl.BlockSpec((1, tm, D_MODEL), row3),
        out_shape=jax.ShapeDtypeStruct((batch, seq, D_MODEL), F32),
        compiler_params=_params(2),
        name="ffn_stage",
    )(x, mod, g_ffn.reshape(1, D_MODEL), g_final.reshape(1, D_MODEL),
      w_gate.astype(BF16), w_up.astype(BF16), w_down.astype(BF16))


def kernel(x, c, positions, rel_bias, w_ada, b_ada, g_mix, w_in, g_q_lora, w_uq, g_kv_lora, w_ukv,
           w_up_a, w_up_b, w_o, g_ffn, w_gate, w_up, w_down, g_final):
    assert w_ada.shape[0] == 1, "single-layer trunk"
    mod = _modulation(c, w_ada[0], b_ada[0])
    a_qkv, gates, q, k, vt = _input_stage(x, mod, g_mix[0], positions, w_in[0], g_q_lora[0], w_uq[0],
                                         g_kv_lora[0], w_ukv[0])
    o_b = _mla_attention(q, k, vt)
    outs_a, lses_a = [], []
    for window, dilation in DILATED_PATTERNS:
        o, lse = _dilated_pattern(a_qkv, rel_bias, window, dilation)
        outs_a.append(o)
        lses_a.append(lse)
    x1 = _merge_stage(x, mod, outs_a, lses_a, o_b, gates, w_up_a[0], w_up_b[0], w_o[0])
    return _ffn_stage(x1, mod, g_ffn[0], g_final, w_gate[0], w_up[0], w_down[0])
```

```python
import functools
import math

import jax
import jax.numpy as jnp
import numpy as np
from jax import lax
from jax.experimental import pallas as pl
from jax.experimental.pallas import tpu as pltpu

D_MODEL = 1024
A_HEADS = 8
A_HEAD_DIM = 64
A_WIDTH = A_HEADS * A_HEAD_DIM
DILATED_PATTERNS = ((128, 1), (512, 4), (2048, 16))
BAND_BLOCK = 128
REL_BUCKETS = 32
REL_MAX_DIST = 2048
M_HEADS = 8
M_NOPE = 64
M_ROPE = 32
M_V = 64
M_Q_LORA = 768
M_KV_LORA = 256
M_WIDTH = M_HEADS * M_V
ROPE_THETA = 10000.0
D_FF = -(-8 * D_MODEL // (3 * 256)) * 256
N_MOD = 6
EPS = 1e-6
NEG_INF = -1e30

LANES = 128
SUBLANES = 8
V7X_VMEM_BYTES = 64 * 1024 * 1024
VMEM_LIMIT_BYTES = V7X_VMEM_BYTES - 8 * 1024 * 1024

M_HEAD_PAD = LANES
M_PAIRS = M_HEADS // 2
ROPE_HALF = M_ROPE // 2
ROPE_LO = M_NOPE
ROPE_MID = M_NOPE + ROPE_HALF
ROPE_HI = M_NOPE + M_ROPE

A_PAIRS = A_HEADS // 2
A_TILES = 3 * A_PAIRS
SUPER_BLOCK = BAND_BLOCK * max(d for _, d in DILATED_PATTERNS)

ROW_TILE = 512
MLA_TILE = 512

F32 = jnp.float32
BF16 = jnp.bfloat16


def _params(n_axes):
    return pltpu.CompilerParams(
        dimension_semantics=("arbitrary",) * n_axes,
        vmem_limit_bytes=VMEM_LIMIT_BYTES,
    )


def _resident(shape):
    zeros = (0,) * len(shape)
    return pl.BlockSpec(shape, lambda *_: zeros, pipeline_mode=pl.Buffered(1))


def _bdot(a, b):
    return jnp.dot(a, b, preferred_element_type=F32)


def _rms(x):
    return x * lax.rsqrt(jnp.mean(x * x, axis=-1, keepdims=True) + EPS)


def _sigmoid(x):
    return 1.0 / (1.0 + jnp.exp(-x))


def _mod_kernel(c_ref, w_ref, b_ref, o_ref):
    c = c_ref[...]
    cond = c * _sigmoid(c)
    o_ref[...] = (
        jnp.dot(cond, w_ref[...], preferred_element_type=F32, precision=lax.Precision.HIGHEST)
        + b_ref[...]
    )


def _modulation(c, w_ada, b_ada):
    batch = c.shape[0]
    rows = -(-batch // SUBLANES) * SUBLANES
    c_pad = jnp.pad(c, ((0, rows - batch), (0, 0)))
    out = pl.pallas_call(
        _mod_kernel,
        grid=(N_MOD,),
        in_specs=[
            pl.BlockSpec((rows, D_MODEL), lambda j: (0, 0)),
            pl.BlockSpec((D_MODEL, D_MODEL), lambda j: (0, j)),
            pl.BlockSpec((1, D_MODEL), lambda j: (0, j)),
        ],
        out_specs=pl.BlockSpec((rows, D_MODEL), lambda j: (0, j)),
        out_shape=jax.ShapeDtypeStruct((rows, N_MOD * D_MODEL), F32),
        compiler_params=_params(1),
        name="adaln_mod",
    )(c_pad, w_ada, b_ada.reshape(1, N_MOD * D_MODEL))
    return out[:batch].reshape(batch, N_MOD, D_MODEL)


def _rope_lanes(x, cos, sin_lo, sin_hi):
    return x * cos + pltpu.roll(x, LANES - ROPE_HALF, 1) * sin_lo + pltpu.roll(x, ROPE_HALF, 1) * sin_hi


def _input_kernel(x_ref, mod_ref, g_ref, pos_ref, freq_ref, wa_ref, wg_ref, wcq_ref, wckvr_ref,
                  gq_ref, wuq_ref, gkv_ref, wuk_ref, wuv_ref,
                  a_ref, gate_ref, q_ref, k_ref, v_ref):
    x = x_ref[0]
    shift = mod_ref[0, 0:1, :]
    scale = mod_ref[0, 1:2, :]
    h = (_rms(x) * g_ref[...]) * (1.0 + scale) + shift
    hb = h.astype(BF16)

    a_all = _bdot(hb, wa_ref[...])
    for j in range(A_TILES):
        a_ref[0, j] = a_all[:, j * LANES:(j + 1) * LANES]
    gate_ref[0] = _bdot(hb, wg_ref[...]).astype(BF16)

    ang = pos_ref[0].astype(F32) * freq_ref[...]
    cos = jnp.cos(ang)
    sin = jnp.sin(ang)
    lane = lax.broadcasted_iota(jnp.int32, ang.shape, 1)
    sin_lo = jnp.where((lane >= ROPE_LO) & (lane < ROPE_MID), -sin, 0.0)
    sin_hi = jnp.where((lane >= ROPE_MID) & (lane < ROPE_HI), sin, 0.0)

    c_q = _bdot(hb, wcq_ref[...])
    q_all = _bdot((_rms(c_q) * gq_ref[...]).astype(BF16), wuq_ref[...])
    q_scale = (M_NOPE + M_ROPE) ** -0.5
    for hd in range(M_HEADS):
        q_h = q_all[:, hd * M_HEAD_PAD:(hd + 1) * M_HEAD_PAD]
        q_ref[0, hd] = (_rope_lanes(q_h, cos, sin_lo, sin_hi) * q_scale).astype(BF16)

    ckvr = _bdot(hb, wckvr_ref[...])
    c_kv = (_rms(ckvr[:, :M_KV_LORA]) * gkv_ref[...]).astype(BF16)
    k_rope = _rope_lanes(ckvr[:, M_KV_LORA:], cos, sin_lo, sin_hi)
    k_all = _bdot(c_kv, wuk_ref[...])
    for hd in range(M_HEADS):
        k_ref[0, hd] = (k_all[:, hd * M_HEAD_PAD:(hd + 1) * M_HEAD_PAD] + k_rope).astype(BF16)
    v_all = _bdot(c_kv, wuv_ref[...])
    for pr in range(M_PAIRS):
        v_ref[0, pr, 0] = v_all[:, pr * LANES:(pr + 1) * LANES].T.astype(BF16)


def _input_stage(x, mod, g_mix, positions, w_in, g_q_lora, w_uq, g_kv_lora, w_ukv):
    batch, seq, _ = x.shape
    tm = MLA_TILE
    s0 = 3 * A_WIDTH
    s1 = s0 + M_Q_LORA
    s2 = s1 + M_KV_LORA
    s3 = s2 + M_ROPE
    w_a = jnp.concatenate([w_in[:, :A_WIDTH] * (A_HEAD_DIM ** -0.5), w_in[:, A_WIDTH:s0]], axis=1).astype(BF16)
    w_cq = w_in[:, s0:s1].astype(BF16)
    zeros = functools.partial(jnp.zeros, dtype=w_in.dtype)
    w_ckvr = jnp.concatenate(
        [w_in[:, s1:s2], zeros((D_MODEL, ROPE_LO)), w_in[:, s2:s3], zeros((D_MODEL, LANES - ROPE_HI))],
        axis=1).astype(BF16)
    w_g = w_in[:, s3:].astype(BF16)
    w_uq_p = jnp.pad(w_uq, ((0, 0), (0, 0), (0, M_HEAD_PAD - M_NOPE - M_ROPE)))
    w_uq_p = w_uq_p.reshape(M_Q_LORA, M_HEADS * M_HEAD_PAD).astype(BF16)
    w_uk_p = jnp.pad(w_ukv[:, :, :M_NOPE], ((0, 0), (0, 0), (0, M_HEAD_PAD - M_NOPE)))
    w_uk_p = w_uk_p.reshape(M_KV_LORA, M_HEADS * M_HEAD_PAD).astype(BF16)
    w_uv = w_ukv[:, :, M_NOPE:].reshape(M_KV_LORA, M_WIDTH).astype(BF16)

    freqs = ROPE_THETA ** (-jnp.arange(ROPE_HALF, dtype=F32) / ROPE_HALF)
    freq_row = jnp.zeros((1, LANES), F32)
    freq_row = freq_row.at[0, ROPE_LO:ROPE_MID].set(freqs).at[0, ROPE_MID:ROPE_HI].set(freqs)

    row3 = lambda b, i: (b, i, 0)
    head4 = lambda b, i: (b, 0, i, 0)
    return pl.pallas_call(
        _input_kernel,
        grid=(batch, seq // tm),
        in_specs=[
            pl.BlockSpec((1, tm, D_MODEL), row3),
            pl.BlockSpec((1, N_MOD, D_MODEL), lambda b, i: (b, 0, 0)),
            _resident((1, D_MODEL)),
            pl.BlockSpec((1, tm, 1), row3),
            _resident((1, LANES)),
            _resident(w_a.shape), _resident(w_g.shape), _resident(w_cq.shape), _resident(w_ckvr.shape),
            _resident((1, M_Q_LORA)), _resident(w_uq_p.shape),
            _resident((1, M_KV_LORA)), _resident(w_uk_p.shape), _resident(w_uv.shape),
        ],
        out_specs=[
            pl.BlockSpec((1, A_TILES, tm, LANES), head4),
            pl.BlockSpec((1, tm, 2 * D_MODEL), row3),
            pl.BlockSpec((1, M_HEADS, tm, M_HEAD_PAD), head4),
            pl.BlockSpec((1, M_HEADS, tm, M_HEAD_PAD), head4),
            pl.BlockSpec((1, M_PAIRS, 1, LANES, tm), lambda b, i: (b, 0, i, 0, 0)),
        ],
        out_shape=[
            jax.ShapeDtypeStruct((batch, A_TILES, seq, LANES), F32),
            jax.ShapeDtypeStruct((batch, seq, 2 * D_MODEL), BF16),
            jax.ShapeDtypeStruct((batch, M_HEADS, seq, M_HEAD_PAD), BF16),
            jax.ShapeDtypeStruct((batch, M_HEADS, seq, M_HEAD_PAD), BF16),
            jax.ShapeDtypeStruct((batch, M_PAIRS, seq // tm, LANES, tm), BF16),
        ],
        compiler_params=_params(2),
        name="input_stage",
    )(x, mod, g_mix.reshape(1, D_MODEL), positions.reshape(batch, seq, 1), freq_row,
      w_a, w_g, w_cq, w_ckvr, g_q_lora.reshape(1, M_Q_LORA), w_uq_p,
      g_kv_lora.reshape(1, M_KV_LORA), w_uk_p, w_uv)


def _mla_kernel(q_ref, k_ref, vt_ref, o_ref, m_scr, l_scr, acc_scr):
    t = MLA_TILE
    qi = pl.program_id(2)
    key_pos = lax.broadcasted_iota(jnp.int32, (t, t), 0)
    query_pos = lax.broadcasted_iota(jnp.int32, (t, t), 1)
    causal = key_pos <= query_pos
    contract_last = (((1,), (1,)), ((), ()))

    m_scr[...] = jnp.full(m_scr.shape, NEG_INF, F32)
    l_scr[...] = jnp.zeros(l_scr.shape, F32)
    acc_scr[...] = jnp.zeros(acc_scr.shape, F32)

    def step(ki, diagonal):
        start = pl.multiple_of(ki * t, t)
        vt = vt_ref[0, 0, ki]
        for hh in range(2):
            k = k_ref[0, hh, pl.ds(start, t), :]
            s = lax.dot_general(k, q_ref[0, hh], contract_last, preferred_element_type=F32)
            if diagonal:
                s = jnp.where(causal, s, NEG_INF)
            m_prev = m_scr[hh]
            m_new = jnp.maximum(m_prev, jnp.max(s, axis=0, keepdims=True))
            alpha = jnp.exp(m_prev - m_new)
            p = jnp.exp(s - m_new)
            l_scr[hh] = alpha * l_scr[hh] + jnp.sum(p, axis=0, keepdims=True)
            acc_scr[hh] = alpha * acc_scr[hh] + _bdot(vt, p.astype(BF16))
            m_scr[hh] = m_new

    def body(ki, carry):
        step(ki, False)
        return carry

    lax.fori_loop(0, qi, body, 0)
    step(qi, True)

    v_row = lax.broadcasted_iota(jnp.int32, (LANES, t), 0)
    o0 = acc_scr[0] * (1.0 / l_scr[0])
    o1 = acc_scr[1] * (1.0 / l_scr[1])
    o_ref[0] = jnp.where(v_row < M_V, o0, o1).T.astype(BF16)


def _mla_attention(q, k, vt):
    batch, _, seq, _ = q.shape
    t = MLA_TILE
    return pl.pallas_call(
        _mla_kernel,
        grid=(batch, M_PAIRS, seq // t),
        in_specs=[
            pl.BlockSpec((1, 2, t, M_HEAD_PAD), lambda b, p, i: (b, p, i, 0)),
            pl.BlockSpec((1, 2, seq, M_HEAD_PAD), lambda b, p, i: (b, p, 0, 0)),
            pl.BlockSpec((1, 1, seq // t, LANES, t), lambda b, p, i: (b, p, 0, 0, 0)),
        ],
        out_specs=pl.BlockSpec((1, t, LANES), lambda b, p, i: (b, i, p)),
        out_shape=jax.ShapeDtypeStruct((batch, seq, M_WIDTH), BF16),
        scratch_shapes=[
            pltpu.VMEM((2, 1, t), F32),
            pltpu.VMEM((2, 1, t), F32),
            pltpu.VMEM((2, LANES, t), F32),
        ],
        compiler_params=_params(3),
        name="mla_attention",
    )(q, k, vt)


def _t5_bucket_table(dilation, n_back):
    blk = BAND_BLOCK
    sub_dist = (np.arange(blk)[:, None] + blk) - np.arange(2 * blk)[None, :]
    dist = np.clip(sub_dist, 0, n_back) * dilation
    max_exact = REL_BUCKETS // 2
    d = np.maximum(dist, 1).astype(np.float32)
    ratio = np.log(d / np.float32(max_exact)) / np.float32(math.log(REL_MAX_DIST / max_exact))
    log_b = max_exact + (ratio * np.float32(REL_BUCKETS - max_exact)).astype(np.int32)
    log_b = np.minimum(log_b, REL_BUCKETS - 1)
    return np.where(dist < max_exact, dist, log_b).astype(np.int32)


def _rows(start, dilation):
    if dilation == 1:
        return pl.ds(start, BAND_BLOCK)
    return pl.ds(start, BAND_BLOCK, stride=dilation)


def _dilated_kernel(rb_ref, bucket_ref, q_ref, kc_ref, kp_ref, vc_ref, vp_ref, o_ref,
                    bias_scr, acc_scr, m_scr):
    blk = BAND_BLOCK
    sup = SUPER_BLOCK
    pair = pl.program_id(2)
    first_step = (pl.program_id(0) == 0) & (pl.program_id(1) == 0) & (pair == 0)

    @pl.when(first_step)
    def _build_bias():
        row = lax.broadcasted_iota(jnp.int32, (blk, 2 * blk), 0)
        col = lax.broadcasted_iota(jnp.int32, (blk, 2 * blk), 1)
        sub_dist = row + blk - col
        in_band = (sub_dist >= 0) & (sub_dist <= blk)
        for g in range(len(DILATED_PATTERNS)):
            bucket = bucket_ref[g]
            for hd in range(A_HEADS):
                bias = jnp.zeros((blk, 2 * blk), F32)
                for bk in range(REL_BUCKETS):
                    bias = jnp.where(bucket == bk, rb_ref[hd, bk], bias)
                bias_scr[g, hd] = jnp.where(in_band, bias, NEG_INF)

    first_valid_col = jnp.where(pl.program_id(1) > 0, 0, blk)
    col = lax.broadcasted_iota(jnp.int32, (blk, 2 * blk), 1)
    lane = lax.broadcasted_iota(jnp.int32, (blk, LANES), 1)
    lane2 = lax.broadcasted_iota(jnp.int32, (2 * blk, LANES), 1)
    contract_last = (((1,), (1,)), ((), ()))
    n_pat = len(DILATED_PATTERNS)

    for g, (_, dil) in enumerate(DILATED_PATTERNS):
        for res in range(dil):
            for n in range(sup // (blk * dil)):
                start = res + blk * dil * n
                rows = _rows(start, dil)
                q = q_ref[0, 0, rows, :]
                if n == 0:
                    prev_rows = _rows(sup - blk * dil + res, dil)
                    k_prev, v_prev = kp_ref[0, 0, prev_rows, :], vp_ref[0, 0, prev_rows, :]
                else:
                    prev_rows = _rows(start - blk * dil, dil)
                    k_prev, v_prev = kc_ref[0, 0, prev_rows, :], vc_ref[0, 0, prev_rows, :]
                k2 = jnp.concatenate([k_prev, kc_ref[0, 0, rows, :]], axis=0).astype(BF16)
                v2 = jnp.concatenate([v_prev, vc_ref[0, 0, rows, :]], axis=0)
                for hh in range(2):
                    in_head = (lane < A_HEAD_DIM) if hh == 0 else (lane >= A_HEAD_DIM)
                    in_head2 = (lane2 < A_HEAD_DIM) if hh == 0 else (lane2 >= A_HEAD_DIM)
                    s = lax.dot_general(jnp.where(in_head, q, 0.0).astype(BF16), k2, contract_last,
                                        preferred_element_type=F32)
                    bias = bias_scr[g, 2 * pair + hh]
                    if n == 0:
                        bias = jnp.where(col >= first_valid_col, bias, NEG_INF)
                    s = s + bias
                    m_blk = jnp.max(s, axis=-1, keepdims=True)
                    p = jnp.exp(s - m_blk).astype(BF16)
                    acc_scr[g, hh, rows, :] = _bdot(p, jnp.where(in_head2, v2, 1.0).astype(BF16))
                    m_scr[g, hh, rows, :] = jnp.broadcast_to(m_blk, (blk, LANES))

    halves = []
    for hh in range(2):
        top = m_scr[0, hh]
        for g in range(1, n_pat):
            top = jnp.maximum(top, m_scr[g, hh])
        total = jnp.exp(m_scr[0, hh] - top) * acc_scr[0, hh]
        for g in range(1, n_pat):
            total = total + jnp.exp(m_scr[g, hh] - top) * acc_scr[g, hh]
        halves.append(total * (1.0 / pltpu.roll(total, A_HEAD_DIM, 1)))
    lane_sup = lax.broadcasted_iota(jnp.int32, (sup, LANES), 1)
    o_ref[0] = jnp.where(lane_sup < A_HEAD_DIM, halves[0], halves[1]).astype(BF16)


def _dilated_attention(a_qkv, rel_bias):
    batch, _, seq, _ = a_qkv.shape
    blk = BAND_BLOCK
    sup = SUPER_BLOCK
    n_pat = len(DILATED_PATTERNS)
    assert all(w // d == blk for w, d in DILATED_PATTERNS), "band of exactly one block behind the query"
    assert seq % sup == 0
    bucket = jnp.asarray(np.stack([_t5_bucket_table(d, w // d) for w, d in DILATED_PATTERNS]))

    def part(which, prev):
        def index(b, s, p):
            return (b, which * A_PAIRS + p, jnp.maximum(s - 1, 0) if prev else s, 0)
        return pl.BlockSpec((1, 1, sup, LANES), index)

    return pl.pallas_call(
        _dilated_kernel,
        grid=(batch, seq // sup, A_PAIRS),
        in_specs=[
            pl.BlockSpec(memory_space=pltpu.SMEM),
            pl.BlockSpec((n_pat, blk, 2 * blk), lambda b, s, p: (0, 0, 0)),
            part(0, False), part(1, False), part(1, True), part(2, False), part(2, True),
        ],
        out_specs=pl.BlockSpec((1, sup, LANES), lambda b, s, p: (b, s, p)),
        out_shape=jax.ShapeDtypeStruct((batch, seq, A_WIDTH), BF16),
        scratch_shapes=[
            pltpu.VMEM((n_pat, A_HEADS, blk, 2 * blk), F32),
            pltpu.VMEM((n_pat, 2, sup, LANES), F32),
            pltpu.VMEM((n_pat, 2, sup, LANES), F32),
        ],
        compiler_params=_params(3),
        name="dilated_attention",
    )(rel_bias, bucket, a_qkv, a_qkv, a_qkv, a_qkv, a_qkv)


def _merge_kernel(x_ref, mod_ref, oa_ref, ob_ref, gate_ref, wa_ref, wb_ref, wo_ref, out_ref):
    y_a = _bdot(oa_ref[0], wa_ref[...])
    y_b = _bdot(ob_ref[0], wb_ref[...])
    gates = gate_ref[0].astype(F32)
    merged = _sigmoid(gates[:, :D_MODEL]) * y_a + _sigmoid(gates[:, D_MODEL:]) * y_b
    mixed = _bdot(merged.astype(BF16), wo_ref[...])
    out_ref[0] = x_ref[0] + mod_ref[0, 2:3, :] * mixed


def _merge_stage(x, mod, o_a, o_b, gates, w_up_a, w_up_b, w_o):
    batch, seq, _ = x.shape
    tm = ROW_TILE
    row3 = lambda b, i: (b, i, 0)
    half = pl.BlockSpec((1, tm, A_WIDTH), row3)
    return pl.pallas_call(
        _merge_kernel,
        grid=(batch, seq // tm),
        in_specs=[
            pl.BlockSpec((1, tm, D_MODEL), row3),
            pl.BlockSpec((1, N_MOD, D_MODEL), lambda b, i: (b, 0, 0)),
            half, half,
            pl.BlockSpec((1, tm, 2 * D_MODEL), row3),
            _resident((A_WIDTH, D_MODEL)), _resident((M_WIDTH, D_MODEL)), _resident((D_MODEL, D_MODEL)),
        ],
        out_specs=pl.BlockSpec((1, tm, D_MODEL), row3),
        out_shape=jax.ShapeDtypeStruct((batch, seq, D_MODEL), F32),
        compiler_params=_params(2),
        name="merge_stage",
    )(x, mod, o_a, o_b, gates,
      w_up_a.astype(BF16), w_up_b.astype(BF16), w_o.astype(BF16))


def _ffn_kernel(x_ref, mod_ref, g_ref, gf_ref, wg_ref, wu_ref, wd_ref, out_ref):
    x = x_ref[0]
    h = (_rms(x) * g_ref[...]) * (1.0 + mod_ref[0, 4:5, :]) + mod_ref[0, 3:4, :]
    hb = h.astype(BF16)
    gate = _bdot(hb, wg_ref[...])
    up = _bdot(hb, wu_ref[...])
    act = (gate * _sigmoid(gate) * up).astype(BF16)
    y = x + mod_ref[0, 5:6, :] * _bdot(act, wd_ref[...])
    out_ref[0] = _rms(y) * gf_ref[...]


def _ffn_stage(x, mod, g_ffn, g_final, w_gate, w_up, w_down):
    batch, seq, _ = x.shape
    tm = ROW_TILE
    row3 = lambda b, i: (b, i, 0)
    return pl.pallas_call(
        _ffn_kernel,
        grid=(batch, seq // tm),
        in_specs=[
            pl.BlockSpec((1, tm, D_MODEL), row3),
            pl.BlockSpec((1, N_MOD, D_MODEL), lambda b, i: (b, 0, 0)),
            _resident((1, D_MODEL)), _resident((1, D_MODEL)),
            _resident((D_MODEL, D_FF)), _resident((D_MODEL, D_FF)), _resident((D_FF, D_MODEL)),
        ],
        out_specs=pl.BlockSpec((1, tm, D_MODEL), row3),
        out_shape=jax.ShapeDtypeStruct((batch, seq, D_MODEL), F32),
        compiler_params=_params(2),
        name="ffn_stage",
    )(x, mod, g_ffn.reshape(1, D_MODEL), g_final.reshape(1, D_MODEL),
      w_gate.astype(BF16), w_up.astype(BF16), w_down.astype(BF16))


def kernel(x, c, positions, rel_bias, w_ada, b_ada, g_mix, w_in, g_q_lora, w_uq, g_kv_lora, w_ukv,
           w_up_a, w_up_b, w_o, g_ffn, w_gate, w_up, w_down, g_final):
    assert w_ada.shape[0] == 1, "single-layer trunk"
    mod = _modulation(c, w_ada[0], b_ada[0])
    a_qkv, gates, q, k, vt = _input_stage(x, mod, g_mix[0], positions, w_in[0], g_q_lora[0], w_uq[0],
                                         g_kv_lora[0], w_ukv[0])
    o_b = _mla_attention(q, k, vt)
    o_a = _dilated_attention(a_qkv, rel_bias)
    x1 = _merge_stage(x, mod, o_a, o_b, gates, w_up_a[0], w_up_b[0], w_o[0])
    return _ffn_stage(x1, mod, g_ffn[0], g_final, w_gate[0], w_up[0], w_down[0])
```

```python
import functools
import math

import jax
import jax.numpy as jnp
import numpy as np
from jax import lax
from jax.experimental import pallas as pl
from jax.experimental.pallas import tpu as pltpu

D_MODEL = 1024
A_HEADS = 8
A_HEAD_DIM = 64
A_WIDTH = A_HEADS * A_HEAD_DIM
DILATED_PATTERNS = ((128, 1), (512, 4), (2048, 16))
BAND_BLOCK = 128
REL_BUCKETS = 32
REL_MAX_DIST = 2048
M_HEADS = 8
M_NOPE = 64
M_ROPE = 32
M_V = 64
M_Q_LORA = 768
M_KV_LORA = 256
M_WIDTH = M_HEADS * M_V
ROPE_THETA = 10000.0
D_FF = -(-8 * D_MODEL // (3 * 256)) * 256
N_MOD = 6
EPS = 1e-6
NEG_INF = -1e30

LANES = 128
SUBLANES = 8
V7X_VMEM_BYTES = 64 * 1024 * 1024
VMEM_LIMIT_BYTES = V7X_VMEM_BYTES - 8 * 1024 * 1024

M_HEAD_PAD = LANES
M_PAIRS = M_HEADS // 2
ROPE_HALF = M_ROPE // 2
ROPE_LO = M_NOPE
ROPE_MID = M_NOPE + ROPE_HALF
ROPE_HI = M_NOPE + M_ROPE

A_PAIRS = A_HEADS // 2
A_TILES = 3 * A_PAIRS
SUPER_BLOCK = BAND_BLOCK * max(d for _, d in DILATED_PATTERNS)

ROW_TILE = 512
MLA_TILE = 512
MLA_QUERY_CHUNK = 256

F32 = jnp.float32
BF16 = jnp.bfloat16


def _params(n_axes, flags=None):
    return pltpu.CompilerParams(
        dimension_semantics=("arbitrary",) * n_axes,
        vmem_limit_bytes=VMEM_LIMIT_BYTES,
        flags=flags,
    )


def _resident(shape):
    zeros = (0,) * len(shape)
    return pl.BlockSpec(shape, lambda *_: zeros, pipeline_mode=pl.Buffered(1))


def _bdot(a, b):
    return jnp.dot(a, b, preferred_element_type=F32)


def _rms(x):
    return x * lax.rsqrt(jnp.mean(x * x, axis=-1, keepdims=True) + EPS)


def _sigmoid(x):
    return 1.0 / (1.0 + jnp.exp(-x))


def _mod_kernel(c_ref, w_ref, b_ref, o_ref):
    c = c_ref[...]
    cond = c * _sigmoid(c)
    o_ref[...] = (
        jnp.dot(cond, w_ref[...], preferred_element_type=F32, precision=lax.Precision.HIGHEST)
        + b_ref[...]
    )


def _modulation(c, w_ada, b_ada):
    batch = c.shape[0]
    rows = -(-batch // SUBLANES) * SUBLANES
    c_pad = jnp.pad(c, ((0, rows - batch), (0, 0)))
    out = pl.pallas_call(
        _mod_kernel,
        grid=(N_MOD,),
        in_specs=[
            pl.BlockSpec((rows, D_MODEL), lambda j: (0, 0)),
            pl.BlockSpec((D_MODEL, D_MODEL), lambda j: (0, j)),
            pl.BlockSpec((1, D_MODEL), lambda j: (0, j)),
        ],
        out_specs=pl.BlockSpec((rows, D_MODEL), lambda j: (0, j)),
        out_shape=jax.ShapeDtypeStruct((rows, N_MOD * D_MODEL), F32),
        compiler_params=_params(1),
        name="adaln_mod",
    )(c_pad, w_ada, b_ada.reshape(1, N_MOD * D_MODEL))
    return out[:batch].reshape(batch, N_MOD, D_MODEL)


def _rope_lanes(x, cos, sin_lo, sin_hi):
    return x * cos + pltpu.roll(x, LANES - ROPE_HALF, 1) * sin_lo + pltpu.roll(x, ROPE_HALF, 1) * sin_hi


def _input_kernel(x_ref, mod_ref, g_ref, pos_ref, freq_ref, wa_ref, wg_ref, wcq_ref, wckvr_ref,
                  gq_ref, wuq_ref, gkv_ref, wuk_ref, wuv_ref,
                  a_ref, gate_ref, q_ref, k_ref, v_ref):
    x = x_ref[0]
    shift = mod_ref[0, 0:1, :]
    scale = mod_ref[0, 1:2, :]
    h = (_rms(x) * g_ref[...]) * (1.0 + scale) + shift
    hb = h.astype(BF16)

    a_all = _bdot(hb, wa_ref[...])
    for j in range(A_TILES):
        a_ref[0, j] = a_all[:, j * LANES:(j + 1) * LANES]
    gate_ref[0] = _bdot(hb, wg_ref[...]).astype(BF16)

    ang = pos_ref[0].astype(F32) * freq_ref[...]
    cos = jnp.cos(ang)
    sin = jnp.sin(ang)
    lane = lax.broadcasted_iota(jnp.int32, ang.shape, 1)
    sin_lo = jnp.where((lane >= ROPE_LO) & (lane < ROPE_MID), -sin, 0.0)
    sin_hi = jnp.where((lane >= ROPE_MID) & (lane < ROPE_HI), sin, 0.0)

    c_q = _bdot(hb, wcq_ref[...])
    q_all = _bdot((_rms(c_q) * gq_ref[...]).astype(BF16), wuq_ref[...])
    q_scale = (M_NOPE + M_ROPE) ** -0.5 * math.log2(math.e)
    for hd in range(M_HEADS):
        q_h = q_all[:, hd * M_HEAD_PAD:(hd + 1) * M_HEAD_PAD]
        q_ref[0, hd] = (_rope_lanes(q_h, cos, sin_lo, sin_hi) * q_scale).astype(BF16)

    ckvr = _bdot(hb, wckvr_ref[...])
    c_kv = (_rms(ckvr[:, :M_KV_LORA]) * gkv_ref[...]).astype(BF16)
    k_rope = _rope_lanes(ckvr[:, M_KV_LORA:], cos, sin_lo, sin_hi)
    k_all = _bdot(c_kv, wuk_ref[...])
    for hd in range(M_HEADS):
        k_ref[0, hd] = (k_all[:, hd * M_HEAD_PAD:(hd + 1) * M_HEAD_PAD] + k_rope).astype(BF16)
    v_all = _bdot(c_kv, wuv_ref[...])
    for pr in range(M_PAIRS):
        v_ref[0, pr, 0] = v_all[:, pr * LANES:(pr + 1) * LANES].T.astype(BF16)


def _input_stage(x, mod, g_mix, positions, w_in, g_q_lora, w_uq, g_kv_lora, w_ukv):
    batch, seq, _ = x.shape
    tm = MLA_TILE
    s0 = 3 * A_WIDTH
    s1 = s0 + M_Q_LORA
    s2 = s1 + M_KV_LORA
    s3 = s2 + M_ROPE
    w_a = jnp.concatenate([w_in[:, :A_WIDTH] * (A_HEAD_DIM ** -0.5), w_in[:, A_WIDTH:s0]], axis=1).astype(BF16)
    w_cq = w_in[:, s0:s1].astype(BF16)
    zeros = functools.partial(jnp.zeros, dtype=w_in.dtype)
    w_ckvr = jnp.concatenate(
        [w_in[:, s1:s2], zeros((D_MODEL, ROPE_LO)), w_in[:, s2:s3], zeros((D_MODEL, LANES - ROPE_HI))],
        axis=1).astype(BF16)
    w_g = w_in[:, s3:].astype(BF16)
    w_uq_p = jnp.pad(w_uq, ((0, 0), (0, 0), (0, M_HEAD_PAD - M_NOPE - M_ROPE)))
    w_uq_p = w_uq_p.reshape(M_Q_LORA, M_HEADS * M_HEAD_PAD).astype(BF16)
    w_uk_p = jnp.pad(w_ukv[:, :, :M_NOPE], ((0, 0), (0, 0), (0, M_HEAD_PAD - M_NOPE)))
    w_uk_p = w_uk_p.reshape(M_KV_LORA, M_HEADS * M_HEAD_PAD).astype(BF16)
    w_uv = w_ukv[:, :, M_NOPE:].reshape(M_KV_LORA, M_WIDTH).astype(BF16)

    freqs = ROPE_THETA ** (-jnp.arange(ROPE_HALF, dtype=F32) / ROPE_HALF)
    freq_row = jnp.zeros((1, LANES), F32)
    freq_row = freq_row.at[0, ROPE_LO:ROPE_MID].set(freqs).at[0, ROPE_MID:ROPE_HI].set(freqs)

    row3 = lambda b, i: (b, i, 0)
    head4 = lambda b, i: (b, 0, i, 0)
    return pl.pallas_call(
        _input_kernel,
        grid=(batch, seq // tm),
        in_specs=[
            pl.BlockSpec((1, tm, D_MODEL), row3),
            pl.BlockSpec((1, N_MOD, D_MODEL), lambda b, i: (b, 0, 0)),
            _resident((1, D_MODEL)),
            pl.BlockSpec((1, tm, 1), row3),
            _resident((1, LANES)),
            _resident(w_a.shape), _resident(w_g.shape), _resident(w_cq.shape), _resident(w_ckvr.shape),
            _resident((1, M_Q_LORA)), _resident(w_uq_p.shape),
            _resident((1, M_KV_LORA)), _resident(w_uk_p.shape), _resident(w_uv.shape),
        ],
        out_specs=[
            pl.BlockSpec((1, A_TILES, tm, LANES), head4),
            pl.BlockSpec((1, tm, 2 * D_MODEL), row3),
            pl.BlockSpec((1, M_HEADS, tm, M_HEAD_PAD), head4),
            pl.BlockSpec((1, M_HEADS, tm, M_HEAD_PAD), head4),
            pl.BlockSpec((1, M_PAIRS, 1, LANES, tm), lambda b, i: (b, 0, i, 0, 0)),
        ],
        out_shape=[
            jax.ShapeDtypeStruct((batch, A_TILES, seq, LANES), F32),
            jax.ShapeDtypeStruct((batch, seq, 2 * D_MODEL), BF16),
            jax.ShapeDtypeStruct((batch, M_HEADS, seq, M_HEAD_PAD), BF16),
            jax.ShapeDtypeStruct((batch, M_HEADS, seq, M_HEAD_PAD), BF16),
            jax.ShapeDtypeStruct((batch, M_PAIRS, seq // tm, LANES, tm), BF16),
        ],
        compiler_params=_params(2),
        name="input_stage",
    )(x, mod, g_mix.reshape(1, D_MODEL), positions.reshape(batch, seq, 1), freq_row,
      w_a, w_g, w_cq, w_ckvr, g_q_lora.reshape(1, M_Q_LORA), w_uq_p,
      g_kv_lora.reshape(1, M_KV_LORA), w_uk_p, w_uv)


def _mla_kernel(q_ref, k_ref, vt_ref, o_ref, m_scr, l_scr, acc_scr, sa_scr, sb_scr):
    t = MLA_TILE
    qc = MLA_QUERY_CHUNK
    qi = pl.program_id(2)
    key_pos = lax.broadcasted_iota(jnp.int32, (t, qc), 0)
    query_pos = lax.broadcasted_iota(jnp.int32, (t, qc), 1)
    contract_last = (((1,), (1,)), ((), ()))
    chains = [(hh, qh * qc) for hh in range(2) for qh in range(t // qc)]

    m_scr[...] = jnp.full(m_scr.shape, NEG_INF, F32)
    l_scr[...] = jnp.zeros(l_scr.shape, F32)
    acc_scr[...] = jnp.zeros(acc_scr.shape, F32)

    def scores(tile, s_scr):
        start = pl.multiple_of(tile * t, t)
        for hh in range(2):
            k = k_ref[0, hh, pl.ds(start, t), :]
            s_scr[hh] = lax.dot_general(k, q_ref[0, hh], contract_last, preferred_element_type=F32)

    def absorb(tile, s_scr, diagonal):
        vt = vt_ref[0, 0, tile]
        state = [(m_scr[hh, :, q0:q0 + qc], l_scr[hh, :, q0:q0 + qc], acc_scr[hh, :, q0:q0 + qc])
                 for hh, q0 in chains]
        for (hh, q0), (m_prev, l_prev, acc_prev) in zip(chains, state):
            s = s_scr[hh, :, q0:q0 + qc]
            if diagonal:
                s = jnp.where(key_pos <= query_pos + q0, s, NEG_INF)
            m_new = jnp.maximum(m_prev, jnp.max(s, axis=0, keepdims=True))
            alpha = jnp.exp2(m_prev - m_new)
            p = jnp.exp2(s - m_new)
            l_scr[hh, :, q0:q0 + qc] = alpha * l_prev + jnp.sum(p, axis=0, keepdims=True)
            acc_scr[hh, :, q0:q0 + qc] = alpha * acc_prev + _bdot(vt, p.astype(BF16))
            m_scr[hh, :, q0:q0 + qc] = m_new

    last_off = jnp.maximum(qi - 1, 0)
    scores(qi, sa_scr)
    scores(0, sb_scr)
    absorb(qi, sa_scr, True)

    def pair(j, carry):
        first = 2 * j
        scores(jnp.minimum(first + 1, last_off), sa_scr)
        absorb(first, sb_scr, False)

        @pl.when(first + 1 < qi)
        def _second():
            scores(jnp.minimum(first + 2, last_off), sb_scr)
            absorb(first + 1, sa_scr, False)

        return carry

    lax.fori_loop(0, (qi + 1) // 2, pair, 0)

    v_row = lax.broadcasted_iota(jnp.int32, (LANES, t), 0)
    o0 = acc_scr[0] * (1.0 / l_scr[0])
    o1 = acc_scr[1] * (1.0 / l_scr[1])
    o_ref[0] = jnp.where(v_row < M_V, o0, o1).T.astype(BF16)


def _mla_attention(q, k, vt):
    batch, _, seq, _ = q.shape
    t = MLA_TILE
    return pl.pallas_call(
        _mla_kernel,
        grid=(batch, M_PAIRS, seq // t),
        in_specs=[
            pl.BlockSpec((1, 2, t, M_HEAD_PAD), lambda b, p, i: (b, p, i, 0)),
            pl.BlockSpec((1, 2, seq, M_HEAD_PAD), lambda b, p, i: (b, p, 0, 0)),
            pl.BlockSpec((1, 1, seq // t, LANES, t), lambda b, p, i: (b, p, 0, 0, 0)),
        ],
        out_specs=pl.BlockSpec((1, t, LANES), lambda b, p, i: (b, i, p)),
        out_shape=jax.ShapeDtypeStruct((batch, seq, M_WIDTH), BF16),
        scratch_shapes=[
            pltpu.VMEM((2, 1, t), F32),
            pltpu.VMEM((2, 1, t), F32),
            pltpu.VMEM((2, LANES, t), F32),
            pltpu.VMEM((2, t, t), F32),
            pltpu.VMEM((2, t, t), F32),
        ],
        compiler_params=_params(3),
        name="mla_attention",
    )(q, k, vt)


def _t5_bucket_table(dilation, n_back):
    blk = BAND_BLOCK
    sub_dist = (np.arange(blk)[:, None] + blk) - np.arange(2 * blk)[None, :]
    dist = np.clip(sub_dist, 0, n_back) * dilation
    max_exact = REL_BUCKETS // 2
    d = np.maximum(dist, 1).astype(np.float32)
    ratio = np.log(d / np.float32(max_exact)) / np.float32(math.log(REL_MAX_DIST / max_exact))
    log_b = max_exact + (ratio * np.float32(REL_BUCKETS - max_exact)).astype(np.int32)
    log_b = np.minimum(log_b, REL_BUCKETS - 1)
    return np.where(dist < max_exact, dist, log_b).astype(np.int32)


def _rows(start, dilation):
    if dilation == 1:
        return pl.ds(start, BAND_BLOCK)
    return pl.ds(start, BAND_BLOCK, stride=dilation)


def _dilated_kernel(rb_ref, bucket_ref, q_ref, kc_ref, kp_ref, vc_ref, vp_ref, o_ref,
                    bias_scr, acc_scr, m_scr):
    blk = BAND_BLOCK
    sup = SUPER_BLOCK
    pair = pl.program_id(2)
    first_step = (pl.program_id(0) == 0) & (pl.program_id(1) == 0) & (pair == 0)

    @pl.when(first_step)
    def _build_bias():
        row = lax.broadcasted_iota(jnp.int32, (blk, 2 * blk), 0)
        col = lax.broadcasted_iota(jnp.int32, (blk, 2 * blk), 1)
        sub_dist = row + blk - col
        in_band = (sub_dist >= 0) & (sub_dist <= blk)
        for g in range(len(DILATED_PATTERNS)):
            bucket = bucket_ref[g]
            for hd in range(A_HEADS):
                bias = jnp.zeros((blk, 2 * blk), F32)
                for bk in range(REL_BUCKETS):
                    bias = jnp.where(bucket == bk, rb_ref[hd, bk], bias)
                bias_scr[g, hd] = jnp.where(in_band, bias, NEG_INF)

    first_valid_col = jnp.where(pl.program_id(1) > 0, 0, blk)
    col = lax.broadcasted_iota(jnp.int32, (blk, 2 * blk), 1)
    lane = lax.broadcasted_iota(jnp.int32, (blk, LANES), 1)
    lane2 = lax.broadcasted_iota(jnp.int32, (2 * blk, LANES), 1)
    contract_last = (((1,), (1,)), ((), ()))
    n_pat = len(DILATED_PATTERNS)

    for g, (_, dil) in enumerate(DILATED_PATTERNS):
        for res in range(dil):
            for n in range(sup // (blk * dil)):
                start = res + blk * dil * n
                rows = _rows(start, dil)
                q = q_ref[0, 0, rows, :]
                if n == 0:
                    prev_rows = _rows(sup - blk * dil + res, dil)
                    k_prev, v_prev = kp_ref[0, 0, prev_rows, :], vp_ref[0, 0, prev_rows, :]
                else:
                    prev_rows = _rows(start - blk * dil, dil)
                    k_prev, v_prev = kc_ref[0, 0, prev_rows, :], vc_ref[0, 0, prev_rows, :]
                k2 = jnp.concatenate([k_prev, kc_ref[0, 0, rows, :]], axis=0).astype(BF16)
                v2 = jnp.concatenate([v_prev, vc_ref[0, 0, rows, :]], axis=0)
                for hh in range(2):
                    in_head = (lane < A_HEAD_DIM) if hh == 0 else (lane >= A_HEAD_DIM)
                    in_head2 = (lane2 < A_HEAD_DIM) if hh == 0 else (lane2 >= A_HEAD_DIM)
                    s = lax.dot_general(jnp.where(in_head, q, 0.0).astype(BF16), k2, contract_last,
                                        preferred_element_type=F32)
                    bias = bias_scr[g, 2 * pair + hh]
                    if n == 0:
                        bias = jnp.where(col >= first_valid_col, bias, NEG_INF)
                    s = s + bias
                    m_blk = jnp.max(s, axis=-1, keepdims=True)
                    p = jnp.exp(s - m_blk).astype(BF16)
                    acc_scr[g, hh, rows, :] = _bdot(p, jnp.where(in_head2, v2, 1.0).astype(BF16))
                    m_scr[g, hh, rows, :] = jnp.broadcast_to(m_blk, (blk, LANES))

    halves = []
    for hh in range(2):
        top = m_scr[0, hh]
        for g in range(1, n_pat):
            top = jnp.maximum(top, m_scr[g, hh])
        total = jnp.exp(m_scr[0, hh] - top) * acc_scr[0, hh]
        for g in range(1, n_pat):
            total = total + jnp.exp(m_scr[g, hh] - top) * acc_scr[g, hh]
        halves.append(total * (1.0 / pltpu.roll(total, A_HEAD_DIM, 1)))
    lane_sup = lax.broadcasted_iota(jnp.int32, (sup, LANES), 1)
    o_ref[0] = jnp.where(lane_sup < A_HEAD_DIM, halves[0], halves[1]).astype(BF16)


def _dilated_attention(a_qkv, rel_bias):
    batch, _, seq, _ = a_qkv.shape
    blk = BAND_BLOCK
    sup = SUPER_BLOCK
    n_pat = len(DILATED_PATTERNS)
    assert all(w // d == blk for w, d in DILATED_PATTERNS), "band of exactly one block behind the query"
    assert seq % sup == 0
    bucket = jnp.asarray(np.stack([_t5_bucket_table(d, w // d) for w, d in DILATED_PATTERNS]))

    def part(which, prev):
        def index(b, s, p):
            return (b, which * A_PAIRS + p, jnp.maximum(s - 1, 0) if prev else s, 0)
        return pl.BlockSpec((1, 1, sup, LANES), index)

    return pl.pallas_call(
        _dilated_kernel,
        grid=(batch, seq // sup, A_PAIRS),
        in_specs=[
            pl.BlockSpec(memory_space=pltpu.SMEM),
            pl.BlockSpec((n_pat, blk, 2 * blk), lambda b, s, p: (0, 0, 0)),
            part(0, False), part(1, False), part(1, True), part(2, False), part(2, True),
        ],
        out_specs=pl.BlockSpec((1, sup, LANES), lambda b, s, p: (b, s, p)),
        out_shape=jax.ShapeDtypeStruct((batch, seq, A_WIDTH), BF16),
        scratch_shapes=[
            pltpu.VMEM((n_pat, A_HEADS, blk, 2 * blk), F32),
            pltpu.VMEM((n_pat, 2, sup, LANES), F32),
            pltpu.VMEM((n_pat, 2, sup, LANES), F32),
        ],
        compiler_params=_params(3),
        name="dilated_attention",
    )(rel_bias, bucket, a_qkv, a_qkv, a_qkv, a_qkv, a_qkv)


def _merge_kernel(x_ref, mod_ref, oa_ref, ob_ref, gate_ref, wa_ref, wb_ref, wo_ref, out_ref):
    y_a = _bdot(oa_ref[0], wa_ref[...])
    y_b = _bdot(ob_ref[0], wb_ref[...])
    gates = gate_ref[0].astype(F32)
    merged = _sigmoid(gates[:, :D_MODEL]) * y_a + _sigmoid(gates[:, D_MODEL:]) * y_b
    mixed = _bdot(merged.astype(BF16), wo_ref[...])
    out_ref[0] = x_ref[0] + mod_ref[0, 2:3, :] * mixed


def _merge_stage(x, mod, o_a, o_b, gates, w_up_a, w_up_b, w_o):
    batch, seq, _ = x.shape
    tm = ROW_TILE
    row3 = lambda b, i: (b, i, 0)
    half = pl.BlockSpec((1, tm, A_WIDTH), row3)
    return pl.pallas_call(
        _merge_kernel,
        grid=(batch, seq // tm),
        in_specs=[
            pl.BlockSpec((1, tm, D_MODEL), row3),
            pl.BlockSpec((1, N_MOD, D_MODEL), lambda b, i: (b, 0, 0)),
            half, half,
            pl.BlockSpec((1, tm, 2 * D_MODEL), row3),
            _resident((A_WIDTH, D_MODEL)), _resident((M_WIDTH, D_MODEL)), _resident((D_MODEL, D_MODEL)),
        ],
        out_specs=pl.BlockSpec((1, tm, D_MODEL), row3),
        out_shape=jax.ShapeDtypeStruct((batch, seq, D_MODEL), F32),
        compiler_params=_params(2),
        name="merge_stage",
    )(x, mod, o_a, o_b, gates,
      w_up_a.astype(BF16), w_up_b.astype(BF16), w_o.astype(BF16))


def _ffn_kernel(x_ref, mod_ref, g_ref, gf_ref, wg_ref, wu_ref, wd_ref, out_ref):
    x = x_ref[0]
    h = (_rms(x) * g_ref[...]) * (1.0 + mod_ref[0, 4:5, :]) + mod_ref[0, 3:4, :]
    hb = h.astype(BF16)
    gate = _bdot(hb, wg_ref[...])
    up = _bdot(hb, wu_ref[...])
    act = (gate * _sigmoid(gate) * up).astype(BF16)
    y = x + mod_ref[0, 5:6, :] * _bdot(act, wd_ref[...])
    out_ref[0] = _rms(y) * gf_ref[...]


def _ffn_stage(x, mod, g_ffn, g_final, w_gate, w_up, w_down):
    batch, seq, _ = x.shape
    tm = ROW_TILE
    row3 = lambda b, i: (b, i, 0)
    return pl.pallas_call(
        _ffn_kernel,
        grid=(batch, seq // tm),
        in_specs=[
            pl.BlockSpec((1, tm, D_MODEL), row3),
            pl.BlockSpec((1, N_MOD, D_MODEL), lambda b, i: (b, 0, 0)),
            _resident((1, D_MODEL)), _resident((1, D_MODEL)),
            _resident((D_MODEL, D_FF)), _resident((D_MODEL, D_FF)), _resident((D_FF, D_MODEL)),
        ],
        out_specs=pl.BlockSpec((1, tm, D_MODEL), row3),
        out_shape=jax.ShapeDtypeStruct((batch, seq, D_MODEL), F32),
        compiler_params=_params(2),
        name="ffn_stage",
    )(x, mod, g_ffn.reshape(1, D_MODEL), g_final.reshape(1, D_MODEL),
      w_gate.astype(BF16), w_up.astype(BF16), w_down.astype(BF16))


def kernel(x, c, positions, rel_bias, w_ada, b_ada, g_mix, w_in, g_q_lora, w_uq, g_kv_lora, w_ukv,
           w_up_a, w_up_b, w_o, g_ffn, w_gate, w_up, w_down, g_final):
    assert w_ada.shape[0] == 1, "single-layer trunk"
    mod = _modulation(c, w_ada[0], b_ada[0])
    a_qkv, gates, q, k, vt = _input_stage(x, mod, g_mix[0], positions, w_in[0], g_q_lora[0], w_uq[0],
                                         g_kv_lora[0], w_ukv[0])
    o_b = _mla_attention(q, k, vt)
    o_a = _dilated_attention(a_qkv, rel_bias)
    x1 = _merge_stage(x, mod, o_a, o_b, gates, w_up_a[0], w_up_b[0], w_o[0])
    return _ffn_stage(x1, mod, g_ffn[0], g_final, w_gate[0], w_up[0], w_down[0])
```

```python
import functools
import math

import jax
import jax.numpy as jnp
import numpy as np
from jax import lax
from jax.experimental import pallas as pl
from jax.experimental.pallas import tpu as pltpu

D_MODEL = 1024
A_HEADS = 8
A_HEAD_DIM = 64
A_WIDTH = A_HEADS * A_HEAD_DIM
DILATED_PATTERNS = ((128, 1), (512, 4), (2048, 16))
BAND_BLOCK = 128
REL_BUCKETS = 32
REL_MAX_DIST = 2048
M_HEADS = 8
M_NOPE = 64
M_ROPE = 32
M_V = 64
M_Q_LORA = 768
M_KV_LORA = 256
M_WIDTH = M_HEADS * M_V
ROPE_THETA = 10000.0
D_FF = -(-8 * D_MODEL // (3 * 256)) * 256
N_MOD = 6
EPS = 1e-6
NEG_INF = -1e30

LANES = 128
SUBLANES = 8
V7X_VMEM_BYTES = 64 * 1024 * 1024
VMEM_LIMIT_BYTES = V7X_VMEM_BYTES - 8 * 1024 * 1024

M_HEAD_PAD = LANES
M_PAIRS = M_HEADS // 2
ROPE_HALF = M_ROPE // 2
ROPE_LO = M_NOPE
ROPE_MID = M_NOPE + ROPE_HALF
ROPE_HI = M_NOPE + M_ROPE

A_PAIRS = A_HEADS // 2
A_TILES = 3 * A_PAIRS
SUPER_BLOCK = BAND_BLOCK * max(d for _, d in DILATED_PATTERNS)

ROW_TILE = 512
MLA_KEY_TILE = 512
MLA_QUERY_TILE = 2 * MLA_KEY_TILE
MLA_QUERY_CHUNK = 256

F32 = jnp.float32
BF16 = jnp.bfloat16


def _params(n_axes, flags=None):
    return pltpu.CompilerParams(
        dimension_semantics=("arbitrary",) * n_axes,
        vmem_limit_bytes=VMEM_LIMIT_BYTES,
        flags=flags,
    )


def _resident(shape):
    zeros = (0,) * len(shape)
    return pl.BlockSpec(shape, lambda *_: zeros, pipeline_mode=pl.Buffered(1))


def _bdot(a, b):
    return jnp.dot(a, b, preferred_element_type=F32)


def _rms(x):
    return x * lax.rsqrt(jnp.mean(x * x, axis=-1, keepdims=True) + EPS)


def _sigmoid(x):
    return 1.0 / (1.0 + jnp.exp(-x))


def _mod_kernel(c_ref, w_ref, b_ref, o_ref):
    c = c_ref[...]
    cond = c * _sigmoid(c)
    o_ref[...] = (
        jnp.dot(cond, w_ref[...], preferred_element_type=F32, precision=lax.Precision.HIGHEST)
        + b_ref[...]
    )


def _modulation(c, w_ada, b_ada):
    batch = c.shape[0]
    rows = -(-batch // SUBLANES) * SUBLANES
    c_pad = jnp.pad(c, ((0, rows - batch), (0, 0)))
    out = pl.pallas_call(
        _mod_kernel,
        grid=(N_MOD,),
        in_specs=[
            pl.BlockSpec((rows, D_MODEL), lambda j: (0, 0)),
            pl.BlockSpec((D_MODEL, D_MODEL), lambda j: (0, j)),
            pl.BlockSpec((1, D_MODEL), lambda j: (0, j)),
        ],
        out_specs=pl.BlockSpec((rows, D_MODEL), lambda j: (0, j)),
        out_shape=jax.ShapeDtypeStruct((rows, N_MOD * D_MODEL), F32),
        compiler_params=_params(1),
        name="adaln_mod",
    )(c_pad, w_ada, b_ada.reshape(1, N_MOD * D_MODEL))
    return out[:batch].reshape(batch, N_MOD, D_MODEL)


def _rope_lanes(x, cos, sin_lo, sin_hi):
    return x * cos + pltpu.roll(x, LANES - ROPE_HALF, 1) * sin_lo + pltpu.roll(x, ROPE_HALF, 1) * sin_hi


def _input_kernel(x_ref, mod_ref, g_ref, pos_ref, freq_ref, wa_ref, wg_ref, wcq_ref, wckvr_ref,
                  gq_ref, wuq_ref, gkv_ref, wuk_ref, wuv_ref,
                  a_ref, gate_ref, q_ref, k_ref, v_ref):
    x = x_ref[0]
    shift = mod_ref[0, 0:1, :]
    scale = mod_ref[0, 1:2, :]
    h = (_rms(x) * g_ref[...]) * (1.0 + scale) + shift
    hb = h.astype(BF16)

    a_all = _bdot(hb, wa_ref[...])
    for j in range(A_TILES):
        a_ref[0, j] = a_all[:, j * LANES:(j + 1) * LANES]
    gate_ref[0] = _bdot(hb, wg_ref[...]).astype(BF16)

    ang = pos_ref[0].astype(F32) * freq_ref[...]
    cos = jnp.cos(ang)
    sin = jnp.sin(ang)
    lane = lax.broadcasted_iota(jnp.int32, ang.shape, 1)
    sin_lo = jnp.where((lane >= ROPE_LO) & (lane < ROPE_MID), -sin, 0.0)
    sin_hi = jnp.where((lane >= ROPE_MID) & (lane < ROPE_HI), sin, 0.0)

    c_q = _bdot(hb, wcq_ref[...])
    q_all = _bdot((_rms(c_q) * gq_ref[...]).astype(BF16), wuq_ref[...])
    q_scale = (M_NOPE + M_ROPE) ** -0.5 * math.log2(math.e)
    for hd in range(M_HEADS):
        q_h = q_all[:, hd * M_HEAD_PAD:(hd + 1) * M_HEAD_PAD]
        q_ref[0, hd] = (_rope_lanes(q_h, cos, sin_lo, sin_hi) * q_scale).astype(BF16)

    ckvr = _bdot(hb, wckvr_ref[...])
    c_kv = (_rms(ckvr[:, :M_KV_LORA]) * gkv_ref[...]).astype(BF16)
    k_rope = _rope_lanes(ckvr[:, M_KV_LORA:], cos, sin_lo, sin_hi)
    k_all = _bdot(c_kv, wuk_ref[...])
    for hd in range(M_HEADS):
        k_ref[0, hd] = (k_all[:, hd * M_HEAD_PAD:(hd + 1) * M_HEAD_PAD] + k_rope).astype(BF16)
    v_all = _bdot(c_kv, wuv_ref[...])
    for pr in range(M_PAIRS):
        v_ref[0, pr, 0] = v_all[:, pr * LANES:(pr + 1) * LANES].T.astype(BF16)


def _input_stage(x, mod, g_mix, positions, w_in, g_q_lora, w_uq, g_kv_lora, w_ukv):
    batch, seq, _ = x.shape
    tm = MLA_KEY_TILE
    s0 = 3 * A_WIDTH
    s1 = s0 + M_Q_LORA
    s2 = s1 + M_KV_LORA
    s3 = s2 + M_ROPE
    w_a = jnp.concatenate([w_in[:, :A_WIDTH] * (A_HEAD_DIM ** -0.5), w_in[:, A_WIDTH:s0]], axis=1).astype(BF16)
    w_cq = w_in[:, s0:s1].astype(BF16)
    zeros = functools.partial(jnp.zeros, dtype=w_in.dtype)
    w_ckvr = jnp.concatenate(
        [w_in[:, s1:s2], zeros((D_MODEL, ROPE_LO)), w_in[:, s2:s3], zeros((D_MODEL, LANES - ROPE_HI))],
        axis=1).astype(BF16)
    w_g = w_in[:, s3:].astype(BF16)
    w_uq_p = jnp.pad(w_uq, ((0, 0), (0, 0), (0, M_HEAD_PAD - M_NOPE - M_ROPE)))
    w_uq_p = w_uq_p.reshape(M_Q_LORA, M_HEADS * M_HEAD_PAD).astype(BF16)
    w_uk_p = jnp.pad(w_ukv[:, :, :M_NOPE], ((0, 0), (0, 0), (0, M_HEAD_PAD - M_NOPE)))
    w_uk_p = w_uk_p.reshape(M_KV_LORA, M_HEADS * M_HEAD_PAD).astype(BF16)
    w_uv = w_ukv[:, :, M_NOPE:].reshape(M_KV_LORA, M_WIDTH).astype(BF16)

    freqs = ROPE_THETA ** (-jnp.arange(ROPE_HALF, dtype=F32) / ROPE_HALF)
    freq_row = jnp.zeros((1, LANES), F32)
    freq_row = freq_row.at[0, ROPE_LO:ROPE_MID].set(freqs).at[0, ROPE_MID:ROPE_HI].set(freqs)

    row3 = lambda b, i: (b, i, 0)
    head4 = lambda b, i: (b, 0, i, 0)
    return pl.pallas_call(
        _input_kernel,
        grid=(batch, seq // tm),
        in_specs=[
            pl.BlockSpec((1, tm, D_MODEL), row3),
            pl.BlockSpec((1, N_MOD, D_MODEL), lambda b, i: (b, 0, 0)),
            _resident((1, D_MODEL)),
            pl.BlockSpec((1, tm, 1), row3),
            _resident((1, LANES)),
            _resident(w_a.shape), _resident(w_g.shape), _resident(w_cq.shape), _resident(w_ckvr.shape),
            _resident((1, M_Q_LORA)), _resident(w_uq_p.shape),
            _resident((1, M_KV_LORA)), _resident(w_uk_p.shape), _resident(w_uv.shape),
        ],
        out_specs=[
            pl.BlockSpec((1, A_TILES, tm, LANES), head4),
            pl.BlockSpec((1, tm, 2 * D_MODEL), row3),
            pl.BlockSpec((1, M_HEADS, tm, M_HEAD_PAD), head4),
            pl.BlockSpec((1, M_HEADS, tm, M_HEAD_PAD), head4),
            pl.BlockSpec((1, M_PAIRS, 1, LANES, tm), lambda b, i: (b, 0, i, 0, 0)),
        ],
        out_shape=[
            jax.ShapeDtypeStruct((batch, A_TILES, seq, LANES), F32),
            jax.ShapeDtypeStruct((batch, seq, 2 * D_MODEL), BF16),
            jax.ShapeDtypeStruct((batch, M_HEADS, seq, M_HEAD_PAD), BF16),
            jax.ShapeDtypeStruct((batch, M_HEADS, seq, M_HEAD_PAD), BF16),
            jax.ShapeDtypeStruct((batch, M_PAIRS, seq // tm, LANES, tm), BF16),
        ],
        compiler_params=_params(2),
        name="input_stage",
    )(x, mod, g_mix.reshape(1, D_MODEL), positions.reshape(batch, seq, 1), freq_row,
      w_a, w_g, w_cq, w_ckvr, g_q_lora.reshape(1, M_Q_LORA), w_uq_p,
      g_kv_lora.reshape(1, M_KV_LORA), w_uk_p, w_uv)


def _mla_kernel(q_ref, k_ref, vt_ref, o_ref, m_scr, acc_scr, sa_scr, sb_scr, max_a_scr, max_b_scr):
    tq, tk = MLA_QUERY_TILE, MLA_KEY_TILE
    qc = MLA_QUERY_CHUNK
    qi = pl.program_id(2)
    contract_last = (((1,), (1,)), ((), ()))
    v_row = lax.broadcasted_iota(jnp.int32, (LANES, tk), 0)
    own_rows = [v_row < M_V, v_row >= M_V]

    m_scr[...] = jnp.full(m_scr.shape, NEG_INF, F32)
    acc_scr[...] = jnp.zeros(acc_scr.shape, F32)

    def scores(tile, s_scr, max_scr, diagonal=None):
        start = pl.multiple_of(tile * tk, tk)
        q_lo = 0 if diagonal is None else diagonal
        for hh in range(2):
            k = k_ref[0, hh, pl.ds(start, tk), :]
            s = lax.dot_general(k, q_ref[0, hh, q_lo:, :], contract_last, preferred_element_type=F32)
            if diagonal is not None:
                key_pos = lax.broadcasted_iota(jnp.int32, s.shape, 0)
                query_pos = lax.broadcasted_iota(jnp.int32, s.shape, 1)
                s = jnp.where(key_pos <= query_pos, s, NEG_INF)
            s_scr[hh, :, q_lo:] = s
            max_scr[hh, :, q_lo:] = jnp.max(s, axis=0, keepdims=True)

    def absorb(tile, s_scr, max_scr, q_lo=0):
        vt = vt_ref[0, 0, tile]
        vt_aug = [jnp.where(own, vt, jnp.ones_like(vt)) for own in own_rows]
        chains = [(hh, q0) for hh in range(2) for q0 in range(q_lo, tq, qc)]
        state = [(m_scr[hh, :, q0:q0 + qc], acc_scr[hh, :, q0:q0 + qc]) for hh, q0 in chains]
        for (hh, q0), (m_prev, acc_prev) in zip(chains, state):
            m_new = jnp.maximum(m_prev, max_scr[hh, :, q0:q0 + qc])
            p = jnp.exp2(s_scr[hh, :, q0:q0 + qc] - m_new).astype(BF16)
            acc_scr[hh, :, q0:q0 + qc] = jnp.exp2(m_prev - m_new) * acc_prev + _bdot(vt_aug[hh], p)
            m_scr[hh, :, q0:q0 + qc] = m_new

    n_off = 2 * qi
    last_off = jnp.maximum(n_off - 1, 0)
    scores(n_off, sa_scr, max_a_scr, diagonal=0)
    scores(n_off + 1, sb_scr, max_b_scr, diagonal=tk)
    absorb(n_off, sa_scr, max_a_scr)
    scores(0, sa_scr, max_a_scr)
    absorb(n_off + 1, sb_scr, max_b_scr, q_lo=tk)

    def pair(j, carry):
        first = 2 * j
        scores(first + 1, sb_scr, max_b_scr)
        absorb(first, sa_scr, max_a_scr)
        scores(jnp.minimum(first + 2, last_off), sa_scr, max_a_scr)
        absorb(first + 1, sb_scr, max_b_scr)
        return carry

    lax.fori_loop(0, qi, pair, 0)

    o0 = acc_scr[0] * (1.0 / acc_scr[0, M_V:M_V + 1, :])
    o1 = acc_scr[1] * (1.0 / acc_scr[1, 0:1, :])
    out_row = lax.broadcasted_iota(jnp.int32, (LANES, tq), 0)
    o_ref[0] = jnp.where(out_row < M_V, o0, o1).T.astype(BF16)


def _mla_attention(q, k, vt):
    batch, _, seq, _ = q.shape
    tq, tk = MLA_QUERY_TILE, MLA_KEY_TILE
    assert tq == 2 * tk and seq % tq == 0
    return pl.pallas_call(
        _mla_kernel,
        grid=(batch, M_PAIRS, seq // tq),
        in_specs=[
            pl.BlockSpec((1, 2, tq, M_HEAD_PAD), lambda b, p, i: (b, p, i, 0)),
            pl.BlockSpec((1, 2, seq, M_HEAD_PAD), lambda b, p, i: (b, p, 0, 0)),
            pl.BlockSpec((1, 1, seq // tk, LANES, tk), lambda b, p, i: (b, p, 0, 0, 0)),
        ],
        out_specs=pl.BlockSpec((1, tq, LANES), lambda b, p, i: (b, i, p)),
        out_shape=jax.ShapeDtypeStruct((batch, seq, M_WIDTH), BF16),
        scratch_shapes=[
            pltpu.VMEM((2, 1, tq), F32),
            pltpu.VMEM((2, LANES, tq), F32),
            pltpu.VMEM((2, tk, tq), F32),
            pltpu.VMEM((2, tk, tq), F32),
            pltpu.VMEM((2, 1, tq), F32),
            pltpu.VMEM((2, 1, tq), F32),
        ],
        compiler_params=_params(3),
        name="mla_attention",
    )(q, k, vt)


def _t5_bucket_table(dilation, n_back):
    blk = BAND_BLOCK
    sub_dist = (np.arange(blk)[:, None] + blk) - np.arange(2 * blk)[None, :]
    dist = np.clip(sub_dist, 0, n_back) * dilation
    max_exact = REL_BUCKETS // 2
    d = np.maximum(dist, 1).astype(np.float32)
    ratio = np.log(d / np.float32(max_exact)) / np.float32(math.log(REL_MAX_DIST / max_exact))
    log_b = max_exact + (ratio * np.float32(REL_BUCKETS - max_exact)).astype(np.int32)
    log_b = np.minimum(log_b, REL_BUCKETS - 1)
    return np.where(dist < max_exact, dist, log_b).astype(np.int32)


def _rows(start, dilation):
    if dilation == 1:
        return pl.ds(start, BAND_BLOCK)
    return pl.ds(start, BAND_BLOCK, stride=dilation)


def _dilated_kernel(rb_ref, bucket_ref, q_ref, kc_ref, kp_ref, vc_ref, vp_ref, o_ref,
                    bias_scr, acc_scr, m_scr):
    blk = BAND_BLOCK
    sup = SUPER_BLOCK
    pair = pl.program_id(2)
    first_step = (pl.program_id(0) == 0) & (pl.program_id(1) == 0) & (pair == 0)

    @pl.when(first_step)
    def _build_bias():
        row = lax.broadcasted_iota(jnp.int32, (blk, 2 * blk), 0)
        col = lax.broadcasted_iota(jnp.int32, (blk, 2 * blk), 1)
        sub_dist = row + blk - col
        in_band = (sub_dist >= 0) & (sub_dist <= blk)
        for g in range(len(DILATED_PATTERNS)):
            bucket = bucket_ref[g]
            for hd in range(A_HEADS):
                bias = jnp.zeros((blk, 2 * blk), F32)
                for bk in range(REL_BUCKETS):
                    bias = jnp.where(bucket == bk, rb_ref[hd, bk], bias)
                bias_scr[g, hd] = jnp.where(in_band, bias, NEG_INF)

    first_valid_col = jnp.where(pl.program_id(1) > 0, 0, blk)
    col = lax.broadcasted_iota(jnp.int32, (blk, 2 * blk), 1)
    lane = lax.broadcasted_iota(jnp.int32, (blk, LANES), 1)
    lane2 = lax.broadcasted_iota(jnp.int32, (2 * blk, LANES), 1)
    contract_last = (((1,), (1,)), ((), ()))
    n_pat = len(DILATED_PATTERNS)

    for g, (_, dil) in enumerate(DILATED_PATTERNS):
        for res in range(dil):
            for n in range(sup // (blk * dil)):
                start = res + blk * dil * n
                rows = _rows(start, dil)
                q = q_ref[0, 0, rows, :]
                if n == 0:
                    prev_rows = _rows(sup - blk * dil + res, dil)
                    k_prev, v_prev = kp_ref[0, 0, prev_rows, :], vp_ref[0, 0, prev_rows, :]
                else:
                    prev_rows = _rows(start - blk * dil, dil)
                    k_prev, v_prev = kc_ref[0, 0, prev_rows, :], vc_ref[0, 0, prev_rows, :]
                k2 = jnp.concatenate([k_prev, kc_ref[0, 0, rows, :]], axis=0).astype(BF16)
                v2 = jnp.concatenate([v_prev, vc_ref[0, 0, rows, :]], axis=0)
                for hh in range(2):
                    in_head = (lane < A_HEAD_DIM) if hh == 0 else (lane >= A_HEAD_DIM)
                    in_head2 = (lane2 < A_HEAD_DIM) if hh == 0 else (lane2 >= A_HEAD_DIM)
                    s = lax.dot_general(jnp.where(in_head, q, 0.0).astype(BF16), k2, contract_last,
                                        preferred_element_type=F32)
                    bias = bias_scr[g, 2 * pair + hh]
                    if n == 0:
                        bias = jnp.where(col >= first_valid_col, bias, NEG_INF)
                    s = s + bias
                    m_blk = jnp.max(s, axis=-1, keepdims=True)
                    p = jnp.exp(s - m_blk).astype(BF16)
                    acc_scr[g, hh, rows, :] = _bdot(p, jnp.where(in_head2, v2, 1.0).astype(BF16))
                    m_scr[g, hh, rows, :] = jnp.broadcast_to(m_blk, (blk, LANES))

    halves = []
    for hh in range(2):
        top = m_scr[0, hh]
        for g in range(1, n_pat):
            top = jnp.maximum(top, m_scr[g, hh])
        total = jnp.exp(m_scr[0, hh] - top) * acc_scr[0, hh]
        for g in range(1, n_pat):
            total = total + jnp.exp(m_scr[g, hh] - top) * acc_scr[g, hh]
        halves.append(total * (1.0 / pltpu.roll(total, A_HEAD_DIM, 1)))
    lane_sup = lax.broadcasted_iota(jnp.int32, (sup, LANES), 1)
    o_ref[0] = jnp.where(lane_sup < A_HEAD_DIM, halves[0], halves[1]).astype(BF16)


def _dilated_attention(a_qkv, rel_bias):
    batch, _, seq, _ = a_qkv.shape
    blk = BAND_BLOCK
    sup = SUPER_BLOCK
    n_pat = len(DILATED_PATTERNS)
    assert all(w // d == blk for w, d in DILATED_PATTERNS), "band of exactly one block behind the query"
    assert seq % sup == 0
    bucket = jnp.asarray(np.stack([_t5_bucket_table(d, w // d) for w, d in DILATED_PATTERNS]))

    def part(which, prev):
        def index(b, s, p):
            return (b, which * A_PAIRS + p, jnp.maximum(s - 1, 0) if prev else s, 0)
        return pl.BlockSpec((1, 1, sup, LANES), index)

    return pl.pallas_call(
        _dilated_kernel,
        grid=(batch, seq // sup, A_PAIRS),
        in_specs=[
            pl.BlockSpec(memory_space=pltpu.SMEM),
            pl.BlockSpec((n_pat, blk, 2 * blk), lambda b, s, p: (0, 0, 0)),
            part(0, False), part(1, False), part(1, True), part(2, False), part(2, True),
        ],
        out_specs=pl.BlockSpec((1, sup, LANES), lambda b, s, p: (b, s, p)),
        out_shape=jax.ShapeDtypeStruct((batch, seq, A_WIDTH), BF16),
        scratch_shapes=[
            pltpu.VMEM((n_pat, A_HEADS, blk, 2 * blk), F32),
            pltpu.VMEM((n_pat, 2, sup, LANES), F32),
            pltpu.VMEM((n_pat, 2, sup, LANES), F32),
        ],
        compiler_params=_params(3),
        name="dilated_attention",
    )(rel_bias, bucket, a_qkv, a_qkv, a_qkv, a_qkv, a_qkv)


def _merge_kernel(x_ref, mod_ref, oa_ref, ob_ref, gate_ref, wa_ref, wb_ref, wo_ref, out_ref):
    y_a = _bdot(oa_ref[0], wa_ref[...])
    y_b = _bdot(ob_ref[0], wb_ref[...])
    gates = gate_ref[0].astype(F32)
    merged = _sigmoid(gates[:, :D_MODEL]) * y_a + _sigmoid(gates[:, D_MODEL:]) * y_b
    mixed = _bdot(merged.astype(BF16), wo_ref[...])
    out_ref[0] = x_ref[0] + mod_ref[0, 2:3, :] * mixed


def _merge_stage(x, mod, o_a, o_b, gates, w_up_a, w_up_b, w_o):
    batch, seq, _ = x.shape
    tm = ROW_TILE
    row3 = lambda b, i: (b, i, 0)
    half = pl.BlockSpec((1, tm, A_WIDTH), row3)
    return pl.pallas_call(
        _merge_kernel,
        grid=(batch, seq // tm),
        in_specs=[
            pl.BlockSpec((1, tm, D_MODEL), row3),
            pl.BlockSpec((1, N_MOD, D_MODEL), lambda b, i: (b, 0, 0)),
            half, half,
            pl.BlockSpec((1, tm, 2 * D_MODEL), row3),
            _resident((A_WIDTH, D_MODEL)), _resident((M_WIDTH, D_MODEL)), _resident((D_MODEL, D_MODEL)),
        ],
        out_specs=pl.BlockSpec((1, tm, D_MODEL), row3),
        out_shape=jax.ShapeDtypeStruct((batch, seq, D_MODEL), F32),
        compiler_params=_params(2),
        name="merge_stage",
    )(x, mod, o_a, o_b, gates,
      w_up_a.astype(BF16), w_up_b.astype(BF16), w_o.astype(BF16))


def _ffn_kernel(x_ref, mod_ref, g_ref, gf_ref, wg_ref, wu_ref, wd_ref, out_ref):
    x = x_ref[0]
    h = (_rms(x) * g_ref[...]) * (1.0 + mod_ref[0, 4:5, :]) + mod_ref[0, 3:4, :]
    hb = h.astype(BF16)
    gate = _bdot(hb, wg_ref[...])
    up = _bdot(hb, wu_ref[...])
    act = (gate * _sigmoid(gate) * up).astype(BF16)
    y = x + mod_ref[0, 5:6, :] * _bdot(act, wd_ref[...])
    out_ref[0] = _rms(y) * gf_ref[...]


def _ffn_stage(x, mod, g_ffn, g_final, w_gate, w_up, w_down):
    batch, seq, _ = x.shape
    tm = ROW_TILE
    row3 = lambda b, i: (b, i, 0)
    return pl.pallas_call(
        _ffn_kernel,
        grid=(batch, seq // tm),
        in_specs=[
            pl.BlockSpec((1, tm, D_MODEL), row3),
            pl.BlockSpec((1, N_MOD, D_MODEL), lambda b, i: (b, 0, 0)),
            _resident((1, D_MODEL)), _resident((1, D_MODEL)),
            _resident((D_MODEL, D_FF)), _resident((D_MODEL, D_FF)), _resident((D_FF, D_MODEL)),
        ],
        out_specs=pl.BlockSpec((1, tm, D_MODEL), row3),
        out_shape=jax.ShapeDtypeStruct((batch, seq, D_MODEL), F32),
        compiler_params=_params(2),
        name="ffn_stage",
    )(x, mod, g_ffn.reshape(1, D_MODEL), g_final.reshape(1, D_MODEL),
      w_gate.astype(BF16), w_up.astype(BF16), w_down.astype(BF16))


def kernel(x, c, positions, rel_bias, w_ada, b_ada, g_mix, w_in, g_q_lora, w_uq, g_kv_lora, w_ukv,
           w_up_a, w_up_b, w_o, g_ffn, w_gate, w_up, w_down, g_final):
    assert w_ada.shape[0] == 1, "single-layer trunk"
    mod = _modulation(c, w_ada[0], b_ada[0])
    a_qkv, gates, q, k, vt = _input_stage(x, mod, g_mix[0], positions, w_in[0], g_q_lora[0], w_uq[0],
                                         g_kv_lora[0], w_ukv[0])
    o_b = _mla_attention(q, k, vt)
    o_a = _dilated_attention(a_qkv, rel_bias)
    x1 = _merge_stage(x, mod, o_a, o_b, gates, w_up_a[0], w_up_b[0], w_o[0])
    return _ffn_stage(x1, mod, g_ffn[0], g_final, w_gate[0], w_up[0], w_down[0])
```

```python
import functools
import math

import jax
import jax.numpy as jnp
import numpy as np
from jax import lax
from jax.experimental import pallas as pl
from jax.experimental.pallas import tpu as pltpu

D_MODEL = 1024
A_HEADS = 8
A_HEAD_DIM = 64
A_WIDTH = A_HEADS * A_HEAD_DIM
DILATED_PATTERNS = ((128, 1), (512, 4), (2048, 16))
BAND_BLOCK = 128
REL_BUCKETS = 32
REL_MAX_DIST = 2048
M_HEADS = 8
M_NOPE = 64
M_ROPE = 32
M_V = 64
M_Q_LORA = 768
M_KV_LORA = 256
M_WIDTH = M_HEADS * M_V
ROPE_THETA = 10000.0
D_FF = -(-8 * D_MODEL // (3 * 256)) * 256
N_MOD = 6
EPS = 1e-6
NEG_INF = -1e30

LANES = 128
SUBLANES = 8
V7X_VMEM_BYTES = 64 * 1024 * 1024
VMEM_LIMIT_BYTES = V7X_VMEM_BYTES - 8 * 1024 * 1024

M_HEAD_PAD = LANES
M_PAIRS = M_HEADS // 2
ROPE_HALF = M_ROPE // 2
ROPE_LO = M_NOPE
ROPE_MID = M_NOPE + ROPE_HALF
ROPE_HI = M_NOPE + M_ROPE

A_PAIRS = A_HEADS // 2
A_TILES = 3 * A_PAIRS
SUPER_BLOCK = BAND_BLOCK * max(d for _, d in DILATED_PATTERNS)
DILATED_REGROUP = 4
LOG2_E = math.log2(math.e)

ROW_TILE = 512
MLA_KEY_TILE = 512
MLA_QUERY_TILE = 2 * MLA_KEY_TILE
MLA_QUERY_CHUNK = 256

F32 = jnp.float32
BF16 = jnp.bfloat16


def _params(n_axes, flags=None):
    return pltpu.CompilerParams(
        dimension_semantics=("arbitrary",) * n_axes,
        vmem_limit_bytes=VMEM_LIMIT_BYTES,
        flags=flags,
    )


def _resident(shape):
    zeros = (0,) * len(shape)
    return pl.BlockSpec(shape, lambda *_: zeros, pipeline_mode=pl.Buffered(1))


def _bdot(a, b):
    return jnp.dot(a, b, preferred_element_type=F32)


def _rms(x):
    return x * lax.rsqrt(jnp.mean(x * x, axis=-1, keepdims=True) + EPS)


def _sigmoid(x):
    return 1.0 / (1.0 + jnp.exp(-x))


def _mod_kernel(c_ref, w_ref, b_ref, o_ref):
    c = c_ref[...]
    cond = c * _sigmoid(c)
    o_ref[...] = (
        jnp.dot(cond, w_ref[...], preferred_element_type=F32, precision=lax.Precision.HIGHEST)
        + b_ref[...]
    )


def _modulation(c, w_ada, b_ada):
    batch = c.shape[0]
    rows = -(-batch // SUBLANES) * SUBLANES
    c_pad = jnp.pad(c, ((0, rows - batch), (0, 0)))
    out = pl.pallas_call(
        _mod_kernel,
        grid=(N_MOD,),
        in_specs=[
            pl.BlockSpec((rows, D_MODEL), lambda j: (0, 0)),
            pl.BlockSpec((D_MODEL, D_MODEL), lambda j: (0, j)),
            pl.BlockSpec((1, D_MODEL), lambda j: (0, j)),
        ],
        out_specs=pl.BlockSpec((rows, D_MODEL), lambda j: (0, j)),
        out_shape=jax.ShapeDtypeStruct((rows, N_MOD * D_MODEL), F32),
        compiler_params=_params(1),
        name="adaln_mod",
    )(c_pad, w_ada, b_ada.reshape(1, N_MOD * D_MODEL))
    return out[:batch].reshape(batch, N_MOD, D_MODEL)


def _rope_lanes(x, cos, sin_lo, sin_hi):
    return x * cos + pltpu.roll(x, LANES - ROPE_HALF, 1) * sin_lo + pltpu.roll(x, ROPE_HALF, 1) * sin_hi


def _input_kernel(x_ref, mod_ref, g_ref, pos_ref, freq_ref, wa_ref, wg_ref, wcq_ref, wckvr_ref,
                  gq_ref, wuq_ref, gkv_ref, wuk_ref, wuv_ref,
                  a_ref, gate_ref, q_ref, k_ref, v_ref):
    x = x_ref[0]
    shift = mod_ref[0, 0:1, :]
    scale = mod_ref[0, 1:2, :]
    h = (_rms(x) * g_ref[...]) * (1.0 + scale) + shift
    hb = h.astype(BF16)

    a_all = _bdot(hb, wa_ref[...])
    for j in range(A_TILES):
        a_ref[0, j] = a_all[:, j * LANES:(j + 1) * LANES]
    gate_ref[0] = _bdot(hb, wg_ref[...]).astype(BF16)

    ang = pos_ref[0].astype(F32) * freq_ref[...]
    cos = jnp.cos(ang)
    sin = jnp.sin(ang)
    lane = lax.broadcasted_iota(jnp.int32, ang.shape, 1)
    sin_lo = jnp.where((lane >= ROPE_LO) & (lane < ROPE_MID), -sin, 0.0)
    sin_hi = jnp.where((lane >= ROPE_MID) & (lane < ROPE_HI), sin, 0.0)

    c_q = _bdot(hb, wcq_ref[...])
    q_all = _bdot((_rms(c_q) * gq_ref[...]).astype(BF16), wuq_ref[...])
    q_scale = (M_NOPE + M_ROPE) ** -0.5 * LOG2_E
    for hd in range(M_HEADS):
        q_h = q_all[:, hd * M_HEAD_PAD:(hd + 1) * M_HEAD_PAD]
        q_ref[0, hd] = (_rope_lanes(q_h, cos, sin_lo, sin_hi) * q_scale).astype(BF16)

    ckvr = _bdot(hb, wckvr_ref[...])
    c_kv = (_rms(ckvr[:, :M_KV_LORA]) * gkv_ref[...]).astype(BF16)
    k_rope = _rope_lanes(ckvr[:, M_KV_LORA:], cos, sin_lo, sin_hi)
    k_all = _bdot(c_kv, wuk_ref[...])
    for hd in range(M_HEADS):
        k_ref[0, hd] = (k_all[:, hd * M_HEAD_PAD:(hd + 1) * M_HEAD_PAD] + k_rope).astype(BF16)
    v_all = _bdot(c_kv, wuv_ref[...])
    for pr in range(M_PAIRS):
        v_ref[0, pr, 0] = v_all[:, pr * LANES:(pr + 1) * LANES].T.astype(BF16)


def _input_stage(x, mod, g_mix, positions, w_in, g_q_lora, w_uq, g_kv_lora, w_ukv):
    batch, seq, _ = x.shape
    tm = MLA_KEY_TILE
    s0 = 3 * A_WIDTH
    s1 = s0 + M_Q_LORA
    s2 = s1 + M_KV_LORA
    s3 = s2 + M_ROPE
    w_a = jnp.concatenate([w_in[:, :A_WIDTH] * (A_HEAD_DIM ** -0.5 * LOG2_E), w_in[:, A_WIDTH:s0]],
                          axis=1).astype(BF16)
    w_cq = w_in[:, s0:s1].astype(BF16)
    zeros = functools.partial(jnp.zeros, dtype=w_in.dtype)
    w_ckvr = jnp.concatenate(
        [w_in[:, s1:s2], zeros((D_MODEL, ROPE_LO)), w_in[:, s2:s3], zeros((D_MODEL, LANES - ROPE_HI))],
        axis=1).astype(BF16)
    w_g = w_in[:, s3:].astype(BF16)
    w_uq_p = jnp.pad(w_uq, ((0, 0), (0, 0), (0, M_HEAD_PAD - M_NOPE - M_ROPE)))
    w_uq_p = w_uq_p.reshape(M_Q_LORA, M_HEADS * M_HEAD_PAD).astype(BF16)
    w_uk_p = jnp.pad(w_ukv[:, :, :M_NOPE], ((0, 0), (0, 0), (0, M_HEAD_PAD - M_NOPE)))
    w_uk_p = w_uk_p.reshape(M_KV_LORA, M_HEADS * M_HEAD_PAD).astype(BF16)
    w_uv = w_ukv[:, :, M_NOPE:].reshape(M_KV_LORA, M_WIDTH).astype(BF16)

    freqs = ROPE_THETA ** (-jnp.arange(ROPE_HALF, dtype=F32) / ROPE_HALF)
    freq_row = jnp.zeros((1, LANES), F32)
    freq_row = freq_row.at[0, ROPE_LO:ROPE_MID].set(freqs).at[0, ROPE_MID:ROPE_HI].set(freqs)

    row3 = lambda b, i: (b, i, 0)
    head4 = lambda b, i: (b, 0, i, 0)
    return pl.pallas_call(
        _input_kernel,
        grid=(batch, seq // tm),
        in_specs=[
            pl.BlockSpec((1, tm, D_MODEL), row3),
            pl.BlockSpec((1, N_MOD, D_MODEL), lambda b, i: (b, 0, 0)),
            _resident((1, D_MODEL)),
            pl.BlockSpec((1, tm, 1), row3),
            _resident((1, LANES)),
            _resident(w_a.shape), _resident(w_g.shape), _resident(w_cq.shape), _resident(w_ckvr.shape),
            _resident((1, M_Q_LORA)), _resident(w_uq_p.shape),
            _resident((1, M_KV_LORA)), _resident(w_uk_p.shape), _resident(w_uv.shape),
        ],
        out_specs=[
            pl.BlockSpec((1, A_TILES, tm, LANES), head4),
            pl.BlockSpec((1, tm, 2 * D_MODEL), row3),
            pl.BlockSpec((1, M_HEADS, tm, M_HEAD_PAD), head4),
            pl.BlockSpec((1, M_HEADS, tm, M_HEAD_PAD), head4),
            pl.BlockSpec((1, M_PAIRS, 1, LANES, tm), lambda b, i: (b, 0, i, 0, 0)),
        ],
        out_shape=[
            jax.ShapeDtypeStruct((batch, A_TILES, seq, LANES), F32),
            jax.ShapeDtypeStruct((batch, seq, 2 * D_MODEL), BF16),
            jax.ShapeDtypeStruct((batch, M_HEADS, seq, M_HEAD_PAD), BF16),
            jax.ShapeDtypeStruct((batch, M_HEADS, seq, M_HEAD_PAD), BF16),
            jax.ShapeDtypeStruct((batch, M_PAIRS, seq // tm, LANES, tm), BF16),
        ],
        compiler_params=_params(2),
        name="input_stage",
    )(x, mod, g_mix.reshape(1, D_MODEL), positions.reshape(batch, seq, 1), freq_row,
      w_a, w_g, w_cq, w_ckvr, g_q_lora.reshape(1, M_Q_LORA), w_uq_p,
      g_kv_lora.reshape(1, M_KV_LORA), w_uk_p, w_uv)


def _mla_kernel(q_ref, k_ref, vt_ref, o_ref, m_scr, acc_scr, sa_scr, sb_scr, max_a_scr, max_b_scr):
    tq, tk = MLA_QUERY_TILE, MLA_KEY_TILE
    qc = MLA_QUERY_CHUNK
    qi = pl.program_id(2)
    contract_last = (((1,), (1,)), ((), ()))
    v_row = lax.broadcasted_iota(jnp.int32, (LANES, tk), 0)
    own_rows = [v_row < M_V, v_row >= M_V]

    m_scr[...] = jnp.full(m_scr.shape, NEG_INF, F32)
    acc_scr[...] = jnp.zeros(acc_scr.shape, F32)

    def scores(tile, s_scr, max_scr, diagonal=None):
        start = pl.multiple_of(tile * tk, tk)
        q_lo = 0 if diagonal is None else diagonal
        for hh in range(2):
            k = k_ref[0, hh, pl.ds(start, tk), :]
            s = lax.dot_general(k, q_ref[0, hh, q_lo:, :], contract_last, preferred_element_type=F32)
            if diagonal is not None:
                key_pos = lax.broadcasted_iota(jnp.int32, s.shape, 0)
                query_pos = lax.broadcasted_iota(jnp.int32, s.shape, 1)
                s = jnp.where(key_pos <= query_pos, s, NEG_INF)
            s_scr[hh, :, q_lo:] = s
            max_scr[hh, :, q_lo:] = jnp.max(s, axis=0, keepdims=True)

    def absorb(tile, s_scr, max_scr, q_lo=0):
        vt = vt_ref[0, 0, tile]
        vt_aug = [jnp.where(own, vt, jnp.ones_like(vt)) for own in own_rows]
        chains = [(hh, q0) for hh in range(2) for q0 in range(q_lo, tq, qc)]
        state = [(m_scr[hh, :, q0:q0 + qc], acc_scr[hh, :, q0:q0 + qc]) for hh, q0 in chains]
        for (hh, q0), (m_prev, acc_prev) in zip(chains, state):
            m_new = jnp.maximum(m_prev, max_scr[hh, :, q0:q0 + qc])
            p = jnp.exp2(s_scr[hh, :, q0:q0 + qc] - m_new).astype(BF16)
            acc_scr[hh, :, q0:q0 + qc] = jnp.exp2(m_prev - m_new) * acc_prev + _bdot(vt_aug[hh], p)
            m_scr[hh, :, q0:q0 + qc] = m_new

    n_off = 2 * qi
    last_off = jnp.maximum(n_off - 1, 0)
    scores(n_off, sa_scr, max_a_scr, diagonal=0)
    scores(n_off + 1, sb_scr, max_b_scr, diagonal=tk)
    absorb(n_off, sa_scr, max_a_scr)
    scores(0, sa_scr, max_a_scr)
    absorb(n_off + 1, sb_scr, max_b_scr, q_lo=tk)

    def pair(j, carry):
        first = 2 * j
        scores(first + 1, sb_scr, max_b_scr)
        absorb(first, sa_scr, max_a_scr)
        scores(jnp.minimum(first + 2, last_off), sa_scr, max_a_scr)
        absorb(first + 1, sb_scr, max_b_scr)
        return carry

    lax.fori_loop(0, qi, pair, 0)

    o0 = acc_scr[0] * (1.0 / acc_scr[0, M_V:M_V + 1, :])
    o1 = acc_scr[1] * (1.0 / acc_scr[1, 0:1, :])
    out_row = lax.broadcasted_iota(jnp.int32, (LANES, tq), 0)
    o_ref[0] = jnp.where(out_row < M_V, o0, o1).T.astype(BF16)


def _mla_attention(q, k, vt):
    batch, _, seq, _ = q.shape
    tq, tk = MLA_QUERY_TILE, MLA_KEY_TILE
    assert tq == 2 * tk and seq % tq == 0
    return pl.pallas_call(
        _mla_kernel,
        grid=(batch, M_PAIRS, seq // tq),
        in_specs=[
            pl.BlockSpec((1, 2, tq, M_HEAD_PAD), lambda b, p, i: (b, p, i, 0)),
            pl.BlockSpec((1, 2, seq, M_HEAD_PAD), lambda b, p, i: (b, p, 0, 0)),
            pl.BlockSpec((1, 1, seq // tk, LANES, tk), lambda b, p, i: (b, p, 0, 0, 0)),
        ],
        out_specs=pl.BlockSpec((1, tq, LANES), lambda b, p, i: (b, i, p)),
        out_shape=jax.ShapeDtypeStruct((batch, seq, M_WIDTH), BF16),
        scratch_shapes=[
            pltpu.VMEM((2, 1, tq), F32),
            pltpu.VMEM((2, LANES, tq), F32),
            pltpu.VMEM((2, tk, tq), F32),
            pltpu.VMEM((2, tk, tq), F32),
            pltpu.VMEM((2, 1, tq), F32),
            pltpu.VMEM((2, 1, tq), F32),
        ],
        compiler_params=_params(3),
        name="mla_attention",
    )(q, k, vt)


def _t5_bucket_table(dilation, n_back):
    blk = BAND_BLOCK
    sub_dist = (np.arange(blk)[:, None] + blk) - np.arange(2 * blk)[None, :]
    dist = np.clip(sub_dist, 0, n_back) * dilation
    max_exact = REL_BUCKETS // 2
    d = np.maximum(dist, 1).astype(np.float32)
    ratio = np.log(d / np.float32(max_exact)) / np.float32(math.log(REL_MAX_DIST / max_exact))
    log_b = max_exact + (ratio * np.float32(REL_BUCKETS - max_exact)).astype(np.int32)
    log_b = np.minimum(log_b, REL_BUCKETS - 1)
    return np.where(dist < max_exact, dist, log_b).astype(np.int32)


def _rows(start, count, stride):
    return pl.ds(start, count) if stride == 1 else pl.ds(start, count, stride=stride)


def _dilated_kernel(rb_ref, bucket_ref, q_ref, kc_ref, kp_ref, vc_ref, vp_ref, o_ref,
                    bias_scr, acc_scr, m_scr, regroup_scr, out_scr):
    blk = BAND_BLOCK
    sup = SUPER_BLOCK
    grp = DILATED_REGROUP
    sub = sup // grp
    pair = pl.program_id(2)
    first_step = (pl.program_id(0) == 0) & (pl.program_id(1) == 0) & (pair == 0)

    @pl.when(first_step)
    def _build_bias():
        row = lax.broadcasted_iota(jnp.int32, (blk, 2 * blk), 0)
        col = lax.broadcasted_iota(jnp.int32, (blk, 2 * blk), 1)
        sub_dist = row + blk - col
        in_band = (sub_dist >= 0) & (sub_dist <= blk)
        for g in range(len(DILATED_PATTERNS)):
            bucket = bucket_ref[g]
            for hd in range(A_HEADS):
                bias = jnp.zeros((blk, 2 * blk), F32)
                for bk in range(REL_BUCKETS):
                    bias = jnp.where(bucket == bk, rb_ref[hd, bk] * LOG2_E, bias)
                bias_scr[g, hd] = jnp.where(in_band, bias, NEG_INF)

    sources = (q_ref, kc_ref, kp_ref, vc_ref, vp_ref)
    for idx, ref in enumerate(sources):
        for r in range(grp):
            regroup_scr[idx, r * sub:(r + 1) * sub, :] = ref[0, 0, _rows(r, sub, grp), :]

    first_valid_col = jnp.where(pl.program_id(1) > 0, 0, blk)
    col = lax.broadcasted_iota(jnp.int32, (blk, 2 * blk), 1)
    lane = lax.broadcasted_iota(jnp.int32, (blk, LANES), 1)
    lane2 = lax.broadcasted_iota(jnp.int32, (2 * blk, LANES), 1)
    contract_last = (((1,), (1,)), ((), ()))
    n_pat = len(DILATED_PATTERNS)

    for g, (_, dil) in enumerate(DILATED_PATTERNS):
        regrouped = dil % grp == 0
        step = dil // grp if regrouped else dil
        for res in range(dil):
            base = (res % grp) * sub + res // grp if regrouped else res
            prev_base = base + (sub if regrouped else sup) - blk * step
            for n in range(sup // (blk * dil)):
                rows = _rows(base + blk * step * n, blk, step)
                if n == 0:
                    prev_rows, k_idx, v_idx = _rows(prev_base, blk, step), 2, 4
                else:
                    prev_rows, k_idx, v_idx = _rows(base + blk * step * (n - 1), blk, step), 1, 3
                if regrouped:
                    load = lambda idx, r: regroup_scr[idx, r, :]
                else:
                    load = lambda idx, r: sources[idx][0, 0, r, :]
                q = load(0, rows)
                k2 = jnp.concatenate([load(k_idx, prev_rows), load(1, rows)], axis=0).astype(BF16)
                v2 = jnp.concatenate([load(v_idx, prev_rows), load(3, rows)], axis=0)
                for hh in range(2):
                    in_head = (lane < A_HEAD_DIM) if hh == 0 else (lane >= A_HEAD_DIM)
                    in_head2 = (lane2 < A_HEAD_DIM) if hh == 0 else (lane2 >= A_HEAD_DIM)
                    s = lax.dot_general(jnp.where(in_head, q, 0.0).astype(BF16), k2, contract_last,
                                        preferred_element_type=F32)
                    bias = bias_scr[g, 2 * pair + hh]
                    if n == 0:
                        bias = jnp.where(col >= first_valid_col, bias, NEG_INF)
                    s = s + bias
                    m_blk = jnp.max(s, axis=-1, keepdims=True)
                    p = jnp.exp2(s - m_blk).astype(BF16)
                    acc_scr[g, hh, rows, :] = _bdot(p, jnp.where(in_head2, v2, 1.0).astype(BF16))
                    m_scr[g, hh, rows, :] = jnp.broadcast_to(m_blk, (blk, LANES))

    lane_sub = lax.broadcasted_iota(jnp.int32, (sub, LANES), 1)
    for r in range(grp):
        chunk = [pl.ds(r * sub, sub) if dil % grp == 0 else _rows(r, sub, grp) for _, dil in DILATED_PATTERNS]
        halves = []
        for hh in range(2):
            maxes = [m_scr[g, hh, chunk[g], :] for g in range(n_pat)]
            top = maxes[0]
            for g in range(1, n_pat):
                top = jnp.maximum(top, maxes[g])
            total = jnp.exp2(maxes[0] - top) * acc_scr[0, hh, chunk[0], :]
            for g in range(1, n_pat):
                total = total + jnp.exp2(maxes[g] - top) * acc_scr[g, hh, chunk[g], :]
            halves.append(total * (1.0 / pltpu.roll(total, A_HEAD_DIM, 1)))
        out_scr[_rows(r, sub, grp), :] = jnp.where(lane_sub < A_HEAD_DIM, halves[0], halves[1])
    o_ref[0] = out_scr[...].astype(BF16)


def _dilated_attention(a_qkv, rel_bias):
    batch, _, seq, _ = a_qkv.shape
    blk = BAND_BLOCK
    sup = SUPER_BLOCK
    n_pat = len(DILATED_PATTERNS)
    assert all(w // d == blk for w, d in DILATED_PATTERNS), "band of exactly one block behind the query"
    assert seq % sup == 0
    bucket = jnp.asarray(np.stack([_t5_bucket_table(d, w // d) for w, d in DILATED_PATTERNS]))

    def part(which, prev):
        def index(b, s, p):
            return (b, which * A_PAIRS + p, jnp.maximum(s - 1, 0) if prev else s, 0)
        return pl.BlockSpec((1, 1, sup, LANES), index)

    return pl.pallas_call(
        _dilated_kernel,
        grid=(batch, seq // sup, A_PAIRS),
        in_specs=[
            pl.BlockSpec(memory_space=pltpu.SMEM),
            pl.BlockSpec((n_pat, blk, 2 * blk), lambda b, s, p: (0, 0, 0)),
            part(0, False), part(1, False), part(1, True), part(2, False), part(2, True),
        ],
        out_specs=pl.BlockSpec((1, sup, LANES), lambda b, s, p: (b, s, p)),
        out_shape=jax.ShapeDtypeStruct((batch, seq, A_WIDTH), BF16),
        scratch_shapes=[
            pltpu.VMEM((n_pat, A_HEADS, blk, 2 * blk), F32),
            pltpu.VMEM((n_pat, 2, sup, LANES), F32),
            pltpu.VMEM((n_pat, 2, sup, LANES), F32),
            pltpu.VMEM((5, sup, LANES), F32),
            pltpu.VMEM((sup, LANES), F32),
        ],
        compiler_params=_params(3),
        name="dilated_attention",
    )(rel_bias, bucket, a_qkv, a_qkv, a_qkv, a_qkv, a_qkv)


def _merge_kernel(x_ref, mod_ref, oa_ref, ob_ref, gate_ref, wa_ref, wb_ref, wo_ref, out_ref):
    y_a = _bdot(oa_ref[0], wa_ref[...])
    y_b = _bdot(ob_ref[0], wb_ref[...])
    gates = gate_ref[0].astype(F32)
    merged = _sigmoid(gates[:, :D_MODEL]) * y_a + _sigmoid(gates[:, D_MODEL:]) * y_b
    mixed = _bdot(merged.astype(BF16), wo_ref[...])
    out_ref[0] = x_ref[0] + mod_ref[0, 2:3, :] * mixed


def _merge_stage(x, mod, o_a, o_b, gates, w_up_a, w_up_b, w_o):
    batch, seq, _ = x.shape
    tm = ROW_TILE
    row3 = lambda b, i: (b, i, 0)
    half = pl.BlockSpec((1, tm, A_WIDTH), row3)
    return pl.pallas_call(
        _merge_kernel,
        grid=(batch, seq // tm),
        in_specs=[
            pl.BlockSpec((1, tm, D_MODEL), row3),
            pl.BlockSpec((1, N_MOD, D_MODEL), lambda b, i: (b, 0, 0)),
            half, half,
            pl.BlockSpec((1, tm, 2 * D_MODEL), row3),
            _resident((A_WIDTH, D_MODEL)), _resident((M_WIDTH, D_MODEL)), _resident((D_MODEL, D_MODEL)),
        ],
        out_specs=pl.BlockSpec((1, tm, D_MODEL), row3),
        out_shape=jax.ShapeDtypeStruct((batch, seq, D_MODEL), F32),
        compiler_params=_params(2),
        name="merge_stage",
    )(x, mod, o_a, o_b, gates,
      w_up_a.astype(BF16), w_up_b.astype(BF16), w_o.astype(BF16))


def _ffn_kernel(x_ref, mod_ref, g_ref, gf_ref, wg_ref, wu_ref, wd_ref, out_ref):
    x = x_ref[0]
    h = (_rms(x) * g_ref[...]) * (1.0 + mod_ref[0, 4:5, :]) + mod_ref[0, 3:4, :]
    hb = h.astype(BF16)
    gate = _bdot(hb, wg_ref[...])
    up = _bdot(hb, wu_ref[...])
    act = (gate * _sigmoid(gate) * up).astype(BF16)
    y = x + mod_ref[0, 5:6, :] * _bdot(act, wd_ref[...])
    out_ref[0] = _rms(y) * gf_ref[...]


def _ffn_stage(x, mod, g_ffn, g_final, w_gate, w_up, w_down):
    batch, seq, _ = x.shape
    tm = ROW_TILE
    row3 = lambda b, i: (b, i, 0)
    return pl.pallas_call(
        _ffn_kernel,
        grid=(batch, seq // tm),
        in_specs=[
            pl.BlockSpec((1, tm, D_MODEL), row3),
            pl.BlockSpec((1, N_MOD, D_MODEL), lambda b, i: (b, 0, 0)),
            _resident((1, D_MODEL)), _resident((1, D_MODEL)),
            _resident((D_MODEL, D_FF)), _resident((D_MODEL, D_FF)), _resident((D_FF, D_MODEL)),
        ],
        out_specs=pl.BlockSpec((1, tm, D_MODEL), row3),
        out_shape=jax.ShapeDtypeStruct((batch, seq, D_MODEL), F32),
        compiler_params=_params(2),
        name="ffn_stage",
    )(x, mod, g_ffn.reshape(1, D_MODEL), g_final.reshape(1, D_MODEL),
      w_gate.astype(BF16), w_up.astype(BF16), w_down.astype(BF16))


def kernel(x, c, positions, rel_bias, w_ada, b_ada, g_mix, w_in, g_q_lora, w_uq, g_kv_lora, w_ukv,
           w_up_a, w_up_b, w_o, g_ffn, w_gate, w_up, w_down, g_final):
    assert w_ada.shape[0] == 1, "single-layer trunk"
    mod = _modulation(c, w_ada[0], b_ada[0])
    a_qkv, gates, q, k, vt = _input_stage(x, mod, g_mix[0], positions, w_in[0], g_q_lora[0], w_uq[0],
                                         g_kv_lora[0], w_ukv[0])
    o_b = _mla_attention(q, k, vt)
    o_a = _dilated_attention(a_qkv, rel_bias)
    x1 = _merge_stage(x, mod, o_a, o_b, gates, w_up_a[0], w_up_b[0], w_o[0])
    return _ffn_stage(x1, mod, g_ffn[0], g_final, w_gate[0], w_up[0], w_down[0])
```

```python
import functools
import math

import jax
import jax.numpy as jnp
import numpy as np
from jax import lax
from jax.experimental import pallas as pl
from jax.experimental.pallas import tpu as pltpu

D_MODEL = 1024
A_HEADS = 8
A_HEAD_DIM = 64
A_WIDTH = A_HEADS * A_HEAD_DIM
DILATED_PATTERNS = ((128, 1), (512, 4), (2048, 16))
BAND_BLOCK = 128
REL_BUCKETS = 32
REL_MAX_DIST = 2048
M_HEADS = 8
M_NOPE = 64
M_ROPE = 32
M_V = 64
M_Q_LORA = 768
M_KV_LORA = 256
M_WIDTH = M_HEADS * M_V
ROPE_THETA = 10000.0
D_FF = -(-8 * D_MODEL // (3 * 256)) * 256
N_MOD = 6
EPS = 1e-6
NEG_INF = -1e30

LANES = 128
SUBLANES = 8
V7X_VMEM_BYTES = 64 * 1024 * 1024
VMEM_LIMIT_BYTES = V7X_VMEM_BYTES - 8 * 1024 * 1024

M_HEAD_PAD = LANES
M_PAIRS = M_HEADS // 2
ROPE_HALF = M_ROPE // 2
ROPE_LO = M_NOPE
ROPE_MID = M_NOPE + ROPE_HALF
ROPE_HI = M_NOPE + M_ROPE

A_PAIRS = A_HEADS // 2
A_TILES = 3 * A_PAIRS
SUPER_BLOCK = BAND_BLOCK * max(d for _, d in DILATED_PATTERNS)
DILATED_REGROUP = 4
LOG2_E = math.log2(math.e)

ROW_TILE = 512
MLA_KEY_TILE = 512
MLA_QUERY_TILE = 2 * MLA_KEY_TILE
MLA_QUERY_CHUNK = 256

F32 = jnp.float32
BF16 = jnp.bfloat16


def _params(n_axes, flags=None):
    return pltpu.CompilerParams(
        dimension_semantics=("arbitrary",) * n_axes,
        vmem_limit_bytes=VMEM_LIMIT_BYTES,
        flags=flags,
    )


def _resident(shape):
    zeros = (0,) * len(shape)
    return pl.BlockSpec(shape, lambda *_: zeros, pipeline_mode=pl.Buffered(1))


def _bdot(a, b):
    return jnp.dot(a, b, preferred_element_type=F32)


def _rms(x):
    return x * lax.rsqrt(jnp.mean(x * x, axis=-1, keepdims=True) + EPS)


def _sigmoid(x):
    return 1.0 / (1.0 + jnp.exp(-x))


def _mod_kernel(c_ref, w_ref, b_ref, o_ref):
    c = c_ref[...]
    cond = c * _sigmoid(c)
    o_ref[...] = (
        jnp.dot(cond, w_ref[...], preferred_element_type=F32, precision=lax.Precision.HIGHEST)
        + b_ref[...]
    )


def _modulation(c, w_ada, b_ada):
    batch = c.shape[0]
    rows = -(-batch // SUBLANES) * SUBLANES
    c_pad = jnp.pad(c, ((0, rows - batch), (0, 0)))
    out = pl.pallas_call(
        _mod_kernel,
        grid=(N_MOD,),
        in_specs=[
            pl.BlockSpec((rows, D_MODEL), lambda j: (0, 0)),
            pl.BlockSpec((D_MODEL, D_MODEL), lambda j: (0, j)),
            pl.BlockSpec((1, D_MODEL), lambda j: (0, j)),
        ],
        out_specs=pl.BlockSpec((rows, D_MODEL), lambda j: (0, j)),
        out_shape=jax.ShapeDtypeStruct((rows, N_MOD * D_MODEL), F32),
        compiler_params=_params(1),
        name="adaln_mod",
    )(c_pad, w_ada, b_ada.reshape(1, N_MOD * D_MODEL))
    return out[:batch].reshape(batch, N_MOD, D_MODEL)


def _rope_lanes(x, cos, signed_sin):
    lane = lax.broadcasted_iota(jnp.int32, x.shape, 1)
    partner = jnp.where(lane < ROPE_MID, pltpu.roll(x, LANES - ROPE_HALF, 1), pltpu.roll(x, ROPE_HALF, 1))
    return x * cos + partner * signed_sin


def _input_kernel(x_ref, mod_ref, g_ref, pos_ref, freq_ref, wa_ref, wg_ref, wcq_ref, wckvr_ref,
                  gq_ref, wuq_ref, gkv_ref, wuk_ref, wuv_ref,
                  a_ref, gate_ref, q_ref, k_ref, v_ref):
    x = x_ref[0]
    shift = mod_ref[0, 0:1, :]
    scale = mod_ref[0, 1:2, :]
    h = (_rms(x) * g_ref[...]) * (1.0 + scale) + shift
    hb = h.astype(BF16)

    c_q = _bdot(hb, wcq_ref[...])
    ckvr = _bdot(hb, wckvr_ref[...])

    ang = freq_ref[...] * pos_ref[0, 0].astype(F32)
    cos_r, sin_r = jnp.cos(ang), jnp.sin(ang)
    rows = ang.shape[1]
    cos = jnp.concatenate(
        [jnp.ones((ROPE_LO, rows), F32), cos_r, jnp.ones((LANES - ROPE_HI, rows), F32)], axis=0).T
    signed_sin = jnp.concatenate(
        [jnp.zeros((ROPE_LO, rows), F32), -sin_r[:ROPE_HALF], sin_r[ROPE_HALF:],
         jnp.zeros((LANES - ROPE_HI, rows), F32)], axis=0).T

    q_all = _bdot((_rms(c_q) * gq_ref[...]).astype(BF16), wuq_ref[...])
    q_scale = (M_NOPE + M_ROPE) ** -0.5 * LOG2_E
    for hd in range(M_HEADS):
        q_h = q_all[:, hd * M_HEAD_PAD:(hd + 1) * M_HEAD_PAD]
        q_ref[0, hd] = (_rope_lanes(q_h, cos, signed_sin) * q_scale).astype(BF16)

    c_kv = (_rms(ckvr[:, :M_KV_LORA]) * gkv_ref[...]).astype(BF16)
    k_rope = _rope_lanes(ckvr[:, M_KV_LORA:], cos, signed_sin)
    k_all = _bdot(c_kv, wuk_ref[...])
    for hd in range(M_HEADS):
        k_ref[0, hd] = (k_all[:, hd * M_HEAD_PAD:(hd + 1) * M_HEAD_PAD] + k_rope).astype(BF16)
    v_all = _bdot(c_kv, wuv_ref[...])
    for pr in range(M_PAIRS):
        v_ref[0, pr, 0] = v_all[:, pr * LANES:(pr + 1) * LANES].T.astype(BF16)

    a_all = _bdot(hb, wa_ref[...])
    for j in range(A_TILES):
        a_ref[0, j] = a_all[:, j * LANES:(j + 1) * LANES]
    gate_ref[0] = _bdot(hb, wg_ref[...]).astype(BF16)


def _input_stage(x, mod, g_mix, positions, w_in, g_q_lora, w_uq, g_kv_lora, w_ukv):
    batch, seq, _ = x.shape
    tm = MLA_KEY_TILE
    s0 = 3 * A_WIDTH
    s1 = s0 + M_Q_LORA
    s2 = s1 + M_KV_LORA
    s3 = s2 + M_ROPE
    w_a = jnp.concatenate([w_in[:, :A_WIDTH] * (A_HEAD_DIM ** -0.5 * LOG2_E), w_in[:, A_WIDTH:s0]],
                          axis=1).astype(BF16)
    w_cq = w_in[:, s0:s1].astype(BF16)
    zeros = functools.partial(jnp.zeros, dtype=w_in.dtype)
    w_ckvr = jnp.concatenate(
        [w_in[:, s1:s2], zeros((D_MODEL, ROPE_LO)), w_in[:, s2:s3], zeros((D_MODEL, LANES - ROPE_HI))],
        axis=1).astype(BF16)
    w_g = w_in[:, s3:].astype(BF16)
    w_uq_p = jnp.pad(w_uq, ((0, 0), (0, 0), (0, M_HEAD_PAD - M_NOPE - M_ROPE)))
    w_uq_p = w_uq_p.reshape(M_Q_LORA, M_HEADS * M_HEAD_PAD).astype(BF16)
    w_uk_p = jnp.pad(w_ukv[:, :, :M_NOPE], ((0, 0), (0, 0), (0, M_HEAD_PAD - M_NOPE)))
    w_uk_p = w_uk_p.reshape(M_KV_LORA, M_HEADS * M_HEAD_PAD).astype(BF16)
    w_uv = w_ukv[:, :, M_NOPE:].reshape(M_KV_LORA, M_WIDTH).astype(BF16)

    freqs = ROPE_THETA ** (-jnp.arange(ROPE_HALF, dtype=F32) / ROPE_HALF)
    freq_col = jnp.concatenate([freqs, freqs]).reshape(M_ROPE, 1)

    row3 = lambda b, i: (b, i, 0)
    head4 = lambda b, i: (b, 0, i, 0)
    return pl.pallas_call(
        _input_kernel,
        grid=(batch, seq // tm),
        in_specs=[
            pl.BlockSpec((1, tm, D_MODEL), row3),
            pl.BlockSpec((1, N_MOD, D_MODEL), lambda b, i: (b, 0, 0)),
            _resident((1, D_MODEL)),
            pl.BlockSpec((1, 1, 1, tm), lambda b, i: (b, i, 0, 0)),
            _resident((M_ROPE, 1)),
            _resident(w_a.shape), _resident(w_g.shape), _resident(w_cq.shape), _resident(w_ckvr.shape),
            _resident((1, M_Q_LORA)), _resident(w_uq_p.shape),
            _resident((1, M_KV_LORA)), _resident(w_uk_p.shape), _resident(w_uv.shape),
        ],
        out_specs=[
            pl.BlockSpec((1, A_TILES, tm, LANES), head4),
            pl.BlockSpec((1, tm, 2 * D_MODEL), row3),
            pl.BlockSpec((1, M_HEADS, tm, M_HEAD_PAD), head4),
            pl.BlockSpec((1, M_HEADS, tm, M_HEAD_PAD), head4),
            pl.BlockSpec((1, M_PAIRS, 1, LANES, tm), lambda b, i: (b, 0, i, 0, 0)),
        ],
        out_shape=[
            jax.ShapeDtypeStruct((batch, A_TILES, seq, LANES), F32),
            jax.ShapeDtypeStruct((batch, seq, 2 * D_MODEL), BF16),
            jax.ShapeDtypeStruct((batch, M_HEADS, seq, M_HEAD_PAD), BF16),
            jax.ShapeDtypeStruct((batch, M_HEADS, seq, M_HEAD_PAD), BF16),
            jax.ShapeDtypeStruct((batch, M_PAIRS, seq // tm, LANES, tm), BF16),
        ],
        compiler_params=_params(2),
        name="input_stage",
    )(x, mod, g_mix.reshape(1, D_MODEL), positions.reshape(batch, seq // tm, 1, tm), freq_col,
      w_a, w_g, w_cq, w_ckvr, g_q_lora.reshape(1, M_Q_LORA), w_uq_p,
      g_kv_lora.reshape(1, M_KV_LORA), w_uk_p, w_uv)


def _mla_kernel(q_ref, k_ref, vt_ref, o_ref, m_scr, acc_scr, sa_scr, sb_scr, max_a_scr, max_b_scr):
    tq, tk = MLA_QUERY_TILE, MLA_KEY_TILE
    qc = MLA_QUERY_CHUNK
    qi = pl.program_id(2)
    contract_last = (((1,), (1,)), ((), ()))
    v_row = lax.broadcasted_iota(jnp.int32, (LANES, tk), 0)
    own_rows = [v_row < M_V, v_row >= M_V]

    m_scr[...] = jnp.full(m_scr.shape, NEG_INF, F32)
    acc_scr[...] = jnp.zeros(acc_scr.shape, F32)

    def scores(tile, s_scr, max_scr, diagonal=None):
        start = pl.multiple_of(tile * tk, tk)
        q_lo = 0 if diagonal is None else diagonal
        for hh in range(2):
            k = k_ref[0, hh, pl.ds(start, tk), :]
            s = lax.dot_general(k, q_ref[0, hh, q_lo:, :], contract_last, preferred_element_type=F32)
            if diagonal is not None:
                key_pos = lax.broadcasted_iota(jnp.int32, s.shape, 0)
                query_pos = lax.broadcasted_iota(jnp.int32, s.shape, 1)
                s = jnp.where(key_pos <= query_pos, s, NEG_INF)
            s_scr[hh, :, q_lo:] = s
            max_scr[hh, :, q_lo:] = jnp.max(s, axis=0, keepdims=True)

    def absorb(tile, s_scr, max_scr, q_lo=0):
        vt = vt_ref[0, 0, tile]
        vt_aug = [jnp.where(own, vt, jnp.ones_like(vt)) for own in own_rows]
        chains = [(hh, q0) for hh in range(2) for q0 in range(q_lo, tq, qc)]
        state = [(m_scr[hh, :, q0:q0 + qc], acc_scr[hh, :, q0:q0 + qc]) for hh, q0 in chains]
        for (hh, q0), (m_prev, acc_prev) in zip(chains, state):
            m_new = jnp.maximum(m_prev, max_scr[hh, :, q0:q0 + qc])
            p = jnp.exp2(s_scr[hh, :, q0:q0 + qc] - m_new).astype(BF16)
            acc_scr[hh, :, q0:q0 + qc] = jnp.exp2(m_prev - m_new) * acc_prev + _bdot(vt_aug[hh], p)
            m_scr[hh, :, q0:q0 + qc] = m_new

    n_off = 2 * qi
    last_off = jnp.maximum(n_off - 1, 0)
    scores(n_off, sa_scr, max_a_scr, diagonal=0)
    scores(n_off + 1, sb_scr, max_b_scr, diagonal=tk)
    absorb(n_off, sa_scr, max_a_scr)
    scores(0, sa_scr, max_a_scr)
    absorb(n_off + 1, sb_scr, max_b_scr, q_lo=tk)

    def pair(j, carry):
        first = 2 * j
        scores(first + 1, sb_scr, max_b_scr)
        absorb(first, sa_scr, max_a_scr)
        scores(jnp.minimum(first + 2, last_off), sa_scr, max_a_scr)
        absorb(first + 1, sb_scr, max_b_scr)
        return carry

    lax.fori_loop(0, qi, pair, 0)

    o0 = acc_scr[0] * (1.0 / acc_scr[0, M_V:M_V + 1, :])
    o1 = acc_scr[1] * (1.0 / acc_scr[1, 0:1, :])
    out_row = lax.broadcasted_iota(jnp.int32, (LANES, tq), 0)
    o_ref[0] = jnp.where(out_row < M_V, o0, o1).T.astype(BF16)


def _mla_attention(q, k, vt):
    batch, _, seq, _ = q.shape
    tq, tk = MLA_QUERY_TILE, MLA_KEY_TILE
    assert tq == 2 * tk and seq % tq == 0
    return pl.pallas_call(
        _mla_kernel,
        grid=(batch, M_PAIRS, seq // tq),
        in_specs=[
            pl.BlockSpec((1, 2, tq, M_HEAD_PAD), lambda b, p, i: (b, p, i, 0)),
            pl.BlockSpec((1, 2, seq, M_HEAD_PAD), lambda b, p, i: (b, p, 0, 0)),
            pl.BlockSpec((1, 1, seq // tk, LANES, tk), lambda b, p, i: (b, p, 0, 0, 0)),
        ],
        out_specs=pl.BlockSpec((1, tq, LANES), lambda b, p, i: (b, i, p)),
        out_shape=jax.ShapeDtypeStruct((batch, seq, M_WIDTH), BF16),
        scratch_shapes=[
            pltpu.VMEM((2, 1, tq), F32),
            pltpu.VMEM((2, LANES, tq), F32),
            pltpu.VMEM((2, tk, tq), F32),
            pltpu.VMEM((2, tk, tq), F32),
            pltpu.VMEM((2, 1, tq), F32),
            pltpu.VMEM((2, 1, tq), F32),
        ],
        compiler_params=_params(3),
        name="mla_attention",
    )(q, k, vt)


def _t5_bucket_table(dilation, n_back):
    blk = BAND_BLOCK
    sub_dist = (np.arange(blk)[:, None] + blk) - np.arange(2 * blk)[None, :]
    dist = np.clip(sub_dist, 0, n_back) * dilation
    max_exact = REL_BUCKETS // 2
    d = np.maximum(dist, 1).astype(np.float32)
    ratio = np.log(d / np.float32(max_exact)) / np.float32(math.log(REL_MAX_DIST / max_exact))
    log_b = max_exact + (ratio * np.float32(REL_BUCKETS - max_exact)).astype(np.int32)
    log_b = np.minimum(log_b, REL_BUCKETS - 1)
    return np.where(dist < max_exact, dist, log_b).astype(np.int32)


def _rows(start, count, stride):
    return pl.ds(start, count) if stride == 1 else pl.ds(start, count, stride=stride)


def _dilated_kernel(rb_ref, bucket_ref, q_ref, kc_ref, kp_ref, vc_ref, vp_ref, o_ref,
                    bias_scr, acc_scr, m_scr, regroup_scr, out_scr):
    blk = BAND_BLOCK
    sup = SUPER_BLOCK
    grp = DILATED_REGROUP
    sub = sup // grp
    pair = pl.program_id(2)
    first_step = (pl.program_id(0) == 0) & (pl.program_id(1) == 0) & (pair == 0)

    @pl.when(first_step)
    def _build_bias():
        row = lax.broadcasted_iota(jnp.int32, (blk, 2 * blk), 0)
        col = lax.broadcasted_iota(jnp.int32, (blk, 2 * blk), 1)
        sub_dist = row + blk - col
        in_band = (sub_dist >= 0) & (sub_dist <= blk)
        for g in range(len(DILATED_PATTERNS)):
            bucket = bucket_ref[g]
            for hd in range(A_HEADS):
                bias = jnp.zeros((blk, 2 * blk), F32)
                for bk in range(REL_BUCKETS):
                    bias = jnp.where(bucket == bk, rb_ref[hd, bk] * LOG2_E, bias)
                bias_scr[g, hd] = jnp.where(in_band, bias, NEG_INF)

    sources = (q_ref, kc_ref, kp_ref, vc_ref, vp_ref)
    for idx, ref in enumerate(sources):
        for r in range(grp):
            regroup_scr[idx, r * sub:(r + 1) * sub, :] = ref[0, 0, _rows(r, sub, grp), :]

    first_valid_col = jnp.where(pl.program_id(1) > 0, 0, blk)
    col = lax.broadcasted_iota(jnp.int32, (blk, 2 * blk), 1)
    lane = lax.broadcasted_iota(jnp.int32, (blk, LANES), 1)
    lane2 = lax.broadcasted_iota(jnp.int32, (2 * blk, LANES), 1)
    contract_last = (((1,), (1,)), ((), ()))
    n_pat = len(DILATED_PATTERNS)

    for g, (_, dil) in enumerate(DILATED_PATTERNS):
        regrouped = dil % grp == 0
        step = dil // grp if regrouped else dil
        for res in range(dil):
            base = (res % grp) * sub + res // grp if regrouped else res
            prev_base = base + (sub if regrouped else sup) - blk * step
            for n in range(sup // (blk * dil)):
                rows = _rows(base + blk * step * n, blk, step)
                if n == 0:
                    prev_rows, k_idx, v_idx = _rows(prev_base, blk, step), 2, 4
                else:
                    prev_rows, k_idx, v_idx = _rows(base + blk * step * (n - 1), blk, step), 1, 3
                if regrouped:
                    load = lambda idx, r: regroup_scr[idx, r, :]
                else:
                    load = lambda idx, r: sources[idx][0, 0, r, :]
                q = load(0, rows)
                k2 = jnp.concatenate([load(k_idx, prev_rows), load(1, rows)], axis=0).astype(BF16)
                v2 = jnp.concatenate([load(v_idx, prev_rows), load(3, rows)], axis=0)
                for hh in range(2):
                    in_head = (lane < A_HEAD_DIM) if hh == 0 else (lane >= A_HEAD_DIM)
                    in_head2 = (lane2 < A_HEAD_DIM) if hh == 0 else (lane2 >= A_HEAD_DIM)
                    s = lax.dot_general(jnp.where(in_head, q, 0.0).astype(BF16), k2, contract_last,
                                        preferred_element_type=F32)
                    bias = bias_scr[g, 2 * pair + hh]
                    if n == 0:
                        bias = jnp.where(col >= first_valid_col, bias, NEG_INF)
                    s = s + bias
                    m_blk = jnp.max(s, axis=-1, keepdims=True)
                    p = jnp.exp2(s - m_blk).astype(BF16)
                    acc_scr[g, hh, rows, :] = _bdot(p, jnp.where(in_head2, v2, 1.0).astype(BF16))
                    m_scr[g, hh, rows, :] = jnp.broadcast_to(m_blk, (blk, LANES))

    lane_sub = lax.broadcasted_iota(jnp.int32, (sub, LANES), 1)
    for r in range(grp):
        chunk = [pl.ds(r * sub, sub) if dil % grp == 0 else _rows(r, sub, grp) for _, dil in DILATED_PATTERNS]
        halves = []
        for hh in range(2):
            maxes = [m_scr[g, hh, chunk[g], :] for g in range(n_pat)]
            top = maxes[0]
            for g in range(1, n_pat):
                top = jnp.maximum(top, maxes[g])
            total = jnp.exp2(maxes[0] - top) * acc_scr[0, hh, chunk[0], :]
            for g in range(1, n_pat):
                total = total + jnp.exp2(maxes[g] - top) * acc_scr[g, hh, chunk[g], :]
            halves.append(total * (1.0 / pltpu.roll(total, A_HEAD_DIM, 1)))
        out_scr[_rows(r, sub, grp), :] = jnp.where(lane_sub < A_HEAD_DIM, halves[0], halves[1])
    o_ref[0] = out_scr[...].astype(BF16)


def _dilated_attention(a_qkv, rel_bias):
    batch, _, seq, _ = a_qkv.shape
    blk = BAND_BLOCK
    sup = SUPER_BLOCK
    n_pat = len(DILATED_PATTERNS)
    assert all(w // d == blk for w, d in DILATED_PATTERNS), "band of exactly one block behind the query"
    assert seq % sup == 0
    bucket = jnp.asarray(np.stack([_t5_bucket_table(d, w // d) for w, d in DILATED_PATTERNS]))

    def part(which, prev):
        def index(b, s, p):
            return (b, which * A_PAIRS + p, jnp.maximum(s - 1, 0) if prev else s, 0)
        return pl.BlockSpec((1, 1, sup, LANES), index)

    return pl.pallas_call(
        _dilated_kernel,
        grid=(batch, seq // sup, A_PAIRS),
        in_specs=[
            pl.BlockSpec(memory_space=pltpu.SMEM),
            pl.BlockSpec((n_pat, blk, 2 * blk), lambda b, s, p: (0, 0, 0)),
            part(0, False), part(1, False), part(1, True), part(2, False), part(2, True),
        ],
        out_specs=pl.BlockSpec((1, sup, LANES), lambda b, s, p: (b, s, p)),
        out_shape=jax.ShapeDtypeStruct((batch, seq, A_WIDTH), BF16),
        scratch_shapes=[
            pltpu.VMEM((n_pat, A_HEADS, blk, 2 * blk), F32),
            pltpu.VMEM((n_pat, 2, sup, LANES), F32),
            pltpu.VMEM((n_pat, 2, sup, LANES), F32),
            pltpu.VMEM((5, sup, LANES), F32),
            pltpu.VMEM((sup, LANES), F32),
        ],
        compiler_params=_params(3),
        name="dilated_attention",
    )(rel_bias, bucket, a_qkv, a_qkv, a_qkv, a_qkv, a_qkv)


def _merge_kernel(x_ref, mod_ref, oa_ref, ob_ref, gate_ref, wa_ref, wb_ref, wo_ref, out_ref):
    y_a = _bdot(oa_ref[0], wa_ref[...])
    y_b = _bdot(ob_ref[0], wb_ref[...])
    gates = gate_ref[0].astype(F32)
    merged = _sigmoid(gates[:, :D_MODEL]) * y_a + _sigmoid(gates[:, D_MODEL:]) * y_b
    mixed = _bdot(merged.astype(BF16), wo_ref[...])
    out_ref[0] = x_ref[0] + mod_ref[0, 2:3, :] * mixed


def _merge_stage(x, mod, o_a, o_b, gates, w_up_a, w_up_b, w_o):
    batch, seq, _ = x.shape
    tm = ROW_TILE
    row3 = lambda b, i: (b, i, 0)
    half = pl.BlockSpec((1, tm, A_WIDTH), row3)
    return pl.pallas_call(
        _merge_kernel,
        grid=(batch, seq // tm),
        in_specs=[
            pl.BlockSpec((1, tm, D_MODEL), row3),
            pl.BlockSpec((1, N_MOD, D_MODEL), lambda b, i: (b, 0, 0)),
            half, half,
            pl.BlockSpec((1, tm, 2 * D_MODEL), row3),
            _resident((A_WIDTH, D_MODEL)), _resident((M_WIDTH, D_MODEL)), _resident((D_MODEL, D_MODEL)),
        ],
        out_specs=pl.BlockSpec((1, tm, D_MODEL), row3),
        out_shape=jax.ShapeDtypeStruct((batch, seq, D_MODEL), F32),
        compiler_params=_params(2),
        name="merge_stage",
    )(x, mod, o_a, o_b, gates,
      w_up_a.astype(BF16), w_up_b.astype(BF16), w_o.astype(BF16))


def _ffn_kernel(x_ref, mod_ref, g_ref, gf_ref, wg_ref, wu_ref, wd_ref, out_ref):
    x = x_ref[0]
    h = (_rms(x) * g_ref[...]) * (1.0 + mod_ref[0, 4:5, :]) + mod_ref[0, 3:4, :]
    hb = h.astype(BF16)
    gate = _bdot(hb, wg_ref[...])
    up = _bdot(hb, wu_ref[...])
    act = (gate * _sigmoid(gate) * up).astype(BF16)
    y = x + mod_ref[0, 5:6, :] * _bdot(act, wd_ref[...])
    out_ref[0] = _rms(y) * gf_ref[...]


def _ffn_stage(x, mod, g_ffn, g_final, w_gate, w_up, w_down):
    batch, seq, _ = x.shape
    tm = ROW_TILE
    row3 = lambda b, i: (b, i, 0)
    return pl.pallas_call(
        _ffn_kernel,
        grid=(batch, seq // tm),
        in_specs=[
            pl.BlockSpec((1, tm, D_MODEL), row3),
            pl.BlockSpec((1, N_MOD, D_MODEL), lambda b, i: (b, 0, 0)),
            _resident((1, D_MODEL)), _resident((1, D_MODEL)),
            _resident((D_MODEL, D_FF)), _resident((D_MODEL, D_FF)), _resident((D_FF, D_MODEL)),
        ],
        out_specs=pl.BlockSpec((1, tm, D_MODEL), row3),
        out_shape=jax.ShapeDtypeStruct((batch, seq, D_MODEL), F32),
        compiler_params=_params(2),
        name="ffn_stage",
    )(x, mod, g_ffn.reshape(1, D_MODEL), g_final.reshape(1, D_MODEL),
      w_gate.astype(BF16), w_up.astype(BF16), w_down.astype(BF16))


def kernel(x, c, positions, rel_bias, w_ada, b_ada, g_mix, w_in, g_q_lora, w_uq, g_kv_lora, w_ukv,
           w_up_a, w_up_b, w_o, g_ffn, w_gate, w_up, w_down, g_final):
    assert w_ada.shape[0] == 1, "single-layer trunk"
    mod = _modulation(c, w_ada[0], b_ada[0])
    a_qkv, gates, q, k, vt = _input_stage(x, mod, g_mix[0], positions, w_in[0], g_q_lora[0], w_uq[0],
                                         g_kv_lora[0], w_ukv[0])
    o_b = _mla_attention(q, k, vt)
    o_a = _dilated_attention(a_qkv, rel_bias)
    x1 = _merge_stage(x, mod, o_a, o_b, gates, w_up_a[0], w_up_b[0], w_o[0])
    return _ffn_stage(x1, mod, g_ffn[0], g_final, w_gate[0], w_up[0], w_down[0])
```

```python
import functools
import math

import jax
import jax.numpy as jnp
import numpy as np
from jax import lax
from jax.experimental import pallas as pl
from jax.experimental.pallas import tpu as pltpu

D_MODEL = 1024
A_HEADS = 8
A_HEAD_DIM = 64
A_WIDTH = A_HEADS * A_HEAD_DIM
DILATED_PATTERNS = ((128, 1), (512, 4), (2048, 16))
BAND_BLOCK = 128
REL_BUCKETS = 32
REL_MAX_DIST = 2048
M_HEADS = 8
M_NOPE = 64
M_ROPE = 32
M_V = 64
M_Q_LORA = 768
M_KV_LORA = 256
M_WIDTH = M_HEADS * M_V
ROPE_THETA = 10000.0
D_FF = -(-8 * D_MODEL // (3 * 256)) * 256
N_MOD = 6
EPS = 1e-6
NEG_INF = -1e30

LANES = 128
SUBLANES = 8
V7X_VMEM_BYTES = 64 * 1024 * 1024
VMEM_LIMIT_BYTES = V7X_VMEM_BYTES - 8 * 1024 * 1024

M_HEAD_PAD = LANES
M_PAIRS = M_HEADS // 2
ROPE_HALF = M_ROPE // 2
ROPE_LO = M_NOPE
ROPE_MID = M_NOPE + ROPE_HALF
ROPE_HI = M_NOPE + M_ROPE

A_PAIRS = A_HEADS // 2
A_TILES = 3 * A_PAIRS
SUPER_BLOCK = BAND_BLOCK * max(d for _, d in DILATED_PATTERNS)
DILATED_REGROUP = 4
LOG2_E = math.log2(math.e)

ROW_TILE = 512
MLA_KEY_TILE = 512
MLA_QUERY_TILE = 2 * MLA_KEY_TILE
MLA_QUERY_CHUNK = 256

F32 = jnp.float32
BF16 = jnp.bfloat16


def _params(n_axes, flags=None):
    return pltpu.CompilerParams(
        dimension_semantics=("arbitrary",) * n_axes,
        vmem_limit_bytes=VMEM_LIMIT_BYTES,
        flags=flags,
    )


def _resident(shape):
    zeros = (0,) * len(shape)
    return pl.BlockSpec(shape, lambda *_: zeros, pipeline_mode=pl.Buffered(1))


def _bdot(a, b):
    return jnp.dot(a, b, preferred_element_type=F32)


def _rms(x):
    return x * lax.rsqrt(jnp.mean(x * x, axis=-1, keepdims=True) + EPS)


def _sigmoid(x):
    return 1.0 / (1.0 + jnp.exp(-x))


def _mod_kernel(c_ref, w_ref, b_ref, o_ref):
    c = c_ref[...]
    cond = c * _sigmoid(c)
    o_ref[...] = (
        jnp.dot(cond, w_ref[...], preferred_element_type=F32, precision=lax.Precision.HIGHEST)
        + b_ref[...]
    )


def _modulation(c, w_ada, b_ada):
    batch = c.shape[0]
    rows = -(-batch // SUBLANES) * SUBLANES
    c_pad = jnp.pad(c, ((0, rows - batch), (0, 0)))
    out = pl.pallas_call(
        _mod_kernel,
        grid=(N_MOD,),
        in_specs=[
            pl.BlockSpec((rows, D_MODEL), lambda j: (0, 0)),
            pl.BlockSpec((D_MODEL, D_MODEL), lambda j: (0, j)),
            pl.BlockSpec((1, D_MODEL), lambda j: (0, j)),
        ],
        out_specs=pl.BlockSpec((rows, D_MODEL), lambda j: (0, j)),
        out_shape=jax.ShapeDtypeStruct((rows, N_MOD * D_MODEL), F32),
        compiler_params=_params(1),
        name="adaln_mod",
    )(c_pad, w_ada, b_ada.reshape(1, N_MOD * D_MODEL))
    return out[:batch].reshape(batch, N_MOD, D_MODEL)


def _rope_lanes(x, cos, signed_sin):
    lane = lax.broadcasted_iota(jnp.int32, x.shape, 1)
    partner = jnp.where(lane < ROPE_MID, pltpu.roll(x, LANES - ROPE_HALF, 1), pltpu.roll(x, ROPE_HALF, 1))
    return x * cos + partner * signed_sin


def _input_kernel(x_ref, mod_ref, g_ref, pos_ref, freq_ref, wa_ref, wg_ref, wcq_ref, wckvr_ref,
                  gq_ref, wuq_ref, gkv_ref, wuk_ref, wuv_ref,
                  a_ref, gate_ref, q_ref, k_ref, v_ref):
    x = x_ref[0]
    shift = mod_ref[0, 0:1, :]
    scale = mod_ref[0, 1:2, :]
    h = (_rms(x) * g_ref[...]) * (1.0 + scale) + shift
    hb = h.astype(BF16)

    c_q = _bdot(hb, wcq_ref[...])
    ckvr = _bdot(hb, wckvr_ref[...])

    ang = freq_ref[...] * pos_ref[0, 0].astype(F32)
    cos_r, sin_r = jnp.cos(ang), jnp.sin(ang)
    rows = ang.shape[1]
    cos = jnp.concatenate(
        [jnp.ones((ROPE_LO, rows), F32), cos_r, jnp.ones((LANES - ROPE_HI, rows), F32)], axis=0).T
    signed_sin = jnp.concatenate(
        [jnp.zeros((ROPE_LO, rows), F32), -sin_r[:ROPE_HALF], sin_r[ROPE_HALF:],
         jnp.zeros((LANES - ROPE_HI, rows), F32)], axis=0).T

    q_all = _bdot((_rms(c_q) * gq_ref[...]).astype(BF16), wuq_ref[...])
    q_scale = (M_NOPE + M_ROPE) ** -0.5 * LOG2_E
    for hd in range(M_HEADS):
        q_h = q_all[:, hd * M_HEAD_PAD:(hd + 1) * M_HEAD_PAD]
        q_ref[0, hd] = (_rope_lanes(q_h, cos, signed_sin) * q_scale).astype(BF16)

    c_kv = (_rms(ckvr[:, :M_KV_LORA]) * gkv_ref[...]).astype(BF16)
    k_rope = _rope_lanes(ckvr[:, M_KV_LORA:], cos, signed_sin)
    k_all = _bdot(c_kv, wuk_ref[...])
    for hd in range(M_HEADS):
        k_ref[0, hd] = (k_all[:, hd * M_HEAD_PAD:(hd + 1) * M_HEAD_PAD] + k_rope).astype(BF16)
    v_all = _bdot(c_kv, wuv_ref[...])
    for pr in range(M_PAIRS):
        v_ref[0, pr, 0] = v_all[:, pr * LANES:(pr + 1) * LANES].T.astype(BF16)

    a_all = _bdot(hb, wa_ref[...])
    for j in range(A_TILES):
        a_ref[0, j] = a_all[:, j * LANES:(j + 1) * LANES]
    gate_ref[0] = _bdot(hb, wg_ref[...]).astype(BF16)


def _input_stage(x, mod, g_mix, positions, w_in, g_q_lora, w_uq, g_kv_lora, w_ukv):
    batch, seq, _ = x.shape
    tm = MLA_KEY_TILE
    s0 = 3 * A_WIDTH
    s1 = s0 + M_Q_LORA
    s2 = s1 + M_KV_LORA
    s3 = s2 + M_ROPE
    w_a = jnp.concatenate([w_in[:, :A_WIDTH] * (A_HEAD_DIM ** -0.5 * LOG2_E), w_in[:, A_WIDTH:s0]],
                          axis=1).astype(BF16)
    w_cq = w_in[:, s0:s1].astype(BF16)
    zeros = functools.partial(jnp.zeros, dtype=w_in.dtype)
    w_ckvr = jnp.concatenate(
        [w_in[:, s1:s2], zeros((D_MODEL, ROPE_LO)), w_in[:, s2:s3], zeros((D_MODEL, LANES - ROPE_HI))],
        axis=1).astype(BF16)
    w_g = w_in[:, s3:].astype(BF16)
    w_uq_p = jnp.pad(w_uq, ((0, 0), (0, 0), (0, M_HEAD_PAD - M_NOPE - M_ROPE)))
    w_uq_p = w_uq_p.reshape(M_Q_LORA, M_HEADS * M_HEAD_PAD).astype(BF16)
    w_uk_p = jnp.pad(w_ukv[:, :, :M_NOPE], ((0, 0), (0, 0), (0, M_HEAD_PAD - M_NOPE)))
    w_uk_p = w_uk_p.reshape(M_KV_LORA, M_HEADS * M_HEAD_PAD).astype(BF16)
    w_uv = w_ukv[:, :, M_NOPE:].reshape(M_KV_LORA, M_WIDTH).astype(BF16)

    freqs = ROPE_THETA ** (-jnp.arange(ROPE_HALF, dtype=F32) / ROPE_HALF)
    freq_col = jnp.concatenate([freqs, freqs]).reshape(M_ROPE, 1)

    row3 = lambda b, i: (b, i, 0)
    head4 = lambda b, i: (b, 0, i, 0)
    return pl.pallas_call(
        _input_kernel,
        grid=(batch, seq // tm),
        in_specs=[
            pl.BlockSpec((1, tm, D_MODEL), row3),
            pl.BlockSpec((1, N_MOD, D_MODEL), lambda b, i: (b, 0, 0)),
            _resident((1, D_MODEL)),
            pl.BlockSpec((1, 1, 1, tm), lambda b, i: (b, i, 0, 0)),
            _resident((M_ROPE, 1)),
            _resident(w_a.shape), _resident(w_g.shape), _resident(w_cq.shape), _resident(w_ckvr.shape),
            _resident((1, M_Q_LORA)), _resident(w_uq_p.shape),
            _resident((1, M_KV_LORA)), _resident(w_uk_p.shape), _resident(w_uv.shape),
        ],
        out_specs=[
            pl.BlockSpec((1, A_TILES, tm, LANES), head4),
            pl.BlockSpec((1, tm, 2 * D_MODEL), row3),
            pl.BlockSpec((1, M_HEADS, tm, M_HEAD_PAD), head4),
            pl.BlockSpec((1, M_HEADS, tm, M_HEAD_PAD), head4),
            pl.BlockSpec((1, M_PAIRS, 1, LANES, tm), lambda b, i: (b, 0, i, 0, 0)),
        ],
        out_shape=[
            jax.ShapeDtypeStruct((batch, A_TILES, seq, LANES), F32),
            jax.ShapeDtypeStruct((batch, seq, 2 * D_MODEL), BF16),
            jax.ShapeDtypeStruct((batch, M_HEADS, seq, M_HEAD_PAD), BF16),
            jax.ShapeDtypeStruct((batch, M_HEADS, seq, M_HEAD_PAD), BF16),
            jax.ShapeDtypeStruct((batch, M_PAIRS, seq // tm, LANES, tm), BF16),
        ],
        compiler_params=_params(2),
        name="input_stage",
    )(x, mod, g_mix.reshape(1, D_MODEL), positions.reshape(batch, seq // tm, 1, tm), freq_col,
      w_a, w_g, w_cq, w_ckvr, g_q_lora.reshape(1, M_Q_LORA), w_uq_p,
      g_kv_lora.reshape(1, M_KV_LORA), w_uk_p, w_uv)


def _mla_kernel(q_ref, k_ref, vt_ref, o_ref, m_scr, acc_scr, sa_scr, sb_scr, max_a_scr, max_b_scr):
    tq, tk = MLA_QUERY_TILE, MLA_KEY_TILE
    qc = MLA_QUERY_CHUNK
    qi = pl.program_id(2)
    contract_last = (((1,), (1,)), ((), ()))
    v_row = lax.broadcasted_iota(jnp.int32, (LANES, tk), 0)
    own_rows = [v_row < M_V, v_row >= M_V]

    m_scr[...] = jnp.full(m_scr.shape, NEG_INF, F32)
    acc_scr[...] = jnp.zeros(acc_scr.shape, F32)

    def block(scored=None, absorbed=None):
        pieces, chains = [], []
        if scored is not None:
            next_tile, next_s, next_max, diagonal = scored
            start = pl.multiple_of(next_tile * tk, tk)
            keys = [k_ref[0, hh, pl.ds(start, tk), :] for hh in range(2)]
            pieces = [(hh, q0) for hh in range(2) for q0 in range(diagonal or 0, tq, qc)]
        if absorbed is not None:
            tile, s_scr, max_scr, q_lo = absorbed
            vt = vt_ref[0, 0, tile]
            vt_aug = [jnp.where(own, vt, jnp.ones_like(vt)) for own in own_rows]
            chains = [(hh, q0) for hh in range(2) for q0 in range(q_lo, tq, qc)]
            state = [(m_scr[hh, :, q0:q0 + qc], acc_scr[hh, :, q0:q0 + qc]) for hh, q0 in chains]

        def score_piece(hh, q0):
            s = lax.dot_general(keys[hh], q_ref[0, hh, q0:q0 + qc, :], contract_last,
                                preferred_element_type=F32)
            if diagonal is not None and q0 < diagonal + tk - 1:
                key_pos = lax.broadcasted_iota(jnp.int32, s.shape, 0) + diagonal
                query_pos = lax.broadcasted_iota(jnp.int32, s.shape, 1) + q0
                s = jnp.where(key_pos <= query_pos, s, NEG_INF)
            next_s[hh, :, q0:q0 + qc] = s
            next_max[hh, :, q0:q0 + qc] = jnp.max(s, axis=0, keepdims=True)

        def absorb_chain(i):
            hh, q0 = chains[i]
            m_prev, acc_prev = state[i]
            m_new = jnp.maximum(m_prev, max_scr[hh, :, q0:q0 + qc])
            p = jnp.exp2(s_scr[hh, :, q0:q0 + qc] - m_new).astype(BF16)
            acc_scr[hh, :, q0:q0 + qc] = jnp.exp2(m_prev - m_new) * acc_prev + _bdot(vt_aug[hh], p)
            m_scr[hh, :, q0:q0 + qc] = m_new

        if pieces:
            score_piece(*pieces[0])
        for i in range(max(len(pieces) - 1, len(chains))):
            if i + 1 < len(pieces):
                score_piece(*pieces[i + 1])
            if i < len(chains):
                absorb_chain(i)

    n_off = 2 * qi
    last_off = jnp.maximum(n_off - 1, 0)
    block(scored=(n_off, sa_scr, max_a_scr, 0))
    block(scored=(n_off + 1, sb_scr, max_b_scr, tk), absorbed=(n_off, sa_scr, max_a_scr, 0))
    block(scored=(0, sa_scr, max_a_scr, None), absorbed=(n_off + 1, sb_scr, max_b_scr, tk))

    def pair(j, carry):
        first = 2 * j
        block(scored=(first + 1, sb_scr, max_b_scr, None), absorbed=(first, sa_scr, max_a_scr, 0))
        block(scored=(jnp.minimum(first + 2, last_off), sa_scr, max_a_scr, None),
              absorbed=(first + 1, sb_scr, max_b_scr, 0))
        return carry

    lax.fori_loop(0, qi, pair, 0)

    o0 = acc_scr[0] * (1.0 / acc_scr[0, M_V:M_V + 1, :])
    o1 = acc_scr[1] * (1.0 / acc_scr[1, 0:1, :])
    out_row = lax.broadcasted_iota(jnp.int32, (LANES, tq), 0)
    o_ref[0] = jnp.where(out_row < M_V, o0, o1).T.astype(BF16)


def _mla_attention(q, k, vt):
    batch, _, seq, _ = q.shape
    tq, tk = MLA_QUERY_TILE, MLA_KEY_TILE
    assert tq == 2 * tk and seq % tq == 0
    return pl.pallas_call(
        _mla_kernel,
        grid=(batch, M_PAIRS, seq // tq),
        in_specs=[
            pl.BlockSpec((1, 2, tq, M_HEAD_PAD), lambda b, p, i: (b, p, i, 0)),
            pl.BlockSpec((1, 2, seq, M_HEAD_PAD), lambda b, p, i: (b, p, 0, 0)),
            pl.BlockSpec((1, 1, seq // tk, LANES, tk), lambda b, p, i: (b, p, 0, 0, 0)),
        ],
        out_specs=pl.BlockSpec((1, tq, LANES), lambda b, p, i: (b, i, p)),
        out_shape=jax.ShapeDtypeStruct((batch, seq, M_WIDTH), BF16),
        scratch_shapes=[
            pltpu.VMEM((2, 1, tq), F32),
            pltpu.VMEM((2, LANES, tq), F32),
            pltpu.VMEM((2, tk, tq), F32),
            pltpu.VMEM((2, tk, tq), F32),
            pltpu.VMEM((2, 1, tq), F32),
            pltpu.VMEM((2, 1, tq), F32),
        ],
        compiler_params=_params(3),
        name="mla_attention",
    )(q, k, vt)


def _t5_bucket_table(dilation, n_back):
    blk = BAND_BLOCK
    sub_dist = (np.arange(blk)[:, None] + blk) - np.arange(2 * blk)[None, :]
    dist = np.clip(sub_dist, 0, n_back) * dilation
    max_exact = REL_BUCKETS // 2
    d = np.maximum(dist, 1).astype(np.float32)
    ratio = np.log(d / np.float32(max_exact)) / np.float32(math.log(REL_MAX_DIST / max_exact))
    log_b = max_exact + (ratio * np.float32(REL_BUCKETS - max_exact)).astype(np.int32)
    log_b = np.minimum(log_b, REL_BUCKETS - 1)
    return np.where(dist < max_exact, dist, log_b).astype(np.int32)


def _rows(start, count, stride):
    return pl.ds(start, count) if stride == 1 else pl.ds(start, count, stride=stride)


def _dilated_kernel(rb_ref, bucket_ref, q_ref, kc_ref, kp_ref, vc_ref, vp_ref, o_ref,
                    bias_scr, acc_scr, m_scr, regroup_scr, out_scr):
    blk = BAND_BLOCK
    sup = SUPER_BLOCK
    grp = DILATED_REGROUP
    sub = sup // grp
    pair = pl.program_id(2)
    first_step = (pl.program_id(0) == 0) & (pl.program_id(1) == 0) & (pair == 0)

    @pl.when(first_step)
    def _build_bias():
        row = lax.broadcasted_iota(jnp.int32, (blk, 2 * blk), 0)
        col = lax.broadcasted_iota(jnp.int32, (blk, 2 * blk), 1)
        sub_dist = row + blk - col
        in_band = (sub_dist >= 0) & (sub_dist <= blk)
        for g in range(len(DILATED_PATTERNS)):
            bucket = bucket_ref[g]
            for hd in range(A_HEADS):
                bias = jnp.zeros((blk, 2 * blk), F32)
                for bk in range(REL_BUCKETS):
                    bias = jnp.where(bucket == bk, rb_ref[hd, bk] * LOG2_E, bias)
                bias_scr[g, hd] = jnp.where(in_band, bias, NEG_INF)

    sources = (q_ref, kc_ref, kp_ref, vc_ref, vp_ref)
    for idx, ref in enumerate(sources):
        for r in range(grp):
            regroup_scr[idx, r * sub:(r + 1) * sub, :] = ref[0, 0, _rows(r, sub, grp), :]

    first_valid_col = jnp.where(pl.program_id(1) > 0, 0, blk)
    col = lax.broadcasted_iota(jnp.int32, (blk, 2 * blk), 1)
    lane = lax.broadcasted_iota(jnp.int32, (blk, LANES), 1)
    lane2 = lax.broadcasted_iota(jnp.int32, (2 * blk, LANES), 1)
    contract_last = (((1,), (1,)), ((), ()))
    n_pat = len(DILATED_PATTERNS)

    for g, (_, dil) in enumerate(DILATED_PATTERNS):
        regrouped = dil % grp == 0
        step = dil // grp if regrouped else dil
        for res in range(dil):
            base = (res % grp) * sub + res // grp if regrouped else res
            prev_base = base + (sub if regrouped else sup) - blk * step
            for n in range(sup // (blk * dil)):
                rows = _rows(base + blk * step * n, blk, step)
                if n == 0:
                    prev_rows, k_idx, v_idx = _rows(prev_base, blk, step), 2, 4
                else:
                    prev_rows, k_idx, v_idx = _rows(base + blk * step * (n - 1), blk, step), 1, 3
                if regrouped:
                    load = lambda idx, r: regroup_scr[idx, r, :]
                else:
                    load = lambda idx, r: sources[idx][0, 0, r, :]
                q = load(0, rows)
                k2 = jnp.concatenate([load(k_idx, prev_rows), load(1, rows)], axis=0).astype(BF16)
                v2 = jnp.concatenate([load(v_idx, prev_rows), load(3, rows)], axis=0)
                for hh in range(2):
                    in_head = (lane < A_HEAD_DIM) if hh == 0 else (lane >= A_HEAD_DIM)
                    in_head2 = (lane2 < A_HEAD_DIM) if hh == 0 else (lane2 >= A_HEAD_DIM)
                    s = lax.dot_general(jnp.where(in_head, q, 0.0).astype(BF16), k2, contract_last,
                                        preferred_element_type=F32)
                    bias = bias_scr[g, 2 * pair + hh]
                    if n == 0:
                        bias = jnp.where(col >= first_valid_col, bias, NEG_INF)
                    s = s + bias
                    m_blk = jnp.max(s, axis=-1, keepdims=True)
                    p = jnp.exp2(s - m_blk).astype(BF16)
                    acc_scr[g, hh, rows, :] = _bdot(p, jnp.where(in_head2, v2, 1.0).astype(BF16))
                    m_scr[g, hh, rows, :] = jnp.broadcast_to(m_blk, (blk, LANES))

    lane_sub = lax.broadcasted_iota(jnp.int32, (sub, LANES), 1)
    for r in range(grp):
        chunk = [pl.ds(r * sub, sub) if dil % grp == 0 else _rows(r, sub, grp) for _, dil in DILATED_PATTERNS]
        halves = []
        for hh in range(2):
            maxes = [m_scr[g, hh, chunk[g], :] for g in range(n_pat)]
            top = maxes[0]
            for g in range(1, n_pat):
                top = jnp.maximum(top, maxes[g])
            total = jnp.exp2(maxes[0] - top) * acc_scr[0, hh, chunk[0], :]
            for g in range(1, n_pat):
                total = total + jnp.exp2(maxes[g] - top) * acc_scr[g, hh, chunk[g], :]
            halves.append(total * (1.0 / pltpu.roll(total, A_HEAD_DIM, 1)))
        out_scr[_rows(r, sub, grp), :] = jnp.where(lane_sub < A_HEAD_DIM, halves[0], halves[1])
    o_ref[0] = out_scr[...].astype(BF16)


def _dilated_attention(a_qkv, rel_bias):
    batch, _, seq, _ = a_qkv.shape
    blk = BAND_BLOCK
    sup = SUPER_BLOCK
    n_pat = len(DILATED_PATTERNS)
    assert all(w // d == blk for w, d in DILATED_PATTERNS), "band of exactly one block behind the query"
    assert seq % sup == 0
    bucket = jnp.asarray(np.stack([_t5_bucket_table(d, w // d) for w, d in DILATED_PATTERNS]))

    def part(which, prev):
        def index(b, s, p):
            return (b, which * A_PAIRS + p, jnp.maximum(s - 1, 0) if prev else s, 0)
        return pl.BlockSpec((1, 1, sup, LANES), index)

    return pl.pallas_call(
        _dilated_kernel,
        grid=(batch, seq // sup, A_PAIRS),
        in_specs=[
            pl.BlockSpec(memory_space=pltpu.SMEM),
            pl.BlockSpec((n_pat, blk, 2 * blk), lambda b, s, p: (0, 0, 0)),
            part(0, False), part(1, False), part(1, True), part(2, False), part(2, True),
        ],
        out_specs=pl.BlockSpec((1, sup, LANES), lambda b, s, p: (b, s, p)),
        out_shape=jax.ShapeDtypeStruct((batch, seq, A_WIDTH), BF16),
        scratch_shapes=[
            pltpu.VMEM((n_pat, A_HEADS, blk, 2 * blk), F32),
            pltpu.VMEM((n_pat, 2, sup, LANES), F32),
            pltpu.VMEM((n_pat, 2, sup, LANES), F32),
            pltpu.VMEM((5, sup, LANES), F32),
            pltpu.VMEM((sup, LANES), F32),
        ],
        compiler_params=_params(3),
        name="dilated_attention",
    )(rel_bias, bucket, a_qkv, a_qkv, a_qkv, a_qkv, a_qkv)


def _merge_kernel(x_ref, mod_ref, oa_ref, ob_ref, gate_ref, wa_ref, wb_ref, wo_ref, out_ref):
    y_a = _bdot(oa_ref[0], wa_ref[...])
    y_b = _bdot(ob_ref[0], wb_ref[...])
    gates = gate_ref[0].astype(F32)
    merged = _sigmoid(gates[:, :D_MODEL]) * y_a + _sigmoid(gates[:, D_MODEL:]) * y_b
    mixed = _bdot(merged.astype(BF16), wo_ref[...])
    out_ref[0] = x_ref[0] + mod_ref[0, 2:3, :] * mixed


def _merge_stage(x, mod, o_a, o_b, gates, w_up_a, w_up_b, w_o):
    batch, seq, _ = x.shape
    tm = ROW_TILE
    row3 = lambda b, i: (b, i, 0)
    half = pl.BlockSpec((1, tm, A_WIDTH), row3)
    return pl.pallas_call(
        _merge_kernel,
        grid=(batch, seq // tm),
        in_specs=[
            pl.BlockSpec((1, tm, D_MODEL), row3),
            pl.BlockSpec((1, N_MOD, D_MODEL), lambda b, i: (b, 0, 0)),
            half, half,
            pl.BlockSpec((1, tm, 2 * D_MODEL), row3),
            _resident((A_WIDTH, D_MODEL)), _resident((M_WIDTH, D_MODEL)), _resident((D_MODEL, D_MODEL)),
        ],
        out_specs=pl.BlockSpec((1, tm, D_MODEL), row3),
        out_shape=jax.ShapeDtypeStruct((batch, seq, D_MODEL), F32),
        compiler_params=_params(2),
        name="merge_stage",
    )(x, mod, o_a, o_b, gates,
      w_up_a.astype(BF16), w_up_b.astype(BF16), w_o.astype(BF16))


def _ffn_kernel(x_ref, mod_ref, g_ref, gf_ref, wg_ref, wu_ref, wd_ref, out_ref):
    x = x_ref[0]
    h = (_rms(x) * g_ref[...]) * (1.0 + mod_ref[0, 4:5, :]) + mod_ref[0, 3:4, :]
    hb = h.astype(BF16)
    gate = _bdot(hb, wg_ref[...])
    up = _bdot(hb, wu_ref[...])
    act = (gate * _sigmoid(gate) * up).astype(BF16)
    y = x + mod_ref[0, 5:6, :] * _bdot(act, wd_ref[...])
    out_ref[0] = _rms(y) * gf_ref[...]


def _ffn_stage(x, mod, g_ffn, g_final, w_gate, w_up, w_down):
    batch, seq, _ = x.shape
    tm = ROW_TILE
    row3 = lambda b, i: (b, i, 0)
    return pl.pallas_call(
        _ffn_kernel,
        grid=(batch, seq // tm),
        in_specs=[
            pl.BlockSpec((1, tm, D_MODEL), row3),
            pl.BlockSpec((1, N_MOD, D_MODEL), lambda b, i: (b, 0, 0)),
            _resident((1, D_MODEL)), _resident((1, D_MODEL)),
            _resident((D_MODEL, D_FF)), _resident((D_MODEL, D_FF)), _resident((D_FF, D_MODEL)),
        ],
        out_specs=pl.BlockSpec((1, tm, D_MODEL), row3),
        out_shape=jax.ShapeDtypeStruct((batch, seq, D_MODEL), F32),
        compiler_params=_params(2),
        name="ffn_stage",
    )(x, mod, g_ffn.reshape(1, D_MODEL), g_final.reshape(1, D_MODEL),
      w_gate.astype(BF16), w_up.astype(BF16), w_down.astype(BF16))


def kernel(x, c, positions, rel_bias, w_ada, b_ada, g_mix, w_in, g_q_lora, w_uq, g_kv_lora, w_ukv,
           w_up_a, w_up_b, w_o, g_ffn, w_gate, w_up, w_down, g_final):
    assert w_ada.shape[0] == 1, "single-layer trunk"
    mod = _modulation(c, w_ada[0], b_ada[0])
    a_qkv, gates, q, k, vt = _input_stage(x, mod, g_mix[0], positions, w_in[0], g_q_lora[0], w_uq[0],
                                         g_kv_lora[0], w_ukv[0])
    o_b = _mla_attention(q, k, vt)
    o_a = _dilated_attention(a_qkv, rel_bias)
    x1 = _merge_stage(x, mod, o_a, o_b, gates, w_up_a[0], w_up_b[0], w_o[0])
    return _ffn_stage(x1, mod, g_ffn[0], g_final, w_gate[0], w_up[0], w_down[0])
```

```python
import math

import jax
import jax.numpy as jnp
import numpy as np
from jax import lax
from jax.experimental import pallas as pl
from jax.experimental.pallas import tpu as pltpu

D_MODEL = 1024
A_HEADS = 8
A_HEAD_DIM = 64
A_WIDTH = A_HEADS * A_HEAD_DIM
DILATED_PATTERNS = ((128, 1), (512, 4), (2048, 16))
BAND_BLOCK = 128
REL_BUCKETS = 32
REL_MAX_DIST = 2048
M_HEADS = 8
M_NOPE = 64
M_ROPE = 32
M_V = 64
M_Q_LORA = 768
M_KV_LORA = 256
M_WIDTH = M_HEADS * M_V
ROPE_THETA = 10000.0
D_FF = -(-8 * D_MODEL // (3 * 256)) * 256
N_MOD = 6
EPS = 1e-6
NEG_INF = -1e30

LANES = 128
SUBLANES = 8
V7X_VMEM_BYTES = 64 * 1024 * 1024
VMEM_LIMIT_BYTES = V7X_VMEM_BYTES - 8 * 1024 * 1024

M_HEAD_PAD = LANES
M_PAIRS = M_HEADS // 2
ROPE_HALF = M_ROPE // 2
ROPE_LO = M_NOPE
ROPE_MID = M_NOPE + ROPE_HALF
ROPE_HI = M_NOPE + M_ROPE

A_PAIRS = A_HEADS // 2
A_TILES = 3 * A_PAIRS
SUPER_BLOCK = BAND_BLOCK * max(d for _, d in DILATED_PATTERNS)
DILATED_REGROUP = 4
LOG2_E = math.log2(math.e)

ROW_TILE = 512
MOD_COLUMN_BLOCK = 2 * D_MODEL
MLA_KEY_TILE = 512
MLA_QUERY_TILE = 2 * MLA_KEY_TILE
MLA_QUERY_CHUNK = 256

F32 = jnp.float32
BF16 = jnp.bfloat16


def _params(n_axes, flags=None):
    return pltpu.CompilerParams(
        dimension_semantics=("arbitrary",) * n_axes,
        vmem_limit_bytes=VMEM_LIMIT_BYTES,
        flags=flags,
    )


def _resident(shape):
    zeros = (0,) * len(shape)
    return pl.BlockSpec(shape, lambda *_: zeros, pipeline_mode=pl.Buffered(1))


def _bdot(a, b):
    return jnp.dot(a, b, preferred_element_type=F32)


def _rms(x):
    return x * lax.rsqrt(jnp.mean(x * x, axis=-1, keepdims=True) + EPS)


def _sigmoid(x):
    return 1.0 / (1.0 + jnp.exp(-x))


def _mod_kernel(c_ref, w_ref, b_ref, o_ref):
    c = c_ref[...]
    cond = c * _sigmoid(c)
    o_ref[...] = (
        jnp.dot(cond, w_ref[...], preferred_element_type=F32, precision=lax.Precision.HIGHEST)
        + b_ref[...]
    )


def _modulation(c, w_ada, b_ada):
    batch = c.shape[0]
    rows = -(-batch // SUBLANES) * SUBLANES
    c_pad = jnp.pad(c, ((0, rows - batch), (0, 0)))
    cols = MOD_COLUMN_BLOCK
    out = pl.pallas_call(
        _mod_kernel,
        grid=(N_MOD * D_MODEL // cols,),
        in_specs=[
            pl.BlockSpec((rows, D_MODEL), lambda j: (0, 0)),
            pl.BlockSpec((D_MODEL, cols), lambda j: (0, j)),
            pl.BlockSpec((1, cols), lambda j: (0, j)),
        ],
        out_specs=pl.BlockSpec((rows, cols), lambda j: (0, j)),
        out_shape=jax.ShapeDtypeStruct((rows, N_MOD * D_MODEL), F32),
        compiler_params=_params(1),
        name="adaln_mod",
    )(c_pad, w_ada, b_ada.reshape(1, N_MOD * D_MODEL))
    return out[:batch].reshape(batch, N_MOD, D_MODEL)


def _rope_lanes(x, cos, signed_sin):
    lane = lax.broadcasted_iota(jnp.int32, x.shape, 1)
    partner = jnp.where(lane < ROPE_MID, pltpu.roll(x, LANES - ROPE_HALF, 1), pltpu.roll(x, ROPE_HALF, 1))
    return x * cos + partner * signed_sin


def _input_kernel(x_ref, mod_ref, g_ref, pos_ref, freq_ref, wa_ref, wg_ref, wcq_ref, wckv_ref, wkr_ref,
                  gq_ref, wuq_ref, gkv_ref, wuk_ref, wuv_ref,
                  a_ref, gate_ref, q_ref, k_ref, v_ref):
    x = x_ref[0]
    shift = mod_ref[0, 0:1, :]
    scale = mod_ref[0, 1:2, :]
    h = (_rms(x) * g_ref[...]) * (1.0 + scale) + shift
    hb = h.astype(BF16)

    c_q = _bdot(hb, wcq_ref[...])
    c_kv = _bdot(hb, wckv_ref[...])
    k_r = _bdot(hb, wkr_ref[...])

    ang = freq_ref[...] * pos_ref[0, 0].astype(F32)
    cos_r, sin_r = jnp.cos(ang), jnp.sin(ang)
    rows = ang.shape[1]
    cos = jnp.concatenate(
        [jnp.ones((ROPE_LO, rows), F32), cos_r, jnp.ones((LANES - ROPE_HI, rows), F32)], axis=0).T
    signed_sin = jnp.concatenate(
        [jnp.zeros((ROPE_LO, rows), F32), -sin_r[:ROPE_HALF], sin_r[ROPE_HALF:],
         jnp.zeros((LANES - ROPE_HI, rows), F32)], axis=0).T

    q_all = _bdot((_rms(c_q) * gq_ref[...]).astype(BF16), wuq_ref[...])
    q_scale = (M_NOPE + M_ROPE) ** -0.5 * LOG2_E
    for hd in range(M_HEADS):
        q_h = q_all[:, hd * M_HEAD_PAD:(hd + 1) * M_HEAD_PAD]
        q_ref[0, hd] = (_rope_lanes(q_h, cos, signed_sin) * q_scale).astype(BF16)

    c_kv = (_rms(c_kv) * gkv_ref[...]).astype(BF16)
    k_rope = _rope_lanes(k_r, cos, signed_sin)
    k_all = _bdot(c_kv, wuk_ref[...])
    for hd in range(M_HEADS):
        k_ref[0, hd] = (k_all[:, hd * M_HEAD_PAD:(hd + 1) * M_HEAD_PAD] + k_rope).astype(BF16)
    v_all = _bdot(c_kv, wuv_ref[...])
    for pr in range(M_PAIRS):
        v_ref[0, pr, 0] = v_all[:, pr * LANES:(pr + 1) * LANES].T.astype(BF16)

    a_all = _bdot(hb, wa_ref[...])
    for j in range(A_TILES):
        tile = a_all[:, j * LANES:(j + 1) * LANES]
        a_ref[0, j] = tile * (A_HEAD_DIM ** -0.5 * LOG2_E) if j < A_PAIRS else tile
    gate_ref[0] = _bdot(hb, wg_ref[...]).astype(BF16)


def _input_stage(x, mod, g_mix, positions, w_in, g_q_lora, w_uq, g_kv_lora, w_ukv):
    batch, seq, _ = x.shape
    tm = MLA_KEY_TILE
    s0 = 3 * A_WIDTH
    s1 = s0 + M_Q_LORA
    s2 = s1 + M_KV_LORA
    s3 = s2 + M_ROPE
    assert s0 % M_Q_LORA == 0 and s1 % M_KV_LORA == 0
    w_all = w_in.astype(BF16)
    w_kr = jnp.pad(w_all[:, s2:s3], ((0, 0), (ROPE_LO, LANES - ROPE_HI)))
    w_g = w_all[:, s3:]

    def columns(width, start):
        return pl.BlockSpec((D_MODEL, width), lambda *_: (0, start // width), pipeline_mode=pl.Buffered(1))
    w_uq_p = jnp.pad(w_uq, ((0, 0), (0, 0), (0, M_HEAD_PAD - M_NOPE - M_ROPE)))
    w_uq_p = w_uq_p.reshape(M_Q_LORA, M_HEADS * M_HEAD_PAD).astype(BF16)
    w_uk_p = jnp.pad(w_ukv[:, :, :M_NOPE], ((0, 0), (0, 0), (0, M_HEAD_PAD - M_NOPE)))
    w_uk_p = w_uk_p.reshape(M_KV_LORA, M_HEADS * M_HEAD_PAD).astype(BF16)
    w_uv = w_ukv[:, :, M_NOPE:].reshape(M_KV_LORA, M_WIDTH).astype(BF16)

    freqs = ROPE_THETA ** (-jnp.arange(ROPE_HALF, dtype=F32) / ROPE_HALF)
    freq_col = jnp.concatenate([freqs, freqs]).reshape(M_ROPE, 1)

    row3 = lambda b, i: (b, i, 0)
    head4 = lambda b, i: (b, 0, i, 0)
    return pl.pallas_call(
        _input_kernel,
        grid=(batch, seq // tm),
        in_specs=[
            pl.BlockSpec((1, tm, D_MODEL), row3),
            pl.BlockSpec((1, N_MOD, D_MODEL), lambda b, i: (b, 0, 0)),
            _resident((1, D_MODEL)),
            pl.BlockSpec((1, 1, 1, tm), lambda b, i: (b, i, 0, 0)),
            _resident((M_ROPE, 1)),
            columns(s0, 0), _resident(w_g.shape), columns(M_Q_LORA, s0), columns(M_KV_LORA, s1),
            _resident(w_kr.shape),
            _resident((1, M_Q_LORA)), _resident(w_uq_p.shape),
            _resident((1, M_KV_LORA)), _resident(w_uk_p.shape), _resident(w_uv.shape),
        ],
        out_specs=[
            pl.BlockSpec((1, A_TILES, tm, LANES), head4),
            pl.BlockSpec((1, tm, 2 * D_MODEL), row3),
            pl.BlockSpec((1, M_HEADS, tm, M_HEAD_PAD), head4),
            pl.BlockSpec((1, M_HEADS, tm, M_HEAD_PAD), head4),
            pl.BlockSpec((1, M_PAIRS, 1, LANES, tm), lambda b, i: (b, 0, i, 0, 0)),
        ],
        out_shape=[
            jax.ShapeDtypeStruct((batch, A_TILES, seq, LANES), F32),
            jax.ShapeDtypeStruct((batch, seq, 2 * D_MODEL), BF16),
            jax.ShapeDtypeStruct((batch, M_HEADS, seq, M_HEAD_PAD), BF16),
            jax.ShapeDtypeStruct((batch, M_HEADS, seq, M_HEAD_PAD), BF16),
            jax.ShapeDtypeStruct((batch, M_PAIRS, seq // tm, LANES, tm), BF16),
        ],
        compiler_params=_params(2),
        name="input_stage",
    )(x, mod, g_mix.reshape(1, D_MODEL), positions.reshape(batch, seq // tm, 1, tm), freq_col,
      w_all, w_g, w_all, w_all, w_kr, g_q_lora.reshape(1, M_Q_LORA), w_uq_p,
      g_kv_lora.reshape(1, M_KV_LORA), w_uk_p, w_uv)


def _mla_kernel(q_ref, k_ref, vt_ref, o_ref, m_scr, acc_scr, sa_scr, sb_scr, max_a_scr, max_b_scr):
    tq, tk = MLA_QUERY_TILE, MLA_KEY_TILE
    qc = MLA_QUERY_CHUNK
    qi = pl.program_id(2)
    contract_last = (((1,), (1,)), ((), ()))
    v_row = lax.broadcasted_iota(jnp.int32, (LANES, tk), 0)
    own_rows = [v_row < M_V, v_row >= M_V]

    m_scr[...] = jnp.full(m_scr.shape, NEG_INF, F32)
    acc_scr[...] = jnp.zeros(acc_scr.shape, F32)

    def block(scored=None, absorbed=None):
        pieces, chains = [], []
        if scored is not None:
            next_tile, next_s, next_max, diagonal = scored
            start = pl.multiple_of(next_tile * tk, tk)
            keys = [k_ref[0, hh, pl.ds(start, tk), :] for hh in range(2)]
            pieces = [(hh, q0) for hh in range(2) for q0 in range(diagonal or 0, tq, qc)]
        if absorbed is not None:
            tile, s_scr, max_scr, q_lo = absorbed
            vt = vt_ref[0, 0, tile]
            vt_aug = [jnp.where(own, vt, jnp.ones_like(vt)) for own in own_rows]
            chains = [(hh, q0) for hh in range(2) for q0 in range(q_lo, tq, qc)]
            state = [(m_scr[hh, :, q0:q0 + qc], acc_scr[hh, :, q0:q0 + qc]) for hh, q0 in chains]

        def score_piece(hh, q0):
            s = lax.dot_general(keys[hh], q_ref[0, hh, q0:q0 + qc, :], contract_last,
                                preferred_element_type=F32)
            if diagonal is not None and q0 < diagonal + tk - 1:
                key_pos = lax.broadcasted_iota(jnp.int32, s.shape, 0) + diagonal
                query_pos = lax.broadcasted_iota(jnp.int32, s.shape, 1) + q0
                s = jnp.where(key_pos <= query_pos, s, NEG_INF)
            next_s[hh, :, q0:q0 + qc] = s
            next_max[hh, :, q0:q0 + qc] = jnp.max(s, axis=0, keepdims=True)

        def absorb_chain(i):
            hh, q0 = chains[i]
            m_prev, acc_prev = state[i]
            m_new = jnp.maximum(m_prev, max_scr[hh, :, q0:q0 + qc])
            p = jnp.exp2(s_scr[hh, :, q0:q0 + qc] - m_new).astype(BF16)
            acc_scr[hh, :, q0:q0 + qc] = jnp.exp2(m_prev - m_new) * acc_prev + _bdot(vt_aug[hh], p)
            m_scr[hh, :, q0:q0 + qc] = m_new

        if pieces:
            score_piece(*pieces[0])
        for i in range(max(len(pieces) - 1, len(chains))):
            if i + 1 < len(pieces):
                score_piece(*pieces[i + 1])
            if i < len(chains):
                absorb_chain(i)

    n_off = 2 * qi
    last_off = jnp.maximum(n_off - 1, 0)
    block(scored=(n_off, sa_scr, max_a_scr, 0))
    block(scored=(n_off + 1, sb_scr, max_b_scr, tk), absorbed=(n_off, sa_scr, max_a_scr, 0))
    block(scored=(0, sa_scr, max_a_scr, None), absorbed=(n_off + 1, sb_scr, max_b_scr, tk))

    def pair(j, carry):
        first = 2 * j
        block(scored=(first + 1, sb_scr, max_b_scr, None), absorbed=(first, sa_scr, max_a_scr, 0))
        block(scored=(jnp.minimum(first + 2, last_off), sa_scr, max_a_scr, None),
              absorbed=(first + 1, sb_scr, max_b_scr, 0))
        return carry

    lax.fori_loop(0, qi, pair, 0)

    o0 = acc_scr[0] * (1.0 / acc_scr[0, M_V:M_V + 1, :])
    o1 = acc_scr[1] * (1.0 / acc_scr[1, 0:1, :])
    out_row = lax.broadcasted_iota(jnp.int32, (LANES, tq), 0)
    o_ref[0] = jnp.where(out_row < M_V, o0, o1).T.astype(BF16)


def _mla_attention(q, k, vt):
    batch, _, seq, _ = q.shape
    tq, tk = MLA_QUERY_TILE, MLA_KEY_TILE
    assert tq == 2 * tk and seq % tq == 0
    return pl.pallas_call(
        _mla_kernel,
        grid=(batch, M_PAIRS, seq // tq),
        in_specs=[
            pl.BlockSpec((1, 2, tq, M_HEAD_PAD), lambda b, p, i: (b, p, i, 0)),
            pl.BlockSpec((1, 2, seq, M_HEAD_PAD), lambda b, p, i: (b, p, 0, 0)),
            pl.BlockSpec((1, 1, seq // tk, LANES, tk), lambda b, p, i: (b, p, 0, 0, 0)),
        ],
        out_specs=pl.BlockSpec((1, tq, LANES), lambda b, p, i: (b, i, p)),
        out_shape=jax.ShapeDtypeStruct((batch, seq, M_WIDTH), BF16),
        scratch_shapes=[
            pltpu.VMEM((2, 1, tq), F32),
            pltpu.VMEM((2, LANES, tq), F32),
            pltpu.VMEM((2, tk, tq), F32),
            pltpu.VMEM((2, tk, tq), F32),
            pltpu.VMEM((2, 1, tq), F32),
            pltpu.VMEM((2, 1, tq), F32),
        ],
        compiler_params=_params(3),
        name="mla_attention",
    )(q, k, vt)


def _t5_bucket_table(dilation, n_back):
    blk = BAND_BLOCK
    sub_dist = (np.arange(blk)[:, None] + blk) - np.arange(2 * blk)[None, :]
    dist = np.clip(sub_dist, 0, n_back) * dilation
    max_exact = REL_BUCKETS // 2
    d = np.maximum(dist, 1).astype(np.float32)
    ratio = np.log(d / np.float32(max_exact)) / np.float32(math.log(REL_MAX_DIST / max_exact))
    log_b = max_exact + (ratio * np.float32(REL_BUCKETS - max_exact)).astype(np.int32)
    log_b = np.minimum(log_b, REL_BUCKETS - 1)
    return np.where(dist < max_exact, dist, log_b).astype(np.int32)


def _rows(start, count, stride):
    return pl.ds(start, count) if stride == 1 else pl.ds(start, count, stride=stride)


def _dilated_kernel(rb_ref, bucket_ref, q_ref, kc_ref, kp_ref, vc_ref, vp_ref, o_ref,
                    bias_scr, acc_scr, m_scr, regroup_scr, out_scr):
    blk = BAND_BLOCK
    sup = SUPER_BLOCK
    grp = DILATED_REGROUP
    sub = sup // grp
    pair = pl.program_id(2)
    first_step = (pl.program_id(0) == 0) & (pl.program_id(1) == 0) & (pair == 0)

    @pl.when(first_step)
    def _build_bias():
        row = lax.broadcasted_iota(jnp.int32, (blk, 2 * blk), 0)
        col = lax.broadcasted_iota(jnp.int32, (blk, 2 * blk), 1)
        sub_dist = row + blk - col
        in_band = (sub_dist >= 0) & (sub_dist <= blk)
        for g in range(len(DILATED_PATTERNS)):
            bucket = bucket_ref[g]
            for hd in range(A_HEADS):
                bias = jnp.zeros((blk, 2 * blk), F32)
                for bk in range(REL_BUCKETS):
                    bias = jnp.where(bucket == bk, rb_ref[hd, bk] * LOG2_E, bias)
                bias_scr[g, hd] = jnp.where(in_band, bias, NEG_INF)

    sources = (q_ref, kc_ref, kp_ref, vc_ref, vp_ref)
    for idx, ref in enumerate(sources):
        for r in range(grp):
            regroup_scr[idx, r * sub:(r + 1) * sub, :] = ref[0, 0, _rows(r, sub, grp), :]

    first_valid_col = jnp.where(pl.program_id(1) > 0, 0, blk)
    col = lax.broadcasted_iota(jnp.int32, (blk, 2 * blk), 1)
    lane = lax.broadcasted_iota(jnp.int32, (blk, LANES), 1)
    lane2 = lax.broadcasted_iota(jnp.int32, (2 * blk, LANES), 1)
    contract_last = (((1,), (1,)), ((), ()))
    n_pat = len(DILATED_PATTERNS)

    for g, (_, dil) in enumerate(DILATED_PATTERNS):
        regrouped = dil % grp == 0
        step = dil // grp if regrouped else dil
        for res in range(dil):
            base = (res % grp) * sub + res // grp if regrouped else res
            prev_base = base + (sub if regrouped else sup) - blk * step
            for n in range(sup // (blk * dil)):
                rows = _rows(base + blk * step * n, blk, step)
                if n == 0:
                    prev_rows, k_idx, v_idx = _rows(prev_base, blk, step), 2, 4
                else:
                    prev_rows, k_idx, v_idx = _rows(base + blk * step * (n - 1), blk, step), 1, 3
                if regrouped:
                    load = lambda idx, r: regroup_scr[idx, r, :]
                else:
                    load = lambda idx, r: sources[idx][0, 0, r, :]
                q = load(0, rows)
                k2 = jnp.concatenate([load(k_idx, prev_rows), load(1, rows)], axis=0).astype(BF16)
                v2 = jnp.concatenate([load(v_idx, prev_rows), load(3, rows)], axis=0)
                for hh in range(2):
                    in_head = (lane < A_HEAD_DIM) if hh == 0 else (lane >= A_HEAD_DIM)
                    in_head2 = (lane2 < A_HEAD_DIM) if hh == 0 else (lane2 >= A_HEAD_DIM)
                    s = lax.dot_general(jnp.where(in_head, q, 0.0).astype(BF16), k2, contract_last,
                                        preferred_element_type=F32)
                    bias = bias_scr[g, 2 * pair + hh]
                    if n == 0:
                        bias = jnp.where(col >= first_valid_col, bias, NEG_INF)
                    s = s + bias
                    m_blk = jnp.max(s, axis=-1, keepdims=True)
                    p = jnp.exp2(s - m_blk).astype(BF16)
                    acc_scr[g, hh, rows, :] = _bdot(p, jnp.where(in_head2, v2, 1.0).astype(BF16))
                    m_scr[g, hh, rows, :] = jnp.broadcast_to(m_blk, (blk, LANES))

    lane_sub = lax.broadcasted_iota(jnp.int32, (sub, LANES), 1)
    for r in range(grp):
        chunk = [pl.ds(r * sub, sub) if dil % grp == 0 else _rows(r, sub, grp) for _, dil in DILATED_PATTERNS]
        halves = []
        for hh in range(2):
            maxes = [m_scr[g, hh, chunk[g], :] for g in range(n_pat)]
            top = maxes[0]
            for g in range(1, n_pat):
                top = jnp.maximum(top, maxes[g])
            total = jnp.exp2(maxes[0] - top) * acc_scr[0, hh, chunk[0], :]
            for g in range(1, n_pat):
                total = total + jnp.exp2(maxes[g] - top) * acc_scr[g, hh, chunk[g], :]
            halves.append(total * (1.0 / pltpu.roll(total, A_HEAD_DIM, 1)))
        out_scr[_rows(r, sub, grp), :] = jnp.where(lane_sub < A_HEAD_DIM, halves[0], halves[1])
    o_ref[0] = out_scr[...].astype(BF16)


def _dilated_attention(a_qkv, rel_bias):
    batch, _, seq, _ = a_qkv.shape
    blk = BAND_BLOCK
    sup = SUPER_BLOCK
    n_pat = len(DILATED_PATTERNS)
    assert all(w // d == blk for w, d in DILATED_PATTERNS), "band of exactly one block behind the query"
    assert seq % sup == 0
    bucket = jnp.asarray(np.stack([_t5_bucket_table(d, w // d) for w, d in DILATED_PATTERNS]))

    def part(which, prev):
        def index(b, s, p):
            return (b, which * A_PAIRS + p, jnp.maximum(s - 1, 0) if prev else s, 0)
        return pl.BlockSpec((1, 1, sup, LANES), index)

    return pl.pallas_call(
        _dilated_kernel,
        grid=(batch, seq // sup, A_PAIRS),
        in_specs=[
            pl.BlockSpec(memory_space=pltpu.SMEM),
            pl.BlockSpec((n_pat, blk, 2 * blk), lambda b, s, p: (0, 0, 0)),
            part(0, False), part(1, False), part(1, True), part(2, False), part(2, True),
        ],
        out_specs=pl.BlockSpec((1, sup, LANES), lambda b, s, p: (b, s, p)),
        out_shape=jax.ShapeDtypeStruct((batch, seq, A_WIDTH), BF16),
        scratch_shapes=[
            pltpu.VMEM((n_pat, A_HEADS, blk, 2 * blk), F32),
            pltpu.VMEM((n_pat, 2, sup, LANES), F32),
            pltpu.VMEM((n_pat, 2, sup, LANES), F32),
            pltpu.VMEM((5, sup, LANES), F32),
            pltpu.VMEM((sup, LANES), F32),
        ],
        compiler_params=_params(3),
        name="dilated_attention",
    )(rel_bias, bucket, a_qkv, a_qkv, a_qkv, a_qkv, a_qkv)


def _merge_kernel(x_ref, mod_ref, oa_ref, ob_ref, gate_ref, wa_ref, wb_ref, wo_ref, out_ref):
    y_a = _bdot(oa_ref[0], wa_ref[...])
    y_b = _bdot(ob_ref[0], wb_ref[...])
    gates = gate_ref[0].astype(F32)
    merged = _sigmoid(gates[:, :D_MODEL]) * y_a + _sigmoid(gates[:, D_MODEL:]) * y_b
    mixed = _bdot(merged.astype(BF16), wo_ref[...])
    out_ref[0] = x_ref[0] + mod_ref[0, 2:3, :] * mixed


def _merge_stage(x, mod, o_a, o_b, gates, w_up_a, w_up_b, w_o):
    batch, seq, _ = x.shape
    tm = ROW_TILE
    row3 = lambda b, i: (b, i, 0)
    half = pl.BlockSpec((1, tm, A_WIDTH), row3)
    return pl.pallas_call(
        _merge_kernel,
        grid=(batch, seq // tm),
        in_specs=[
            pl.BlockSpec((1, tm, D_MODEL), row3),
            pl.BlockSpec((1, N_MOD, D_MODEL), lambda b, i: (b, 0, 0)),
            half, half,
            pl.BlockSpec((1, tm, 2 * D_MODEL), row3),
            _resident((A_WIDTH, D_MODEL)), _resident((M_WIDTH, D_MODEL)), _resident((D_MODEL, D_MODEL)),
        ],
        out_specs=pl.BlockSpec((1, tm, D_MODEL), row3),
        out_shape=jax.ShapeDtypeStruct((batch, seq, D_MODEL), F32),
        compiler_params=_params(2),
        name="merge_stage",
    )(x, mod, o_a, o_b, gates,
      w_up_a.astype(BF16), w_up_b.astype(BF16), w_o.astype(BF16))


def _ffn_kernel(x_ref, mod_ref, g_ref, gf_ref, wg_ref, wu_ref, wd_ref, out_ref):
    x = x_ref[0]
    h = (_rms(x) * g_ref[...]) * (1.0 + mod_ref[0, 4:5, :]) + mod_ref[0, 3:4, :]
    hb = h.astype(BF16)
    gate = _bdot(hb, wg_ref[...])
    up = _bdot(hb, wu_ref[...])
    act = (gate * _sigmoid(gate) * up).astype(BF16)
    y = x + mod_ref[0, 5:6, :] * _bdot(act, wd_ref[...])
    out_ref[0] = _rms(y) * gf_ref[...]


def _ffn_stage(x, mod, g_ffn, g_final, w_gate, w_up, w_down):
    batch, seq, _ = x.shape
    tm = ROW_TILE
    row3 = lambda b, i: (b, i, 0)
    return pl.pallas_call(
        _ffn_kernel,
        grid=(batch, seq // tm),
        in_specs=[
            pl.BlockSpec((1, tm, D_MODEL), row3),
            pl.BlockSpec((1, N_MOD, D_MODEL), lambda b, i: (b, 0, 0)),
            _resident((1, D_MODEL)), _resident((1, D_MODEL)),
            _resident((D_MODEL, D_FF)), _resident((D_MODEL, D_FF)), _resident((D_FF, D_MODEL)),
        ],
        out_specs=pl.BlockSpec((1, tm, D_MODEL), row3),
        out_shape=jax.ShapeDtypeStruct((batch, seq, D_MODEL), F32),
        compiler_params=_params(2),
        name="ffn_stage",
    )(x, mod, g_ffn.reshape(1, D_MODEL), g_final.reshape(1, D_MODEL),
      w_gate.astype(BF16), w_up.astype(BF16), w_down.astype(BF16))


def kernel(x, c, positions, rel_bias, w_ada, b_ada, g_mix, w_in, g_q_lora, w_uq, g_kv_lora, w_ukv,
           w_up_a, w_up_b, w_o, g_ffn, w_gate, w_up, w_down, g_final):
    assert w_ada.shape[0] == 1, "single-layer trunk"
    mod = _modulation(c, w_ada[0], b_ada[0])
    a_qkv, gates, q, k, vt = _input_stage(x, mod, g_mix[0], positions, w_in[0], g_q_lora[0], w_uq[0],
                                         g_kv_lora[0], w_ukv[0])
    o_b = _mla_attention(q, k, vt)
    o_a = _dilated_attention(a_qkv, rel_bias)
    x1 = _merge_stage(x, mod, o_a, o_b, gates, w_up_a[0], w_up_b[0], w_o[0])
    return _ffn_stage(x1, mod, g_ffn[0], g_final, w_gate[0], w_up[0], w_down[0])
```

```python
import math

import jax
import jax.numpy as jnp
import numpy as np
from jax import lax
from jax.experimental import pallas as pl
from jax.experimental.pallas import tpu as pltpu

D_MODEL = 1024
A_HEADS = 8
A_HEAD_DIM = 64
A_WIDTH = A_HEADS * A_HEAD_DIM
DILATED_PATTERNS = ((128, 1), (512, 4), (2048, 16))
BAND_BLOCK = 128
REL_BUCKETS = 32
REL_MAX_DIST = 2048
M_HEADS = 8
M_NOPE = 64
M_ROPE = 32
M_V = 64
M_Q_LORA = 768
M_KV_LORA = 256
M_WIDTH = M_HEADS * M_V
ROPE_THETA = 10000.0
D_FF = -(-8 * D_MODEL // (3 * 256)) * 256
N_MOD = 6
EPS = 1e-6
NEG_INF = -1e30

LANES = 128
SUBLANES = 8
V7X_VMEM_BYTES = 64 * 1024 * 1024
VMEM_LIMIT_BYTES = V7X_VMEM_BYTES - 8 * 1024 * 1024

M_HEAD_PAD = LANES
M_PAIRS = M_HEADS // 2
ROPE_HALF = M_ROPE // 2
ROPE_LO = M_NOPE
ROPE_MID = M_NOPE + ROPE_HALF
ROPE_HI = M_NOPE + M_ROPE

A_PAIRS = A_HEADS // 2
A_TILES = 3 * A_PAIRS
SUPER_BLOCK = BAND_BLOCK * max(d for _, d in DILATED_PATTERNS)
DILATED_REGROUP = 4
LOG2_E = math.log2(math.e)

ROW_TILE = 512
MOD_COLUMN_BLOCK = D_MODEL
MLA_KEY_TILE = 512
MLA_QUERY_TILE = 2 * MLA_KEY_TILE
MLA_QUERY_CHUNK = 256

F32 = jnp.float32
BF16 = jnp.bfloat16


def _params(n_axes, flags=None):
    return pltpu.CompilerParams(
        dimension_semantics=("arbitrary",) * n_axes,
        vmem_limit_bytes=VMEM_LIMIT_BYTES,
        flags=flags,
    )


def _resident(shape):
    zeros = (0,) * len(shape)
    return pl.BlockSpec(shape, lambda *_: zeros, pipeline_mode=pl.Buffered(1))


def _bdot(a, b):
    return jnp.dot(a, b, preferred_element_type=F32)


def _rms(x):
    return x * lax.rsqrt(jnp.mean(x * x, axis=-1, keepdims=True) + EPS)


def _sigmoid(x):
    return 1.0 / (1.0 + jnp.exp(-x))


def _mod_kernel(c_ref, w_ref, b_ref, o_ref):
    c = c_ref[...]
    cond = c * _sigmoid(c)
    o_ref[...] = (
        jnp.dot(cond, w_ref[...], preferred_element_type=F32, precision=lax.Precision.HIGHEST)
        + b_ref[...]
    )


def _modulation(c, w_ada, b_ada):
    batch = c.shape[0]
    rows = -(-batch // SUBLANES) * SUBLANES
    c_pad = jnp.pad(c, ((0, rows - batch), (0, 0)))
    cols = MOD_COLUMN_BLOCK
    out = pl.pallas_call(
        _mod_kernel,
        grid=(N_MOD * D_MODEL // cols,),
        in_specs=[
            pl.BlockSpec((rows, D_MODEL), lambda j: (0, 0)),
            pl.BlockSpec((D_MODEL, cols), lambda j: (0, j)),
            pl.BlockSpec((1, cols), lambda j: (0, j)),
        ],
        out_specs=pl.BlockSpec((rows, cols), lambda j: (0, j)),
        out_shape=jax.ShapeDtypeStruct((rows, N_MOD * D_MODEL), F32),
        compiler_params=_params(1),
        name="adaln_mod",
    )(c_pad, w_ada, b_ada.reshape(1, N_MOD * D_MODEL))
    return out[:batch].reshape(batch, N_MOD, D_MODEL)


def _rope_lanes(x, cos, signed_sin):
    lane = lax.broadcasted_iota(jnp.int32, x.shape, 1)
    partner = jnp.where(lane < ROPE_MID, pltpu.roll(x, LANES - ROPE_HALF, 1), pltpu.roll(x, ROPE_HALF, 1))
    return x * cos + partner * signed_sin


def _input_kernel(x_ref, mod_ref, g_ref, pos_ref, freq_ref, wa_ref, wg_ref, wcq_ref, wckv_ref, wkr_ref,
                  gq_ref, wuq_ref, gkv_ref, wuk_ref, wuv_ref,
                  a_ref, gate_ref, q_ref, k_ref, v_ref):
    x = x_ref[0]
    shift = mod_ref[0, 0:1, :]
    scale = mod_ref[0, 1:2, :]
    h = (_rms(x) * g_ref[...]) * (1.0 + scale) + shift
    hb = h.astype(BF16)

    c_q = _bdot(hb, wcq_ref[...])
    c_kv = _bdot(hb, wckv_ref[...])
    k_r = _bdot(hb, wkr_ref[...])

    ang = freq_ref[...] * pos_ref[0, 0].astype(F32)
    cos_r, sin_r = jnp.cos(ang), jnp.sin(ang)
    rows = ang.shape[1]
    cos = jnp.concatenate(
        [jnp.ones((ROPE_LO, rows), F32), cos_r, jnp.ones((LANES - ROPE_HI, rows), F32)], axis=0).T
    signed_sin = jnp.concatenate(
        [jnp.zeros((ROPE_LO, rows), F32), -sin_r[:ROPE_HALF], sin_r[ROPE_HALF:],
         jnp.zeros((LANES - ROPE_HI, rows), F32)], axis=0).T

    q_all = _bdot((_rms(c_q) * gq_ref[...]).astype(BF16), wuq_ref[...])
    q_scale = (M_NOPE + M_ROPE) ** -0.5 * LOG2_E
    for hd in range(M_HEADS):
        q_h = q_all[:, hd * M_HEAD_PAD:(hd + 1) * M_HEAD_PAD]
        q_ref[0, hd] = (_rope_lanes(q_h, cos, signed_sin) * q_scale).astype(BF16)

    c_kv = (_rms(c_kv) * gkv_ref[...]).astype(BF16)
    k_rope = _rope_lanes(k_r, cos, signed_sin)
    k_all = _bdot(c_kv, wuk_ref[...])
    for hd in range(M_HEADS):
        k_ref[0, hd] = (k_all[:, hd * M_HEAD_PAD:(hd + 1) * M_HEAD_PAD] + k_rope).astype(BF16)
    v_all = _bdot(c_kv, wuv_ref[...])
    for pr in range(M_PAIRS):
        v_ref[0, pr, 0] = v_all[:, pr * LANES:(pr + 1) * LANES].T.astype(BF16)

    a_all = _bdot(hb, wa_ref[...])
    for j in range(A_TILES):
        tile = a_all[:, j * LANES:(j + 1) * LANES]
        a_ref[0, j] = tile * (A_HEAD_DIM ** -0.5 * LOG2_E) if j < A_PAIRS else tile
    gate_ref[0] = _bdot(hb, wg_ref[...]).astype(BF16)


def _input_stage(x, mod, g_mix, positions, w_in, g_q_lora, w_uq, g_kv_lora, w_ukv):
    batch, seq, _ = x.shape
    tm = MLA_KEY_TILE
    s0 = 3 * A_WIDTH
    s1 = s0 + M_Q_LORA
    s2 = s1 + M_KV_LORA
    s3 = s2 + M_ROPE
    assert s0 % M_Q_LORA == 0 and s1 % M_KV_LORA == 0
    w_all = w_in.astype(BF16)
    w_kr = jnp.pad(w_all[:, s2:s3], ((0, 0), (ROPE_LO, LANES - ROPE_HI)))
    w_g = w_all[:, s3:]

    def columns(width, start):
        return pl.BlockSpec((D_MODEL, width), lambda *_: (0, start // width), pipeline_mode=pl.Buffered(1))
    w_uq_p = jnp.pad(w_uq, ((0, 0), (0, 0), (0, M_HEAD_PAD - M_NOPE - M_ROPE)))
    w_uq_p = w_uq_p.reshape(M_Q_LORA, M_HEADS * M_HEAD_PAD).astype(BF16)
    w_uk_p = jnp.pad(w_ukv[:, :, :M_NOPE], ((0, 0), (0, 0), (0, M_HEAD_PAD - M_NOPE)))
    w_uk_p = w_uk_p.reshape(M_KV_LORA, M_HEADS * M_HEAD_PAD).astype(BF16)
    w_uv = w_ukv[:, :, M_NOPE:].reshape(M_KV_LORA, M_WIDTH).astype(BF16)

    freqs = ROPE_THETA ** (-jnp.arange(ROPE_HALF, dtype=F32) / ROPE_HALF)
    freq_col = jnp.concatenate([freqs, freqs]).reshape(M_ROPE, 1)

    row3 = lambda b, i: (b, i, 0)
    head4 = lambda b, i: (b, 0, i, 0)
    return pl.pallas_call(
        _input_kernel,
        grid=(batch, seq // tm),
        in_specs=[
            pl.BlockSpec((1, tm, D_MODEL), row3),
            pl.BlockSpec((1, N_MOD, D_MODEL), lambda b, i: (b, 0, 0)),
            _resident((1, D_MODEL)),
            pl.BlockSpec((1, 1, 1, tm), lambda b, i: (b, i, 0, 0)),
            _resident((M_ROPE, 1)),
            columns(s0, 0), _resident(w_g.shape), columns(M_Q_LORA, s0), columns(M_KV_LORA, s1),
            _resident(w_kr.shape),
            _resident((1, M_Q_LORA)), _resident(w_uq_p.shape),
            _resident((1, M_KV_LORA)), _resident(w_uk_p.shape), _resident(w_uv.shape),
        ],
        out_specs=[
            pl.BlockSpec((1, A_TILES, tm, LANES), head4),
            pl.BlockSpec((1, tm, 2 * D_MODEL), row3),
            pl.BlockSpec((1, M_HEADS, tm, M_HEAD_PAD), head4),
            pl.BlockSpec((1, M_HEADS, tm, M_HEAD_PAD), head4),
            pl.BlockSpec((1, M_PAIRS, 1, LANES, tm), lambda b, i: (b, 0, i, 0, 0)),
        ],
        out_shape=[
            jax.ShapeDtypeStruct((batch, A_TILES, seq, LANES), F32),
            jax.ShapeDtypeStruct((batch, seq, 2 * D_MODEL), BF16),
            jax.ShapeDtypeStruct((batch, M_HEADS, seq, M_HEAD_PAD), BF16),
            jax.ShapeDtypeStruct((batch, M_HEADS, seq, M_HEAD_PAD), BF16),
            jax.ShapeDtypeStruct((batch, M_PAIRS, seq // tm, LANES, tm), BF16),
        ],
        compiler_params=_params(2),
        name="input_stage",
    )(x, mod, g_mix.reshape(1, D_MODEL), positions.reshape(batch, seq // tm, 1, tm), freq_col,
      w_all, w_g, w_all, w_all, w_kr, g_q_lora.reshape(1, M_Q_LORA), w_uq_p,
      g_kv_lora.reshape(1, M_KV_LORA), w_uk_p, w_uv)


def _mla_kernel(q_ref, k_ref, vt_ref, o_ref, m_scr, acc_scr, sa_scr, sb_scr, max_a_scr, max_b_scr):
    tq, tk = MLA_QUERY_TILE, MLA_KEY_TILE
    qc = MLA_QUERY_CHUNK
    qi = pl.program_id(2)
    contract_last = (((1,), (1,)), ((), ()))
    v_row = lax.broadcasted_iota(jnp.int32, (LANES, tk), 0)
    own_rows = [v_row < M_V, v_row >= M_V]

    m_scr[...] = jnp.full(m_scr.shape, NEG_INF, F32)
    acc_scr[...] = jnp.zeros(acc_scr.shape, F32)

    def block(scored=None, absorbed=None):
        pieces, chains = [], []
        if scored is not None:
            next_tile, next_s, next_max, diagonal = scored
            start = pl.multiple_of(next_tile * tk, tk)
            keys = [k_ref[0, hh, pl.ds(start, tk), :] for hh in range(2)]
            pieces = [(hh, q0) for hh in range(2) for q0 in range(diagonal or 0, tq, qc)]
        if absorbed is not None:
            tile, s_scr, max_scr, q_lo = absorbed
            vt = vt_ref[0, 0, tile]
            vt_aug = [jnp.where(own, vt, jnp.ones_like(vt)) for own in own_rows]
            chains = [(hh, q0) for hh in range(2) for q0 in range(q_lo, tq, qc)]
            state = [(m_scr[hh, :, q0:q0 + qc], acc_scr[hh, :, q0:q0 + qc]) for hh, q0 in chains]

        def score_piece(hh, q0):
            s = lax.dot_general(keys[hh], q_ref[0, hh, q0:q0 + qc, :], contract_last,
                                preferred_element_type=F32)
            if diagonal is not None and q0 < diagonal + tk - 1:
                key_pos = lax.broadcasted_iota(jnp.int32, s.shape, 0) + diagonal
                query_pos = lax.broadcasted_iota(jnp.int32, s.shape, 1) + q0
                s = jnp.where(key_pos <= query_pos, s, NEG_INF)
            next_s[hh, :, q0:q0 + qc] = s
            next_max[hh, :, q0:q0 + qc] = jnp.max(s, axis=0, keepdims=True)

        def absorb_chain(i):
            hh, q0 = chains[i]
            m_prev, acc_prev = state[i]
            m_new = jnp.maximum(m_prev, max_scr[hh, :, q0:q0 + qc])
            p = jnp.exp2(s_scr[hh, :, q0:q0 + qc] - m_new).astype(BF16)
            acc_scr[hh, :, q0:q0 + qc] = jnp.exp2(m_prev - m_new) * acc_prev + _bdot(vt_aug[hh], p)
            m_scr[hh, :, q0:q0 + qc] = m_new

        if pieces:
            score_piece(*pieces[0])
        for i in range(max(len(pieces) - 1, len(chains))):
            if i + 1 < len(pieces):
                score_piece(*pieces[i + 1])
            if i < len(chains):
                absorb_chain(i)

    n_off = 2 * qi
    last_off = jnp.maximum(n_off - 1, 0)
    block(scored=(n_off, sa_scr, max_a_scr, 0))
    block(scored=(n_off + 1, sb_scr, max_b_scr, tk), absorbed=(n_off, sa_scr, max_a_scr, 0))
    block(scored=(0, sa_scr, max_a_scr, None), absorbed=(n_off + 1, sb_scr, max_b_scr, tk))

    def pair(j):
        first = 2 * j
        block(scored=(first + 1, sb_scr, max_b_scr, None), absorbed=(first, sa_scr, max_a_scr, 0))
        block(scored=(jnp.minimum(first + 2, last_off), sa_scr, max_a_scr, None),
              absorbed=(first + 1, sb_scr, max_b_scr, 0))

    def two_pairs(j, carry):
        pair(2 * j)
        pair(2 * j + 1)
        return carry

    lax.fori_loop(0, qi // 2, two_pairs, 0)

    @pl.when(qi % 2 == 1)
    def _odd_pair():
        pair(qi - 1)

    o0 = acc_scr[0] * (1.0 / acc_scr[0, M_V:M_V + 1, :])
    o1 = acc_scr[1] * (1.0 / acc_scr[1, 0:1, :])
    out_row = lax.broadcasted_iota(jnp.int32, (LANES, tq), 0)
    o_ref[0] = jnp.where(out_row < M_V, o0, o1).T.astype(BF16)


def _mla_attention(q, k, vt):
    batch, _, seq, _ = q.shape
    tq, tk = MLA_QUERY_TILE, MLA_KEY_TILE
    assert tq == 2 * tk and seq % tq == 0
    return pl.pallas_call(
        _mla_kernel,
        grid=(batch, M_PAIRS, seq // tq),
        in_specs=[
            pl.BlockSpec((1, 2, tq, M_HEAD_PAD), lambda b, p, i: (b, p, i, 0)),
            pl.BlockSpec((1, 2, seq, M_HEAD_PAD), lambda b, p, i: (b, p, 0, 0)),
            pl.BlockSpec((1, 1, seq // tk, LANES, tk), lambda b, p, i: (b, p, 0, 0, 0)),
        ],
        out_specs=pl.BlockSpec((1, tq, LANES), lambda b, p, i: (b, i, p)),
        out_shape=jax.ShapeDtypeStruct((batch, seq, M_WIDTH), BF16),
        scratch_shapes=[
            pltpu.VMEM((2, 1, tq), F32),
            pltpu.VMEM((2, LANES, tq), F32),
            pltpu.VMEM((2, tk, tq), F32),
            pltpu.VMEM((2, tk, tq), F32),
            pltpu.VMEM((2, 1, tq), F32),
            pltpu.VMEM((2, 1, tq), F32),
        ],
        compiler_params=_params(3),
        name="mla_attention",
    )(q, k, vt)


def _t5_bucket_table(dilation, n_back):
    blk = BAND_BLOCK
    sub_dist = (np.arange(blk)[:, None] + blk) - np.arange(2 * blk)[None, :]
    dist = np.clip(sub_dist, 0, n_back) * dilation
    max_exact = REL_BUCKETS // 2
    d = np.maximum(dist, 1).astype(np.float32)
    ratio = np.log(d / np.float32(max_exact)) / np.float32(math.log(REL_MAX_DIST / max_exact))
    log_b = max_exact + (ratio * np.float32(REL_BUCKETS - max_exact)).astype(np.int32)
    log_b = np.minimum(log_b, REL_BUCKETS - 1)
    return np.where(dist < max_exact, dist, log_b).astype(np.int32)


def _rows(start, count, stride):
    return pl.ds(start, count) if stride == 1 else pl.ds(start, count, stride=stride)


def _dilated_kernel(rb_ref, bucket_ref, q_ref, kc_ref, kp_ref, vc_ref, vp_ref, o_ref,
                    bias_scr, acc_scr, m_scr, regroup_scr, out_scr):
    blk = BAND_BLOCK
    sup = SUPER_BLOCK
    grp = DILATED_REGROUP
    sub = sup // grp
    pair = pl.program_id(2)
    first_step = (pl.program_id(0) == 0) & (pl.program_id(1) == 0) & (pair == 0)

    @pl.when(first_step)
    def _build_bias():
        row = lax.broadcasted_iota(jnp.int32, (blk, 2 * blk), 0)
        col = lax.broadcasted_iota(jnp.int32, (blk, 2 * blk), 1)
        sub_dist = row + blk - col
        in_band = (sub_dist >= 0) & (sub_dist <= blk)
        for g in range(len(DILATED_PATTERNS)):
            bucket = bucket_ref[g]
            for hd in range(A_HEADS):
                bias = jnp.zeros((blk, 2 * blk), F32)
                for bk in range(REL_BUCKETS):
                    bias = jnp.where(bucket == bk, rb_ref[hd, bk] * LOG2_E, bias)
                bias_scr[g, hd] = jnp.where(in_band, bias, NEG_INF)

    sources = (q_ref, kc_ref, kp_ref, vc_ref, vp_ref)
    for idx, ref in enumerate(sources):
        for r in range(grp):
            regroup_scr[idx, r * sub:(r + 1) * sub, :] = ref[0, 0, _rows(r, sub, grp), :]

    first_valid_col = jnp.where(pl.program_id(1) > 0, 0, blk)
    col = lax.broadcasted_iota(jnp.int32, (blk, 2 * blk), 1)
    lane = lax.broadcasted_iota(jnp.int32, (blk, LANES), 1)
    lane2 = lax.broadcasted_iota(jnp.int32, (2 * blk, LANES), 1)
    contract_last = (((1,), (1,)), ((), ()))
    n_pat = len(DILATED_PATTERNS)

    for g, (_, dil) in enumerate(DILATED_PATTERNS):
        regrouped = dil % grp == 0
        step = dil // grp if regrouped else dil
        for res in range(dil):
            base = (res % grp) * sub + res // grp if regrouped else res
            prev_base = base + (sub if regrouped else sup) - blk * step
            for n in range(sup // (blk * dil)):
                rows = _rows(base + blk * step * n, blk, step)
                if n == 0:
                    prev_rows, k_idx, v_idx = _rows(prev_base, blk, step), 2, 4
                else:
                    prev_rows, k_idx, v_idx = _rows(base + blk * step * (n - 1), blk, step), 1, 3
                if regrouped:
                    load = lambda idx, r: regroup_scr[idx, r, :]
                else:
                    load = lambda idx, r: sources[idx][0, 0, r, :]
                q = load(0, rows).astype(BF16)
                k2 = jnp.concatenate([load(k_idx, prev_rows), load(1, rows)], axis=0).astype(BF16)
                v2 = jnp.concatenate([load(v_idx, prev_rows), load(3, rows)], axis=0).astype(BF16)
                for hh in range(2):
                    in_head = (lane < A_HEAD_DIM) if hh == 0 else (lane >= A_HEAD_DIM)
                    in_head2 = (lane2 < A_HEAD_DIM) if hh == 0 else (lane2 >= A_HEAD_DIM)
                    s = lax.dot_general(jnp.where(in_head, q, jnp.zeros_like(q)), k2, contract_last,
                                        preferred_element_type=F32)
                    bias = bias_scr[g, 2 * pair + hh]
                    if n == 0:
                        bias = jnp.where(col >= first_valid_col, bias, NEG_INF)
                    s = s + bias
                    m_blk = jnp.max(s, axis=-1, keepdims=True)
                    p = jnp.exp2(s - m_blk).astype(BF16)
                    acc_scr[g, hh, rows, :] = _bdot(p, jnp.where(in_head2, v2, jnp.ones_like(v2)))
                    m_scr[g, hh, rows, :] = jnp.broadcast_to(m_blk, (blk, LANES))

    lane_sub = lax.broadcasted_iota(jnp.int32, (sub, LANES), 1)
    for r in range(grp):
        chunk = [pl.ds(r * sub, sub) if dil % grp == 0 else _rows(r, sub, grp) for _, dil in DILATED_PATTERNS]
        halves = []
        for hh in range(2):
            maxes = [m_scr[g, hh, chunk[g], :] for g in range(n_pat)]
            top = maxes[0]
            for g in range(1, n_pat):
                top = jnp.maximum(top, maxes[g])
            total = jnp.exp2(maxes[0] - top) * acc_scr[0, hh, chunk[0], :]
            for g in range(1, n_pat):
                total = total + jnp.exp2(maxes[g] - top) * acc_scr[g, hh, chunk[g], :]
            halves.append(total * (1.0 / pltpu.roll(total, A_HEAD_DIM, 1)))
        out_scr[_rows(r, sub, grp), :] = jnp.where(lane_sub < A_HEAD_DIM, halves[0], halves[1])
    o_ref[0] = out_scr[...].astype(BF16)


def _dilated_attention(a_qkv, rel_bias):
    batch, _, seq, _ = a_qkv.shape
    blk = BAND_BLOCK
    sup = SUPER_BLOCK
    n_pat = len(DILATED_PATTERNS)
    assert all(w // d == blk for w, d in DILATED_PATTERNS), "band of exactly one block behind the query"
    assert seq % sup == 0
    bucket = jnp.asarray(np.stack([_t5_bucket_table(d, w // d) for w, d in DILATED_PATTERNS]))

    def part(which, prev):
        def index(b, s, p):
            return (b, which * A_PAIRS + p, jnp.maximum(s - 1, 0) if prev else s, 0)
        return pl.BlockSpec((1, 1, sup, LANES), index)

    return pl.pallas_call(
        _dilated_kernel,
        grid=(batch, seq // sup, A_PAIRS),
        in_specs=[
            pl.BlockSpec(memory_space=pltpu.SMEM),
            pl.BlockSpec((n_pat, blk, 2 * blk), lambda b, s, p: (0, 0, 0)),
            part(0, False), part(1, False), part(1, True), part(2, False), part(2, True),
        ],
        out_specs=pl.BlockSpec((1, sup, LANES), lambda b, s, p: (b, s, p)),
        out_shape=jax.ShapeDtypeStruct((batch, seq, A_WIDTH), BF16),
        scratch_shapes=[
            pltpu.VMEM((n_pat, A_HEADS, blk, 2 * blk), F32),
            pltpu.VMEM((n_pat, 2, sup, LANES), F32),
            pltpu.VMEM((n_pat, 2, sup, LANES), F32),
            pltpu.VMEM((5, sup, LANES), F32),
            pltpu.VMEM((sup, LANES), F32),
        ],
        compiler_params=_params(3),
        name="dilated_attention",
    )(rel_bias, bucket, a_qkv, a_qkv, a_qkv, a_qkv, a_qkv)


def _merge_kernel(x_ref, mod_ref, oa_ref, ob_ref, gate_ref, wa_ref, wb_ref, wo_ref, out_ref):
    y_a = _bdot(oa_ref[0], wa_ref[...])
    y_b = _bdot(ob_ref[0], wb_ref[...])
    gates = gate_ref[0].astype(F32)
    merged = _sigmoid(gates[:, :D_MODEL]) * y_a + _sigmoid(gates[:, D_MODEL:]) * y_b
    mixed = _bdot(merged.astype(BF16), wo_ref[...])
    out_ref[0] = x_ref[0] + mod_ref[0, 2:3, :] * mixed


def _merge_stage(x, mod, o_a, o_b, gates, w_up_a, w_up_b, w_o):
    batch, seq, _ = x.shape
    tm = ROW_TILE
    row3 = lambda b, i: (b, i, 0)
    half = pl.BlockSpec((1, tm, A_WIDTH), row3)
    return pl.pallas_call(
        _merge_kernel,
        grid=(batch, seq // tm),
        in_specs=[
            pl.BlockSpec((1, tm, D_MODEL), row3),
            pl.BlockSpec((1, N_MOD, D_MODEL), lambda b, i: (b, 0, 0)),
            half, half,
            pl.BlockSpec((1, tm, 2 * D_MODEL), row3),
            _resident((A_WIDTH, D_MODEL)), _resident((M_WIDTH, D_MODEL)), _resident((D_MODEL, D_MODEL)),
        ],
        out_specs=pl.BlockSpec((1, tm, D_MODEL), row3),
        out_shape=jax.ShapeDtypeStruct((batch, seq, D_MODEL), F32),
        compiler_params=_params(2),
        name="merge_stage",
    )(x, mod, o_a, o_b, gates,
      w_up_a.astype(BF16), w_up_b.astype(BF16), w_o.astype(BF16))


def _ffn_kernel(x_ref, mod_ref, g_ref, gf_ref, wg_ref, wu_ref, wd_ref, out_ref):
    x = x_ref[0]
    h = (_rms(x) * g_ref[...]) * (1.0 + mod_ref[0, 4:5, :]) + mod_ref[0, 3:4, :]
    hb = h.astype(BF16)
    gate = _bdot(hb, wg_ref[...])
    up = _bdot(hb, wu_ref[...])
    act = (gate * _sigmoid(gate) * up).astype(BF16)
    y = x + mod_ref[0, 5:6, :] * _bdot(act, wd_ref[...])
    out_ref[0] = _rms(y) * gf_ref[...]


def _ffn_stage(x, mod, g_ffn, g_final, w_gate, w_up, w_down):
    batch, seq, _ = x.shape
    tm = ROW_TILE
    row3 = lambda b, i: (b, i, 0)
    return pl.pallas_call(
        _ffn_kernel,
        grid=(batch, seq // tm),
        in_specs=[
            pl.BlockSpec((1, tm, D_MODEL), row3),
            pl.BlockSpec((1, N_MOD, D_MODEL), lambda b, i: (b, 0, 0)),
            _resident((1, D_MODEL)), _resident((1, D_MODEL)),
            _resident((D_MODEL, D_FF)), _resident((D_MODEL, D_FF)), _resident((D_FF, D_MODEL)),
        ],
        out_specs=pl.BlockSpec((1, tm, D_MODEL), row3),
        out_shape=jax.ShapeDtypeStruct((batch, seq, D_MODEL), F32),
        compiler_params=_params(2),
        name="ffn_stage",
    )(x, mod, g_ffn.reshape(1, D_MODEL), g_final.reshape(1, D_MODEL),
      w_gate.astype(BF16), w_up.astype(BF16), w_down.astype(BF16))


def kernel(x, c, positions, rel_bias, w_ada, b_ada, g_mix, w_in, g_q_lora, w_uq, g_kv_lora, w_ukv,
           w_up_a, w_up_b, w_o, g_ffn, w_gate, w_up, w_down, g_final):
    assert w_ada.shape[0] == 1, "single-layer trunk"
    mod = _modulation(c, w_ada[0], b_ada[0])
    a_qkv, gates, q, k, vt = _input_stage(x, mod, g_mix[0], positions, w_in[0], g_q_lora[0], w_uq[0],
                                         g_kv_lora[0], w_ukv[0])
    o_b = _mla_attention(q, k, vt)
    o_a = _dilated_attention(a_qkv, rel_bias)
    x1 = _merge_stage(x, mod, o_a, o_b, gates, w_up_a[0], w_up_b[0], w_o[0])
    return _ffn_stage(x1, mod, g_ffn[0], g_final, w_gate[0], w_up[0], w_down[0])
```

```python
import math

import jax
import jax.numpy as jnp
import numpy as np
from jax import lax
from jax.experimental import pallas as pl
from jax.experimental.pallas import tpu as pltpu

D_MODEL = 1024
A_HEADS = 8
A_HEAD_DIM = 64
A_WIDTH = A_HEADS * A_HEAD_DIM
DILATED_PATTERNS = ((128, 1), (512, 4), (2048, 16))
BAND_BLOCK = 128
REL_BUCKETS = 32
REL_MAX_DIST = 2048
M_HEADS = 8
M_NOPE = 64
M_ROPE = 32
M_V = 64
M_Q_LORA = 768
M_KV_LORA = 256
M_WIDTH = M_HEADS * M_V
ROPE_THETA = 10000.0
D_FF = -(-8 * D_MODEL // (3 * 256)) * 256
N_MOD = 6
EPS = 1e-6
NEG_INF = -1e30

LANES = 128
SUBLANES = 8
V7X_VMEM_BYTES = 64 * 1024 * 1024
VMEM_LIMIT_BYTES = V7X_VMEM_BYTES - 8 * 1024 * 1024

M_HEAD_PAD = LANES
M_PAIRS = M_HEADS // 2
ROPE_HALF = M_ROPE // 2
ROPE_LO = M_NOPE
ROPE_MID = M_NOPE + ROPE_HALF
ROPE_HI = M_NOPE + M_ROPE

A_PAIRS = A_HEADS // 2
A_TILES = 3 * A_PAIRS
SUPER_BLOCK = BAND_BLOCK * max(d for _, d in DILATED_PATTERNS)
DILATED_REGROUP = 4
LOG2_E = math.log2(math.e)

ROW_TILE = 512
MOD_COLUMN_BLOCK = D_MODEL
MLA_KEY_TILE = 512
MLA_QUERY_TILE = 2 * MLA_KEY_TILE
MLA_QUERY_CHUNK = 256

F32 = jnp.float32
BF16 = jnp.bfloat16


def _params(n_axes, flags=None):
    return pltpu.CompilerParams(
        dimension_semantics=("arbitrary",) * n_axes,
        vmem_limit_bytes=VMEM_LIMIT_BYTES,
        flags=flags,
    )


def _resident(shape):
    zeros = (0,) * len(shape)
    return pl.BlockSpec(shape, lambda *_: zeros, pipeline_mode=pl.Buffered(1))


def _bdot(a, b):
    return jnp.dot(a, b, preferred_element_type=F32)


def _rms(x):
    return x * lax.rsqrt(jnp.mean(x * x, axis=-1, keepdims=True) + EPS)


def _sigmoid(x):
    return 1.0 / (1.0 + jnp.exp(-x))


def _mod_kernel(c_ref, w_ref, b_ref, o_ref):
    c = c_ref[...]
    cond = c * _sigmoid(c)
    o_ref[...] = (
        jnp.dot(cond, w_ref[...], preferred_element_type=F32, precision=lax.Precision.HIGHEST)
        + b_ref[...]
    )


def _modulation(c, w_ada, b_ada):
    batch = c.shape[0]
    rows = -(-batch // SUBLANES) * SUBLANES
    c_pad = jnp.pad(c, ((0, rows - batch), (0, 0)))
    cols = MOD_COLUMN_BLOCK
    out = pl.pallas_call(
        _mod_kernel,
        grid=(N_MOD * D_MODEL // cols,),
        in_specs=[
            pl.BlockSpec((rows, D_MODEL), lambda j: (0, 0)),
            pl.BlockSpec((D_MODEL, cols), lambda j: (0, j)),
            pl.BlockSpec((1, cols), lambda j: (0, j)),
        ],
        out_specs=pl.BlockSpec((rows, cols), lambda j: (0, j)),
        out_shape=jax.ShapeDtypeStruct((rows, N_MOD * D_MODEL), F32),
        compiler_params=_params(1),
        name="adaln_mod",
    )(c_pad, w_ada, b_ada.reshape(1, N_MOD * D_MODEL))
    return out[:batch].reshape(batch, N_MOD, D_MODEL)


def _rope_lanes(x, cos, signed_sin):
    lane = lax.broadcasted_iota(jnp.int32, x.shape, 1)
    partner = jnp.where(lane < ROPE_MID, pltpu.roll(x, LANES - ROPE_HALF, 1), pltpu.roll(x, ROPE_HALF, 1))
    return x * cos + partner * signed_sin


def _input_kernel(x_ref, mod_ref, g_ref, pos_ref, freq_ref, wa_ref, wg_ref, wcq_ref, wckv_ref, wkr_ref,
                  gq_ref, wuq_ref, gkv_ref, wuk_ref, wuv_ref,
                  a_ref, gate_ref, q_ref, k_ref, v_ref):
    x = x_ref[0]
    shift = mod_ref[0, 0:1, :]
    scale = mod_ref[0, 1:2, :]
    h = (_rms(x) * g_ref[...]) * (1.0 + scale) + shift
    hb = h.astype(BF16)

    c_q = _bdot(hb, wcq_ref[...])
    c_kv = _bdot(hb, wckv_ref[...])
    k_r = _bdot(hb, wkr_ref[...])

    ang = freq_ref[...] * pos_ref[0, 0].astype(F32)
    cos_r, sin_r = jnp.cos(ang), jnp.sin(ang)
    rows = ang.shape[1]
    cos = jnp.concatenate(
        [jnp.ones((ROPE_LO, rows), F32), cos_r, jnp.ones((LANES - ROPE_HI, rows), F32)], axis=0).T
    signed_sin = jnp.concatenate(
        [jnp.zeros((ROPE_LO, rows), F32), -sin_r[:ROPE_HALF], sin_r[ROPE_HALF:],
         jnp.zeros((LANES - ROPE_HI, rows), F32)], axis=0).T

    q_all = _bdot((_rms(c_q) * gq_ref[...]).astype(BF16), wuq_ref[...])
    q_scale = (M_NOPE + M_ROPE) ** -0.5 * LOG2_E
    for hd in range(M_HEADS):
        q_h = q_all[:, hd * M_HEAD_PAD:(hd + 1) * M_HEAD_PAD]
        q_ref[0, hd] = (_rope_lanes(q_h, cos, signed_sin) * q_scale).astype(BF16)

    c_kv = (_rms(c_kv) * gkv_ref[...]).astype(BF16)
    k_rope = _rope_lanes(k_r, cos, signed_sin)
    k_all = _bdot(c_kv, wuk_ref[...])
    for hd in range(M_HEADS):
        k_ref[0, hd] = (k_all[:, hd * M_HEAD_PAD:(hd + 1) * M_HEAD_PAD] + k_rope).astype(BF16)
    v_all = _bdot(c_kv, wuv_ref[...])
    for pr in range(M_PAIRS):
        v_ref[0, pr, 0] = v_all[:, pr * LANES:(pr + 1) * LANES].T.astype(BF16)

    a_all = _bdot(hb, wa_ref[...])
    for j in range(A_TILES):
        tile = a_all[:, j * LANES:(j + 1) * LANES]
        a_ref[0, j] = tile * (A_HEAD_DIM ** -0.5 * LOG2_E) if j < A_PAIRS else tile
    gate_ref[0] = _bdot(hb, wg_ref[...]).astype(BF16)


def _input_stage(x, mod, g_mix, positions, w_in, g_q_lora, w_uq, g_kv_lora, w_ukv):
    batch, seq, _ = x.shape
    tm = MLA_KEY_TILE
    s0 = 3 * A_WIDTH
    s1 = s0 + M_Q_LORA
    s2 = s1 + M_KV_LORA
    s3 = s2 + M_ROPE
    assert s0 % M_Q_LORA == 0 and s1 % M_KV_LORA == 0
    w_all = w_in.astype(BF16)
    w_kr = jnp.pad(w_all[:, s2:s3], ((0, 0), (ROPE_LO, LANES - ROPE_HI)))
    w_g = w_all[:, s3:]

    def columns(width, start):
        return pl.BlockSpec((D_MODEL, width), lambda *_: (0, start // width), pipeline_mode=pl.Buffered(1))
    w_uq_p = jnp.pad(w_uq, ((0, 0), (0, 0), (0, M_HEAD_PAD - M_NOPE - M_ROPE)))
    w_uq_p = w_uq_p.reshape(M_Q_LORA, M_HEADS * M_HEAD_PAD).astype(BF16)
    w_uk_p = jnp.pad(w_ukv[:, :, :M_NOPE], ((0, 0), (0, 0), (0, M_HEAD_PAD - M_NOPE)))
    w_uk_p = w_uk_p.reshape(M_KV_LORA, M_HEADS * M_HEAD_PAD).astype(BF16)
    w_uv = w_ukv[:, :, M_NOPE:].reshape(M_KV_LORA, M_WIDTH).astype(BF16)

    freqs = ROPE_THETA ** (-jnp.arange(ROPE_HALF, dtype=F32) / ROPE_HALF)
    freq_col = jnp.concatenate([freqs, freqs]).reshape(M_ROPE, 1)

    row3 = lambda b, i: (b, i, 0)
    head4 = lambda b, i: (b, 0, i, 0)
    return pl.pallas_call(
        _input_kernel,
        grid=(batch, seq // tm),
        in_specs=[
            pl.BlockSpec((1, tm, D_MODEL), row3),
            pl.BlockSpec((1, N_MOD, D_MODEL), lambda b, i: (b, 0, 0)),
            _resident((1, D_MODEL)),
            pl.BlockSpec((1, 1, 1, tm), lambda b, i: (b, i, 0, 0)),
            _resident((M_ROPE, 1)),
            columns(s0, 0), _resident(w_g.shape), columns(M_Q_LORA, s0), columns(M_KV_LORA, s1),
            _resident(w_kr.shape),
            _resident((1, M_Q_LORA)), _resident(w_uq_p.shape),
            _resident((1, M_KV_LORA)), _resident(w_uk_p.shape), _resident(w_uv.shape),
        ],
        out_specs=[
            pl.BlockSpec((1, A_TILES, tm, LANES), head4),
            pl.BlockSpec((1, tm, 2 * D_MODEL), row3),
            pl.BlockSpec((1, M_HEADS, tm, M_HEAD_PAD), head4),
            pl.BlockSpec((1, M_HEADS, tm, M_HEAD_PAD), head4),
            pl.BlockSpec((1, M_PAIRS, 1, LANES, tm), lambda b, i: (b, 0, i, 0, 0)),
        ],
        out_shape=[
            jax.ShapeDtypeStruct((batch, A_TILES, seq, LANES), F32),
            jax.ShapeDtypeStruct((batch, seq, 2 * D_MODEL), BF16),
            jax.ShapeDtypeStruct((batch, M_HEADS, seq, M_HEAD_PAD), BF16),
            jax.ShapeDtypeStruct((batch, M_HEADS, seq, M_HEAD_PAD), BF16),
            jax.ShapeDtypeStruct((batch, M_PAIRS, seq // tm, LANES, tm), BF16),
        ],
        compiler_params=_params(2),
        name="input_stage",
    )(x, mod, g_mix.reshape(1, D_MODEL), positions.reshape(batch, seq // tm, 1, tm), freq_col,
      w_all, w_g, w_all, w_all, w_kr, g_q_lora.reshape(1, M_Q_LORA), w_uq_p,
      g_kv_lora.reshape(1, M_KV_LORA), w_uk_p, w_uv)


def _mla_kernel(q_ref, k_ref, vt_ref, o_ref, m_scr, acc_scr, sa_scr, sb_scr, max_a_scr, max_b_scr):
    tq, tk = MLA_QUERY_TILE, MLA_KEY_TILE
    qc = MLA_QUERY_CHUNK
    qi = pl.program_id(2)
    contract_last = (((1,), (1,)), ((), ()))
    v_row = lax.broadcasted_iota(jnp.int32, (LANES, tk), 0)
    own_rows = [v_row < M_V, v_row >= M_V]

    m_scr[...] = jnp.full(m_scr.shape, NEG_INF, F32)
    acc_scr[...] = jnp.zeros(acc_scr.shape, F32)

    def block(scored=None, absorbed=None):
        pieces, chains = [], []
        if scored is not None:
            next_tile, next_s, next_max, diagonal = scored
            start = pl.multiple_of(next_tile * tk, tk)
            keys = [k_ref[0, hh, pl.ds(start, tk), :] for hh in range(2)]
            pieces = [(hh, q0) for hh in range(2) for q0 in range(diagonal or 0, tq, qc)]
        if absorbed is not None:
            tile, s_scr, max_scr, q_lo = absorbed
            vt = vt_ref[0, 0, tile]
            vt_aug = [jnp.where(own, vt, jnp.ones_like(vt)) for own in own_rows]
            chains = [(hh, q0) for hh in range(2) for q0 in range(q_lo, tq, qc)]
            state = [(m_scr[hh, :, q0:q0 + qc], acc_scr[hh, :, q0:q0 + qc]) for hh, q0 in chains]

        def score_piece(hh, q0):
            s = lax.dot_general(keys[hh], q_ref[0, hh, q0:q0 + qc, :], contract_last,
                                preferred_element_type=F32)
            if diagonal is not None and q0 < diagonal + tk - 1:
                key_pos = lax.broadcasted_iota(jnp.int32, s.shape, 0) + diagonal
                query_pos = lax.broadcasted_iota(jnp.int32, s.shape, 1) + q0
                s = jnp.where(key_pos <= query_pos, s, NEG_INF)
            next_s[hh, :, q0:q0 + qc] = s
            next_max[hh, :, q0:q0 + qc] = jnp.max(s, axis=0, keepdims=True)

        def absorb_chain(i):
            hh, q0 = chains[i]
            m_prev, acc_prev = state[i]
            m_new = jnp.maximum(m_prev, max_scr[hh, :, q0:q0 + qc])
            p = jnp.exp2(s_scr[hh, :, q0:q0 + qc] - m_new).astype(BF16)
            acc_scr[hh, :, q0:q0 + qc] = jnp.exp2(m_prev - m_new) * acc_prev + _bdot(vt_aug[hh], p)
            m_scr[hh, :, q0:q0 + qc] = m_new

        if pieces:
            score_piece(*pieces[0])
        for i in range(max(len(pieces) - 1, len(chains))):
            if i + 1 < len(pieces):
                score_piece(*pieces[i + 1])
            if i < len(chains):
                absorb_chain(i)

    n_off = 2 * qi
    last_off = jnp.maximum(n_off - 1, 0)
    block(scored=(n_off, sa_scr, max_a_scr, 0))
    block(scored=(n_off + 1, sb_scr, max_b_scr, tk), absorbed=(n_off, sa_scr, max_a_scr, 0))
    block(scored=(0, sa_scr, max_a_scr, None), absorbed=(n_off + 1, sb_scr, max_b_scr, tk))

    def pair(j):
        first = 2 * j
        block(scored=(first + 1, sb_scr, max_b_scr, None), absorbed=(first, sa_scr, max_a_scr, 0))
        block(scored=(jnp.minimum(first + 2, last_off), sa_scr, max_a_scr, None),
              absorbed=(first + 1, sb_scr, max_b_scr, 0))

    def two_pairs(j, carry):
        pair(2 * j)
        pair(2 * j + 1)
        return carry

    lax.fori_loop(0, qi // 2, two_pairs, 0)

    @pl.when(qi % 2 == 1)
    def _odd_pair():
        pair(qi - 1)

    o0 = acc_scr[0] * (1.0 / acc_scr[0, M_V:M_V + 1, :])
    o1 = acc_scr[1] * (1.0 / acc_scr[1, 0:1, :])
    out_row = lax.broadcasted_iota(jnp.int32, (LANES, tq), 0)
    o_ref[0] = jnp.where(out_row < M_V, o0, o1).T.astype(BF16)


def _mla_attention(q, k, vt):
    batch, _, seq, _ = q.shape
    tq, tk = MLA_QUERY_TILE, MLA_KEY_TILE
    assert tq == 2 * tk and seq % tq == 0
    return pl.pallas_call(
        _mla_kernel,
        grid=(batch, M_PAIRS, seq // tq),
        in_specs=[
            pl.BlockSpec((1, 2, tq, M_HEAD_PAD), lambda b, p, i: (b, p, i, 0)),
            pl.BlockSpec((1, 2, seq, M_HEAD_PAD), lambda b, p, i: (b, p, 0, 0)),
            pl.BlockSpec((1, 1, seq // tk, LANES, tk), lambda b, p, i: (b, p, 0, 0, 0)),
        ],
        out_specs=pl.BlockSpec((1, tq, LANES), lambda b, p, i: (b, i, p)),
        out_shape=jax.ShapeDtypeStruct((batch, seq, M_WIDTH), BF16),
        scratch_shapes=[
            pltpu.VMEM((2, 1, tq), F32),
            pltpu.VMEM((2, LANES, tq), F32),
            pltpu.VMEM((2, tk, tq), F32),
            pltpu.VMEM((2, tk, tq), F32),
            pltpu.VMEM((2, 1, tq), F32),
            pltpu.VMEM((2, 1, tq), F32),
        ],
        compiler_params=_params(3),
        name="mla_attention",
    )(q, k, vt)


def _t5_bucket_table(dilation, n_back):
    blk = BAND_BLOCK
    sub_dist = (np.arange(blk)[:, None] + blk) - np.arange(2 * blk)[None, :]
    dist = np.clip(sub_dist, 0, n_back) * dilation
    max_exact = REL_BUCKETS // 2
    d = np.maximum(dist, 1).astype(np.float32)
    ratio = np.log(d / np.float32(max_exact)) / np.float32(math.log(REL_MAX_DIST / max_exact))
    log_b = max_exact + (ratio * np.float32(REL_BUCKETS - max_exact)).astype(np.int32)
    log_b = np.minimum(log_b, REL_BUCKETS - 1)
    return np.where(dist < max_exact, dist, log_b).astype(np.int32)


def _rows(start, count, stride):
    return pl.ds(start, count) if stride == 1 else pl.ds(start, count, stride=stride)


def _dilated_kernel(rb_ref, bucket_ref, q_ref, kc_ref, kp_ref, vc_ref, vp_ref, o_ref,
                    bias_scr, acc_scr, m_scr, regroup_scr, out_scr):
    blk = BAND_BLOCK
    sup = SUPER_BLOCK
    grp = DILATED_REGROUP
    sub = sup // grp
    pair = pl.program_id(2)
    first_step = (pl.program_id(0) == 0) & (pl.program_id(1) == 0) & (pair == 0)

    @pl.when(first_step)
    def _build_bias():
        row = lax.broadcasted_iota(jnp.int32, (blk, 2 * blk), 0)
        col = lax.broadcasted_iota(jnp.int32, (blk, 2 * blk), 1)
        sub_dist = row + blk - col
        in_band = (sub_dist >= 0) & (sub_dist <= blk)
        for g in range(len(DILATED_PATTERNS)):
            bucket = bucket_ref[g]
            for hd in range(A_HEADS):
                bias = jnp.zeros((blk, 2 * blk), F32)
                for bk in range(REL_BUCKETS):
                    bias = jnp.where(bucket == bk, rb_ref[hd, bk] * LOG2_E, bias)
                bias_scr[g, hd] = jnp.where(in_band, bias, NEG_INF)

    sources = (q_ref, kc_ref, kp_ref, vc_ref, vp_ref)
    for idx, ref in enumerate(sources):
        for r in range(grp):
            regroup_scr[idx, r * sub:(r + 1) * sub, :] = ref[0, 0, _rows(r, sub, grp), :]

    first_valid_col = jnp.where(pl.program_id(1) > 0, 0, blk)
    col = lax.broadcasted_iota(jnp.int32, (blk, 2 * blk), 1)
    lane = lax.broadcasted_iota(jnp.int32, (blk, LANES), 1)
    lane2 = lax.broadcasted_iota(jnp.int32, (2 * blk, LANES), 1)
    contract_last = (((1,), (1,)), ((), ()))
    n_pat = len(DILATED_PATTERNS)

    for g, (_, dil) in enumerate(DILATED_PATTERNS):
        regrouped = dil % grp == 0
        step = dil // grp if regrouped else dil
        for res in range(dil):
            base = (res % grp) * sub + res // grp if regrouped else res
            prev_base = base + (sub if regrouped else sup) - blk * step
            for n in range(sup // (blk * dil)):
                rows = _rows(base + blk * step * n, blk, step)
                if n == 0:
                    prev_rows, k_idx, v_idx = _rows(prev_base, blk, step), 2, 4
                else:
                    prev_rows, k_idx, v_idx = _rows(base + blk * step * (n - 1), blk, step), 1, 3
                if regrouped:
                    load = lambda idx, r: regroup_scr[idx, r, :]
                else:
                    load = lambda idx, r: sources[idx][0, 0, r, :]
                q = load(0, rows).astype(BF16)
                k2 = jnp.concatenate([load(k_idx, prev_rows), load(1, rows)], axis=0).astype(BF16)
                v2 = jnp.concatenate([load(v_idx, prev_rows), load(3, rows)], axis=0).astype(BF16)
                for hh in range(2):
                    in_head = (lane < A_HEAD_DIM) if hh == 0 else (lane >= A_HEAD_DIM)
                    in_head2 = (lane2 < A_HEAD_DIM) if hh == 0 else (lane2 >= A_HEAD_DIM)
                    s = lax.dot_general(jnp.where(in_head, q, jnp.zeros_like(q)), k2, contract_last,
                                        preferred_element_type=F32)
                    bias = bias_scr[g, 2 * pair + hh]
                    if n == 0:
                        bias = jnp.where(col >= first_valid_col, bias, NEG_INF)
                    s = s + bias
                    m_blk = jnp.max(s, axis=-1, keepdims=True)
                    p = jnp.exp2(s - m_blk).astype(BF16)
                    acc_scr[g, hh, rows, :] = _bdot(p, jnp.where(in_head2, v2, jnp.ones_like(v2)))
                    m_scr[g, hh, rows, :] = jnp.broadcast_to(m_blk, (blk, LANES))

    lane_sub = lax.broadcasted_iota(jnp.int32, (sub, LANES), 1)
    for r in range(grp):
        chunk = [pl.ds(r * sub, sub) if dil % grp == 0 else _rows(r, sub, grp) for _, dil in DILATED_PATTERNS]
        halves = []
        for hh in range(2):
            maxes = [m_scr[g, hh, chunk[g], :] for g in range(n_pat)]
            top = maxes[0]
            for g in range(1, n_pat):
                top = jnp.maximum(top, maxes[g])
            total = jnp.exp2(maxes[0] - top) * acc_scr[0, hh, chunk[0], :]
            for g in range(1, n_pat):
                total = total + jnp.exp2(maxes[g] - top) * acc_scr[g, hh, chunk[g], :]
            halves.append(total * (1.0 / pltpu.roll(total, A_HEAD_DIM, 1)))
        out_scr[_rows(r, sub, grp), :] = jnp.where(lane_sub < A_HEAD_DIM, halves[0], halves[1])
    o_ref[0] = out_scr[...].astype(BF16)


def _dilated_attention(a_qkv, rel_bias):
    batch, _, seq, _ = a_qkv.shape
    blk = BAND_BLOCK
    sup = SUPER_BLOCK
    n_pat = len(DILATED_PATTERNS)
    assert all(w // d == blk for w, d in DILATED_PATTERNS), "band of exactly one block behind the query"
    assert seq % sup == 0
    bucket = jnp.asarray(np.stack([_t5_bucket_table(d, w // d) for w, d in DILATED_PATTERNS]))

    def part(which, prev):
        def index(b, s, p):
            return (b, which * A_PAIRS + p, jnp.maximum(s - 1, 0) if prev else s, 0)
        return pl.BlockSpec((1, 1, sup, LANES), index)

    return pl.pallas_call(
        _dilated_kernel,
        grid=(batch, seq // sup, A_PAIRS),
        in_specs=[
            pl.BlockSpec(memory_space=pltpu.SMEM),
            pl.BlockSpec((n_pat, blk, 2 * blk), lambda b, s, p: (0, 0, 0)),
            part(0, False), part(1, False), part(1, True), part(2, False), part(2, True),
        ],
        out_specs=pl.BlockSpec((1, sup, LANES), lambda b, s, p: (b, s, p)),
        out_shape=jax.ShapeDtypeStruct((batch, seq, A_WIDTH), BF16),
        scratch_shapes=[
            pltpu.VMEM((n_pat, A_HEADS, blk, 2 * blk), F32),
            pltpu.VMEM((n_pat, 2, sup, LANES), F32),
            pltpu.VMEM((n_pat, 2, sup, LANES), F32),
            pltpu.VMEM((5, sup, LANES), F32),
            pltpu.VMEM((sup, LANES), F32),
        ],
        compiler_params=_params(3),
        name="dilated_attention",
    )(rel_bias, bucket, a_qkv, a_qkv, a_qkv, a_qkv, a_qkv)


def _output_kernel(x_ref, mod_ref, oa_ref, ob_ref, gate_ref, wa_ref, wb_ref, wo_ref,
                   g_ref, gf_ref, wg_ref, wu_ref, wd_ref, out_ref):
    rows = x_ref.shape[1] // 2
    halves = [slice(0, rows), slice(rows, 2 * rows)]

    mixed = []
    for r in halves:
        y_a = _bdot(oa_ref[0, r, :], wa_ref[...])
        y_b = _bdot(ob_ref[0, r, :], wb_ref[...])
        gates = gate_ref[0, r, :].astype(F32)
        merged = _sigmoid(gates[:, :D_MODEL]) * y_a + _sigmoid(gates[:, D_MODEL:]) * y_b
        mixed.append(_bdot(merged.astype(BF16), wo_ref[...]))

    xs, acts = [], []
    for r, mix in zip(halves, mixed):
        x = x_ref[0, r, :] + mod_ref[0, 2:3, :] * mix
        hb = ((_rms(x) * g_ref[...]) * (1.0 + mod_ref[0, 4:5, :]) + mod_ref[0, 3:4, :]).astype(BF16)
        gate = _bdot(hb, wg_ref[...])
        up = _bdot(hb, wu_ref[...])
        xs.append(x)
        acts.append((gate * _sigmoid(gate) * up).astype(BF16))

    for r, x, act in zip(halves, xs, acts):
        y = x + mod_ref[0, 5:6, :] * _bdot(act, wd_ref[...])
        out_ref[0, r, :] = _rms(y) * gf_ref[...]


def _output_stage(x, mod, o_a, o_b, gates, w_up_a, w_up_b, w_o, g_ffn, g_final, w_gate, w_up, w_down):
    batch, seq, _ = x.shape
    tm = ROW_TILE
    row3 = lambda b, i: (b, i, 0)
    half = pl.BlockSpec((1, tm, A_WIDTH), row3)
    return pl.pallas_call(
        _output_kernel,
        grid=(batch, seq // tm),
        in_specs=[
            pl.BlockSpec((1, tm, D_MODEL), row3),
            pl.BlockSpec((1, N_MOD, D_MODEL), lambda b, i: (b, 0, 0)),
            half, half,
            pl.BlockSpec((1, tm, 2 * D_MODEL), row3),
            _resident((A_WIDTH, D_MODEL)), _resident((M_WIDTH, D_MODEL)), _resident((D_MODEL, D_MODEL)),
            _resident((1, D_MODEL)), _resident((1, D_MODEL)),
            _resident((D_MODEL, D_FF)), _resident((D_MODEL, D_FF)), _resident((D_FF, D_MODEL)),
        ],
        out_specs=pl.BlockSpec((1, tm, D_MODEL), row3),
        out_shape=jax.ShapeDtypeStruct((batch, seq, D_MODEL), F32),
        compiler_params=_params(2),
        name="output_stage",
    )(x, mod, o_a, o_b, gates,
      w_up_a.astype(BF16), w_up_b.astype(BF16), w_o.astype(BF16),
      g_ffn.reshape(1, D_MODEL), g_final.reshape(1, D_MODEL),
      w_gate.astype(BF16), w_up.astype(BF16), w_down.astype(BF16))


def kernel(x, c, positions, rel_bias, w_ada, b_ada, g_mix, w_in, g_q_lora, w_uq, g_kv_lora, w_ukv,
           w_up_a, w_up_b, w_o, g_ffn, w_gate, w_up, w_down, g_final):
    assert w_ada.shape[0] == 1, "single-layer trunk"
    mod = _modulation(c, w_ada[0], b_ada[0])
    a_qkv, gates, q, k, vt = _input_stage(x, mod, g_mix[0], positions, w_in[0], g_q_lora[0], w_uq[0],
                                         g_kv_lora[0], w_ukv[0])
    o_b = _mla_attention(q, k, vt)
    o_a = _dilated_attention(a_qkv, rel_bias)
    return _output_stage(x, mod, o_a, o_b, gates, w_up_a[0], w_up_b[0], w_o[0],
                         g_ffn[0], g_final, w_gate[0], w_up[0], w_down[0])
```

```python
import math

import jax
import jax.numpy as jnp
import numpy as np
from jax import lax
from jax.experimental import pallas as pl
from jax.experimental.pallas import tpu as pltpu

D_MODEL = 1024
A_HEADS = 8
A_HEAD_DIM = 64
A_WIDTH = A_HEADS * A_HEAD_DIM
DILATED_PATTERNS = ((128, 1), (512, 4), (2048, 16))
BAND_BLOCK = 128
REL_BUCKETS = 32
REL_MAX_DIST = 2048
M_HEADS = 8
M_NOPE = 64
M_ROPE = 32
M_V = 64
M_Q_LORA = 768
M_KV_LORA = 256
M_WIDTH = M_HEADS * M_V
ROPE_THETA = 10000.0
D_FF = -(-8 * D_MODEL // (3 * 256)) * 256
N_MOD = 6
EPS = 1e-6
NEG_INF = -1e30

LANES = 128
SUBLANES = 8
V7X_VMEM_BYTES = 64 * 1024 * 1024
VMEM_LIMIT_BYTES = V7X_VMEM_BYTES - 8 * 1024 * 1024

M_HEAD_PAD = LANES
M_PAIRS = M_HEADS // 2
ROPE_HALF = M_ROPE // 2
ROPE_LO = M_NOPE
ROPE_MID = M_NOPE + ROPE_HALF
ROPE_HI = M_NOPE + M_ROPE

A_PAIRS = A_HEADS // 2
A_TILES = 3 * A_PAIRS
SUPER_BLOCK = BAND_BLOCK * max(d for _, d in DILATED_PATTERNS)
DILATED_REGROUP = 4
LOG2_E = math.log2(math.e)

ROW_TILE = 512
MOD_COLUMN_BLOCK = D_MODEL
MLA_KEY_TILE = 512
MLA_QUERY_TILE = 2 * MLA_KEY_TILE
MLA_QUERY_CHUNK = 256

F32 = jnp.float32
BF16 = jnp.bfloat16


def _params(n_axes, flags=None):
    return pltpu.CompilerParams(
        dimension_semantics=("arbitrary",) * n_axes,
        vmem_limit_bytes=VMEM_LIMIT_BYTES,
        flags=flags,
    )


def _resident(shape):
    zeros = (0,) * len(shape)
    return pl.BlockSpec(shape, lambda *_: zeros, pipeline_mode=pl.Buffered(1))


def _bdot(a, b):
    return jnp.dot(a, b, preferred_element_type=F32)


def _rms(x):
    return x * lax.rsqrt(jnp.mean(x * x, axis=-1, keepdims=True) + EPS)


def _sigmoid(x):
    return 1.0 / (1.0 + jnp.exp(-x))


def _mod_kernel(c_ref, w_ref, b_ref, o_ref):
    c = c_ref[...]
    cond = c * _sigmoid(c)
    o_ref[...] = (
        jnp.dot(cond, w_ref[...], preferred_element_type=F32, precision=lax.Precision.HIGHEST)
        + b_ref[...]
    )


def _modulation(c, w_ada, b_ada):
    batch = c.shape[0]
    rows = -(-batch // SUBLANES) * SUBLANES
    c_pad = jnp.pad(c, ((0, rows - batch), (0, 0)))
    cols = MOD_COLUMN_BLOCK
    out = pl.pallas_call(
        _mod_kernel,
        grid=(N_MOD * D_MODEL // cols,),
        in_specs=[
            pl.BlockSpec((rows, D_MODEL), lambda j: (0, 0)),
            pl.BlockSpec((D_MODEL, cols), lambda j: (0, j)),
            pl.BlockSpec((1, cols), lambda j: (0, j)),
        ],
        out_specs=pl.BlockSpec((rows, cols), lambda j: (0, j)),
        out_shape=jax.ShapeDtypeStruct((rows, N_MOD * D_MODEL), F32),
        compiler_params=_params(1),
        name="adaln_mod",
    )(c_pad, w_ada, b_ada.reshape(1, N_MOD * D_MODEL))
    return out[:batch].reshape(batch, N_MOD, D_MODEL)


def _rope_lanes(x, cos, signed_sin):
    lane = lax.broadcasted_iota(jnp.int32, x.shape, 1)
    partner = jnp.where(lane < ROPE_MID, pltpu.roll(x, LANES - ROPE_HALF, 1), pltpu.roll(x, ROPE_HALF, 1))
    return x * cos + partner * signed_sin


def _input_kernel(x_ref, mod_ref, g_ref, pos_ref, freq_ref, wa_ref, wg_ref, wcq_ref, wckv_ref, wkr_ref,
                  gq_ref, wuq_ref, gkv_ref, wuk_ref, wuv_ref,
                  a_ref, gate_ref, q_ref, k_ref, v_ref):
    x = x_ref[0]
    shift = mod_ref[0, 0:1, :]
    scale = mod_ref[0, 1:2, :]
    h = (_rms(x) * g_ref[...]) * (1.0 + scale) + shift
    hb = h.astype(BF16)

    c_q = _bdot(hb, wcq_ref[...])
    c_kv = _bdot(hb, wckv_ref[...])
    k_r = _bdot(hb, wkr_ref[...])

    ang = freq_ref[...] * pos_ref[0, 0].astype(F32)
    cos_r, sin_r = jnp.cos(ang), jnp.sin(ang)
    rows = ang.shape[1]
    cos = jnp.concatenate(
        [jnp.ones((ROPE_LO, rows), F32), cos_r, jnp.ones((LANES - ROPE_HI, rows), F32)], axis=0).T
    signed_sin = jnp.concatenate(
        [jnp.zeros((ROPE_LO, rows), F32), -sin_r[:ROPE_HALF], sin_r[ROPE_HALF:],
         jnp.zeros((LANES - ROPE_HI, rows), F32)], axis=0).T

    q_all = _bdot((_rms(c_q) * gq_ref[...]).astype(BF16), wuq_ref[...])
    q_scale = (M_NOPE + M_ROPE) ** -0.5 * LOG2_E
    for hd in range(M_HEADS):
        q_h = q_all[:, hd * M_HEAD_PAD:(hd + 1) * M_HEAD_PAD]
        q_ref[0, hd] = (_rope_lanes(q_h, cos, signed_sin) * q_scale).astype(BF16)

    c_kv = (_rms(c_kv) * gkv_ref[...]).astype(BF16)
    k_rope = _rope_lanes(k_r, cos, signed_sin)
    k_all = _bdot(c_kv, wuk_ref[...])
    for hd in range(M_HEADS):
        k_ref[0, hd] = (k_all[:, hd * M_HEAD_PAD:(hd + 1) * M_HEAD_PAD] + k_rope).astype(BF16)
    v_all = _bdot(c_kv, wuv_ref[...])
    for pr in range(M_PAIRS):
        v_ref[0, pr, 0] = v_all[:, pr * LANES:(pr + 1) * LANES].T.astype(BF16)

    a_all = _bdot(hb, wa_ref[...])
    for j in range(A_TILES):
        tile = a_all[:, j * LANES:(j + 1) * LANES]
        a_ref[0, j] = tile * (A_HEAD_DIM ** -0.5 * LOG2_E) if j < A_PAIRS else tile
    gate_ref[0] = _bdot(hb, wg_ref[...]).astype(BF16)


def _input_stage(x, mod, g_mix, positions, w_in, g_q_lora, w_uq, g_kv_lora, w_ukv):
    batch, seq, _ = x.shape
    tm = MLA_KEY_TILE
    s0 = 3 * A_WIDTH
    s1 = s0 + M_Q_LORA
    s2 = s1 + M_KV_LORA
    s3 = s2 + M_ROPE
    assert s0 % M_Q_LORA == 0 and s1 % M_KV_LORA == 0
    w_all = w_in.astype(BF16)
    w_kr = jnp.pad(w_all[:, s2:s3], ((0, 0), (ROPE_LO, LANES - ROPE_HI)))
    w_g = w_all[:, s3:]

    def columns(width, start):
        return pl.BlockSpec((D_MODEL, width), lambda *_: (0, start // width), pipeline_mode=pl.Buffered(1))
    w_uq_p = jnp.pad(w_uq, ((0, 0), (0, 0), (0, M_HEAD_PAD - M_NOPE - M_ROPE)))
    w_uq_p = w_uq_p.reshape(M_Q_LORA, M_HEADS * M_HEAD_PAD).astype(BF16)
    w_uk_p = jnp.pad(w_ukv[:, :, :M_NOPE], ((0, 0), (0, 0), (0, M_HEAD_PAD - M_NOPE)))
    w_uk_p = w_uk_p.reshape(M_KV_LORA, M_HEADS * M_HEAD_PAD).astype(BF16)
    w_uv = w_ukv[:, :, M_NOPE:].reshape(M_KV_LORA, M_WIDTH).astype(BF16)

    freqs = ROPE_THETA ** (-jnp.arange(ROPE_HALF, dtype=F32) / ROPE_HALF)
    freq_col = jnp.concatenate([freqs, freqs]).reshape(M_ROPE, 1)

    row3 = lambda b, i: (b, i, 0)
    head4 = lambda b, i: (b, 0, i, 0)
    return pl.pallas_call(
        _input_kernel,
        grid=(batch, seq // tm),
        in_specs=[
            pl.BlockSpec((1, tm, D_MODEL), row3),
            pl.BlockSpec((1, N_MOD, D_MODEL), lambda b, i: (b, 0, 0)),
            _resident((1, D_MODEL)),
            pl.BlockSpec((1, 1, 1, tm), lambda b, i: (b, i, 0, 0)),
            _resident((M_ROPE, 1)),
            columns(s0, 0), _resident(w_g.shape), columns(M_Q_LORA, s0), columns(M_KV_LORA, s1),
            _resident(w_kr.shape),
            _resident((1, M_Q_LORA)), _resident(w_uq_p.shape),
            _resident((1, M_KV_LORA)), _resident(w_uk_p.shape), _resident(w_uv.shape),
        ],
        out_specs=[
            pl.BlockSpec((1, A_TILES, tm, LANES), head4),
            pl.BlockSpec((1, tm, 2 * D_MODEL), row3),
            pl.BlockSpec((1, M_HEADS, tm, M_HEAD_PAD), head4),
            pl.BlockSpec((1, M_HEADS, tm, M_HEAD_PAD), head4),
            pl.BlockSpec((1, M_PAIRS, 1, LANES, tm), lambda b, i: (b, 0, i, 0, 0)),
        ],
        out_shape=[
            jax.ShapeDtypeStruct((batch, A_TILES, seq, LANES), F32),
            jax.ShapeDtypeStruct((batch, seq, 2 * D_MODEL), BF16),
            jax.ShapeDtypeStruct((batch, M_HEADS, seq, M_HEAD_PAD), BF16),
            jax.ShapeDtypeStruct((batch, M_HEADS, seq, M_HEAD_PAD), BF16),
            jax.ShapeDtypeStruct((batch, M_PAIRS, seq // tm, LANES, tm), BF16),
        ],
        compiler_params=_params(2),
        name="input_stage",
    )(x, mod, g_mix.reshape(1, D_MODEL), positions.reshape(batch, seq // tm, 1, tm), freq_col,
      w_all, w_g, w_all, w_all, w_kr, g_q_lora.reshape(1, M_Q_LORA), w_uq_p,
      g_kv_lora.reshape(1, M_KV_LORA), w_uk_p, w_uv)


def _mla_kernel(q_ref, k_ref, vt_ref, o_ref, m_scr, acc_scr, sa_scr, sb_scr, max_a_scr, max_b_scr):
    tq, tk = MLA_QUERY_TILE, MLA_KEY_TILE
    qc = MLA_QUERY_CHUNK
    n_query_tiles = q_ref.shape[2] // tq
    contract_last = (((1,), (1,)), ((), ()))
    v_row = lax.broadcasted_iota(jnp.int32, (LANES, tk), 0)
    own_rows = [v_row < M_V, v_row >= M_V]
    out_row = lax.broadcasted_iota(jnp.int32, (LANES, tq), 0)

    def reset():
        m_scr[...] = jnp.full(m_scr.shape, NEG_INF, F32)
        acc_scr[...] = jnp.zeros(acc_scr.shape, F32)

    def finish(q_tile):
        o0 = acc_scr[0] * (1.0 / acc_scr[0, M_V:M_V + 1, :])
        o1 = acc_scr[1] * (1.0 / acc_scr[1, 0:1, :])
        rows = pl.ds(pl.multiple_of(q_tile * tq, tq), tq)
        o_ref[0, rows, :] = jnp.where(out_row < M_V, o0, o1).T.astype(BF16)

    def block(scored=None, absorbed=None):
        pieces, chains = [], []
        if scored is not None:
            next_tile, next_s, next_max, diagonal, q_tile = scored
            start = pl.multiple_of(next_tile * tk, tk)
            keys = [k_ref[0, hh, pl.ds(start, tk), :] for hh in range(2)]
            pieces = [(hh, q0) for hh in range(2) for q0 in range(diagonal or 0, tq, qc)]
        if absorbed is not None:
            tile, s_scr, max_scr, q_lo = absorbed
            vt = vt_ref[0, 0, tile]
            vt_aug = [jnp.where(own, vt, jnp.ones_like(vt)) for own in own_rows]
            chains = [(hh, q0) for hh in range(2) for q0 in range(q_lo, tq, qc)]
            state = [(m_scr[hh, :, q0:q0 + qc], acc_scr[hh, :, q0:q0 + qc]) for hh, q0 in chains]

        def score_piece(hh, q0):
            q_rows = pl.ds(pl.multiple_of(q_tile * tq + q0, qc), qc)
            s = lax.dot_general(keys[hh], q_ref[0, hh, q_rows, :], contract_last,
                                preferred_element_type=F32)
            if diagonal is not None and q0 < diagonal + tk - 1:
                key_pos = lax.broadcasted_iota(jnp.int32, s.shape, 0) + diagonal
                query_pos = lax.broadcasted_iota(jnp.int32, s.shape, 1) + q0
                s = jnp.where(key_pos <= query_pos, s, NEG_INF)
            next_s[hh, :, q0:q0 + qc] = s
            next_max[hh, :, q0:q0 + qc] = jnp.max(s, axis=0, keepdims=True)

        def absorb_chain(i):
            hh, q0 = chains[i]
            m_prev, acc_prev = state[i]
            m_new = jnp.maximum(m_prev, max_scr[hh, :, q0:q0 + qc])
            p = jnp.exp2(s_scr[hh, :, q0:q0 + qc] - m_new).astype(BF16)
            acc_scr[hh, :, q0:q0 + qc] = jnp.exp2(m_prev - m_new) * acc_prev + _bdot(vt_aug[hh], p)
            m_scr[hh, :, q0:q0 + qc] = m_new

        if pieces:
            score_piece(*pieces[0])
        for i in range(max(len(pieces) - 1, len(chains))):
            if i + 1 < len(pieces):
                score_piece(*pieces[i + 1])
            if i < len(chains):
                absorb_chain(i)

    buf_a, buf_b = (sa_scr, max_a_scr), (sb_scr, max_b_scr)

    reset()
    block(scored=(0, *buf_a, 0, 0))
    block(scored=(1, *buf_b, tk, 0), absorbed=(0, *buf_a, 0))
    block(scored=(2, *buf_a, 0, 1), absorbed=(1, *buf_b, tk))
    finish(0)

    def query_tile(i, carry):
        reset()
        block(scored=(2 * i + 1, *buf_b, tk, i), absorbed=(2 * i, *buf_a, 0))
        block(scored=(0, *buf_a, None, i), absorbed=(2 * i + 1, *buf_b, tk))

        def pair(j):
            block(scored=(2 * j + 1, *buf_b, None, i), absorbed=(2 * j, *buf_a, 0))
            block(scored=(2 * j + 2, *buf_a, None, i), absorbed=(2 * j + 1, *buf_b, 0))

        def two_pairs(j, inner):
            pair(2 * j)
            pair(2 * j + 1)
            return inner

        lax.fori_loop(0, (i - 1) // 2, two_pairs, 0)

        @pl.when((i - 1) % 2 == 1)
        def _odd_pair():
            pair(i - 2)

        block(scored=(2 * i - 1, *buf_b, None, i), absorbed=(2 * i - 2, *buf_a, 0))
        nxt = jnp.minimum(i + 1, n_query_tiles - 1)
        block(scored=(2 * nxt, *buf_a, 0, nxt), absorbed=(2 * i - 1, *buf_b, 0))
        finish(i)
        return carry

    lax.fori_loop(1, n_query_tiles, query_tile, 0)


def _mla_attention(q, k, vt):
    batch, _, seq, _ = q.shape
    tq, tk = MLA_QUERY_TILE, MLA_KEY_TILE
    assert tq == 2 * tk and seq % tq == 0 and seq // tq >= 2
    return pl.pallas_call(
        _mla_kernel,
        grid=(batch, M_PAIRS),
        in_specs=[
            pl.BlockSpec((1, 2, seq, M_HEAD_PAD), lambda b, p: (b, p, 0, 0)),
            pl.BlockSpec((1, 2, seq, M_HEAD_PAD), lambda b, p: (b, p, 0, 0)),
            pl.BlockSpec((1, 1, seq // tk, LANES, tk), lambda b, p: (b, p, 0, 0, 0)),
        ],
        out_specs=pl.BlockSpec((1, seq, LANES), lambda b, p: (b, 0, p)),
        out_shape=jax.ShapeDtypeStruct((batch, seq, M_WIDTH), BF16),
        scratch_shapes=[
            pltpu.VMEM((2, 1, tq), F32),
            pltpu.VMEM((2, LANES, tq), F32),
            pltpu.VMEM((2, tk, tq), F32),
            pltpu.VMEM((2, tk, tq), F32),
            pltpu.VMEM((2, 1, tq), F32),
            pltpu.VMEM((2, 1, tq), F32),
        ],
        compiler_params=_params(2),
        name="mla_attention",
    )(q, k, vt)


def _t5_bucket_table(dilation, n_back):
    blk = BAND_BLOCK
    sub_dist = (np.arange(blk)[:, None] + blk) - np.arange(2 * blk)[None, :]
    dist = np.clip(sub_dist, 0, n_back) * dilation
    max_exact = REL_BUCKETS // 2
    d = np.maximum(dist, 1).astype(np.float32)
    ratio = np.log(d / np.float32(max_exact)) / np.float32(math.log(REL_MAX_DIST / max_exact))
    log_b = max_exact + (ratio * np.float32(REL_BUCKETS - max_exact)).astype(np.int32)
    log_b = np.minimum(log_b, REL_BUCKETS - 1)
    return np.where(dist < max_exact, dist, log_b).astype(np.int32)


def _rows(start, count, stride):
    return pl.ds(start, count) if stride == 1 else pl.ds(start, count, stride=stride)


def _dilated_kernel(rb_ref, bucket_ref, q_ref, kc_ref, kp_ref, vc_ref, vp_ref, o_ref,
                    bias_scr, acc_scr, m_scr, regroup_scr, out_scr):
    blk = BAND_BLOCK
    sup = SUPER_BLOCK
    grp = DILATED_REGROUP
    sub = sup // grp
    pair = pl.program_id(2)
    first_step = (pl.program_id(0) == 0) & (pl.program_id(1) == 0) & (pair == 0)

    @pl.when(first_step)
    def _build_bias():
        row = lax.broadcasted_iota(jnp.int32, (blk, 2 * blk), 0)
        col = lax.broadcasted_iota(jnp.int32, (blk, 2 * blk), 1)
        sub_dist = row + blk - col
        in_band = (sub_dist >= 0) & (sub_dist <= blk)
        for g in range(len(DILATED_PATTERNS)):
            bucket = bucket_ref[g]
            for hd in range(A_HEADS):
                bias = jnp.zeros((blk, 2 * blk), F32)
                for bk in range(REL_BUCKETS):
                    bias = jnp.where(bucket == bk, rb_ref[hd, bk] * LOG2_E, bias)
                bias_scr[g, hd] = jnp.where(in_band, bias, NEG_INF)

    sources = (q_ref, kc_ref, kp_ref, vc_ref, vp_ref)
    for idx, ref in enumerate(sources):
        for r in range(grp):
            regroup_scr[idx, r * sub:(r + 1) * sub, :] = ref[0, 0, _rows(r, sub, grp), :]

    first_valid_col = jnp.where(pl.program_id(1) > 0, 0, blk)
    col = lax.broadcasted_iota(jnp.int32, (blk, 2 * blk), 1)
    lane = lax.broadcasted_iota(jnp.int32, (blk, LANES), 1)
    lane2 = lax.broadcasted_iota(jnp.int32, (2 * blk, LANES), 1)
    contract_last = (((1,), (1,)), ((), ()))
    n_pat = len(DILATED_PATTERNS)

    for g, (_, dil) in enumerate(DILATED_PATTERNS):
        regrouped = dil % grp == 0
        step = dil // grp if regrouped else dil
        for res in range(dil):
            base = (res % grp) * sub + res // grp if regrouped else res
            prev_base = base + (sub if regrouped else sup) - blk * step
            for n in range(sup // (blk * dil)):
                rows = _rows(base + blk * step * n, blk, step)
                if n == 0:
                    prev_rows, k_idx, v_idx = _rows(prev_base, blk, step), 2, 4
                else:
                    prev_rows, k_idx, v_idx = _rows(base + blk * step * (n - 1), blk, step), 1, 3
                if regrouped:
                    load = lambda idx, r: regroup_scr[idx, r, :]
                else:
                    load = lambda idx, r: sources[idx][0, 0, r, :]
                q = load(0, rows).astype(BF16)
                k2 = jnp.concatenate([load(k_idx, prev_rows), load(1, rows)], axis=0).astype(BF16)
                v2 = jnp.concatenate([load(v_idx, prev_rows), load(3, rows)], axis=0).astype(BF16)
                for hh in range(2):
                    in_head = (lane < A_HEAD_DIM) if hh == 0 else (lane >= A_HEAD_DIM)
                    in_head2 = (lane2 < A_HEAD_DIM) if hh == 0 else (lane2 >= A_HEAD_DIM)
                    s = lax.dot_general(jnp.where(in_head, q, jnp.zeros_like(q)), k2, contract_last,
                                        preferred_element_type=F32)
                    bias = bias_scr[g, 2 * pair + hh]
                    if n == 0:
                        bias = jnp.where(col >= first_valid_col, bias, NEG_INF)
                    s = s + bias
                    m_blk = jnp.max(s, axis=-1, keepdims=True)
                    p = jnp.exp2(s - m_blk).astype(BF16)
                    acc_scr[g, hh, rows, :] = _bdot(p, jnp.where(in_head2, v2, jnp.ones_like(v2)))
                    m_scr[g, hh, rows, :] = jnp.broadcast_to(m_blk, (blk, LANES))

    lane_sub = lax.broadcasted_iota(jnp.int32, (sub, LANES), 1)
    for r in range(grp):
        chunk = [pl.ds(r * sub, sub) if dil % grp == 0 else _rows(r, sub, grp) for _, dil in DILATED_PATTERNS]
        halves = []
        for hh in range(2):
            maxes = [m_scr[g, hh, chunk[g], :] for g in range(n_pat)]
            top = maxes[0]
            for g in range(1, n_pat):
                top = jnp.maximum(top, maxes[g])
            total = jnp.exp2(maxes[0] - top) * acc_scr[0, hh, chunk[0], :]
            for g in range(1, n_pat):
                total = total + jnp.exp2(maxes[g] - top) * acc_scr[g, hh, chunk[g], :]
            halves.append(total * (1.0 / pltpu.roll(total, A_HEAD_DIM, 1)))
        out_scr[_rows(r, sub, grp), :] = jnp.where(lane_sub < A_HEAD_DIM, halves[0], halves[1])
    o_ref[0] = out_scr[...].astype(BF16)


def _dilated_attention(a_qkv, rel_bias):
    batch, _, seq, _ = a_qkv.shape
    blk = BAND_BLOCK
    sup = SUPER_BLOCK
    n_pat = len(DILATED_PATTERNS)
    assert all(w // d == blk for w, d in DILATED_PATTERNS), "band of exactly one block behind the query"
    assert seq % sup == 0
    bucket = jnp.asarray(np.stack([_t5_bucket_table(d, w // d) for w, d in DILATED_PATTERNS]))

    def part(which, prev):
        def index(b, s, p):
            return (b, which * A_PAIRS + p, jnp.maximum(s - 1, 0) if prev else s, 0)
        return pl.BlockSpec((1, 1, sup, LANES), index)

    return pl.pallas_call(
        _dilated_kernel,
        grid=(batch, seq // sup, A_PAIRS),
        in_specs=[
            pl.BlockSpec(memory_space=pltpu.SMEM),
            pl.BlockSpec((n_pat, blk, 2 * blk), lambda b, s, p: (0, 0, 0)),
            part(0, False), part(1, False), part(1, True), part(2, False), part(2, True),
        ],
        out_specs=pl.BlockSpec((1, sup, LANES), lambda b, s, p: (b, s, p)),
        out_shape=jax.ShapeDtypeStruct((batch, seq, A_WIDTH), BF16),
        scratch_shapes=[
            pltpu.VMEM((n_pat, A_HEADS, blk, 2 * blk), F32),
            pltpu.VMEM((n_pat, 2, sup, LANES), F32),
            pltpu.VMEM((n_pat, 2, sup, LANES), F32),
            pltpu.VMEM((5, sup, LANES), F32),
            pltpu.VMEM((sup, LANES), F32),
        ],
        compiler_params=_params(3),
        name="dilated_attention",
    )(rel_bias, bucket, a_qkv, a_qkv, a_qkv, a_qkv, a_qkv)


def _output_kernel(x_ref, mod_ref, oa_ref, ob_ref, gate_ref, wa_ref, wb_ref, wo_ref,
                   g_ref, gf_ref, wg_ref, wu_ref, wd_ref, out_ref):
    rows = x_ref.shape[1] // 2
    halves = [slice(0, rows), slice(rows, 2 * rows)]

    mixed = []
    for r in halves:
        y_a = _bdot(oa_ref[0, r, :], wa_ref[...])
        y_b = _bdot(ob_ref[0, r, :], wb_ref[...])
        gates = gate_ref[0, r, :].astype(F32)
        merged = _sigmoid(gates[:, :D_MODEL]) * y_a + _sigmoid(gates[:, D_MODEL:]) * y_b
        mixed.append(_bdot(merged.astype(BF16), wo_ref[...]))

    xs, acts = [], []
    for r, mix in zip(halves, mixed):
        x = x_ref[0, r, :] + mod_ref[0, 2:3, :] * mix
        hb = ((_rms(x) * g_ref[...]) * (1.0 + mod_ref[0, 4:5, :]) + mod_ref[0, 3:4, :]).astype(BF16)
        gate = _bdot(hb, wg_ref[...])
        up = _bdot(hb, wu_ref[...])
        xs.append(x)
        acts.append((gate * _sigmoid(gate) * up).astype(BF16))

    for r, x, act in zip(halves, xs, acts):
        y = x + mod_ref[0, 5:6, :] * _bdot(act, wd_ref[...])
        out_ref[0, r, :] = _rms(y) * gf_ref[...]


def _output_stage(x, mod, o_a, o_b, gates, w_up_a, w_up_b, w_o, g_ffn, g_final, w_gate, w_up, w_down):
    batch, seq, _ = x.shape
    tm = ROW_TILE
    row3 = lambda b, i: (b, i, 0)
    half = pl.BlockSpec((1, tm, A_WIDTH), row3)
    return pl.pallas_call(
        _output_kernel,
        grid=(batch, seq // tm),
        in_specs=[
            pl.BlockSpec((1, tm, D_MODEL), row3),
            pl.BlockSpec((1, N_MOD, D_MODEL), lambda b, i: (b, 0, 0)),
            half, half,
            pl.BlockSpec((1, tm, 2 * D_MODEL), row3),
            _resident((A_WIDTH, D_MODEL)), _resident((M_WIDTH, D_MODEL)), _resident((D_MODEL, D_MODEL)),
            _resident((1, D_MODEL)), _resident((1, D_MODEL)),
            _resident((D_MODEL, D_FF)), _resident((D_MODEL, D_FF)), _resident((D_FF, D_MODEL)),
        ],
        out_specs=pl.BlockSpec((1, tm, D_MODEL), row3),
        out_shape=jax.ShapeDtypeStruct((batch, seq, D_MODEL), F32),
        compiler_params=_params(2),
        name="output_stage",
    )(x, mod, o_a, o_b, gates,
      w_up_a.astype(BF16), w_up_b.astype(BF16), w_o.astype(BF16),
      g_ffn.reshape(1, D_MODEL), g_final.reshape(1, D_MODEL),
      w_gate.astype(BF16), w_up.astype(BF16), w_down.astype(BF16))


def kernel(x, c, positions, rel_bias, w_ada, b_ada, g_mix, w_in, g_q_lora, w_uq, g_kv_lora, w_ukv,
           w_up_a, w_up_b, w_o, g_ffn, w_gate, w_up, w_down, g_final):
    assert w_ada.shape[0] == 1, "single-layer trunk"
    mod = _modulation(c, w_ada[0], b_ada[0])
    a_qkv, gates, q, k, vt = _input_stage(x, mod, g_mix[0], positions, w_in[0], g_q_lora[0], w_uq[0],
                                         g_kv_lora[0], w_ukv[0])
    o_b = _mla_attention(q, k, vt)
    o_a = _dilated_attention(a_qkv, rel_bias)
    return _output_stage(x, mod, o_a, o_b, gates, w_up_a[0], w_up_b[0], w_o[0],
                         g_ffn[0], g_final, w_gate[0], w_up[0], w_down[0])
```

```python
import math

import jax
import jax.numpy as jnp
import numpy as np
from jax import lax
from jax.experimental import pallas as pl
from jax.experimental.pallas import tpu as pltpu

D_MODEL = 1024
A_HEADS = 8
A_HEAD_DIM = 64
A_WIDTH = A_HEADS * A_HEAD_DIM
DILATED_PATTERNS = ((128, 1), (512, 4), (2048, 16))
BAND_BLOCK = 128
REL_BUCKETS = 32
REL_MAX_DIST = 2048
M_HEADS = 8
M_NOPE = 64
M_ROPE = 32
M_V = 64
M_Q_LORA = 768
M_KV_LORA = 256
M_WIDTH = M_HEADS * M_V
ROPE_THETA = 10000.0
D_FF = -(-8 * D_MODEL // (3 * 256)) * 256
N_MOD = 6
EPS = 1e-6
NEG_INF = -1e30

LANES = 128
SUBLANES = 8
V7X_VMEM_BYTES = 64 * 1024 * 1024
VMEM_LIMIT_BYTES = V7X_VMEM_BYTES - 8 * 1024 * 1024

M_HEAD_PAD = LANES
M_PAIRS = M_HEADS // 2
ROPE_HALF = M_ROPE // 2
ROPE_LO = M_NOPE
ROPE_MID = M_NOPE + ROPE_HALF
ROPE_HI = M_NOPE + M_ROPE

A_PAIRS = A_HEADS // 2
A_TILES = 3 * A_PAIRS
SUPER_BLOCK = BAND_BLOCK * max(d for _, d in DILATED_PATTERNS)
DILATED_REGROUP = 4
LOG2_E = math.log2(math.e)

ROW_TILE = 512
MOD_COLUMN_BLOCK = D_MODEL
MLA_KEY_TILE = 512
MLA_QUERY_TILE = 2 * MLA_KEY_TILE
MLA_QUERY_CHUNK = 256

F32 = jnp.float32
BF16 = jnp.bfloat16


def _params(n_axes):
    return pltpu.CompilerParams(
        dimension_semantics=("arbitrary",) * n_axes,
        vmem_limit_bytes=VMEM_LIMIT_BYTES,
    )


def _resident(shape):
    zeros = (0,) * len(shape)
    return pl.BlockSpec(shape, lambda *_: zeros, pipeline_mode=pl.Buffered(1))


def _bdot(a, b):
    return jnp.dot(a, b, preferred_element_type=F32)


def _rms(x):
    return x * lax.rsqrt(jnp.mean(x * x, axis=-1, keepdims=True) + EPS)


def _sigmoid(x):
    return 1.0 / (1.0 + jnp.exp(-x))


def _mod_kernel(c_ref, w_ref, b_ref, o_ref):
    c = c_ref[...]
    cond = c * _sigmoid(c)
    o_ref[...] = (
        jnp.dot(cond, w_ref[...], preferred_element_type=F32, precision=lax.Precision.HIGHEST)
        + b_ref[...]
    )


def _modulation(c, w_ada, b_ada):
    batch = c.shape[0]
    rows = -(-batch // SUBLANES) * SUBLANES
    c_pad = jnp.pad(c, ((0, rows - batch), (0, 0)))
    cols = MOD_COLUMN_BLOCK
    out = pl.pallas_call(
        _mod_kernel,
        grid=(N_MOD * D_MODEL // cols,),
        in_specs=[
            pl.BlockSpec((rows, D_MODEL), lambda j: (0, 0)),
            pl.BlockSpec((D_MODEL, cols), lambda j: (0, j)),
            pl.BlockSpec((1, cols), lambda j: (0, j)),
        ],
        out_specs=pl.BlockSpec((rows, cols), lambda j: (0, j)),
        out_shape=jax.ShapeDtypeStruct((rows, N_MOD * D_MODEL), F32),
        compiler_params=_params(1),
        name="adaln_mod",
    )(c_pad, w_ada, b_ada.reshape(1, N_MOD * D_MODEL))
    return out[:batch].reshape(batch, N_MOD, D_MODEL)


def _rope_lanes(x, cos, signed_sin):
    lane = lax.broadcasted_iota(jnp.int32, x.shape, 1)
    partner = jnp.where(lane < ROPE_MID, pltpu.roll(x, LANES - ROPE_HALF, 1), pltpu.roll(x, ROPE_HALF, 1))
    return x * cos + partner * signed_sin


def _input_kernel(x_ref, mod_ref, g_ref, pos_ref, freq_ref, wa_ref, wg_ref, wcq_ref, wckv_ref, wkr_ref,
                  gq_ref, wuq_ref, gkv_ref, wuk_ref, wuv_ref,
                  a_ref, gate_ref, q_ref, k_ref, v_ref):
    x = x_ref[0]
    shift = mod_ref[0, 0:1, :]
    scale = mod_ref[0, 1:2, :]
    h = (_rms(x) * g_ref[...]) * (1.0 + scale) + shift
    hb = h.astype(BF16)

    c_q = _bdot(hb, wcq_ref[...].astype(BF16))
    c_kv = _bdot(hb, wckv_ref[...].astype(BF16))
    k_r = _bdot(hb, wkr_ref[...])

    ang = freq_ref[...] * pos_ref[0, 0].astype(F32)
    cos_r, sin_r = jnp.cos(ang), jnp.sin(ang)
    rows = ang.shape[1]
    cos = jnp.concatenate(
        [jnp.ones((ROPE_LO, rows), F32), cos_r, jnp.ones((LANES - ROPE_HI, rows), F32)], axis=0).T
    signed_sin = jnp.concatenate(
        [jnp.zeros((ROPE_LO, rows), F32), -sin_r[:ROPE_HALF], sin_r[ROPE_HALF:],
         jnp.zeros((LANES - ROPE_HI, rows), F32)], axis=0).T

    q_all = _bdot((_rms(c_q) * gq_ref[...]).astype(BF16), wuq_ref[...])
    q_scale = (M_NOPE + M_ROPE) ** -0.5 * LOG2_E
    for hd in range(M_HEADS):
        q_h = q_all[:, hd * M_HEAD_PAD:(hd + 1) * M_HEAD_PAD]
        q_ref[0, hd] = (_rope_lanes(q_h, cos, signed_sin) * q_scale).astype(BF16)

    c_kv = (_rms(c_kv) * gkv_ref[...]).astype(BF16)
    k_rope = _rope_lanes(k_r, cos, signed_sin)
    k_all = _bdot(c_kv, wuk_ref[...])
    for hd in range(M_HEADS):
        k_ref[0, hd] = (k_all[:, hd * M_HEAD_PAD:(hd + 1) * M_HEAD_PAD] + k_rope).astype(BF16)
    v_all = _bdot(c_kv, wuv_ref[...])
    for pr in range(M_PAIRS):
        v_ref[0, pr, 0] = v_all[:, pr * LANES:(pr + 1) * LANES].T.astype(BF16)

    a_all = _bdot(hb, wa_ref[...].astype(BF16))
    for j in range(A_TILES):
        tile = a_all[:, j * LANES:(j + 1) * LANES]
        a_ref[0, j] = tile * (A_HEAD_DIM ** -0.5 * LOG2_E) if j < A_PAIRS else tile
    gate_ref[0] = _bdot(hb, wg_ref[...]).astype(BF16)


def _input_stage(x, mod, g_mix, positions, w_in, g_q_lora, w_uq, g_kv_lora, w_ukv):
    batch, seq, _ = x.shape
    tm = MLA_KEY_TILE
    s0 = 3 * A_WIDTH
    s1 = s0 + M_Q_LORA
    s2 = s1 + M_KV_LORA
    s3 = s2 + M_ROPE
    assert s0 % M_Q_LORA == 0 and s1 % M_KV_LORA == 0
    w_all = w_in
    w_kr = jnp.pad(w_in[:, s2:s3], ((0, 0), (ROPE_LO, LANES - ROPE_HI))).astype(BF16)
    w_g = w_in[:, s3:].astype(BF16)

    def columns(width, start):
        return pl.BlockSpec((D_MODEL, width), lambda *_: (0, start // width), pipeline_mode=pl.Buffered(1))
    w_uq_p = jnp.pad(w_uq, ((0, 0), (0, 0), (0, M_HEAD_PAD - M_NOPE - M_ROPE)))
    w_uq_p = w_uq_p.reshape(M_Q_LORA, M_HEADS * M_HEAD_PAD).astype(BF16)
    w_uk_p = jnp.pad(w_ukv[:, :, :M_NOPE], ((0, 0), (0, 0), (0, M_HEAD_PAD - M_NOPE)))
    w_uk_p = w_uk_p.reshape(M_KV_LORA, M_HEADS * M_HEAD_PAD).astype(BF16)
    w_uv = w_ukv[:, :, M_NOPE:].reshape(M_KV_LORA, M_WIDTH).astype(BF16)

    freqs = ROPE_THETA ** (-jnp.arange(ROPE_HALF, dtype=F32) / ROPE_HALF)
    freq_col = jnp.concatenate([freqs, freqs]).reshape(M_ROPE, 1)

    row3 = lambda b, i: (b, i, 0)
    head4 = lambda b, i: (b, 0, i, 0)
    return pl.pallas_call(
        _input_kernel,
        grid=(batch, seq // tm),
        in_specs=[
            pl.BlockSpec((1, tm, D_MODEL), row3),
            pl.BlockSpec((1, N_MOD, D_MODEL), lambda b, i: (b, 0, 0)),
            _resident((1, D_MODEL)),
            pl.BlockSpec((1, 1, 1, tm), lambda b, i: (b, i, 0, 0)),
            _resident((M_ROPE, 1)),
            columns(s0, 0), _resident(w_g.shape), columns(M_Q_LORA, s0), columns(M_KV_LORA, s1),
            _resident(w_kr.shape),
            _resident((1, M_Q_LORA)), _resident(w_uq_p.shape),
            _resident((1, M_KV_LORA)), _resident(w_uk_p.shape), _resident(w_uv.shape),
        ],
        out_specs=[
            pl.BlockSpec((1, A_TILES, tm, LANES), head4),
            pl.BlockSpec((1, tm, 2 * D_MODEL), row3),
            pl.BlockSpec((1, M_HEADS, tm, M_HEAD_PAD), head4),
            pl.BlockSpec((1, M_HEADS, tm, M_HEAD_PAD), head4),
            pl.BlockSpec((1, M_PAIRS, 1, LANES, tm), lambda b, i: (b, 0, i, 0, 0)),
        ],
        out_shape=[
            jax.ShapeDtypeStruct((batch, A_TILES, seq, LANES), F32),
            jax.ShapeDtypeStruct((batch, seq, 2 * D_MODEL), BF16),
            jax.ShapeDtypeStruct((batch, M_HEADS, seq, M_HEAD_PAD), BF16),
            jax.ShapeDtypeStruct((batch, M_HEADS, seq, M_HEAD_PAD), BF16),
            jax.ShapeDtypeStruct((batch, M_PAIRS, seq // tm, LANES, tm), BF16),
        ],
        compiler_params=_params(2),
        name="input_stage",
    )(x, mod, g_mix.reshape(1, D_MODEL), positions.reshape(batch, seq // tm, 1, tm), freq_col,
      w_all, w_g, w_all, w_all, w_kr, g_q_lora.reshape(1, M_Q_LORA), w_uq_p,
      g_kv_lora.reshape(1, M_KV_LORA), w_uk_p, w_uv)


def _mla_kernel(q_ref, k_ref, vt_ref, o_ref, m_scr, acc_scr, sa_scr, sb_scr, max_a_scr, max_b_scr):
    tq, tk = MLA_QUERY_TILE, MLA_KEY_TILE
    qc = MLA_QUERY_CHUNK
    n_query_tiles = q_ref.shape[2] // tq
    contract_last = (((1,), (1,)), ((), ()))
    v_row = lax.broadcasted_iota(jnp.int32, (LANES, tk), 0)
    own_rows = [v_row < M_V, v_row >= M_V]
    out_row = lax.broadcasted_iota(jnp.int32, (LANES, tq), 0)

    def reset():
        m_scr[...] = jnp.full(m_scr.shape, NEG_INF, F32)
        acc_scr[...] = jnp.zeros(acc_scr.shape, F32)

    def finish(q_tile):
        o0 = acc_scr[0] * (1.0 / acc_scr[0, M_V:M_V + 1, :])
        o1 = acc_scr[1] * (1.0 / acc_scr[1, 0:1, :])
        rows = pl.ds(pl.multiple_of(q_tile * tq, tq), tq)
        o_ref[0, rows, :] = jnp.where(out_row < M_V, o0, o1).T.astype(BF16)

    def block(scored=None, absorbed=None):
        pieces, chains = [], []
        if scored is not None:
            next_tile, next_s, next_max, diagonal, q_tile = scored
            start = pl.multiple_of(next_tile * tk, tk)
            keys = [k_ref[0, hh, pl.ds(start, tk), :] for hh in range(2)]
            pieces = [(hh, q0) for hh in range(2) for q0 in range(diagonal or 0, tq, qc)]
        if absorbed is not None:
            tile, s_scr, max_scr, q_lo = absorbed
            vt = vt_ref[0, 0, tile]
            vt_aug = [jnp.where(own, vt, jnp.ones_like(vt)) for own in own_rows]
            chains = [(hh, q0) for hh in range(2) for q0 in range(q_lo, tq, qc)]
            state = [(m_scr[hh, :, q0:q0 + qc], acc_scr[hh, :, q0:q0 + qc]) for hh, q0 in chains]

        def score_piece(hh, q0):
            q_rows = pl.ds(pl.multiple_of(q_tile * tq + q0, qc), qc)
            s = lax.dot_general(keys[hh], q_ref[0, hh, q_rows, :], contract_last,
                                preferred_element_type=F32)
            if diagonal is not None and q0 < diagonal + tk - 1:
                key_pos = lax.broadcasted_iota(jnp.int32, s.shape, 0) + diagonal
                query_pos = lax.broadcasted_iota(jnp.int32, s.shape, 1) + q0
                s = jnp.where(key_pos <= query_pos, s, NEG_INF)
            next_s[hh, :, q0:q0 + qc] = s
            next_max[hh, :, q0:q0 + qc] = jnp.max(s, axis=0, keepdims=True)

        def absorb_chain(i):
            hh, q0 = chains[i]
            m_prev, acc_prev = state[i]
            m_new = jnp.maximum(m_prev, max_scr[hh, :, q0:q0 + qc])
            p = jnp.exp2(s_scr[hh, :, q0:q0 + qc] - m_new).astype(BF16)
            acc_scr[hh, :, q0:q0 + qc] = jnp.exp2(m_prev - m_new) * acc_prev + _bdot(vt_aug[hh], p)
            m_scr[hh, :, q0:q0 + qc] = m_new

        if pieces:
            score_piece(*pieces[0])
        for i in range(max(len(pieces) - 1, len(chains))):
            if i + 1 < len(pieces):
                score_piece(*pieces[i + 1])
            if i < len(chains):
                absorb_chain(i)

    buf_a, buf_b = (sa_scr, max_a_scr), (sb_scr, max_b_scr)

    reset()
    block(scored=(0, *buf_a, 0, 0))
    block(scored=(1, *buf_b, tk, 0), absorbed=(0, *buf_a, 0))
    block(scored=(2, *buf_a, 0, 1), absorbed=(1, *buf_b, tk))
    finish(0)

    def query_tile(i, carry):
        reset()
        block(scored=(2 * i + 1, *buf_b, tk, i), absorbed=(2 * i, *buf_a, 0))
        block(scored=(0, *buf_a, None, i), absorbed=(2 * i + 1, *buf_b, tk))

        def pair(j):
            block(scored=(2 * j + 1, *buf_b, None, i), absorbed=(2 * j, *buf_a, 0))
            block(scored=(2 * j + 2, *buf_a, None, i), absorbed=(2 * j + 1, *buf_b, 0))

        def two_pairs(j, inner):
            pair(2 * j)
            pair(2 * j + 1)
            return inner

        lax.fori_loop(0, (i - 1) // 2, two_pairs, 0)

        @pl.when((i - 1) % 2 == 1)
        def _odd_pair():
            pair(i - 2)

        block(scored=(2 * i - 1, *buf_b, None, i), absorbed=(2 * i - 2, *buf_a, 0))
        nxt = jnp.minimum(i + 1, n_query_tiles - 1)
        block(scored=(2 * nxt, *buf_a, 0, nxt), absorbed=(2 * i - 1, *buf_b, 0))
        finish(i)
        return carry

    lax.fori_loop(1, n_query_tiles, query_tile, 0)


def _mla_attention(q, k, vt):
    batch, _, seq, _ = q.shape
    tq, tk = MLA_QUERY_TILE, MLA_KEY_TILE
    assert tq == 2 * tk and seq % tq == 0 and seq // tq >= 2
    return pl.pallas_call(
        _mla_kernel,
        grid=(batch, M_PAIRS),
        in_specs=[
            pl.BlockSpec((1, 2, seq, M_HEAD_PAD), lambda b, p: (b, p, 0, 0)),
            pl.BlockSpec((1, 2, seq, M_HEAD_PAD), lambda b, p: (b, p, 0, 0)),
            pl.BlockSpec((1, 1, seq // tk, LANES, tk), lambda b, p: (b, p, 0, 0, 0)),
        ],
        out_specs=pl.BlockSpec((1, seq, LANES), lambda b, p: (b, 0, p)),
        out_shape=jax.ShapeDtypeStruct((batch, seq, M_WIDTH), BF16),
        scratch_shapes=[
            pltpu.VMEM((2, 1, tq), F32),
            pltpu.VMEM((2, LANES, tq), F32),
            pltpu.VMEM((2, tk, tq), F32),
            pltpu.VMEM((2, tk, tq), F32),
            pltpu.VMEM((2, 1, tq), F32),
            pltpu.VMEM((2, 1, tq), F32),
        ],
        compiler_params=_params(2),
        name="mla_attention",
    )(q, k, vt)


def _t5_bucket_table(dilation, n_back):
    blk = BAND_BLOCK
    sub_dist = (np.arange(blk)[:, None] + blk) - np.arange(2 * blk)[None, :]
    dist = np.clip(sub_dist, 0, n_back) * dilation
    max_exact = REL_BUCKETS // 2
    d = np.maximum(dist, 1).astype(np.float32)
    ratio = np.log(d / np.float32(max_exact)) / np.float32(math.log(REL_MAX_DIST / max_exact))
    log_b = max_exact + (ratio * np.float32(REL_BUCKETS - max_exact)).astype(np.int32)
    log_b = np.minimum(log_b, REL_BUCKETS - 1)
    return np.where(dist < max_exact, dist, log_b).astype(np.int32)


def _rows(start, count, stride):
    return pl.ds(start, count) if stride == 1 else pl.ds(start, count, stride=stride)


def _dilated_kernel(rb_ref, bucket_ref, q_ref, kc_ref, kp_ref, vc_ref, vp_ref, o_ref,
                    bias_scr, acc_scr, m_scr, regroup_scr, out_scr):
    blk = BAND_BLOCK
    sup = SUPER_BLOCK
    grp = DILATED_REGROUP
    sub = sup // grp
    pair = pl.program_id(2)
    first_step = (pl.program_id(0) == 0) & (pl.program_id(1) == 0) & (pair == 0)

    @pl.when(first_step)
    def _build_bias():
        row = lax.broadcasted_iota(jnp.int32, (blk, 2 * blk), 0)
        col = lax.broadcasted_iota(jnp.int32, (blk, 2 * blk), 1)
        sub_dist = row + blk - col
        in_band = (sub_dist >= 0) & (sub_dist <= blk)
        for g in range(len(DILATED_PATTERNS)):
            bucket = bucket_ref[g]
            for hd in range(A_HEADS):
                bias = jnp.zeros((blk, 2 * blk), F32)
                for bk in range(REL_BUCKETS):
                    bias = jnp.where(bucket == bk, rb_ref[hd, bk] * LOG2_E, bias)
                bias_scr[g, hd] = jnp.where(in_band, bias, NEG_INF)

    sources = (q_ref, kc_ref, kp_ref, vc_ref, vp_ref)
    for idx, ref in enumerate(sources):
        for r in range(grp):
            regroup_scr[idx, r * sub:(r + 1) * sub, :] = ref[0, 0, _rows(r, sub, grp), :]

    first_valid_col = jnp.where(pl.program_id(1) > 0, 0, blk)
    col = lax.broadcasted_iota(jnp.int32, (blk, 2 * blk), 1)
    lane = lax.broadcasted_iota(jnp.int32, (blk, LANES), 1)
    lane2 = lax.broadcasted_iota(jnp.int32, (2 * blk, LANES), 1)
    contract_last = (((1,), (1,)), ((), ()))
    n_pat = len(DILATED_PATTERNS)

    for g, (_, dil) in enumerate(DILATED_PATTERNS):
        regrouped = dil % grp == 0
        step = dil // grp if regrouped else dil
        for res in range(dil):
            base = (res % grp) * sub + res // grp if regrouped else res
            prev_base = base + (sub if regrouped else sup) - blk * step
            if regrouped:
                load = lambda idx, r: regroup_scr[idx, r, :].astype(BF16)
            else:
                load = lambda idx, r: sources[idx][0, 0, r, :].astype(BF16)
            prev_rows = _rows(prev_base, blk, step)
            k_prev, v_prev = load(2, prev_rows), load(4, prev_rows)
            for n in range(sup // (blk * dil)):
                rows = _rows(base + blk * step * n, blk, step)
                q, k_cur, v_cur = load(0, rows), load(1, rows), load(3, rows)
                k2 = jnp.concatenate([k_prev, k_cur], axis=0)
                v2 = jnp.concatenate([v_prev, v_cur], axis=0)
                k_prev, v_prev = k_cur, v_cur
                for hh in range(2):
                    in_head = (lane < A_HEAD_DIM) if hh == 0 else (lane >= A_HEAD_DIM)
                    in_head2 = (lane2 < A_HEAD_DIM) if hh == 0 else (lane2 >= A_HEAD_DIM)
                    s = lax.dot_general(jnp.where(in_head, q, jnp.zeros_like(q)), k2, contract_last,
                                        preferred_element_type=F32)
                    bias = bias_scr[g, 2 * pair + hh]
                    if n == 0:
                        bias = jnp.where(col >= first_valid_col, bias, NEG_INF)
                    s = s + bias
                    m_blk = jnp.max(s, axis=-1, keepdims=True)
                    p = jnp.exp2(s - m_blk).astype(BF16)
                    acc_scr[g, hh, rows, :] = _bdot(p, jnp.where(in_head2, v2, jnp.ones_like(v2)))
                    m_scr[g, hh, rows, :] = jnp.broadcast_to(m_blk, (blk, LANES))

    lane_sub = lax.broadcasted_iota(jnp.int32, (sub, LANES), 1)
    for r in range(grp):
        chunk = [pl.ds(r * sub, sub) if dil % grp == 0 else _rows(r, sub, grp) for _, dil in DILATED_PATTERNS]
        halves = []
        for hh in range(2):
            maxes = [m_scr[g, hh, chunk[g], :] for g in range(n_pat)]
            top = maxes[0]
            for g in range(1, n_pat):
                top = jnp.maximum(top, maxes[g])
            total = jnp.exp2(maxes[0] - top) * acc_scr[0, hh, chunk[0], :]
            for g in range(1, n_pat):
                total = total + jnp.exp2(maxes[g] - top) * acc_scr[g, hh, chunk[g], :]
            halves.append(total * (1.0 / pltpu.roll(total, A_HEAD_DIM, 1)))
        out_scr[_rows(r, sub, grp), :] = jnp.where(lane_sub < A_HEAD_DIM, halves[0], halves[1])
    o_ref[0] = out_scr[...].astype(BF16)


def _dilated_attention(a_qkv, rel_bias):
    batch, _, seq, _ = a_qkv.shape
    blk = BAND_BLOCK
    sup = SUPER_BLOCK
    n_pat = len(DILATED_PATTERNS)
    assert all(w // d == blk for w, d in DILATED_PATTERNS), "band of exactly one block behind the query"
    assert seq % sup == 0
    bucket = jnp.asarray(np.stack([_t5_bucket_table(d, w // d) for w, d in DILATED_PATTERNS]))

    def part(which, prev):
        def index(b, s, p):
            return (b, which * A_PAIRS + p, jnp.maximum(s - 1, 0) if prev else s, 0)
        return pl.BlockSpec((1, 1, sup, LANES), index)

    return pl.pallas_call(
        _dilated_kernel,
        grid=(batch, seq // sup, A_PAIRS),
        in_specs=[
            pl.BlockSpec(memory_space=pltpu.SMEM),
            pl.BlockSpec((n_pat, blk, 2 * blk), lambda b, s, p: (0, 0, 0)),
            part(0, False), part(1, False), part(1, True), part(2, False), part(2, True),
        ],
        out_specs=pl.BlockSpec((1, sup, LANES), lambda b, s, p: (b, s, p)),
        out_shape=jax.ShapeDtypeStruct((batch, seq, A_WIDTH), BF16),
        scratch_shapes=[
            pltpu.VMEM((n_pat, A_HEADS, blk, 2 * blk), F32),
            pltpu.VMEM((n_pat, 2, sup, LANES), F32),
            pltpu.VMEM((n_pat, 2, sup, LANES), F32),
            pltpu.VMEM((5, sup, LANES), F32),
            pltpu.VMEM((sup, LANES), F32),
        ],
        compiler_params=_params(3),
        name="dilated_attention",
    )(rel_bias, bucket, a_qkv, a_qkv, a_qkv, a_qkv, a_qkv)


def _output_kernel(x_ref, mod_ref, oa_ref, ob_ref, gate_ref, wa_ref, wb_ref, wo_ref,
                   g_ref, gf_ref, wg_ref, wu_ref, wd_ref, out_ref):
    rows = x_ref.shape[1] // 2
    halves = [slice(0, rows), slice(rows, 2 * rows)]

    mixed = []
    for r in halves:
        y_a = _bdot(oa_ref[0, r, :], wa_ref[...])
        y_b = _bdot(ob_ref[0, r, :], wb_ref[...])
        gates = gate_ref[0, r, :].astype(F32)
        merged = _sigmoid(gates[:, :D_MODEL]) * y_a + _sigmoid(gates[:, D_MODEL:]) * y_b
        mixed.append(_bdot(merged.astype(BF16), wo_ref[...]))

    xs, acts = [], []
    for r, mix in zip(halves, mixed):
        x = x_ref[0, r, :] + mod_ref[0, 2:3, :] * mix
        hb = ((_rms(x) * g_ref[...]) * (1.0 + mod_ref[0, 4:5, :]) + mod_ref[0, 3:4, :]).astype(BF16)
        gate = _bdot(hb, wg_ref[...])
        up = _bdot(hb, wu_ref[...])
        xs.append(x)
        acts.append((gate * _sigmoid(gate) * up).astype(BF16))

    for r, x, act in zip(halves, xs, acts):
        y = x + mod_ref[0, 5:6, :] * _bdot(act, wd_ref[...])
        out_ref[0, r, :] = _rms(y) * gf_ref[...]


def _output_stage(x, mod, o_a, o_b, gates, w_up_a, w_up_b, w_o, g_ffn, g_final, w_gate, w_up, w_down):
    batch, seq, _ = x.shape
    tm = ROW_TILE
    row3 = lambda b, i: (b, i, 0)
    half = pl.BlockSpec((1, tm, A_WIDTH), row3)
    return pl.pallas_call(
        _output_kernel,
        grid=(batch, seq // tm),
        in_specs=[
            pl.BlockSpec((1, tm, D_MODEL), row3),
            pl.BlockSpec((1, N_MOD, D_MODEL), lambda b, i: (b, 0, 0)),
            half, half,
            pl.BlockSpec((1, tm, 2 * D_MODEL), row3),
            _resident((A_WIDTH, D_MODEL)), _resident((M_WIDTH, D_MODEL)), _resident((D_MODEL, D_MODEL)),
            _resident((1, D_MODEL)), _resident((1, D_MODEL)),
            _resident((D_MODEL, D_FF)), _resident((D_MODEL, D_FF)), _resident((D_FF, D_MODEL)),
        ],
        out_specs=pl.BlockSpec((1, tm, D_MODEL), row3),
        out_shape=jax.ShapeDtypeStruct((batch, seq, D_MODEL), F32),
        compiler_params=_params(2),
        name="output_stage",
    )(x, mod, o_a, o_b, gates,
      w_up_a.astype(BF16), w_up_b.astype(BF16), w_o.astype(BF16),
      g_ffn.reshape(1, D_MODEL), g_final.reshape(1, D_MODEL),
      w_gate.astype(BF16), w_up.astype(BF16), w_down.astype(BF16))


def kernel(x, c, positions, rel_bias, w_ada, b_ada, g_mix, w_in, g_q_lora, w_uq, g_kv_lora, w_ukv,
           w_up_a, w_up_b, w_o, g_ffn, w_gate, w_up, w_down, g_final):
    assert w_ada.shape[0] == 1, "single-layer trunk"
    mod = _modulation(c, w_ada[0], b_ada[0])
    a_qkv, gates, q, k, vt = _input_stage(x, mod, g_mix[0], positions, w_in[0], g_q_lora[0], w_uq[0],
                                         g_kv_lora[0], w_ukv[0])
    o_b = _mla_attention(q, k, vt)
    o_a = _dilated_attention(a_qkv, rel_bias)
    return _output_stage(x, mod, o_a, o_b, gates, w_up_a[0], w_up_b[0], w_o[0],
                         g_ffn[0], g_final, w_gate[0], w_up[0], w_down[0])
```

```python
import math

import jax
import jax.numpy as jnp
import numpy as np
from jax import lax
from jax.experimental import pallas as pl
from jax.experimental.pallas import tpu as pltpu

D_MODEL = 1024
A_HEADS = 8
A_HEAD_DIM = 64
A_WIDTH = A_HEADS * A_HEAD_DIM
DILATED_PATTERNS = ((128, 1), (512, 4), (2048, 16))
BAND_BLOCK = 128
REL_BUCKETS = 32
REL_MAX_DIST = 2048
M_HEADS = 8
M_NOPE = 64
M_ROPE = 32
M_V = 64
M_Q_LORA = 768
M_KV_LORA = 256
M_WIDTH = M_HEADS * M_V
ROPE_THETA = 10000.0
D_FF = -(-8 * D_MODEL // (3 * 256)) * 256
N_MOD = 6
EPS = 1e-6
NEG_INF = -1e30

LANES = 128
SUBLANES = 8
V7X_VMEM_BYTES = 64 * 1024 * 1024
VMEM_LIMIT_BYTES = V7X_VMEM_BYTES - 8 * 1024 * 1024

M_HEAD_PAD = LANES
M_PAIRS = M_HEADS // 2
ROPE_HALF = M_ROPE // 2
ROPE_LO = M_NOPE
ROPE_MID = M_NOPE + ROPE_HALF
ROPE_HI = M_NOPE + M_ROPE

A_PAIRS = A_HEADS // 2
A_TILES = 3 * A_PAIRS
SUPER_BLOCK = BAND_BLOCK * max(d for _, d in DILATED_PATTERNS)
DILATED_REGROUP = 4
LOG2_E = math.log2(math.e)

ROW_TILE = 512
MOD_COLUMN_BLOCK = D_MODEL
MLA_KEY_TILE = 512
MLA_QUERY_TILE = 2 * MLA_KEY_TILE
MLA_QUERY_CHUNK = 256

F32 = jnp.float32
BF16 = jnp.bfloat16


def _params(n_axes):
    return pltpu.CompilerParams(
        dimension_semantics=("arbitrary",) * n_axes,
        vmem_limit_bytes=VMEM_LIMIT_BYTES,
    )


def _resident(shape):
    zeros = (0,) * len(shape)
    return pl.BlockSpec(shape, lambda *_: zeros, pipeline_mode=pl.Buffered(1))


def _bdot(a, b):
    return jnp.dot(a, b, preferred_element_type=F32)


def _bdot_t(a, b_t):
    return lax.dot_general(a, b_t, (((1,), (1,)), ((), ())), preferred_element_type=F32)


def _rms(x):
    return x * lax.rsqrt(jnp.mean(x * x, axis=-1, keepdims=True) + EPS)


def _sigmoid(x):
    return 1.0 / (1.0 + jnp.exp(-x))


def _mod_kernel(c_ref, w_ref, b_ref, o_ref):
    c = c_ref[...]
    cond = c * _sigmoid(c)
    o_ref[...] = (
        jnp.dot(cond, w_ref[...], preferred_element_type=F32, precision=lax.Precision.HIGHEST)
        + b_ref[...]
    )


def _modulation(c, w_ada, b_ada):
    batch = c.shape[0]
    rows = -(-batch // SUBLANES) * SUBLANES
    c_pad = jnp.pad(c, ((0, rows - batch), (0, 0)))
    cols = MOD_COLUMN_BLOCK
    out = pl.pallas_call(
        _mod_kernel,
        grid=(N_MOD * D_MODEL // cols,),
        in_specs=[
            pl.BlockSpec((rows, D_MODEL), lambda j: (0, 0)),
            pl.BlockSpec((D_MODEL, cols), lambda j: (0, j)),
            pl.BlockSpec((1, cols), lambda j: (0, j)),
        ],
        out_specs=pl.BlockSpec((rows, cols), lambda j: (0, j)),
        out_shape=jax.ShapeDtypeStruct((rows, N_MOD * D_MODEL), F32),
        compiler_params=_params(1),
        name="adaln_mod",
    )(c_pad, w_ada, b_ada.reshape(1, N_MOD * D_MODEL))
    return out[:batch].reshape(batch, N_MOD, D_MODEL)


def _rope_lanes(x, cos, signed_sin):
    lane = lax.broadcasted_iota(jnp.int32, x.shape, 1)
    partner = jnp.where(lane < ROPE_MID, pltpu.roll(x, LANES - ROPE_HALF, 1), pltpu.roll(x, ROPE_HALF, 1))
    return x * cos + partner * signed_sin


def _input_kernel(x_ref, mod_ref, g_ref, pos_ref, freq_ref, wa_ref, wg_ref, wcq_ref, wckv_ref, wkr_ref,
                  gq_ref, wuq_ref, gkv_ref, wuk_ref, wuv_ref,
                  a_ref, gate_ref, q_ref, k_ref, v_ref):
    x = x_ref[0]
    shift = mod_ref[0, 0:1, :]
    scale = mod_ref[0, 1:2, :]
    h = (_rms(x) * g_ref[...]) * (1.0 + scale) + shift
    hb = h.astype(BF16)

    c_q = _bdot_t(hb, wcq_ref[...])
    c_kv = _bdot_t(hb, wckv_ref[...])
    k_r = _bdot_t(hb, wkr_ref[...])

    ang = freq_ref[...] * pos_ref[0, 0].astype(F32)
    cos_r, sin_r = jnp.cos(ang), jnp.sin(ang)
    rows = ang.shape[1]
    cos = jnp.concatenate(
        [jnp.ones((ROPE_LO, rows), F32), cos_r, jnp.ones((LANES - ROPE_HI, rows), F32)], axis=0).T
    signed_sin = jnp.concatenate(
        [jnp.zeros((ROPE_LO, rows), F32), -sin_r[:ROPE_HALF], sin_r[ROPE_HALF:],
         jnp.zeros((LANES - ROPE_HI, rows), F32)], axis=0).T

    q_all = _bdot((_rms(c_q) * gq_ref[...]).astype(BF16), wuq_ref[...])
    q_scale = (M_NOPE + M_ROPE) ** -0.5 * LOG2_E
    for hd in range(M_HEADS):
        q_h = q_all[:, hd * M_HEAD_PAD:(hd + 1) * M_HEAD_PAD]
        q_ref[0, hd] = (_rope_lanes(q_h, cos, signed_sin) * q_scale).astype(BF16)

    c_kv = (_rms(c_kv) * gkv_ref[...]).astype(BF16)
    k_rope = _rope_lanes(k_r, cos, signed_sin)
    k_all = _bdot(c_kv, wuk_ref[...])
    for hd in range(M_HEADS):
        k_ref[0, hd] = (k_all[:, hd * M_HEAD_PAD:(hd + 1) * M_HEAD_PAD] + k_rope).astype(BF16)
    v_all = _bdot(c_kv, wuv_ref[...])
    for pr in range(M_PAIRS):
        v_ref[0, pr, 0] = v_all[:, pr * LANES:(pr + 1) * LANES].T.astype(BF16)

    a_all = _bdot_t(hb, wa_ref[...])
    for j in range(A_TILES):
        tile = a_all[:, j * LANES:(j + 1) * LANES]
        a_ref[0, j] = tile * (A_HEAD_DIM ** -0.5 * LOG2_E) if j < A_PAIRS else tile
    gate_ref[0] = _bdot_t(hb, wg_ref[...]).astype(BF16)


def _input_stage(x, mod, g_mix, positions, w_in, g_q_lora, w_uq, g_kv_lora, w_ukv):
    batch, seq, _ = x.shape
    tm = MLA_KEY_TILE
    s0 = 3 * A_WIDTH
    s1 = s0 + M_Q_LORA
    s2 = s1 + M_KV_LORA
    s3 = s2 + M_ROPE
    assert s0 % M_Q_LORA == 0 and s1 % M_KV_LORA == 0
    w_all = w_in.T.astype(BF16)
    w_kr = jnp.pad(w_all[s2:s3], ((ROPE_LO, LANES - ROPE_HI), (0, 0)))
    w_g = w_all[s3:]

    def columns(width, start):
        return pl.BlockSpec((width, D_MODEL), lambda *_: (start // width, 0), pipeline_mode=pl.Buffered(1))
    w_uq_p = jnp.pad(w_uq, ((0, 0), (0, 0), (0, M_HEAD_PAD - M_NOPE - M_ROPE)))
    w_uq_p = w_uq_p.reshape(M_Q_LORA, M_HEADS * M_HEAD_PAD).astype(BF16)
    w_uk_p = jnp.pad(w_ukv[:, :, :M_NOPE], ((0, 0), (0, 0), (0, M_HEAD_PAD - M_NOPE)))
    w_uk_p = w_uk_p.reshape(M_KV_LORA, M_HEADS * M_HEAD_PAD).astype(BF16)
    w_uv = w_ukv[:, :, M_NOPE:].reshape(M_KV_LORA, M_WIDTH).astype(BF16)

    freqs = ROPE_THETA ** (-jnp.arange(ROPE_HALF, dtype=F32) / ROPE_HALF)
    freq_col = jnp.concatenate([freqs, freqs]).reshape(M_ROPE, 1)

    row3 = lambda b, i: (b, i, 0)
    head4 = lambda b, i: (b, 0, i, 0)
    return pl.pallas_call(
        _input_kernel,
        grid=(batch, seq // tm),
        in_specs=[
            pl.BlockSpec((1, tm, D_MODEL), row3),
            pl.BlockSpec((1, N_MOD, D_MODEL), lambda b, i: (b, 0, 0)),
            _resident((1, D_MODEL)),
            pl.BlockSpec((1, 1, 1, tm), lambda b, i: (b, i, 0, 0)),
            _resident((M_ROPE, 1)),
            columns(s0, 0), _resident(w_g.shape), columns(M_Q_LORA, s0), columns(M_KV_LORA, s1),
            _resident(w_kr.shape),
            _resident((1, M_Q_LORA)), _resident(w_uq_p.shape),
            _resident((1, M_KV_LORA)), _resident(w_uk_p.shape), _resident(w_uv.shape),
        ],
        out_specs=[
            pl.BlockSpec((1, A_TILES, tm, LANES), head4),
            pl.BlockSpec((1, tm, 2 * D_MODEL), row3),
            pl.BlockSpec((1, M_HEADS, tm, M_HEAD_PAD), head4),
            pl.BlockSpec((1, M_HEADS, tm, M_HEAD_PAD), head4),
            pl.BlockSpec((1, M_PAIRS, 1, LANES, tm), lambda b, i: (b, 0, i, 0, 0)),
        ],
        out_shape=[
            jax.ShapeDtypeStruct((batch, A_TILES, seq, LANES), F32),
            jax.ShapeDtypeStruct((batch, seq, 2 * D_MODEL), BF16),
            jax.ShapeDtypeStruct((batch, M_HEADS, seq, M_HEAD_PAD), BF16),
            jax.ShapeDtypeStruct((batch, M_HEADS, seq, M_HEAD_PAD), BF16),
            jax.ShapeDtypeStruct((batch, M_PAIRS, seq // tm, LANES, tm), BF16),
        ],
        compiler_params=_params(2),
        name="input_stage",
    )(x, mod, g_mix.reshape(1, D_MODEL), positions.reshape(batch, seq // tm, 1, tm), freq_col,
      w_all, w_g, w_all, w_all, w_kr, g_q_lora.reshape(1, M_Q_LORA), w_uq_p,
      g_kv_lora.reshape(1, M_KV_LORA), w_uk_p, w_uv)


def _mla_kernel(q_ref, k_ref, vt_ref, o_ref, m_scr, acc_scr, sa_scr, sb_scr, max_a_scr, max_b_scr):
    tq, tk = MLA_QUERY_TILE, MLA_KEY_TILE
    qc = MLA_QUERY_CHUNK
    n_query_tiles = q_ref.shape[2] // tq
    contract_last = (((1,), (1,)), ((), ()))
    v_row = lax.broadcasted_iota(jnp.int32, (LANES, tk), 0)
    own_rows = [v_row < M_V, v_row >= M_V]
    out_row = lax.broadcasted_iota(jnp.int32, (LANES, tq), 0)

    def reset():
        m_scr[...] = jnp.full(m_scr.shape, NEG_INF, F32)
        acc_scr[...] = jnp.zeros(acc_scr.shape, F32)

    def finish(q_tile):
        o0 = acc_scr[0] * (1.0 / acc_scr[0, M_V:M_V + 1, :])
        o1 = acc_scr[1] * (1.0 / acc_scr[1, 0:1, :])
        rows = pl.ds(pl.multiple_of(q_tile * tq, tq), tq)
        o_ref[0, rows, :] = jnp.where(out_row < M_V, o0, o1).T.astype(BF16)

    def block(scored=None, absorbed=None):
        pieces, chains = [], []
        if scored is not None:
            next_tile, next_s, next_max, diagonal, q_tile = scored
            start = pl.multiple_of(next_tile * tk, tk)
            keys = [k_ref[0, hh, pl.ds(start, tk), :] for hh in range(2)]
            pieces = [(hh, q0) for hh in range(2) for q0 in range(diagonal or 0, tq, qc)]
        if absorbed is not None:
            tile, s_scr, max_scr, q_lo = absorbed
            vt = vt_ref[0, 0, tile]
            vt_aug = [jnp.where(own, vt, jnp.ones_like(vt)) for own in own_rows]
            chains = [(hh, q0) for hh in range(2) for q0 in range(q_lo, tq, qc)]
            state = [(m_scr[hh, :, q0:q0 + qc], acc_scr[hh, :, q0:q0 + qc]) for hh, q0 in chains]

        def score_piece(hh, q0):
            q_rows = pl.ds(pl.multiple_of(q_tile * tq + q0, qc), qc)
            s = lax.dot_general(keys[hh], q_ref[0, hh, q_rows, :], contract_last,
                                preferred_element_type=F32)
            if diagonal is not None and q0 < diagonal + tk - 1:
                key_pos = lax.broadcasted_iota(jnp.int32, s.shape, 0) + diagonal
                query_pos = lax.broadcasted_iota(jnp.int32, s.shape, 1) + q0
                s = jnp.where(key_pos <= query_pos, s, NEG_INF)
            next_s[hh, :, q0:q0 + qc] = s
            next_max[hh, :, q0:q0 + qc] = jnp.max(s, axis=0, keepdims=True)

        def absorb_chain(i):
            hh, q0 = chains[i]
            m_prev, acc_prev = state[i]
            m_new = jnp.maximum(m_prev, max_scr[hh, :, q0:q0 + qc])
            p = jnp.exp2(s_scr[hh, :, q0:q0 + qc] - m_new).astype(BF16)
            acc_scr[hh, :, q0:q0 + qc] = jnp.exp2(m_prev - m_new) * acc_prev + _bdot(vt_aug[hh], p)
            m_scr[hh, :, q0:q0 + qc] = m_new

        if pieces:
            score_piece(*pieces[0])
        for i in range(max(len(pieces) - 1, len(chains))):
            if i + 1 < len(pieces):
                score_piece(*pieces[i + 1])
            if i < len(chains):
                absorb_chain(i)

    buf_a, buf_b = (sa_scr, max_a_scr), (sb_scr, max_b_scr)

    reset()
    block(scored=(0, *buf_a, 0, 0))
    block(scored=(1, *buf_b, tk, 0), absorbed=(0, *buf_a, 0))
    block(scored=(2, *buf_a, 0, 1), absorbed=(1, *buf_b, tk))
    finish(0)

    def query_tile(i, carry):
        reset()
        block(scored=(2 * i + 1, *buf_b, tk, i), absorbed=(2 * i, *buf_a, 0))
        block(scored=(0, *buf_a, None, i), absorbed=(2 * i + 1, *buf_b, tk))

        def pair(j):
            block(scored=(2 * j + 1, *buf_b, None, i), absorbed=(2 * j, *buf_a, 0))
            block(scored=(2 * j + 2, *buf_a, None, i), absorbed=(2 * j + 1, *buf_b, 0))

        def two_pairs(j, inner):
            pair(2 * j)
            pair(2 * j + 1)
            return inner

        lax.fori_loop(0, (i - 1) // 2, two_pairs, 0)

        @pl.when((i - 1) % 2 == 1)
        def _odd_pair():
            pair(i - 2)

        block(scored=(2 * i - 1, *buf_b, None, i), absorbed=(2 * i - 2, *buf_a, 0))
        nxt = jnp.minimum(i + 1, n_query_tiles - 1)
        block(scored=(2 * nxt, *buf_a, 0, nxt), absorbed=(2 * i - 1, *buf_b, 0))
        finish(i)
        return carry

    lax.fori_loop(1, n_query_tiles, query_tile, 0)


def _mla_attention(q, k, vt):
    batch, _, seq, _ = q.shape
    tq, tk = MLA_QUERY_TILE, MLA_KEY_TILE
    assert tq == 2 * tk and seq % tq == 0 and seq // tq >= 2
    return pl.pallas_call(
        _mla_kernel,
        grid=(batch, M_PAIRS),
        in_specs=[
            pl.BlockSpec((1, 2, seq, M_HEAD_PAD), lambda b, p: (b, p, 0, 0)),
            pl.BlockSpec((1, 2, seq, M_HEAD_PAD), lambda b, p: (b, p, 0, 0)),
            pl.BlockSpec((1, 1, seq // tk, LANES, tk), lambda b, p: (b, p, 0, 0, 0)),
        ],
        out_specs=pl.BlockSpec((1, seq, LANES), lambda b, p: (b, 0, p)),
        out_shape=jax.ShapeDtypeStruct((batch, seq, M_WIDTH), BF16),
        scratch_shapes=[
            pltpu.VMEM((2, 1, tq), F32),
            pltpu.VMEM((2, LANES, tq), F32),
            pltpu.VMEM((2, tk, tq), F32),
            pltpu.VMEM((2, tk, tq), F32),
            pltpu.VMEM((2, 1, tq), F32),
            pltpu.VMEM((2, 1, tq), F32),
        ],
        compiler_params=_params(2),
        name="mla_attention",
    )(q, k, vt)


def _t5_bucket_table(dilation, n_back):
    blk = BAND_BLOCK
    sub_dist = (np.arange(blk)[:, None] + blk) - np.arange(2 * blk)[None, :]
    dist = np.clip(sub_dist, 0, n_back) * dilation
    max_exact = REL_BUCKETS // 2
    d = np.maximum(dist, 1).astype(np.float32)
    ratio = np.log(d / np.float32(max_exact)) / np.float32(math.log(REL_MAX_DIST / max_exact))
    log_b = max_exact + (ratio * np.float32(REL_BUCKETS - max_exact)).astype(np.int32)
    log_b = np.minimum(log_b, REL_BUCKETS - 1)
    return np.where(dist < max_exact, dist, log_b).astype(np.int32)


def _rows(start, count, stride):
    return pl.ds(start, count) if stride == 1 else pl.ds(start, count, stride=stride)


def _dilated_kernel(rb_ref, bucket_ref, q_ref, kc_ref, kp_ref, vc_ref, vp_ref, o_ref,
                    bias_scr, acc_scr, m_scr, regroup_scr, out_scr):
    blk = BAND_BLOCK
    sup = SUPER_BLOCK
    grp = DILATED_REGROUP
    sub = sup // grp
    pair = pl.program_id(2)
    first_step = (pl.program_id(0) == 0) & (pl.program_id(1) == 0) & (pair == 0)

    @pl.when(first_step)
    def _build_bias():
        row = lax.broadcasted_iota(jnp.int32, (blk, 2 * blk), 0)
        col = lax.broadcasted_iota(jnp.int32, (blk, 2 * blk), 1)
        sub_dist = row + blk - col
        in_band = (sub_dist >= 0) & (sub_dist <= blk)
        for g in range(len(DILATED_PATTERNS)):
            bucket = bucket_ref[g]
            for hd in range(A_HEADS):
                bias = jnp.zeros((blk, 2 * blk), F32)
                for bk in range(REL_BUCKETS):
                    bias = jnp.where(bucket == bk, rb_ref[hd, bk] * LOG2_E, bias)
                bias_scr[g, hd] = jnp.where(in_band, bias, NEG_INF)

    sources = (q_ref, kc_ref, kp_ref, vc_ref, vp_ref)
    for idx, ref in enumerate(sources):
        for r in range(grp):
            regroup_scr[idx, r * sub:(r + 1) * sub, :] = ref[0, 0, _rows(r, sub, grp), :]

    first_valid_col = jnp.where(pl.program_id(1) > 0, 0, blk)
    col = lax.broadcasted_iota(jnp.int32, (blk, 2 * blk), 1)
    lane = lax.broadcasted_iota(jnp.int32, (blk, LANES), 1)
    lane2 = lax.broadcasted_iota(jnp.int32, (2 * blk, LANES), 1)
    contract_last = (((1,), (1,)), ((), ()))
    n_pat = len(DILATED_PATTERNS)

    for g, (_, dil) in enumerate(DILATED_PATTERNS):
        regrouped = dil % grp == 0
        step = dil // grp if regrouped else dil
        for res in range(dil):
            base = (res % grp) * sub + res // grp if regrouped else res
            prev_base = base + (sub if regrouped else sup) - blk * step
            if regrouped:
                load = lambda idx, r: regroup_scr[idx, r, :].astype(BF16)
            else:
                load = lambda idx, r: sources[idx][0, 0, r, :].astype(BF16)
            prev_rows = _rows(prev_base, blk, step)
            k_prev, v_prev = load(2, prev_rows), load(4, prev_rows)
            for n in range(sup // (blk * dil)):
                rows = _rows(base + blk * step * n, blk, step)
                q, k_cur, v_cur = load(0, rows), load(1, rows), load(3, rows)
                k2 = jnp.concatenate([k_prev, k_cur], axis=0)
                v2 = jnp.concatenate([v_prev, v_cur], axis=0)
                k_prev, v_prev = k_cur, v_cur
                for hh in range(2):
                    in_head = (lane < A_HEAD_DIM) if hh == 0 else (lane >= A_HEAD_DIM)
                    in_head2 = (lane2 < A_HEAD_DIM) if hh == 0 else (lane2 >= A_HEAD_DIM)
                    s = lax.dot_general(jnp.where(in_head, q, jnp.zeros_like(q)), k2, contract_last,
                                        preferred_element_type=F32)
                    bias = bias_scr[g, 2 * pair + hh]
                    if n == 0:
                        bias = jnp.where(col >= first_valid_col, bias, NEG_INF)
                    s = s + bias
                    m_blk = jnp.max(s, axis=-1, keepdims=True)
                    p = jnp.exp2(s - m_blk).astype(BF16)
                    acc_scr[g, hh, rows, :] = _bdot(p, jnp.where(in_head2, v2, jnp.ones_like(v2)))
                    m_scr[g, hh, rows, :] = jnp.broadcast_to(m_blk, (blk, LANES))

    lane_sub = lax.broadcasted_iota(jnp.int32, (sub, LANES), 1)
    for r in range(grp):
        chunk = [pl.ds(r * sub, sub) if dil % grp == 0 else _rows(r, sub, grp) for _, dil in DILATED_PATTERNS]
        halves = []
        for hh in range(2):
            maxes = [m_scr[g, hh, chunk[g], :] for g in range(n_pat)]
            top = maxes[0]
            for g in range(1, n_pat):
                top = jnp.maximum(top, maxes[g])
            total = jnp.exp2(maxes[0] - top) * acc_scr[0, hh, chunk[0], :]
            for g in range(1, n_pat):
                total = total + jnp.exp2(maxes[g] - top) * acc_scr[g, hh, chunk[g], :]
            halves.append(total * (1.0 / pltpu.roll(total, A_HEAD_DIM, 1)))
        out_scr[_rows(r, sub, grp), :] = jnp.where(lane_sub < A_HEAD_DIM, halves[0], halves[1])
    o_ref[0] = out_scr[...].astype(BF16)


def _dilated_attention(a_qkv, rel_bias):
    batch, _, seq, _ = a_qkv.shape
    blk = BAND_BLOCK
    sup = SUPER_BLOCK
    n_pat = len(DILATED_PATTERNS)
    assert all(w // d == blk for w, d in DILATED_PATTERNS), "band of exactly one block behind the query"
    assert seq % sup == 0
    bucket = jnp.asarray(np.stack([_t5_bucket_table(d, w // d) for w, d in DILATED_PATTERNS]))

    def part(which, prev):
        def index(b, s, p):
            return (b, which * A_PAIRS + p, jnp.maximum(s - 1, 0) if prev else s, 0)
        return pl.BlockSpec((1, 1, sup, LANES), index)

    return pl.pallas_call(
        _dilated_kernel,
        grid=(batch, seq // sup, A_PAIRS),
        in_specs=[
            pl.BlockSpec(memory_space=pltpu.SMEM),
            pl.BlockSpec((n_pat, blk, 2 * blk), lambda b, s, p: (0, 0, 0)),
            part(0, False), part(1, False), part(1, True), part(2, False), part(2, True),
        ],
        out_specs=pl.BlockSpec((1, sup, LANES), lambda b, s, p: (b, s, p)),
        out_shape=jax.ShapeDtypeStruct((batch, seq, A_WIDTH), BF16),
        scratch_shapes=[
            pltpu.VMEM((n_pat, A_HEADS, blk, 2 * blk), F32),
            pltpu.VMEM((n_pat, 2, sup, LANES), F32),
            pltpu.VMEM((n_pat, 2, sup, LANES), F32),
            pltpu.VMEM((5, sup, LANES), F32),
            pltpu.VMEM((sup, LANES), F32),
        ],
        compiler_params=_params(3),
        name="dilated_attention",
    )(rel_bias, bucket, a_qkv, a_qkv, a_qkv, a_qkv, a_qkv)


def _output_kernel(x_ref, mod_ref, oa_ref, ob_ref, gate_ref, wa_ref, wb_ref, wo_ref,
                   g_ref, gf_ref, wg_ref, wu_ref, wd_ref, out_ref):
    rows = x_ref.shape[1] // 2
    halves = [slice(0, rows), slice(rows, 2 * rows)]

    mixed = []
    for r in halves:
        y_a = _bdot(oa_ref[0, r, :], wa_ref[...])
        y_b = _bdot(ob_ref[0, r, :], wb_ref[...])
        gates = gate_ref[0, r, :].astype(F32)
        merged = _sigmoid(gates[:, :D_MODEL]) * y_a + _sigmoid(gates[:, D_MODEL:]) * y_b
        mixed.append(_bdot(merged.astype(BF16), wo_ref[...]))

    xs, acts = [], []
    for r, mix in zip(halves, mixed):
        x = x_ref[0, r, :] + mod_ref[0, 2:3, :] * mix
        hb = ((_rms(x) * g_ref[...]) * (1.0 + mod_ref[0, 4:5, :]) + mod_ref[0, 3:4, :]).astype(BF16)
        gate = _bdot(hb, wg_ref[...])
        up = _bdot(hb, wu_ref[...])
        xs.append(x)
        acts.append((gate * _sigmoid(gate) * up).astype(BF16))

    for r, x, act in zip(halves, xs, acts):
        y = x + mod_ref[0, 5:6, :] * _bdot(act, wd_ref[...])
        out_ref[0, r, :] = _rms(y) * gf_ref[...]


def _output_stage(x, mod, o_a, o_b, gates, w_up_a, w_up_b, w_o, g_ffn, g_final, w_gate, w_up, w_down):
    batch, seq, _ = x.shape
    tm = ROW_TILE
    row3 = lambda b, i: (b, i, 0)
    half = pl.BlockSpec((1, tm, A_WIDTH), row3)
    return pl.pallas_call(
        _output_kernel,
        grid=(batch, seq // tm),
        in_specs=[
            pl.BlockSpec((1, tm, D_MODEL), row3),
            pl.BlockSpec((1, N_MOD, D_MODEL), lambda b, i: (b, 0, 0)),
            half, half,
            pl.BlockSpec((1, tm, 2 * D_MODEL), row3),
            _resident((A_WIDTH, D_MODEL)), _resident((M_WIDTH, D_MODEL)), _resident((D_MODEL, D_MODEL)),
            _resident((1, D_MODEL)), _resident((1, D_MODEL)),
            _resident((D_MODEL, D_FF)), _resident((D_MODEL, D_FF)), _resident((D_FF, D_MODEL)),
        ],
        out_specs=pl.BlockSpec((1, tm, D_MODEL), row3),
        out_shape=jax.ShapeDtypeStruct((batch, seq, D_MODEL), F32),
        compiler_params=_params(2),
        name="output_stage",
    )(x, mod, o_a, o_b, gates,
      w_up_a.astype(BF16), w_up_b.astype(BF16), w_o.astype(BF16),
      g_ffn.reshape(1, D_MODEL), g_final.reshape(1, D_MODEL),
      w_gate.astype(BF16), w_up.astype(BF16), w_down.astype(BF16))


def kernel(x, c, positions, rel_bias, w_ada, b_ada, g_mix, w_in, g_q_lora, w_uq, g_kv_lora, w_ukv,
           w_up_a, w_up_b, w_o, g_ffn, w_gate, w_up, w_down, g_final):
    assert w_ada.shape[0] == 1, "single-layer trunk"
    mod = _modulation(c, w_ada[0], b_ada[0])
    a_qkv, gates, q, k, vt = _input_stage(x, mod, g_mix[0], positions, w_in[0], g_q_lora[0], w_uq[0],
                                         g_kv_lora[0], w_ukv[0])
    o_b = _mla_attention(q, k, vt)
    o_a = _dilated_attention(a_qkv, rel_bias)
    return _output_stage(x, mod, o_a, o_b, gates, w_up_a[0], w_up_b[0], w_o[0],
                         g_ffn[0], g_final, w_gate[0], w_up[0], w_down[0])
```

```python
import math

import jax
import jax.numpy as jnp
import numpy as np
from jax import lax
from jax.experimental import pallas as pl
from jax.experimental.pallas import tpu as pltpu

D_MODEL = 1024
A_HEADS = 8
A_HEAD_DIM = 64
A_WIDTH = A_HEADS * A_HEAD_DIM
DILATED_PATTERNS = ((128, 1), (512, 4), (2048, 16))
BAND_BLOCK = 128
REL_BUCKETS = 32
REL_MAX_DIST = 2048
M_HEADS = 8
M_NOPE = 64
M_ROPE = 32
M_V = 64
M_Q_LORA = 768
M_KV_LORA = 256
M_WIDTH = M_HEADS * M_V
ROPE_THETA = 10000.0
D_FF = -(-8 * D_MODEL // (3 * 256)) * 256
N_MOD = 6
EPS = 1e-6
NEG_INF = -1e30

LANES = 128
SUBLANES = 8
V7X_VMEM_BYTES = 64 * 1024 * 1024
VMEM_LIMIT_BYTES = V7X_VMEM_BYTES - 8 * 1024 * 1024

M_HEAD_PAD = LANES
M_PAIRS = M_HEADS // 2
ROPE_HALF = M_ROPE // 2
ROPE_LO = M_NOPE
ROPE_MID = M_NOPE + ROPE_HALF
ROPE_HI = M_NOPE + M_ROPE

A_PAIRS = A_HEADS // 2
A_TILES = 3 * A_PAIRS
SUPER_BLOCK = BAND_BLOCK * max(d for _, d in DILATED_PATTERNS)
DILATED_REGROUP = 4
LOG2_E = math.log2(math.e)

ROW_TILE = 512
MOD_COLUMN_BLOCK = D_MODEL
MLA_KEY_TILE = 512
MLA_QUERY_TILE = 2 * MLA_KEY_TILE
MLA_QUERY_CHUNK = 256

F32 = jnp.float32
BF16 = jnp.bfloat16


def _params(n_axes):
    return pltpu.CompilerParams(
        dimension_semantics=("arbitrary",) * n_axes,
        vmem_limit_bytes=VMEM_LIMIT_BYTES,
    )


def _resident(shape):
    zeros = (0,) * len(shape)
    return pl.BlockSpec(shape, lambda *_: zeros, pipeline_mode=pl.Buffered(1))


def _bdot(a, b):
    return jnp.dot(a, b, preferred_element_type=F32)


def _bdot_t(a, b_t):
    return lax.dot_general(a, b_t, (((1,), (1,)), ((), ())), preferred_element_type=F32)


def _rms(x):
    return x * lax.rsqrt(jnp.mean(x * x, axis=-1, keepdims=True) + EPS)


def _sigmoid(x):
    return 1.0 / (1.0 + jnp.exp(-x))


def _mod_kernel(ct_ref, w_ref, b_ref, o_ref):
    c = ct_ref[...]
    cond = c * _sigmoid(c)
    w = w_ref[...]
    o_ref[...] = jnp.zeros(o_ref.shape, F32)
    for b in range(c.shape[1]):
        o_ref[b:b + 1, :] = jnp.sum(cond[:, b:b + 1] * w, axis=0, keepdims=True) + b_ref[...]


def _modulation(c, w_ada, b_ada):
    batch = c.shape[0]
    rows = -(-batch // SUBLANES) * SUBLANES
    cols = MOD_COLUMN_BLOCK
    out = pl.pallas_call(
        _mod_kernel,
        grid=(N_MOD * D_MODEL // cols,),
        in_specs=[
            pl.BlockSpec((D_MODEL, batch), lambda j: (0, 0)),
            pl.BlockSpec((D_MODEL, cols), lambda j: (0, j)),
            pl.BlockSpec((1, cols), lambda j: (0, j)),
        ],
        out_specs=pl.BlockSpec((rows, cols), lambda j: (0, j)),
        out_shape=jax.ShapeDtypeStruct((rows, N_MOD * D_MODEL), F32),
        compiler_params=_params(1),
        name="adaln_mod",
    )(c.T, w_ada, b_ada.reshape(1, N_MOD * D_MODEL))
    return out[:batch].reshape(batch, N_MOD, D_MODEL)


def _rope_lanes(x, cos, signed_sin):
    lane = lax.broadcasted_iota(jnp.int32, x.shape, 1)
    partner = jnp.where(lane < ROPE_MID, pltpu.roll(x, LANES - ROPE_HALF, 1), pltpu.roll(x, ROPE_HALF, 1))
    return x * cos + partner * signed_sin


def _input_kernel(x_ref, mod_ref, g_ref, pos_ref, freq_ref, wa_ref, wg_ref, wcq_ref, wckv_ref, wkr_ref,
                  gq_ref, wuq_ref, gkv_ref, wuk_ref, wuv_ref,
                  a_ref, gate_ref, q_ref, k_ref, v_ref):
    shift = mod_ref[0, 0:1, :]
    scale = mod_ref[0, 1:2, :]
    half = x_ref.shape[1] // 2
    hb_halves = [((_rms(x_ref[0, r, :]) * g_ref[...]) * (1.0 + scale) + shift).astype(BF16)
                 for r in (slice(0, half), slice(half, 2 * half))]

    c_q = jnp.concatenate([_bdot_t(hb_half, wcq_ref[...]) for hb_half in hb_halves], axis=0)
    hb = jnp.concatenate(hb_halves, axis=0)
    c_kv = _bdot_t(hb, wckv_ref[...])
    k_r = _bdot_t(hb, wkr_ref[...])

    ang = freq_ref[...] * pos_ref[0, 0].astype(F32)
    cos_r, sin_r = jnp.cos(ang), jnp.sin(ang)
    rows = ang.shape[1]
    cos = jnp.concatenate(
        [jnp.ones((ROPE_LO, rows), F32), cos_r, jnp.ones((LANES - ROPE_HI, rows), F32)], axis=0).T
    signed_sin = jnp.concatenate(
        [jnp.zeros((ROPE_LO, rows), F32), -sin_r[:ROPE_HALF], sin_r[ROPE_HALF:],
         jnp.zeros((LANES - ROPE_HI, rows), F32)], axis=0).T

    q_all = _bdot((_rms(c_q) * gq_ref[...]).astype(BF16), wuq_ref[...])
    q_scale = (M_NOPE + M_ROPE) ** -0.5 * LOG2_E
    for hd in range(M_HEADS):
        q_h = q_all[:, hd * M_HEAD_PAD:(hd + 1) * M_HEAD_PAD]
        q_ref[0, hd] = (_rope_lanes(q_h, cos, signed_sin) * q_scale).astype(BF16)

    c_kv = (_rms(c_kv) * gkv_ref[...]).astype(BF16)
    k_rope = _rope_lanes(k_r, cos, signed_sin)
    k_all = _bdot(c_kv, wuk_ref[...])
    for hd in range(M_HEADS):
        k_ref[0, hd] = (k_all[:, hd * M_HEAD_PAD:(hd + 1) * M_HEAD_PAD] + k_rope).astype(BF16)
    v_all = _bdot(c_kv, wuv_ref[...])
    for pr in range(M_PAIRS):
        v_ref[0, pr, 0] = v_all[:, pr * LANES:(pr + 1) * LANES].T.astype(BF16)

    a_all = _bdot_t(hb, wa_ref[...])
    for j in range(A_TILES):
        tile = a_all[:, j * LANES:(j + 1) * LANES]
        a_ref[0, j] = tile * (A_HEAD_DIM ** -0.5 * LOG2_E) if j < A_PAIRS else tile
    gate_ref[0] = _bdot_t(hb, wg_ref[...]).astype(BF16)


def _input_stage(x, mod, g_mix, positions, w_in, g_q_lora, w_uq, g_kv_lora, w_ukv):
    batch, seq, _ = x.shape
    tm = MLA_KEY_TILE
    s0 = 3 * A_WIDTH
    s1 = s0 + M_Q_LORA
    s2 = s1 + M_KV_LORA
    s3 = s2 + M_ROPE
    assert s0 % M_Q_LORA == 0 and s1 % M_KV_LORA == 0
    w_all = w_in.T.astype(BF16)
    w_kr = jnp.pad(w_all[s2:s3], ((ROPE_LO, LANES - ROPE_HI), (0, 0)))
    w_g = w_all[s3:]

    def columns(width, start):
        return pl.BlockSpec((width, D_MODEL), lambda *_: (start // width, 0), pipeline_mode=pl.Buffered(1))
    w_uq_p = jnp.pad(w_uq, ((0, 0), (0, 0), (0, M_HEAD_PAD - M_NOPE - M_ROPE)))
    w_uq_p = w_uq_p.reshape(M_Q_LORA, M_HEADS * M_HEAD_PAD).astype(BF16)
    w_uk_p = jnp.pad(w_ukv[:, :, :M_NOPE], ((0, 0), (0, 0), (0, M_HEAD_PAD - M_NOPE)))
    w_uk_p = w_uk_p.reshape(M_KV_LORA, M_HEADS * M_HEAD_PAD).astype(BF16)
    w_uv = w_ukv[:, :, M_NOPE:].reshape(M_KV_LORA, M_WIDTH).astype(BF16)

    freqs = ROPE_THETA ** (-jnp.arange(ROPE_HALF, dtype=F32) / ROPE_HALF)
    freq_col = jnp.concatenate([freqs, freqs]).reshape(M_ROPE, 1)

    row3 = lambda b, i: (b, i, 0)
    head4 = lambda b, i: (b, 0, i, 0)
    return pl.pallas_call(
        _input_kernel,
        grid=(batch, seq // tm),
        in_specs=[
            pl.BlockSpec((1, tm, D_MODEL), row3),
            pl.BlockSpec((1, N_MOD, D_MODEL), lambda b, i: (b, 0, 0)),
            _resident((1, D_MODEL)),
            pl.BlockSpec((1, 1, 1, tm), lambda b, i: (b, i, 0, 0)),
            _resident((M_ROPE, 1)),
            columns(s0, 0), _resident(w_g.shape), columns(M_Q_LORA, s0), columns(M_KV_LORA, s1),
            _resident(w_kr.shape),
            _resident((1, M_Q_LORA)), _resident(w_uq_p.shape),
            _resident((1, M_KV_LORA)), _resident(w_uk_p.shape), _resident(w_uv.shape),
        ],
        out_specs=[
            pl.BlockSpec((1, A_TILES, tm, LANES), head4),
            pl.BlockSpec((1, tm, 2 * D_MODEL), row3),
            pl.BlockSpec((1, M_HEADS, tm, M_HEAD_PAD), head4),
            pl.BlockSpec((1, M_HEADS, tm, M_HEAD_PAD), head4),
            pl.BlockSpec((1, M_PAIRS, 1, LANES, tm), lambda b, i: (b, 0, i, 0, 0)),
        ],
        out_shape=[
            jax.ShapeDtypeStruct((batch, A_TILES, seq, LANES), F32),
            jax.ShapeDtypeStruct((batch, seq, 2 * D_MODEL), BF16),
            jax.ShapeDtypeStruct((batch, M_HEADS, seq, M_HEAD_PAD), BF16),
            jax.ShapeDtypeStruct((batch, M_HEADS, seq, M_HEAD_PAD), BF16),
            jax.ShapeDtypeStruct((batch, M_PAIRS, seq // tm, LANES, tm), BF16),
        ],
        compiler_params=_params(2),
        name="input_stage",
    )(x, mod, g_mix.reshape(1, D_MODEL), positions.reshape(batch, seq // tm, 1, tm), freq_col,
      w_all, w_g, w_all, w_all, w_kr, g_q_lora.reshape(1, M_Q_LORA), w_uq_p,
      g_kv_lora.reshape(1, M_KV_LORA), w_uk_p, w_uv)


def _mla_kernel(q_ref, k_ref, vt_ref, o_ref, m_scr, acc_scr, sa_scr, sb_scr, max_a_scr, max_b_scr):
    tq, tk = MLA_QUERY_TILE, MLA_KEY_TILE
    qc = MLA_QUERY_CHUNK
    n_query_tiles = q_ref.shape[2] // tq
    contract_last = (((1,), (1,)), ((), ()))
    v_row = lax.broadcasted_iota(jnp.int32, (LANES, tk), 0)
    own_rows = [v_row < M_V, v_row >= M_V]
    out_row = lax.broadcasted_iota(jnp.int32, (LANES, tq), 0)

    def reset():
        m_scr[...] = jnp.full(m_scr.shape, NEG_INF, F32)
        acc_scr[...] = jnp.zeros(acc_scr.shape, F32)

    def finish(q_tile):
        o0 = acc_scr[0] * (1.0 / acc_scr[0, M_V:M_V + 1, :])
        o1 = acc_scr[1] * (1.0 / acc_scr[1, 0:1, :])
        rows = pl.ds(pl.multiple_of(q_tile * tq, tq), tq)
        o_ref[0, rows, :] = jnp.where(out_row < M_V, o0, o1).T.astype(BF16)

    def block(scored=None, absorbed=None):
        pieces, chains = [], []
        if scored is not None:
            next_tile, next_s, next_max, diagonal, q_tile = scored
            start = pl.multiple_of(next_tile * tk, tk)
            keys = [k_ref[0, hh, pl.ds(start, tk), :] for hh in range(2)]
            pieces = [(hh, q0) for hh in range(2) for q0 in range(diagonal or 0, tq, qc)]
        if absorbed is not None:
            tile, s_scr, max_scr, q_lo = absorbed
            vt = vt_ref[0, 0, tile]
            vt_aug = [jnp.where(own, vt, jnp.ones_like(vt)) for own in own_rows]
            chains = [(hh, q0) for hh in range(2) for q0 in range(q_lo, tq, qc)]
            state = [(m_scr[hh, :, q0:q0 + qc], acc_scr[hh, :, q0:q0 + qc]) for hh, q0 in chains]

        def score_piece(hh, q0):
            q_rows = pl.ds(pl.multiple_of(q_tile * tq + q0, qc), qc)
            s = lax.dot_general(keys[hh], q_ref[0, hh, q_rows, :], contract_last,
                                preferred_element_type=F32)
            if diagonal is not None and q0 < diagonal + tk - 1:
                key_pos = lax.broadcasted_iota(jnp.int32, s.shape, 0) + diagonal
                query_pos = lax.broadcasted_iota(jnp.int32, s.shape, 1) + q0
                s = jnp.where(key_pos <= query_pos, s, NEG_INF)
            next_s[hh, :, q0:q0 + qc] = s
            next_max[hh, :, q0:q0 + qc] = jnp.max(s, axis=0, keepdims=True)

        def absorb_chain(i):
            hh, q0 = chains[i]
            m_prev, acc_prev = state[i]
            m_new = jnp.maximum(m_prev, max_scr[hh, :, q0:q0 + qc])
            p = jnp.exp2(s_scr[hh, :, q0:q0 + qc] - m_new).astype(BF16)
            acc_scr[hh, :, q0:q0 + qc] = jnp.exp2(m_prev - m_new) * acc_prev + _bdot(vt_aug[hh], p)
            m_scr[hh, :, q0:q0 + qc] = m_new

        if pieces:
            score_piece(*pieces[0])
        for i in range(max(len(pieces) - 1, len(chains))):
            if i + 1 < len(pieces):
                score_piece(*pieces[i + 1])
            if i < len(chains):
                absorb_chain(i)

    buf_a, buf_b = (sa_scr, max_a_scr), (sb_scr, max_b_scr)

    reset()
    block(scored=(0, *buf_a, 0, 0))
    block(scored=(1, *buf_b, tk, 0), absorbed=(0, *buf_a, 0))
    block(scored=(2, *buf_a, 0, 1), absorbed=(1, *buf_b, tk))
    finish(0)

    def query_tile(i, carry):
        reset()
        block(scored=(2 * i + 1, *buf_b, tk, i), absorbed=(2 * i, *buf_a, 0))
        block(scored=(0, *buf_a, None, i), absorbed=(2 * i + 1, *buf_b, tk))

        def pair(j):
            block(scored=(2 * j + 1, *buf_b, None, i), absorbed=(2 * j, *buf_a, 0))
            block(scored=(2 * j + 2, *buf_a, None, i), absorbed=(2 * j + 1, *buf_b, 0))

        def two_pairs(j, inner):
            pair(2 * j)
            pair(2 * j + 1)
            return inner

        lax.fori_loop(0, (i - 1) // 2, two_pairs, 0)

        @pl.when((i - 1) % 2 == 1)
        def _odd_pair():
            pair(i - 2)

        block(scored=(2 * i - 1, *buf_b, None, i), absorbed=(2 * i - 2, *buf_a, 0))
        nxt = jnp.minimum(i + 1, n_query_tiles - 1)
        block(scored=(2 * nxt, *buf_a, 0, nxt), absorbed=(2 * i - 1, *buf_b, 0))
        finish(i)
        return carry

    lax.fori_loop(1, n_query_tiles, query_tile, 0)


def _mla_attention(q, k, vt):
    batch, _, seq, _ = q.shape
    tq, tk = MLA_QUERY_TILE, MLA_KEY_TILE
    assert tq == 2 * tk and seq % tq == 0 and seq // tq >= 2
    return pl.pallas_call(
        _mla_kernel,
        grid=(batch, M_PAIRS),
        in_specs=[
            pl.BlockSpec((1, 2, seq, M_HEAD_PAD), lambda b, p: (b, p, 0, 0)),
            pl.BlockSpec((1, 2, seq, M_HEAD_PAD), lambda b, p: (b, p, 0, 0)),
            pl.BlockSpec((1, 1, seq // tk, LANES, tk), lambda b, p: (b, p, 0, 0, 0)),
        ],
        out_specs=pl.BlockSpec((1, seq, LANES), lambda b, p: (b, 0, p)),
        out_shape=jax.ShapeDtypeStruct((batch, seq, M_WIDTH), BF16),
        scratch_shapes=[
            pltpu.VMEM((2, 1, tq), F32),
            pltpu.VMEM((2, LANES, tq), F32),
            pltpu.VMEM((2, tk, tq), F32),
            pltpu.VMEM((2, tk, tq), F32),
            pltpu.VMEM((2, 1, tq), F32),
            pltpu.VMEM((2, 1, tq), F32),
        ],
        compiler_params=_params(2),
        name="mla_attention",
    )(q, k, vt)


def _t5_bucket_table(dilation, n_back):
    blk = BAND_BLOCK
    sub_dist = (np.arange(blk)[:, None] + blk) - np.arange(2 * blk)[None, :]
    dist = np.clip(sub_dist, 0, n_back) * dilation
    max_exact = REL_BUCKETS // 2
    d = np.maximum(dist, 1).astype(np.float32)
    ratio = np.log(d / np.float32(max_exact)) / np.float32(math.log(REL_MAX_DIST / max_exact))
    log_b = max_exact + (ratio * np.float32(REL_BUCKETS - max_exact)).astype(np.int32)
    log_b = np.minimum(log_b, REL_BUCKETS - 1)
    return np.where(dist < max_exact, dist, log_b).astype(np.int32)


def _rows(start, count, stride):
    return pl.ds(start, count) if stride == 1 else pl.ds(start, count, stride=stride)


def _dilated_kernel(rb_ref, bucket_ref, q_ref, kc_ref, kp_ref, vc_ref, vp_ref, o_ref,
                    bias_scr, acc_scr, m_scr, regroup_scr, out_scr):
    blk = BAND_BLOCK
    sup = SUPER_BLOCK
    grp = DILATED_REGROUP
    sub = sup // grp
    pair = pl.program_id(2)
    first_step = (pl.program_id(0) == 0) & (pl.program_id(1) == 0) & (pair == 0)

    @pl.when(first_step)
    def _build_bias():
        row = lax.broadcasted_iota(jnp.int32, (blk, 2 * blk), 0)
        col = lax.broadcasted_iota(jnp.int32, (blk, 2 * blk), 1)
        sub_dist = row + blk - col
        in_band = (sub_dist >= 0) & (sub_dist <= blk)
        for g in range(len(DILATED_PATTERNS)):
            bucket = bucket_ref[g]
            for hd in range(A_HEADS):
                bias = jnp.zeros((blk, 2 * blk), F32)
                for bk in range(REL_BUCKETS):
                    bias = jnp.where(bucket == bk, rb_ref[hd, bk] * LOG2_E, bias)
                bias_scr[g, hd] = jnp.where(in_band, bias, NEG_INF)

    sources = (q_ref, kc_ref, kp_ref, vc_ref, vp_ref)
    for idx, ref in enumerate(sources):
        for r in range(grp):
            regroup_scr[idx, r * sub:(r + 1) * sub, :] = ref[0, 0, _rows(r, sub, grp), :]

    first_valid_col = jnp.where(pl.program_id(1) > 0, 0, blk)
    col = lax.broadcasted_iota(jnp.int32, (blk, 2 * blk), 1)
    lane = lax.broadcasted_iota(jnp.int32, (blk, LANES), 1)
    lane2 = lax.broadcasted_iota(jnp.int32, (2 * blk, LANES), 1)
    contract_last = (((1,), (1,)), ((), ()))
    n_pat = len(DILATED_PATTERNS)

    for g, (_, dil) in enumerate(DILATED_PATTERNS):
        regrouped = dil % grp == 0
        step = dil // grp if regrouped else dil
        for res in range(dil):
            base = (res % grp) * sub + res // grp if regrouped else res
            prev_base = base + (sub if regrouped else sup) - blk * step
            if regrouped:
                load = lambda idx, r: regroup_scr[idx, r, :].astype(BF16)
            else:
                load = lambda idx, r: sources[idx][0, 0, r, :].astype(BF16)
            prev_rows = _rows(prev_base, blk, step)
            k_prev, v_prev = load(2, prev_rows), load(4, prev_rows)
            for n in range(sup // (blk * dil)):
                rows = _rows(base + blk * step * n, blk, step)
                q, k_cur, v_cur = load(0, rows), load(1, rows), load(3, rows)
                k2 = jnp.concatenate([k_prev, k_cur], axis=0)
                v2 = jnp.concatenate([v_prev, v_cur], axis=0)
                k_prev, v_prev = k_cur, v_cur
                for hh in range(2):
                    in_head = (lane < A_HEAD_DIM) if hh == 0 else (lane >= A_HEAD_DIM)
                    in_head2 = (lane2 < A_HEAD_DIM) if hh == 0 else (lane2 >= A_HEAD_DIM)
                    s = lax.dot_general(jnp.where(in_head, q, jnp.zeros_like(q)), k2, contract_last,
                                        preferred_element_type=F32)
                    bias = bias_scr[g, 2 * pair + hh]
                    if n == 0:
                        bias = jnp.where(col >= first_valid_col, bias, NEG_INF)
                    s = s + bias
                    m_blk = jnp.max(s, axis=-1, keepdims=True)
                    p = jnp.exp2(s - m_blk).astype(BF16)
                    acc_scr[g, hh, rows, :] = _bdot(p, jnp.where(in_head2, v2, jnp.ones_like(v2)))
                    m_scr[g, hh, rows, :] = jnp.broadcast_to(m_blk, (blk, LANES))

    lane_sub = lax.broadcasted_iota(jnp.int32, (sub, LANES), 1)
    for r in range(grp):
        chunk = [pl.ds(r * sub, sub) if dil % grp == 0 else _rows(r, sub, grp) for _, dil in DILATED_PATTERNS]
        halves = []
        for hh in range(2):
            maxes = [m_scr[g, hh, chunk[g], :] for g in range(n_pat)]
            top = maxes[0]
            for g in range(1, n_pat):
                top = jnp.maximum(top, maxes[g])
            total = jnp.exp2(maxes[0] - top) * acc_scr[0, hh, chunk[0], :]
            for g in range(1, n_pat):
                total = total + jnp.exp2(maxes[g] - top) * acc_scr[g, hh, chunk[g], :]
            halves.append(total * (1.0 / pltpu.roll(total, A_HEAD_DIM, 1)))
        out_scr[_rows(r, sub, grp), :] = jnp.where(lane_sub < A_HEAD_DIM, halves[0], halves[1])
    o_ref[0] = out_scr[...].astype(BF16)


def _dilated_attention(a_qkv, rel_bias):
    batch, _, seq, _ = a_qkv.shape
    blk = BAND_BLOCK
    sup = SUPER_BLOCK
    n_pat = len(DILATED_PATTERNS)
    assert all(w // d == blk for w, d in DILATED_PATTERNS), "band of exactly one block behind the query"
    assert seq % sup == 0
    bucket = jnp.asarray(np.stack([_t5_bucket_table(d, w // d) for w, d in DILATED_PATTERNS]))

    def part(which, prev):
        def index(b, s, p):
            return (b, which * A_PAIRS + p, jnp.maximum(s - 1, 0) if prev else s, 0)
        return pl.BlockSpec((1, 1, sup, LANES), index)

    return pl.pallas_call(
        _dilated_kernel,
        grid=(batch, seq // sup, A_PAIRS),
        in_specs=[
            pl.BlockSpec(memory_space=pltpu.SMEM),
            pl.BlockSpec((n_pat, blk, 2 * blk), lambda b, s, p: (0, 0, 0)),
            part(0, False), part(1, False), part(1, True), part(2, False), part(2, True),
        ],
        out_specs=pl.BlockSpec((1, sup, LANES), lambda b, s, p: (b, s, p)),
        out_shape=jax.ShapeDtypeStruct((batch, seq, A_WIDTH), BF16),
        scratch_shapes=[
            pltpu.VMEM((n_pat, A_HEADS, blk, 2 * blk), F32),
            pltpu.VMEM((n_pat, 2, sup, LANES), F32),
            pltpu.VMEM((n_pat, 2, sup, LANES), F32),
            pltpu.VMEM((5, sup, LANES), F32),
            pltpu.VMEM((sup, LANES), F32),
        ],
        compiler_params=_params(3),
        name="dilated_attention",
    )(rel_bias, bucket, a_qkv, a_qkv, a_qkv, a_qkv, a_qkv)


def _output_kernel(x_ref, mod_ref, oa_ref, ob_ref, gate_ref, wa_ref, wb_ref, wo_ref,
                   g_ref, gf_ref, wg_ref, wu_ref, wd_ref, out_ref):
    rows = x_ref.shape[1] // 2
    halves = [slice(0, rows), slice(rows, 2 * rows)]

    mixed = []
    for r in halves:
        y_a = _bdot(oa_ref[0, r, :], wa_ref[...])
        y_b = _bdot(ob_ref[0, r, :], wb_ref[...])
        gates = gate_ref[0, r, :].astype(F32)
        merged = _sigmoid(gates[:, :D_MODEL]) * y_a + _sigmoid(gates[:, D_MODEL:]) * y_b
        mixed.append(_bdot(merged.astype(BF16), wo_ref[...]))

    xs, acts = [], []
    for r, mix in zip(halves, mixed):
        x = x_ref[0, r, :] + mod_ref[0, 2:3, :] * mix
        hb = ((_rms(x) * g_ref[...]) * (1.0 + mod_ref[0, 4:5, :]) + mod_ref[0, 3:4, :]).astype(BF16)
        gate = _bdot(hb, wg_ref[...])
        up = _bdot(hb, wu_ref[...])
        xs.append(x)
        acts.append((gate * _sigmoid(gate) * up).astype(BF16))

    for r, x, act in zip(halves, xs, acts):
        y = x + mod_ref[0, 5:6, :] * _bdot(act, wd_ref[...])
        out_ref[0, r, :] = _rms(y) * gf_ref[...]


def _output_stage(x, mod, o_a, o_b, gates, w_up_a, w_up_b, w_o, g_ffn, g_final, w_gate, w_up, w_down):
    batch, seq, _ = x.shape
    tm = ROW_TILE
    row3 = lambda b, i: (b, i, 0)
    half = pl.BlockSpec((1, tm, A_WIDTH), row3)
    return pl.pallas_call(
        _output_kernel,
        grid=(batch, seq // tm),
        in_specs=[
            pl.BlockSpec((1, tm, D_MODEL), row3),
            pl.BlockSpec((1, N_MOD, D_MODEL), lambda b, i: (b, 0, 0)),
            half, half,
            pl.BlockSpec((1, tm, 2 * D_MODEL), row3),
            _resident((A_WIDTH, D_MODEL)), _resident((M_WIDTH, D_MODEL)), _resident((D_MODEL, D_MODEL)),
            _resident((1, D_MODEL)), _resident((1, D_MODEL)),
            _resident((D_MODEL, D_FF)), _resident((D_MODEL, D_FF)), _resident((D_FF, D_MODEL)),
        ],
        out_specs=pl.BlockSpec((1, tm, D_MODEL), row3),
        out_shape=jax.ShapeDtypeStruct((batch, seq, D_MODEL), F32),
        compiler_params=_params(2),
        name="output_stage",
    )(x, mod, o_a, o_b, gates,
      w_up_a.astype(BF16), w_up_b.astype(BF16), w_o.astype(BF16),
      g_ffn.reshape(1, D_MODEL), g_final.reshape(1, D_MODEL),
      w_gate.astype(BF16), w_up.astype(BF16), w_down.astype(BF16))


def kernel(x, c, positions, rel_bias, w_ada, b_ada, g_mix, w_in, g_q_lora, w_uq, g_kv_lora, w_ukv,
           w_up_a, w_up_b, w_o, g_ffn, w_gate, w_up, w_down, g_final):
    assert w_ada.shape[0] == 1, "single-layer trunk"
    mod = _modulation(c, w_ada[0], b_ada[0])
    a_qkv, gates, q, k, vt = _input_stage(x, mod, g_mix[0], positions, w_in[0], g_q_lora[0], w_uq[0],
                                         g_kv_lora[0], w_ukv[0])
    o_b = _mla_attention(q, k, vt)
    o_a = _dilated_attention(a_qkv, rel_bias)
    return _output_stage(x, mod, o_a, o_b, gates, w_up_a[0], w_up_b[0], w_o[0],
                         g_ffn[0], g_final, w_gate[0], w_up[0], w_down[0])
```

```python
import math

import jax
import jax.numpy as jnp
import numpy as np
from jax import lax
from jax.experimental import pallas as pl
from jax.experimental.pallas import tpu as pltpu

D_MODEL = 1024
A_HEADS = 8
A_HEAD_DIM = 64
A_WIDTH = A_HEADS * A_HEAD_DIM
DILATED_PATTERNS = ((128, 1), (512, 4), (2048, 16))
BAND_BLOCK = 128
REL_BUCKETS = 32
REL_MAX_DIST = 2048
M_HEADS = 8
M_NOPE = 64
M_ROPE = 32
M_V = 64
M_Q_LORA = 768
M_KV_LORA = 256
M_WIDTH = M_HEADS * M_V
ROPE_THETA = 10000.0
D_FF = -(-8 * D_MODEL // (3 * 256)) * 256
N_MOD = 6
EPS = 1e-6
NEG_INF = -1e30

LANES = 128
SUBLANES = 8
V7X_VMEM_BYTES = 64 * 1024 * 1024
VMEM_LIMIT_BYTES = V7X_VMEM_BYTES - 8 * 1024 * 1024

M_HEAD_PAD = LANES
M_PAIRS = M_HEADS // 2
ROPE_HALF = M_ROPE // 2
ROPE_LO = M_NOPE
ROPE_MID = M_NOPE + ROPE_HALF
ROPE_HI = M_NOPE + M_ROPE

A_PAIRS = A_HEADS // 2
A_TILES = 3 * A_PAIRS
SUPER_BLOCK = BAND_BLOCK * max(d for _, d in DILATED_PATTERNS)
DILATED_REGROUP = 4
LOG2_E = math.log2(math.e)

ROW_TILE = 512
MOD_COLUMN_BLOCK = D_MODEL
MLA_KEY_TILE = 512
MLA_QUERY_TILE = 2 * MLA_KEY_TILE
MLA_QUERY_CHUNK = 256

F32 = jnp.float32
BF16 = jnp.bfloat16


def _params(n_axes):
    return pltpu.CompilerParams(
        dimension_semantics=("arbitrary",) * n_axes,
        vmem_limit_bytes=VMEM_LIMIT_BYTES,
    )


def _resident(shape):
    zeros = (0,) * len(shape)
    return pl.BlockSpec(shape, lambda *_: zeros, pipeline_mode=pl.Buffered(1))


def _bdot(a, b):
    return jnp.dot(a, b, preferred_element_type=F32)


def _bdot_t(a, b_t):
    return lax.dot_general(a, b_t, (((1,), (1,)), ((), ())), preferred_element_type=F32)


def _rms(x):
    return x * lax.rsqrt(jnp.mean(x * x, axis=-1, keepdims=True) + EPS)


def _sigmoid(x):
    return 1.0 / (1.0 + jnp.exp(-x))


def _mod_kernel(ct_ref, w_ref, b_ref, o_ref):
    c = ct_ref[...]
    cond = c * _sigmoid(c)
    w = w_ref[...]
    o_ref[...] = jnp.zeros(o_ref.shape, F32)
    for b in range(c.shape[1]):
        o_ref[b:b + 1, :] = jnp.sum(cond[:, b:b + 1] * w, axis=0, keepdims=True) + b_ref[...]


def _modulation(c, w_ada, b_ada):
    batch = c.shape[0]
    rows = -(-batch // SUBLANES) * SUBLANES
    cols = MOD_COLUMN_BLOCK
    out = pl.pallas_call(
        _mod_kernel,
        grid=(N_MOD * D_MODEL // cols,),
        in_specs=[
            pl.BlockSpec((D_MODEL, batch), lambda j: (0, 0)),
            pl.BlockSpec((D_MODEL, cols), lambda j: (0, j)),
            pl.BlockSpec((1, cols), lambda j: (0, j)),
        ],
        out_specs=pl.BlockSpec((rows, cols), lambda j: (0, j)),
        out_shape=jax.ShapeDtypeStruct((rows, N_MOD * D_MODEL), F32),
        compiler_params=_params(1),
        name="adaln_mod",
    )(c.T, w_ada, b_ada.reshape(1, N_MOD * D_MODEL))
    return out[:batch].reshape(batch, N_MOD, D_MODEL)


def _rope_lanes(x, cos, signed_sin):
    lane = lax.broadcasted_iota(jnp.int32, x.shape, 1)
    partner = jnp.where(lane < ROPE_MID, pltpu.roll(x, LANES - ROPE_HALF, 1), pltpu.roll(x, ROPE_HALF, 1))
    return x * cos + partner * signed_sin


def _input_kernel(x_ref, mod_ref, g_ref, pos_ref, freq_ref, wa_ref, wg_ref, wcq_ref, wckv_ref, wkr_ref,
                  gq_ref, wuq_ref, gkv_ref, wuk_ref, wuv_ref,
                  a_ref, gate_ref, q_ref, k_ref, v_ref):
    shift = mod_ref[0, 0:1, :]
    scale = mod_ref[0, 1:2, :]
    half = x_ref.shape[1] // 2
    hb_halves = [((_rms(x_ref[0, r, :]) * g_ref[...]) * (1.0 + scale) + shift).astype(BF16)
                 for r in (slice(0, half), slice(half, 2 * half))]

    c_q = jnp.concatenate([_bdot_t(hb_half, wcq_ref[...]) for hb_half in hb_halves], axis=0)
    hb = jnp.concatenate(hb_halves, axis=0)
    c_kv = _bdot_t(hb, wckv_ref[...])
    k_r = _bdot_t(hb, wkr_ref[...])

    ang = freq_ref[...] * pos_ref[0, 0].astype(F32)
    cos_r, sin_r = jnp.cos(ang), jnp.sin(ang)
    rows = ang.shape[1]
    cos = jnp.concatenate(
        [jnp.ones((ROPE_LO, rows), F32), cos_r, jnp.ones((LANES - ROPE_HI, rows), F32)], axis=0).T
    signed_sin = jnp.concatenate(
        [jnp.zeros((ROPE_LO, rows), F32), -sin_r[:ROPE_HALF], sin_r[ROPE_HALF:],
         jnp.zeros((LANES - ROPE_HI, rows), F32)], axis=0).T

    q_all = _bdot((_rms(c_q) * gq_ref[...]).astype(BF16), wuq_ref[...])
    q_scale = (M_NOPE + M_ROPE) ** -0.5 * LOG2_E
    for hd in range(M_HEADS):
        q_h = q_all[:, hd * M_HEAD_PAD:(hd + 1) * M_HEAD_PAD]
        q_ref[0, hd] = (_rope_lanes(q_h, cos, signed_sin) * q_scale).astype(BF16)

    c_kv = (_rms(c_kv) * gkv_ref[...]).astype(BF16)
    k_rope = _rope_lanes(k_r, cos, signed_sin)
    k_all = _bdot(c_kv, wuk_ref[...])
    for hd in range(M_HEADS):
        k_ref[0, hd] = (k_all[:, hd * M_HEAD_PAD:(hd + 1) * M_HEAD_PAD] + k_rope).astype(BF16)
    v_all = _bdot(c_kv, wuv_ref[...])
    for pr in range(M_PAIRS):
        v_ref[0, pr, 0] = v_all[:, pr * LANES:(pr + 1) * LANES].T.astype(BF16)

    a_all = _bdot_t(hb, wa_ref[...])
    for j in range(A_TILES):
        tile = a_all[:, j * LANES:(j + 1) * LANES]
        a_ref[0, j] = tile * (A_HEAD_DIM ** -0.5 * LOG2_E) if j < A_PAIRS else tile
    gate_ref[0] = _bdot_t(hb, wg_ref[...]).astype(BF16)


def _input_stage(x, mod, g_mix, positions, w_in, g_q_lora, w_uq, g_kv_lora, w_ukv):
    batch, seq, _ = x.shape
    tm = MLA_KEY_TILE
    s0 = 3 * A_WIDTH
    s1 = s0 + M_Q_LORA
    s2 = s1 + M_KV_LORA
    s3 = s2 + M_ROPE
    assert s0 % M_Q_LORA == 0 and s1 % M_KV_LORA == 0
    w_all = w_in.T.astype(BF16)
    w_kr = jnp.pad(w_all[s2:s3], ((ROPE_LO, LANES - ROPE_HI), (0, 0)))
    w_g = w_all[s3:]

    def columns(width, start):
        return pl.BlockSpec((width, D_MODEL), lambda *_: (start // width, 0), pipeline_mode=pl.Buffered(1))
    w_uq_p = jnp.pad(w_uq, ((0, 0), (0, 0), (0, M_HEAD_PAD - M_NOPE - M_ROPE)))
    w_uq_p = w_uq_p.reshape(M_Q_LORA, M_HEADS * M_HEAD_PAD).astype(BF16)
    w_uk_p = jnp.pad(w_ukv[:, :, :M_NOPE], ((0, 0), (0, 0), (0, M_HEAD_PAD - M_NOPE)))
    w_uk_p = w_uk_p.reshape(M_KV_LORA, M_HEADS * M_HEAD_PAD).astype(BF16)
    w_uv = w_ukv[:, :, M_NOPE:].reshape(M_KV_LORA, M_WIDTH).astype(BF16)

    freqs = ROPE_THETA ** (-jnp.arange(ROPE_HALF, dtype=F32) / ROPE_HALF)
    freq_col = jnp.concatenate([freqs, freqs]).reshape(M_ROPE, 1)

    row3 = lambda b, i: (b, i, 0)
    head4 = lambda b, i: (b, 0, i, 0)
    return pl.pallas_call(
        _input_kernel,
        grid=(batch, seq // tm),
        in_specs=[
            pl.BlockSpec((1, tm, D_MODEL), row3),
            pl.BlockSpec((1, N_MOD, D_MODEL), lambda b, i: (b, 0, 0)),
            _resident((1, D_MODEL)),
            pl.BlockSpec((1, 1, 1, tm), lambda b, i: (b, i, 0, 0)),
            _resident((M_ROPE, 1)),
            columns(s0, 0), _resident(w_g.shape), columns(M_Q_LORA, s0), columns(M_KV_LORA, s1),
            _resident(w_kr.shape),
            _resident((1, M_Q_LORA)), _resident(w_uq_p.shape),
            _resident((1, M_KV_LORA)), _resident(w_uk_p.shape), _resident(w_uv.shape),
        ],
        out_specs=[
            pl.BlockSpec((1, A_TILES, tm, LANES), head4),
            pl.BlockSpec((1, tm, 2 * D_MODEL), row3),
            pl.BlockSpec((1, M_HEADS, tm, M_HEAD_PAD), head4),
            pl.BlockSpec((1, M_HEADS, tm, M_HEAD_PAD), head4),
            pl.BlockSpec((1, M_PAIRS, 1, LANES, tm), lambda b, i: (b, 0, i, 0, 0)),
        ],
        out_shape=[
            jax.ShapeDtypeStruct((batch, A_TILES, seq, LANES), F32),
            jax.ShapeDtypeStruct((batch, seq, 2 * D_MODEL), BF16),
            jax.ShapeDtypeStruct((batch, M_HEADS, seq, M_HEAD_PAD), BF16),
            jax.ShapeDtypeStruct((batch, M_HEADS, seq, M_HEAD_PAD), BF16),
            jax.ShapeDtypeStruct((batch, M_PAIRS, seq // tm, LANES, tm), BF16),
        ],
        compiler_params=_params(2),
        name="input_stage",
    )(x, mod, g_mix.reshape(1, D_MODEL), positions.reshape(batch, seq // tm, 1, tm), freq_col,
      w_all, w_g, w_all, w_all, w_kr, g_q_lora.reshape(1, M_Q_LORA), w_uq_p,
      g_kv_lora.reshape(1, M_KV_LORA), w_uk_p, w_uv)


def _mla_kernel(q_ref, k_ref, vt_ref, o_ref, m_scr, acc_scr, sa_scr, sb_scr, max_a_scr, max_b_scr):
    tq, tk = MLA_QUERY_TILE, MLA_KEY_TILE
    qc = MLA_QUERY_CHUNK
    n_query_tiles = q_ref.shape[2] // tq
    contract_last = (((1,), (1,)), ((), ()))
    v_row = lax.broadcasted_iota(jnp.int32, (LANES, tk), 0)
    own_rows = [v_row < M_V, v_row >= M_V]
    out_row = lax.broadcasted_iota(jnp.int32, (LANES, qc), 0)


    def reset():
        m_scr[...] = jnp.full(m_scr.shape, NEG_INF, F32)
        acc_scr[...] = jnp.zeros(acc_scr.shape, F32)

    def finish(q_tile):
        for c in range(tq // qc):
            o0 = acc_scr[0, c] * (1.0 / acc_scr[0, c, M_V:M_V + 1, :])
            o1 = acc_scr[1, c] * (1.0 / acc_scr[1, c, 0:1, :])
            rows = pl.ds(pl.multiple_of(q_tile * tq + c * qc, qc), qc)
            o_ref[0, rows, :] = jnp.where(out_row < M_V, o0, o1).T.astype(BF16)

    def block(scored=None, absorbed=None):
        pieces, chains = [], []
        if scored is not None:
            next_tile, next_s, next_max, diagonal, q_tile = scored
            start = pl.multiple_of(next_tile * tk, tk)
            keys = [k_ref[0, hh, pl.ds(start, tk), :] for hh in range(2)]
            pieces = [(hh, q0) for hh in range(2) for q0 in range(diagonal or 0, tq, qc)]
        if absorbed is not None:
            tile, s_scr, max_scr, q_lo = absorbed
            vt = vt_ref[0, 0, tile]
            vt_aug = [jnp.where(own, vt, jnp.ones_like(vt)) for own in own_rows]
            chains = [(hh, q0) for hh in range(2) for q0 in range(q_lo, tq, qc)]
            state = [(m_scr[hh, q0 // qc], acc_scr[hh, q0 // qc]) for hh, q0 in chains]

        def score_piece(hh, q0):
            q_rows = pl.ds(pl.multiple_of(q_tile * tq + q0, qc), qc)
            s = lax.dot_general(keys[hh], q_ref[0, hh, q_rows, :], contract_last,
                                preferred_element_type=F32)
            if diagonal is not None and q0 < diagonal + tk - 1:
                key_pos = lax.broadcasted_iota(jnp.int32, s.shape, 0) + diagonal
                query_pos = lax.broadcasted_iota(jnp.int32, s.shape, 1) + q0
                s = jnp.where(key_pos <= query_pos, s, NEG_INF)
            next_s[hh, q0 // qc] = s
            next_max[hh, q0 // qc] = jnp.max(s, axis=0, keepdims=True)

        def absorb_chain(i):
            hh, q0 = chains[i]
            c = q0 // qc
            m_prev, acc_prev = state[i]
            m_new = jnp.maximum(m_prev, max_scr[hh, c])
            p = jnp.exp2(s_scr[hh, c] - m_new).astype(BF16)
            acc_scr[hh, c] = jnp.exp2(m_prev - m_new) * acc_prev + _bdot(vt_aug[hh], p)
            m_scr[hh, c] = m_new

        if pieces:
            score_piece(*pieces[0])
        for i in range(max(len(pieces) - 1, len(chains))):
            if i + 1 < len(pieces):
                score_piece(*pieces[i + 1])
            if i < len(chains):
                absorb_chain(i)

    buf_a, buf_b = (sa_scr, max_a_scr), (sb_scr, max_b_scr)

    reset()
    block(scored=(0, *buf_a, 0, 0))
    block(scored=(1, *buf_b, tk, 0), absorbed=(0, *buf_a, 0))
    block(scored=(2, *buf_a, 0, 1), absorbed=(1, *buf_b, tk))
    finish(0)

    def query_tile(i, carry):
        reset()
        block(scored=(2 * i + 1, *buf_b, tk, i), absorbed=(2 * i, *buf_a, 0))
        block(scored=(0, *buf_a, None, i), absorbed=(2 * i + 1, *buf_b, tk))

        def pair(j):
            block(scored=(2 * j + 1, *buf_b, None, i), absorbed=(2 * j, *buf_a, 0))
            block(scored=(2 * j + 2, *buf_a, None, i), absorbed=(2 * j + 1, *buf_b, 0))

        def two_pairs(j, inner):
            pair(2 * j)
            pair(2 * j + 1)
            return inner

        lax.fori_loop(0, (i - 1) // 2, two_pairs, 0)

        @pl.when((i - 1) % 2 == 1)
        def _odd_pair():
            pair(i - 2)

        block(scored=(2 * i - 1, *buf_b, None, i), absorbed=(2 * i - 2, *buf_a, 0))
        nxt = jnp.minimum(i + 1, n_query_tiles - 1)
        block(scored=(2 * nxt, *buf_a, 0, nxt), absorbed=(2 * i - 1, *buf_b, 0))
        finish(i)
        return carry

    lax.fori_loop(1, n_query_tiles, query_tile, 0)


def _mla_attention(q, k, vt):
    batch, _, seq, _ = q.shape
    tq, tk, qc = MLA_QUERY_TILE, MLA_KEY_TILE, MLA_QUERY_CHUNK
    assert tq == 2 * tk and seq % tq == 0 and seq // tq >= 2
    return pl.pallas_call(
        _mla_kernel,
        grid=(batch, M_PAIRS),
        in_specs=[
            pl.BlockSpec((1, 2, seq, M_HEAD_PAD), lambda b, p: (b, p, 0, 0)),
            pl.BlockSpec((1, 2, seq, M_HEAD_PAD), lambda b, p: (b, p, 0, 0)),
            pl.BlockSpec((1, 1, seq // tk, LANES, tk), lambda b, p: (b, p, 0, 0, 0)),
        ],
        out_specs=pl.BlockSpec((1, seq, LANES), lambda b, p: (b, 0, p)),
        out_shape=jax.ShapeDtypeStruct((batch, seq, M_WIDTH), BF16),
        scratch_shapes=[
            pltpu.VMEM((2, tq // qc, 1, qc), F32),
            pltpu.VMEM((2, tq // qc, LANES, qc), F32),
            pltpu.VMEM((2, tq // qc, tk, qc), F32),
            pltpu.VMEM((2, tq // qc, tk, qc), F32),
            pltpu.VMEM((2, tq // qc, 1, qc), F32),
            pltpu.VMEM((2, tq // qc, 1, qc), F32),
        ],
        compiler_params=_params(2),
        name="mla_attention",
    )(q, k, vt)


def _t5_bucket_table(dilation, n_back):
    blk = BAND_BLOCK
    sub_dist = (np.arange(blk)[:, None] + blk) - np.arange(2 * blk)[None, :]
    dist = np.clip(sub_dist, 0, n_back) * dilation
    max_exact = REL_BUCKETS // 2
    d = np.maximum(dist, 1).astype(np.float32)
    ratio = np.log(d / np.float32(max_exact)) / np.float32(math.log(REL_MAX_DIST / max_exact))
    log_b = max_exact + (ratio * np.float32(REL_BUCKETS - max_exact)).astype(np.int32)
    log_b = np.minimum(log_b, REL_BUCKETS - 1)
    return np.where(dist < max_exact, dist, log_b).astype(np.int32)


def _rows(start, count, stride):
    return pl.ds(start, count) if stride == 1 else pl.ds(start, count, stride=stride)


def _dilated_kernel(rb_ref, bucket_ref, q_ref, kc_ref, kp_ref, vc_ref, vp_ref, o_ref,
                    bias_scr, acc_scr, m_scr, regroup_scr, out_scr):
    blk = BAND_BLOCK
    sup = SUPER_BLOCK
    grp = DILATED_REGROUP
    sub = sup // grp
    pair = pl.program_id(2)
    first_step = (pl.program_id(0) == 0) & (pl.program_id(1) == 0) & (pair == 0)

    @pl.when(first_step)
    def _build_bias():
        row = lax.broadcasted_iota(jnp.int32, (blk, 2 * blk), 0)
        col = lax.broadcasted_iota(jnp.int32, (blk, 2 * blk), 1)
        sub_dist = row + blk - col
        in_band = (sub_dist >= 0) & (sub_dist <= blk)
        for g in range(len(DILATED_PATTERNS)):
            bucket = bucket_ref[g]
            for hd in range(A_HEADS):
                bias = jnp.zeros((blk, 2 * blk), F32)
                for bk in range(REL_BUCKETS):
                    bias = jnp.where(bucket == bk, rb_ref[hd, bk] * LOG2_E, bias)
                bias_scr[g, hd] = jnp.where(in_band, bias, NEG_INF)

    sources = (q_ref, kc_ref, kp_ref, vc_ref, vp_ref)
    for idx, ref in enumerate(sources):
        for r in range(grp):
            regroup_scr[idx, r * sub:(r + 1) * sub, :] = ref[0, 0, _rows(r, sub, grp), :]

    first_valid_col = jnp.where(pl.program_id(1) > 0, 0, blk)
    col = lax.broadcasted_iota(jnp.int32, (blk, 2 * blk), 1)
    lane = lax.broadcasted_iota(jnp.int32, (blk, LANES), 1)
    lane2 = lax.broadcasted_iota(jnp.int32, (2 * blk, LANES), 1)
    contract_last = (((1,), (1,)), ((), ()))
    n_pat = len(DILATED_PATTERNS)

    for g, (_, dil) in enumerate(DILATED_PATTERNS):
        regrouped = dil % grp == 0
        step = dil // grp if regrouped else dil
        for res in range(dil):
            base = (res % grp) * sub + res // grp if regrouped else res
            prev_base = base + (sub if regrouped else sup) - blk * step
            if regrouped:
                load = lambda idx, r: regroup_scr[idx, r, :].astype(BF16)
            else:
                load = lambda idx, r: sources[idx][0, 0, r, :].astype(BF16)
            prev_rows = _rows(prev_base, blk, step)
            k_prev, v_prev = load(2, prev_rows), load(4, prev_rows)
            for n in range(sup // (blk * dil)):
                rows = _rows(base + blk * step * n, blk, step)
                q, k_cur, v_cur = load(0, rows), load(1, rows), load(3, rows)
                k2 = jnp.concatenate([k_prev, k_cur], axis=0)
                v2 = jnp.concatenate([v_prev, v_cur], axis=0)
                k_prev, v_prev = k_cur, v_cur
                for hh in range(2):
                    in_head = (lane < A_HEAD_DIM) if hh == 0 else (lane >= A_HEAD_DIM)
                    in_head2 = (lane2 < A_HEAD_DIM) if hh == 0 else (lane2 >= A_HEAD_DIM)
                    s = lax.dot_general(jnp.where(in_head, q, jnp.zeros_like(q)), k2, contract_last,
                                        preferred_element_type=F32)
                    bias = bias_scr[g, 2 * pair + hh]
                    if n == 0:
                        bias = jnp.where(col >= first_valid_col, bias, NEG_INF)
                    s = s + bias
                    m_blk = jnp.max(s, axis=-1, keepdims=True)
                    p = jnp.exp2(s - m_blk).astype(BF16)
                    acc_scr[g, hh, rows, :] = _bdot(p, jnp.where(in_head2, v2, jnp.ones_like(v2)))
                    m_scr[g, hh, rows, :] = jnp.broadcast_to(m_blk, (blk, LANES))

    lane_sub = lax.broadcasted_iota(jnp.int32, (sub, LANES), 1)
    for r in range(grp):
        chunk = [pl.ds(r * sub, sub) if dil % grp == 0 else _rows(r, sub, grp) for _, dil in DILATED_PATTERNS]
        halves = []
        for hh in range(2):
            maxes = [m_scr[g, hh, chunk[g], :] for g in range(n_pat)]
            top = maxes[0]
            for g in range(1, n_pat):
                top = jnp.maximum(top, maxes[g])
            total = jnp.exp2(maxes[0] - top) * acc_scr[0, hh, chunk[0], :]
            for g in range(1, n_pat):
                total = total + jnp.exp2(maxes[g] - top) * acc_scr[g, hh, chunk[g], :]
            halves.append(total * (1.0 / pltpu.roll(total, A_HEAD_DIM, 1)))
        out_scr[_rows(r, sub, grp), :] = jnp.where(lane_sub < A_HEAD_DIM, halves[0], halves[1])
    o_ref[0] = out_scr[...].astype(BF16)


def _dilated_attention(a_qkv, rel_bias):
    batch, _, seq, _ = a_qkv.shape
    blk = BAND_BLOCK
    sup = SUPER_BLOCK
    n_pat = len(DILATED_PATTERNS)
    assert all(w // d == blk for w, d in DILATED_PATTERNS), "band of exactly one block behind the query"
    assert seq % sup == 0
    bucket = jnp.asarray(np.stack([_t5_bucket_table(d, w // d) for w, d in DILATED_PATTERNS]))

    def part(which, prev):
        def index(b, s, p):
            return (b, which * A_PAIRS + p, jnp.maximum(s - 1, 0) if prev else s, 0)
        return pl.BlockSpec((1, 1, sup, LANES), index)

    return pl.pallas_call(
        _dilated_kernel,
        grid=(batch, seq // sup, A_PAIRS),
        in_specs=[
            pl.BlockSpec(memory_space=pltpu.SMEM),
            pl.BlockSpec((n_pat, blk, 2 * blk), lambda b, s, p: (0, 0, 0)),
            part(0, False), part(1, False), part(1, True), part(2, False), part(2, True),
        ],
        out_specs=pl.BlockSpec((1, sup, LANES), lambda b, s, p: (b, s, p)),
        out_shape=jax.ShapeDtypeStruct((batch, seq, A_WIDTH), BF16),
        scratch_shapes=[
            pltpu.VMEM((n_pat, A_HEADS, blk, 2 * blk), F32),
            pltpu.VMEM((n_pat, 2, sup, LANES), F32),
            pltpu.VMEM((n_pat, 2, sup, LANES), F32),
            pltpu.VMEM((5, sup, LANES), F32),
            pltpu.VMEM((sup, LANES), F32),
        ],
        compiler_params=_params(3),
        name="dilated_attention",
    )(rel_bias, bucket, a_qkv, a_qkv, a_qkv, a_qkv, a_qkv)


def _output_kernel(x_ref, mod_ref, oa_ref, ob_ref, gate_ref, wa_ref, wb_ref, wo_ref,
                   g_ref, gf_ref, wg_ref, wu_ref, wd_ref, out_ref):
    rows = x_ref.shape[1] // 2
    halves = [slice(0, rows), slice(rows, 2 * rows)]

    mixed = []
    for r in halves:
        y_a = _bdot(oa_ref[0, r, :], wa_ref[...])
        y_b = _bdot(ob_ref[0, r, :], wb_ref[...])
        gates = gate_ref[0, r, :].astype(F32)
        merged = _sigmoid(gates[:, :D_MODEL]) * y_a + _sigmoid(gates[:, D_MODEL:]) * y_b
        mixed.append(_bdot(merged.astype(BF16), wo_ref[...]))

    xs, acts = [], []
    for r, mix in zip(halves, mixed):
        x = x_ref[0, r, :] + mod_ref[0, 2:3, :] * mix
        hb = ((_rms(x) * g_ref[...]) * (1.0 + mod_ref[0, 4:5, :]) + mod_ref[0, 3:4, :]).astype(BF16)
        gate = _bdot(hb, wg_ref[...])
        up = _bdot(hb, wu_ref[...])
        xs.append(x)
        acts.append((gate * _sigmoid(gate) * up).astype(BF16))

    for r, x, act in zip(halves, xs, acts):
        y = x + mod_ref[0, 5:6, :] * _bdot(act, wd_ref[...])
        out_ref[0, r, :] = _rms(y) * gf_ref[...]


def _output_stage(x, mod, o_a, o_b, gates, w_up_a, w_up_b, w_o, g_ffn, g_final, w_gate, w_up, w_down):
    batch, seq, _ = x.shape
    tm = ROW_TILE
    row3 = lambda b, i: (b, i, 0)
    half = pl.BlockSpec((1, tm, A_WIDTH), row3)
    return pl.pallas_call(
        _output_kernel,
        grid=(batch, seq // tm),
        in_specs=[
            pl.BlockSpec((1, tm, D_MODEL), row3),
            pl.BlockSpec((1, N_MOD, D_MODEL), lambda b, i: (b, 0, 0)),
            half, half,
            pl.BlockSpec((1, tm, 2 * D_MODEL), row3),
            _resident((A_WIDTH, D_MODEL)), _resident((M_WIDTH, D_MODEL)), _resident((D_MODEL, D_MODEL)),
            _resident((1, D_MODEL)), _resident((1, D_MODEL)),
            _resident((D_MODEL, D_FF)), _resident((D_MODEL, D_FF)), _resident((D_FF, D_MODEL)),
        ],
        out_specs=pl.BlockSpec((1, tm, D_MODEL), row3),
        out_shape=jax.ShapeDtypeStruct((batch, seq, D_MODEL), F32),
        compiler_params=_params(2),
        name="output_stage",
    )(x, mod, o_a, o_b, gates,
      w_up_a.astype(BF16), w_up_b.astype(BF16), w_o.astype(BF16),
      g_ffn.reshape(1, D_MODEL), g_final.reshape(1, D_MODEL),
      w_gate.astype(BF16), w_up.astype(BF16), w_down.astype(BF16))


def kernel(x, c, positions, rel_bias, w_ada, b_ada, g_mix, w_in, g_q_lora, w_uq, g_kv_lora, w_ukv,
           w_up_a, w_up_b, w_o, g_ffn, w_gate, w_up, w_down, g_final):
    assert w_ada.shape[0] == 1, "single-layer trunk"
    mod = _modulation(c, w_ada[0], b_ada[0])
    a_qkv, gates, q, k, vt = _input_stage(x, mod, g_mix[0], positions, w_in[0], g_q_lora[0], w_uq[0],
                                         g_kv_lora[0], w_ukv[0])
    o_b = _mla_attention(q, k, vt)
    o_a = _dilated_attention(a_qkv, rel_bias)
    return _output_stage(x, mod, o_a, o_b, gates, w_up_a[0], w_up_b[0], w_o[0],
                         g_ffn[0], g_final, w_gate[0], w_up[0], w_down[0])
```

```python
import math

import jax
import jax.numpy as jnp
import numpy as np
from jax import lax
from jax.experimental import pallas as pl
from jax.experimental.pallas import tpu as pltpu

D_MODEL = 1024
A_HEADS = 8
A_HEAD_DIM = 64
A_WIDTH = A_HEADS * A_HEAD_DIM
DILATED_PATTERNS = ((128, 1), (512, 4), (2048, 16))
BAND_BLOCK = 128
REL_BUCKETS = 32
REL_MAX_DIST = 2048
M_HEADS = 8
M_NOPE = 64
M_ROPE = 32
M_V = 64
M_Q_LORA = 768
M_KV_LORA = 256
M_WIDTH = M_HEADS * M_V
ROPE_THETA = 10000.0
D_FF = -(-8 * D_MODEL // (3 * 256)) * 256
N_MOD = 6
EPS = 1e-6
NEG_INF = -1e30

LANES = 128
SUBLANES = 8
V7X_VMEM_BYTES = 64 * 1024 * 1024
VMEM_LIMIT_BYTES = V7X_VMEM_BYTES - 8 * 1024 * 1024

M_HEAD_PAD = LANES
M_PAIRS = M_HEADS // 2
ROPE_HALF = M_ROPE // 2
ROPE_LO = M_NOPE
ROPE_MID = M_NOPE + ROPE_HALF
ROPE_HI = M_NOPE + M_ROPE

A_PAIRS = A_HEADS // 2
A_TILES = 3 * A_PAIRS
SUPER_BLOCK = BAND_BLOCK * max(d for _, d in DILATED_PATTERNS)
DILATED_REGROUP = 4
LOG2_E = math.log2(math.e)

ROW_TILE = 512
MOD_COLUMN_BLOCK = D_MODEL
MLA_KEY_TILE = 512
MLA_QUERY_TILE = 2 * MLA_KEY_TILE
MLA_QUERY_CHUNK = 256

F32 = jnp.float32
BF16 = jnp.bfloat16


def _params(n_axes):
    return pltpu.CompilerParams(
        dimension_semantics=("arbitrary",) * n_axes,
        vmem_limit_bytes=VMEM_LIMIT_BYTES,
    )


def _resident(shape):
    zeros = (0,) * len(shape)
    return pl.BlockSpec(shape, lambda *_: zeros, pipeline_mode=pl.Buffered(1))


def _bdot(a, b):
    return jnp.dot(a, b, preferred_element_type=F32)


def _bdot_t(a, b_t):
    return lax.dot_general(a, b_t, (((1,), (1,)), ((), ())), preferred_element_type=F32)


def _rms(x):
    return x * lax.rsqrt(jnp.mean(x * x, axis=-1, keepdims=True) + EPS)


def _sigmoid(x):
    return 1.0 / (1.0 + jnp.exp(-x))


def _mod_kernel(ct_ref, w_ref, b_ref, o_ref):
    c = ct_ref[...]
    cond = c * _sigmoid(c)
    w = w_ref[...]
    o_ref[...] = jnp.zeros(o_ref.shape, F32)
    for b in range(c.shape[1]):
        o_ref[b:b + 1, :] = jnp.sum(cond[:, b:b + 1] * w, axis=0, keepdims=True) + b_ref[...]


def _modulation(c, w_ada, b_ada):
    batch = c.shape[0]
    rows = -(-batch // SUBLANES) * SUBLANES
    cols = MOD_COLUMN_BLOCK
    out = pl.pallas_call(
        _mod_kernel,
        grid=(N_MOD * D_MODEL // cols,),
        in_specs=[
            pl.BlockSpec((D_MODEL, batch), lambda j: (0, 0)),
            pl.BlockSpec((D_MODEL, cols), lambda j: (0, j)),
            pl.BlockSpec((1, cols), lambda j: (0, j)),
        ],
        out_specs=pl.BlockSpec((rows, cols), lambda j: (0, j)),
        out_shape=jax.ShapeDtypeStruct((rows, N_MOD * D_MODEL), F32),
        compiler_params=_params(1),
        name="adaln_mod",
    )(c.T, w_ada, b_ada.reshape(1, N_MOD * D_MODEL))
    return out[:batch].reshape(batch, N_MOD, D_MODEL)


def _rope_lanes(x, cos, signed_sin):
    lane = lax.broadcasted_iota(jnp.int32, x.shape, 1)
    partner = jnp.where(lane < ROPE_MID, pltpu.roll(x, LANES - ROPE_HALF, 1), pltpu.roll(x, ROPE_HALF, 1))
    return x * cos + partner * signed_sin


def _input_kernel(x_ref, mod_ref, g_ref, pos_ref, freq_ref, wa_ref, wg_ref, wcq_ref, wckv_ref, wkr_ref,
                  gq_ref, wuq_ref, gkv_ref, wuk_ref, wuv_ref,
                  a_ref, gate_ref, q_ref, k_ref, v_ref):
    shift = mod_ref[0, 0:1, :]
    scale = mod_ref[0, 1:2, :]
    half = x_ref.shape[1] // 2
    hb_halves = [((_rms(x_ref[0, r, :]) * g_ref[...]) * (1.0 + scale) + shift).astype(BF16)
                 for r in (slice(0, half), slice(half, 2 * half))]

    c_q = jnp.concatenate([_bdot_t(hb_half, wcq_ref[...]) for hb_half in hb_halves], axis=0)
    hb = jnp.concatenate(hb_halves, axis=0)
    c_kv = _bdot_t(hb, wckv_ref[...])
    k_r = _bdot_t(hb, wkr_ref[...])

    ang = freq_ref[...] * pos_ref[0, 0].astype(F32)
    cos_r, sin_r = jnp.cos(ang), jnp.sin(ang)
    rows = ang.shape[1]
    cos = jnp.concatenate(
        [jnp.ones((ROPE_LO, rows), F32), cos_r, jnp.ones((LANES - ROPE_HI, rows), F32)], axis=0).T
    signed_sin = jnp.concatenate(
        [jnp.zeros((ROPE_LO, rows), F32), -sin_r[:ROPE_HALF], sin_r[ROPE_HALF:],
         jnp.zeros((LANES - ROPE_HI, rows), F32)], axis=0).T

    q_all = _bdot((_rms(c_q) * gq_ref[...]).astype(BF16), wuq_ref[...])
    q_scale = (M_NOPE + M_ROPE) ** -0.5 * LOG2_E
    for hd in range(M_HEADS):
        q_h = q_all[:, hd * M_HEAD_PAD:(hd + 1) * M_HEAD_PAD]
        q_ref[0, hd] = (_rope_lanes(q_h, cos, signed_sin) * q_scale).astype(BF16)

    c_kv = (_rms(c_kv) * gkv_ref[...]).astype(BF16)
    k_rope = _rope_lanes(k_r, cos, signed_sin)
    k_all = _bdot(c_kv, wuk_ref[...])
    for hd in range(M_HEADS):
        k_ref[0, hd] = (k_all[:, hd * M_HEAD_PAD:(hd + 1) * M_HEAD_PAD] + k_rope).astype(BF16)
    v_all = _bdot(c_kv, wuv_ref[...])
    for pr in range(M_PAIRS):
        v_ref[0, pr, 0] = v_all[:, pr * LANES:(pr + 1) * LANES].T.astype(BF16)

    a_all = _bdot_t(hb, wa_ref[...])
    for j in range(A_TILES):
        tile = a_all[:, j * LANES:(j + 1) * LANES]
        a_ref[0, j] = tile * (A_HEAD_DIM ** -0.5 * LOG2_E) if j < A_PAIRS else tile
    gate_ref[0] = _bdot_t(hb, wg_ref[...]).astype(BF16)


def _input_stage(x, mod, g_mix, positions, w_in, g_q_lora, w_uq, g_kv_lora, w_ukv):
    batch, seq, _ = x.shape
    tm = MLA_KEY_TILE
    s0 = 3 * A_WIDTH
    s1 = s0 + M_Q_LORA
    s2 = s1 + M_KV_LORA
    s3 = s2 + M_ROPE
    assert s0 % M_Q_LORA == 0 and s1 % M_KV_LORA == 0
    w_all = w_in.T.astype(BF16)
    w_kr = jnp.pad(w_all[s2:s3], ((ROPE_LO, LANES - ROPE_HI), (0, 0)))
    w_g = w_all[s3:]

    def columns(width, start):
        return pl.BlockSpec((width, D_MODEL), lambda *_: (start // width, 0), pipeline_mode=pl.Buffered(1))
    w_uq_p = jnp.pad(w_uq, ((0, 0), (0, 0), (0, M_HEAD_PAD - M_NOPE - M_ROPE)))
    w_uq_p = w_uq_p.reshape(M_Q_LORA, M_HEADS * M_HEAD_PAD).astype(BF16)
    w_uk_p = jnp.pad(w_ukv[:, :, :M_NOPE], ((0, 0), (0, 0), (0, M_HEAD_PAD - M_NOPE)))
    w_uk_p = w_uk_p.reshape(M_KV_LORA, M_HEADS * M_HEAD_PAD).astype(BF16)
    w_uv = w_ukv[:, :, M_NOPE:].reshape(M_KV_LORA, M_WIDTH).astype(BF16)

    freqs = ROPE_THETA ** (-jnp.arange(ROPE_HALF, dtype=F32) / ROPE_HALF)
    freq_col = jnp.concatenate([freqs, freqs]).reshape(M_ROPE, 1)

    row3 = lambda b, i: (b, i, 0)
    head4 = lambda b, i: (b, 0, i, 0)
    return pl.pallas_call(
        _input_kernel,
        grid=(batch, seq // tm),
        in_specs=[
            pl.BlockSpec((1, tm, D_MODEL), row3),
            pl.BlockSpec((1, N_MOD, D_MODEL), lambda b, i: (b, 0, 0)),
            _resident((1, D_MODEL)),
            pl.BlockSpec((1, 1, 1, tm), lambda b, i: (b, i, 0, 0)),
            _resident((M_ROPE, 1)),
            columns(s0, 0), _resident(w_g.shape), columns(M_Q_LORA, s0), columns(M_KV_LORA, s1),
            _resident(w_kr.shape),
            _resident((1, M_Q_LORA)), _resident(w_uq_p.shape),
            _resident((1, M_KV_LORA)), _resident(w_uk_p.shape), _resident(w_uv.shape),
        ],
        out_specs=[
            pl.BlockSpec((1, A_TILES, tm, LANES), head4),
            pl.BlockSpec((1, tm, 2 * D_MODEL), row3),
            pl.BlockSpec((1, M_HEADS, tm, M_HEAD_PAD), head4),
            pl.BlockSpec((1, M_HEADS, tm, M_HEAD_PAD), head4),
            pl.BlockSpec((1, M_PAIRS, 1, LANES, tm), lambda b, i: (b, 0, i, 0, 0)),
        ],
        out_shape=[
            jax.ShapeDtypeStruct((batch, A_TILES, seq, LANES), F32),
            jax.ShapeDtypeStruct((batch, seq, 2 * D_MODEL), BF16),
            jax.ShapeDtypeStruct((batch, M_HEADS, seq, M_HEAD_PAD), BF16),
            jax.ShapeDtypeStruct((batch, M_HEADS, seq, M_HEAD_PAD), BF16),
            jax.ShapeDtypeStruct((batch, M_PAIRS, seq // tm, LANES, tm), BF16),
        ],
        compiler_params=_params(2),
        name="input_stage",
    )(x, mod, g_mix.reshape(1, D_MODEL), positions.reshape(batch, seq // tm, 1, tm), freq_col,
      w_all, w_g, w_all, w_all, w_kr, g_q_lora.reshape(1, M_Q_LORA), w_uq_p,
      g_kv_lora.reshape(1, M_KV_LORA), w_uk_p, w_uv)


def _mla_kernel(q_ref, k_ref, vt_ref, o_ref, m_scr, acc_scr, sa_scr, sb_scr, max_a_scr, max_b_scr):
    tq, tk = MLA_QUERY_TILE, MLA_KEY_TILE
    qc = MLA_QUERY_CHUNK
    n_query_tiles = q_ref.shape[2] // tq
    contract_last = (((1,), (1,)), ((), ()))
    v_row = lax.broadcasted_iota(jnp.int32, (LANES, tk), 0)
    own_rows = [v_row < M_V, v_row >= M_V]
    out_row = lax.broadcasted_iota(jnp.int32, (LANES, qc), 0)


    def reset():
        m_scr[...] = jnp.full(m_scr.shape, NEG_INF, F32)
        acc_scr[...] = jnp.zeros(acc_scr.shape, F32)

    def finish(q_tile):
        for c in range(tq // qc):
            o0 = acc_scr[0, c] * (1.0 / acc_scr[0, c, M_V:M_V + 1, :])
            o1 = acc_scr[1, c] * (1.0 / acc_scr[1, c, 0:1, :])
            rows = pl.ds(pl.multiple_of(q_tile * tq + c * qc, qc), qc)
            o_ref[0, rows, :] = jnp.where(out_row < M_V, o0, o1).T.astype(BF16)

    def block(scored=None, absorbed=None):
        def visible_keys(q0, diagonal):
            return qc if diagonal is not None and q0 == diagonal else tk

        pieces, chains = [], []
        if scored is not None:
            next_tile, next_s, next_max, diagonal, q_tile = scored
            start = pl.multiple_of(next_tile * tk, tk)
            keys = [k_ref[0, hh, pl.ds(start, tk), :] for hh in range(2)]
            pieces = [(hh, q0) for hh in range(2) for q0 in range(diagonal or 0, tq, qc)]
        if absorbed is not None:
            tile, s_scr, max_scr, absorbed_diagonal = absorbed
            vt = vt_ref[0, 0, tile]
            vt_aug = [jnp.where(own, vt, jnp.ones_like(vt)) for own in own_rows]
            chains = [(hh, q0) for hh in range(2) for q0 in range(absorbed_diagonal or 0, tq, qc)]
            state = [(m_scr[hh, q0 // qc], acc_scr[hh, q0 // qc]) for hh, q0 in chains]

        def score_piece(hh, q0):
            n_keys = visible_keys(q0, diagonal)
            q_rows = pl.ds(pl.multiple_of(q_tile * tq + q0, qc), qc)
            s = lax.dot_general(keys[hh][:n_keys], q_ref[0, hh, q_rows, :], contract_last,
                                preferred_element_type=F32)
            if diagonal is not None and q0 < diagonal + n_keys - 1:
                key_pos = lax.broadcasted_iota(jnp.int32, s.shape, 0) + diagonal
                query_pos = lax.broadcasted_iota(jnp.int32, s.shape, 1) + q0
                s = jnp.where(key_pos <= query_pos, s, NEG_INF)
            next_s[hh, q0 // qc, :n_keys, :] = s
            next_max[hh, q0 // qc] = jnp.max(s, axis=0, keepdims=True)

        def absorb_chain(i):
            hh, q0 = chains[i]
            c = q0 // qc
            n_keys = visible_keys(q0, absorbed_diagonal)
            m_prev, acc_prev = state[i]
            m_new = jnp.maximum(m_prev, max_scr[hh, c])
            p = jnp.exp2(s_scr[hh, c, :n_keys, :] - m_new).astype(BF16)
            acc_scr[hh, c] = jnp.exp2(m_prev - m_new) * acc_prev + _bdot(vt_aug[hh][:, :n_keys], p)
            m_scr[hh, c] = m_new

        if pieces:
            score_piece(*pieces[0])
        for i in range(max(len(pieces) - 1, len(chains))):
            if i + 1 < len(pieces):
                score_piece(*pieces[i + 1])
            if i < len(chains):
                absorb_chain(i)

    buf_a, buf_b = (sa_scr, max_a_scr), (sb_scr, max_b_scr)

    reset()
    block(scored=(0, *buf_a, 0, 0))
    block(scored=(1, *buf_b, tk, 0), absorbed=(0, *buf_a, 0))
    block(scored=(2, *buf_a, 0, 1), absorbed=(1, *buf_b, tk))
    finish(0)

    def query_tile(i, carry):
        reset()
        block(scored=(2 * i + 1, *buf_b, tk, i), absorbed=(2 * i, *buf_a, 0))
        block(scored=(0, *buf_a, None, i), absorbed=(2 * i + 1, *buf_b, tk))

        def pair(j):
            block(scored=(2 * j + 1, *buf_b, None, i), absorbed=(2 * j, *buf_a, None))
            block(scored=(2 * j + 2, *buf_a, None, i), absorbed=(2 * j + 1, *buf_b, None))

        def two_pairs(j, inner):
            pair(2 * j)
            pair(2 * j + 1)
            return inner

        lax.fori_loop(0, (i - 1) // 2, two_pairs, 0)

        @pl.when((i - 1) % 2 == 1)
        def _odd_pair():
            pair(i - 2)

        block(scored=(2 * i - 1, *buf_b, None, i), absorbed=(2 * i - 2, *buf_a, None))
        nxt = jnp.minimum(i + 1, n_query_tiles - 1)
        block(scored=(2 * nxt, *buf_a, 0, nxt), absorbed=(2 * i - 1, *buf_b, None))
        finish(i)
        return carry

    lax.fori_loop(1, n_query_tiles, query_tile, 0)


def _mla_attention(q, k, vt):
    batch, _, seq, _ = q.shape
    tq, tk, qc = MLA_QUERY_TILE, MLA_KEY_TILE, MLA_QUERY_CHUNK
    assert tq == 2 * tk and seq % tq == 0 and seq // tq >= 2
    return pl.pallas_call(
        _mla_kernel,
        grid=(batch, M_PAIRS),
        in_specs=[
            pl.BlockSpec((1, 2, seq, M_HEAD_PAD), lambda b, p: (b, p, 0, 0)),
            pl.BlockSpec((1, 2, seq, M_HEAD_PAD), lambda b, p: (b, p, 0, 0)),
            pl.BlockSpec((1, 1, seq // tk, LANES, tk), lambda b, p: (b, p, 0, 0, 0)),
        ],
        out_specs=pl.BlockSpec((1, seq, LANES), lambda b, p: (b, 0, p)),
        out_shape=jax.ShapeDtypeStruct((batch, seq, M_WIDTH), BF16),
        scratch_shapes=[
            pltpu.VMEM((2, tq // qc, 1, qc), F32),
            pltpu.VMEM((2, tq // qc, LANES, qc), F32),
            pltpu.VMEM((2, tq // qc, tk, qc), F32),
            pltpu.VMEM((2, tq // qc, tk, qc), F32),
            pltpu.VMEM((2, tq // qc, 1, qc), F32),
            pltpu.VMEM((2, tq // qc, 1, qc), F32),
        ],
        compiler_params=_params(2),
        name="mla_attention",
    )(q, k, vt)


def _t5_bucket_table(dilation, n_back):
    blk = BAND_BLOCK
    sub_dist = (np.arange(blk)[:, None] + blk) - np.arange(2 * blk)[None, :]
    dist = np.clip(sub_dist, 0, n_back) * dilation
    max_exact = REL_BUCKETS // 2
    d = np.maximum(dist, 1).astype(np.float32)
    ratio = np.log(d / np.float32(max_exact)) / np.float32(math.log(REL_MAX_DIST / max_exact))
    log_b = max_exact + (ratio * np.float32(REL_BUCKETS - max_exact)).astype(np.int32)
    log_b = np.minimum(log_b, REL_BUCKETS - 1)
    return np.where(dist < max_exact, dist, log_b).astype(np.int32)


def _rows(start, count, stride):
    return pl.ds(start, count) if stride == 1 else pl.ds(start, count, stride=stride)


def _dilated_kernel(rb_ref, bucket_ref, q_ref, kc_ref, kp_ref, vc_ref, vp_ref, o_ref,
                    bias_scr, acc_scr, m_scr, regroup_scr, out_scr):
    blk = BAND_BLOCK
    sup = SUPER_BLOCK
    grp = DILATED_REGROUP
    sub = sup // grp
    pair = pl.program_id(2)
    first_step = (pl.program_id(0) == 0) & (pl.program_id(1) == 0) & (pair == 0)

    @pl.when(first_step)
    def _build_bias():
        row = lax.broadcasted_iota(jnp.int32, (blk, 2 * blk), 0)
        col = lax.broadcasted_iota(jnp.int32, (blk, 2 * blk), 1)
        sub_dist = row + blk - col
        in_band = (sub_dist >= 0) & (sub_dist <= blk)
        for g in range(len(DILATED_PATTERNS)):
            bucket = bucket_ref[g]
            for hd in range(A_HEADS):
                bias = jnp.zeros((blk, 2 * blk), F32)
                for bk in range(REL_BUCKETS):
                    bias = jnp.where(bucket == bk, rb_ref[hd, bk] * LOG2_E, bias)
                bias_scr[g, hd] = jnp.where(in_band, bias, NEG_INF)

    sources = (q_ref, kc_ref, kp_ref, vc_ref, vp_ref)
    for idx, ref in enumerate(sources):
        for r in range(grp):
            regroup_scr[idx, r * sub:(r + 1) * sub, :] = ref[0, 0, _rows(r, sub, grp), :]

    first_valid_col = jnp.where(pl.program_id(1) > 0, 0, blk)
    col = lax.broadcasted_iota(jnp.int32, (blk, 2 * blk), 1)
    lane = lax.broadcasted_iota(jnp.int32, (blk, LANES), 1)
    lane2 = lax.broadcasted_iota(jnp.int32, (2 * blk, LANES), 1)
    contract_last = (((1,), (1,)), ((), ()))
    n_pat = len(DILATED_PATTERNS)

    for g, (_, dil) in enumerate(DILATED_PATTERNS):
        regrouped = dil % grp == 0
        step = dil // grp if regrouped else dil
        for res in range(dil):
            base = (res % grp) * sub + res // grp if regrouped else res
            prev_base = base + (sub if regrouped else sup) - blk * step
            if regrouped:
                load = lambda idx, r: regroup_scr[idx, r, :].astype(BF16)
            else:
                load = lambda idx, r: sources[idx][0, 0, r, :].astype(BF16)
            prev_rows = _rows(prev_base, blk, step)
            k_prev, v_prev = load(2, prev_rows), load(4, prev_rows)
            for n in range(sup // (blk * dil)):
                rows = _rows(base + blk * step * n, blk, step)
                q, k_cur, v_cur = load(0, rows), load(1, rows), load(3, rows)
                k2 = jnp.concatenate([k_prev, k_cur], axis=0)
                v2 = jnp.concatenate([v_prev, v_cur], axis=0)
                k_prev, v_prev = k_cur, v_cur
                for hh in range(2):
                    in_head = (lane < A_HEAD_DIM) if hh == 0 else (lane >= A_HEAD_DIM)
                    in_head2 = (lane2 < A_HEAD_DIM) if hh == 0 else (lane2 >= A_HEAD_DIM)
                    s = lax.dot_general(jnp.where(in_head, q, jnp.zeros_like(q)), k2, contract_last,
                                        preferred_element_type=F32)
                    bias = bias_scr[g, 2 * pair + hh]
                    if n == 0:
                        bias = jnp.where(col >= first_valid_col, bias, NEG_INF)
                    s = s + bias
                    m_blk = jnp.max(s, axis=-1, keepdims=True)
                    p = jnp.exp2(s - m_blk).astype(BF16)
                    acc_scr[g, hh, rows, :] = _bdot(p, jnp.where(in_head2, v2, jnp.ones_like(v2)))
                    m_scr[g, hh, rows, :] = jnp.broadcast_to(m_blk, (blk, LANES))

    lane_sub = lax.broadcasted_iota(jnp.int32, (sub, LANES), 1)
    for r in range(grp):
        chunk = [pl.ds(r * sub, sub) if dil % grp == 0 else _rows(r, sub, grp) for _, dil in DILATED_PATTERNS]
        halves = []
        for hh in range(2):
            maxes = [m_scr[g, hh, chunk[g], :] for g in range(n_pat)]
            top = maxes[0]
            for g in range(1, n_pat):
                top = jnp.maximum(top, maxes[g])
            total = jnp.exp2(maxes[0] - top) * acc_scr[0, hh, chunk[0], :]
            for g in range(1, n_pat):
                total = total + jnp.exp2(maxes[g] - top) * acc_scr[g, hh, chunk[g], :]
            halves.append(total * (1.0 / pltpu.roll(total, A_HEAD_DIM, 1)))
        out_scr[_rows(r, sub, grp), :] = jnp.where(lane_sub < A_HEAD_DIM, halves[0], halves[1])
    o_ref[0] = out_scr[...].astype(BF16)


def _dilated_attention(a_qkv, rel_bias):
    batch, _, seq, _ = a_qkv.shape
    blk = BAND_BLOCK
    sup = SUPER_BLOCK
    n_pat = len(DILATED_PATTERNS)
    assert all(w // d == blk for w, d in DILATED_PATTERNS), "band of exactly one block behind the query"
    assert seq % sup == 0
    bucket = jnp.asarray(np.stack([_t5_bucket_table(d, w // d) for w, d in DILATED_PATTERNS]))

    def part(which, prev):
        def index(b, s, p):
            return (b, which * A_PAIRS + p, jnp.maximum(s - 1, 0) if prev else s, 0)
        return pl.BlockSpec((1, 1, sup, LANES), index)

    return pl.pallas_call(
        _dilated_kernel,
        grid=(batch, seq // sup, A_PAIRS),
        in_specs=[
            pl.BlockSpec(memory_space=pltpu.SMEM),
            pl.BlockSpec((n_pat, blk, 2 * blk), lambda b, s, p: (0, 0, 0)),
            part(0, False), part(1, False), part(1, True), part(2, False), part(2, True),
        ],
        out_specs=pl.BlockSpec((1, sup, LANES), lambda b, s, p: (b, s, p)),
        out_shape=jax.ShapeDtypeStruct((batch, seq, A_WIDTH), BF16),
        scratch_shapes=[
            pltpu.VMEM((n_pat, A_HEADS, blk, 2 * blk), F32),
            pltpu.VMEM((n_pat, 2, sup, LANES), F32),
            pltpu.VMEM((n_pat, 2, sup, LANES), F32),
            pltpu.VMEM((5, sup, LANES), F32),
            pltpu.VMEM((sup, LANES), F32),
        ],
        compiler_params=_params(3),
        name="dilated_attention",
    )(rel_bias, bucket, a_qkv, a_qkv, a_qkv, a_qkv, a_qkv)


def _output_kernel(x_ref, mod_ref, oa_ref, ob_ref, gate_ref, wa_ref, wb_ref, wo_ref,
                   g_ref, gf_ref, wg_ref, wu_ref, wd_ref, out_ref):
    rows = x_ref.shape[1] // 2
    halves = [slice(0, rows), slice(rows, 2 * rows)]

    mixed = []
    for r in halves:
        y_a = _bdot(oa_ref[0, r, :], wa_ref[...])
        y_b = _bdot(ob_ref[0, r, :], wb_ref[...])
        gates = gate_ref[0, r, :].astype(F32)
        merged = _sigmoid(gates[:, :D_MODEL]) * y_a + _sigmoid(gates[:, D_MODEL:]) * y_b
        mixed.append(_bdot(merged.astype(BF16), wo_ref[...]))

    xs, acts = [], []
    for r, mix in zip(halves, mixed):
        x = x_ref[0, r, :] + mod_ref[0, 2:3, :] * mix
        hb = ((_rms(x) * g_ref[...]) * (1.0 + mod_ref[0, 4:5, :]) + mod_ref[0, 3:4, :]).astype(BF16)
        gate = _bdot(hb, wg_ref[...])
        up = _bdot(hb, wu_ref[...])
        xs.append(x)
        acts.append((gate * _sigmoid(gate) * up).astype(BF16))

    for r, x, act in zip(halves, xs, acts):
        y = x + mod_ref[0, 5:6, :] * _bdot(act, wd_ref[...])
        out_ref[0, r, :] = _rms(y) * gf_ref[...]


def _output_stage(x, mod, o_a, o_b, gates, w_up_a, w_up_b, w_o, g_ffn, g_final, w_gate, w_up, w_down):
    batch, seq, _ = x.shape
    tm = ROW_TILE
    row3 = lambda b, i: (b, i, 0)
    half = pl.BlockSpec((1, tm, A_WIDTH), row3)
    return pl.pallas_call(
        _output_kernel,
        grid=(batch, seq // tm),
        in_specs=[
            pl.BlockSpec((1, tm, D_MODEL), row3),
            pl.BlockSpec((1, N_MOD, D_MODEL), lambda b, i: (b, 0, 0)),
            half, half,
            pl.BlockSpec((1, tm, 2 * D_MODEL), row3),
            _resident((A_WIDTH, D_MODEL)), _resident((M_WIDTH, D_MODEL)), _resident((D_MODEL, D_MODEL)),
            _resident((1, D_MODEL)), _resident((1, D_MODEL)),
            _resident((D_MODEL, D_FF)), _resident((D_MODEL, D_FF)), _resident((D_FF, D_MODEL)),
        ],
        out_specs=pl.BlockSpec((1, tm, D_MODEL), row3),
        out_shape=jax.ShapeDtypeStruct((batch, seq, D_MODEL), F32),
        compiler_params=_params(2),
        name="output_stage",
    )(x, mod, o_a, o_b, gates,
      w_up_a.astype(BF16), w_up_b.astype(BF16), w_o.astype(BF16),
      g_ffn.reshape(1, D_MODEL), g_final.reshape(1, D_MODEL),
      w_gate.astype(BF16), w_up.astype(BF16), w_down.astype(BF16))


def kernel(x, c, positions, rel_bias, w_ada, b_ada, g_mix, w_in, g_q_lora, w_uq, g_kv_lora, w_ukv,
           w_up_a, w_up_b, w_o, g_ffn, w_gate, w_up, w_down, g_final):
    assert w_ada.shape[0] == 1, "single-layer trunk"
    mod = _modulation(c, w_ada[0], b_ada[0])
    a_qkv, gates, q, k, vt = _input_stage(x, mod, g_mix[0], positions, w_in[0], g_q_lora[0], w_uq[0],
                                         g_kv_lora[0], w_ukv[0])
    o_b = _mla_attention(q, k, vt)
    o_a = _dilated_attention(a_qkv, rel_bias)
    return _output_stage(x, mod, o_a, o_b, gates, w_up_a[0], w_up_b[0], w_o[0],
                         g_ffn[0], g_final, w_gate[0], w_up[0], w_down[0])
```

```python
import math

import jax
import jax.numpy as jnp
import numpy as np
from jax import lax
from jax.experimental import pallas as pl
from jax.experimental.pallas import tpu as pltpu

D_MODEL = 1024
A_HEADS = 8
A_HEAD_DIM = 64
A_WIDTH = A_HEADS * A_HEAD_DIM
DILATED_PATTERNS = ((128, 1), (512, 4), (2048, 16))
BAND_BLOCK = 128
REL_BUCKETS = 32
REL_MAX_DIST = 2048
M_HEADS = 8
M_NOPE = 64
M_ROPE = 32
M_V = 64
M_Q_LORA = 768
M_KV_LORA = 256
M_WIDTH = M_HEADS * M_V
ROPE_THETA = 10000.0
D_FF = -(-8 * D_MODEL // (3 * 256)) * 256
N_MOD = 6
EPS = 1e-6
NEG_INF = -1e30

LANES = 128
SUBLANES = 8
V7X_VMEM_BYTES = 64 * 1024 * 1024
VMEM_LIMIT_BYTES = V7X_VMEM_BYTES - 8 * 1024 * 1024

M_HEAD_PAD = LANES
M_PAIRS = M_HEADS // 2
ROPE_HALF = M_ROPE // 2
ROPE_LO = M_NOPE
ROPE_MID = M_NOPE + ROPE_HALF
ROPE_HI = M_NOPE + M_ROPE

A_PAIRS = A_HEADS // 2
A_TILES = 3 * A_PAIRS
SUPER_BLOCK = BAND_BLOCK * max(d for _, d in DILATED_PATTERNS)
DILATED_REGROUP = 4
LOG2_E = math.log2(math.e)

ROW_TILE = 512
MOD_COLUMN_BLOCK = D_MODEL
MLA_KEY_TILE = 512
MLA_QUERY_TILE = 2 * MLA_KEY_TILE
MLA_QUERY_CHUNK = 256

F32 = jnp.float32
BF16 = jnp.bfloat16


def _params(n_axes):
    return pltpu.CompilerParams(
        dimension_semantics=("arbitrary",) * n_axes,
        vmem_limit_bytes=VMEM_LIMIT_BYTES,
    )


def _resident(shape):
    zeros = (0,) * len(shape)
    return pl.BlockSpec(shape, lambda *_: zeros, pipeline_mode=pl.Buffered(1))


def _bdot(a, b):
    return jnp.dot(a, b, preferred_element_type=F32)


def _bdot_t(a, b_t):
    return lax.dot_general(a, b_t, (((1,), (1,)), ((), ())), preferred_element_type=F32)


def _rms(x):
    return x * lax.rsqrt(jnp.mean(x * x, axis=-1, keepdims=True) + EPS)


def _sigmoid(x):
    return 1.0 / (1.0 + jnp.exp(-x))


def _mod_kernel(ct_ref, w_ref, b_ref, o_ref):
    c = ct_ref[...]
    cond = c * _sigmoid(c)
    w = w_ref[...]
    o_ref[...] = jnp.zeros(o_ref.shape, F32)
    for b in range(c.shape[1]):
        o_ref[b:b + 1, :] = jnp.sum(cond[:, b:b + 1] * w, axis=0, keepdims=True) + b_ref[...]


def _modulation(c, w_ada, b_ada):
    batch = c.shape[0]
    rows = -(-batch // SUBLANES) * SUBLANES
    cols = MOD_COLUMN_BLOCK
    out = pl.pallas_call(
        _mod_kernel,
        grid=(N_MOD * D_MODEL // cols,),
        in_specs=[
            pl.BlockSpec((D_MODEL, batch), lambda j: (0, 0)),
            pl.BlockSpec((D_MODEL, cols), lambda j: (0, j)),
            pl.BlockSpec((1, cols), lambda j: (0, j)),
        ],
        out_specs=pl.BlockSpec((rows, cols), lambda j: (0, j)),
        out_shape=jax.ShapeDtypeStruct((rows, N_MOD * D_MODEL), F32),
        compiler_params=_params(1),
        name="adaln_mod",
    )(c.T, w_ada, b_ada.reshape(1, N_MOD * D_MODEL))
    return out[:batch].reshape(batch, N_MOD, D_MODEL)


def _rope_lanes(x, cos, signed_sin):
    lane = lax.broadcasted_iota(jnp.int32, x.shape, 1)
    partner = jnp.where(lane < ROPE_MID, pltpu.roll(x, LANES - ROPE_HALF, 1), pltpu.roll(x, ROPE_HALF, 1))
    return x * cos + partner * signed_sin


def _input_kernel(x_ref, mod_ref, g_ref, pos_ref, freq_ref, wa_ref, wg_ref, wcq_ref, wckv_ref, wkr_ref,
                  gq_ref, wuq_ref, gkv_ref, wuk_ref, wuv_ref,
                  a_ref, gate_ref, q_ref, k_ref, v_ref):
    shift = mod_ref[0, 0:1, :]
    scale = mod_ref[0, 1:2, :]
    half = x_ref.shape[1] // 2
    hb_halves = [((_rms(x_ref[0, r, :]) * g_ref[...]) * (1.0 + scale) + shift).astype(BF16)
                 for r in (slice(0, half), slice(half, 2 * half))]

    c_q = jnp.concatenate([_bdot_t(hb_half, wcq_ref[...]) for hb_half in hb_halves], axis=0)
    hb = jnp.concatenate(hb_halves, axis=0)
    c_kv = _bdot_t(hb, wckv_ref[...])
    k_r = _bdot_t(hb, wkr_ref[...])

    ang = freq_ref[...] * pos_ref[0, 0].astype(F32)
    cos_r, sin_r = jnp.cos(ang), jnp.sin(ang)
    rows = ang.shape[1]
    cos = jnp.concatenate(
        [jnp.ones((ROPE_LO, rows), F32), cos_r, jnp.ones((LANES - ROPE_HI, rows), F32)], axis=0).T
    signed_sin = jnp.concatenate(
        [jnp.zeros((ROPE_LO, rows), F32), -sin_r[:ROPE_HALF], sin_r[ROPE_HALF:],
         jnp.zeros((LANES - ROPE_HI, rows), F32)], axis=0).T

    q_all = _bdot((_rms(c_q) * gq_ref[...]).astype(BF16), wuq_ref[...])
    q_scale = (M_NOPE + M_ROPE) ** -0.5 * LOG2_E
    for hd in range(M_HEADS):
        q_h = q_all[:, hd * M_HEAD_PAD:(hd + 1) * M_HEAD_PAD]
        q_ref[0, hd] = (_rope_lanes(q_h, cos, signed_sin) * q_scale).astype(BF16)

    c_kv = (_rms(c_kv) * gkv_ref[...]).astype(BF16)
    k_rope = _rope_lanes(k_r, cos, signed_sin)
    k_all = _bdot(c_kv, wuk_ref[...])
    for hd in range(M_HEADS):
        k_ref[0, hd] = (k_all[:, hd * M_HEAD_PAD:(hd + 1) * M_HEAD_PAD] + k_rope).astype(BF16)
    v_all = _bdot(c_kv, wuv_ref[...])
    for pr in range(M_PAIRS):
        v_ref[0, pr, 0] = v_all[:, pr * LANES:(pr + 1) * LANES].T.astype(BF16)

    a_all = _bdot_t(hb, wa_ref[...])
    for j in range(A_TILES):
        tile = a_all[:, j * LANES:(j + 1) * LANES]
        a_ref[0, j] = tile * (A_HEAD_DIM ** -0.5 * LOG2_E) if j < A_PAIRS else tile
    gate_ref[0] = _bdot_t(hb, wg_ref[...]).astype(BF16)


def _input_stage(x, mod, g_mix, positions, w_in, g_q_lora, w_uq, g_kv_lora, w_ukv):
    batch, seq, _ = x.shape
    tm = MLA_KEY_TILE
    s0 = 3 * A_WIDTH
    s1 = s0 + M_Q_LORA
    s2 = s1 + M_KV_LORA
    s3 = s2 + M_ROPE
    assert s0 % M_Q_LORA == 0 and s1 % M_KV_LORA == 0
    w_all = w_in.T.astype(BF16)
    w_kr = jnp.pad(w_all[s2:s3], ((ROPE_LO, LANES - ROPE_HI), (0, 0)))
    w_g = w_all[s3:]

    def columns(width, start):
        return pl.BlockSpec((width, D_MODEL), lambda *_: (start // width, 0), pipeline_mode=pl.Buffered(1))
    w_uq_p = jnp.pad(w_uq, ((0, 0), (0, 0), (0, M_HEAD_PAD - M_NOPE - M_ROPE)))
    w_uq_p = w_uq_p.reshape(M_Q_LORA, M_HEADS * M_HEAD_PAD).astype(BF16)
    w_uk_p = jnp.pad(w_ukv[:, :, :M_NOPE], ((0, 0), (0, 0), (0, M_HEAD_PAD - M_NOPE)))
    w_uk_p = w_uk_p.reshape(M_KV_LORA, M_HEADS * M_HEAD_PAD).astype(BF16)
    w_uv = w_ukv[:, :, M_NOPE:].reshape(M_KV_LORA, M_WIDTH).astype(BF16)

    freqs = ROPE_THETA ** (-jnp.arange(ROPE_HALF, dtype=F32) / ROPE_HALF)
    freq_col = jnp.concatenate([freqs, freqs]).reshape(M_ROPE, 1)

    row3 = lambda b, i: (b, i, 0)
    head4 = lambda b, i: (b, 0, i, 0)
    return pl.pallas_call(
        _input_kernel,
        grid=(batch, seq // tm),
        in_specs=[
            pl.BlockSpec((1, tm, D_MODEL), row3),
            pl.BlockSpec((1, N_MOD, D_MODEL), lambda b, i: (b, 0, 0)),
            _resident((1, D_MODEL)),
            pl.BlockSpec((1, 1, 1, tm), lambda b, i: (b, i, 0, 0)),
            _resident((M_ROPE, 1)),
            columns(s0, 0), _resident(w_g.shape), columns(M_Q_LORA, s0), columns(M_KV_LORA, s1),
            _resident(w_kr.shape),
            _resident((1, M_Q_LORA)), _resident(w_uq_p.shape),
            _resident((1, M_KV_LORA)), _resident(w_uk_p.shape), _resident(w_uv.shape),
        ],
        out_specs=[
            pl.BlockSpec((1, A_TILES, tm, LANES), head4),
            pl.BlockSpec((1, tm, 2 * D_MODEL), row3),
            pl.BlockSpec((1, M_HEADS, tm, M_HEAD_PAD), head4),
            pl.BlockSpec((1, M_HEADS, tm, M_HEAD_PAD), head4),
            pl.BlockSpec((1, M_PAIRS, 1, LANES, tm), lambda b, i: (b, 0, i, 0, 0)),
        ],
        out_shape=[
            jax.ShapeDtypeStruct((batch, A_TILES, seq, LANES), F32),
            jax.ShapeDtypeStruct((batch, seq, 2 * D_MODEL), BF16),
            jax.ShapeDtypeStruct((batch, M_HEADS, seq, M_HEAD_PAD), BF16),
            jax.ShapeDtypeStruct((batch, M_HEADS, seq, M_HEAD_PAD), BF16),
            jax.ShapeDtypeStruct((batch, M_PAIRS, seq // tm, LANES, tm), BF16),
        ],
        compiler_params=_params(2),
        name="input_stage",
    )(x, mod, g_mix.reshape(1, D_MODEL), positions.reshape(batch, seq // tm, 1, tm), freq_col,
      w_all, w_g, w_all, w_all, w_kr, g_q_lora.reshape(1, M_Q_LORA), w_uq_p,
      g_kv_lora.reshape(1, M_KV_LORA), w_uk_p, w_uv)


def _mla_kernel(q_ref, k_ref, vt_ref, o_ref, m_scr, acc_scr, sa_scr, sb_scr, max_a_scr, max_b_scr):
    tq, tk = MLA_QUERY_TILE, MLA_KEY_TILE
    qc = MLA_QUERY_CHUNK
    n_query_tiles = q_ref.shape[2] // tq
    contract_last = (((1,), (1,)), ((), ()))
    v_row = lax.broadcasted_iota(jnp.int32, (LANES, tk), 0)
    own_rows = [v_row < M_V, v_row >= M_V]
    out_row = lax.broadcasted_iota(jnp.int32, (LANES, qc), 0)


    def reset():
        m_scr[...] = jnp.full(m_scr.shape, NEG_INF, F32)
        acc_scr[...] = jnp.zeros(acc_scr.shape, F32)

    def finish(q_tile):
        for c in range(tq // qc):
            o0 = acc_scr[0, c] * (1.0 / acc_scr[0, c, M_V:M_V + 1, :])
            o1 = acc_scr[1, c] * (1.0 / acc_scr[1, c, 0:1, :])
            rows = pl.ds(pl.multiple_of(q_tile * tq + c * qc, qc), qc)
            o_ref[0, rows, :] = jnp.where(out_row < M_V, o0, o1).T.astype(BF16)

    def block(scored=None, absorbed=None):
        def visible_keys(q0, diagonal):
            return qc if diagonal is not None and q0 == diagonal else tk

        pieces, chains = [], []
        if scored is not None:
            next_tile, next_s, next_max, diagonal, q_tile = scored
            start = pl.multiple_of(next_tile * tk, tk)
            keys = [k_ref[0, hh, pl.ds(start, tk), :] for hh in range(2)]
            pieces = [(hh, q0) for hh in range(2) for q0 in range(diagonal or 0, tq, qc)]
        if absorbed is not None:
            tile, s_scr, max_scr, absorbed_diagonal = absorbed
            vt = vt_ref[0, 0, tile]
            vt_aug = [jnp.where(own, vt, jnp.ones_like(vt)) for own in own_rows]
            chains = [(hh, q0) for hh in range(2) for q0 in range(absorbed_diagonal or 0, tq, qc)]
            state = [(m_scr[hh, q0 // qc], acc_scr[hh, q0 // qc]) for hh, q0 in chains]

        def score_piece(hh, q0):
            n_keys = visible_keys(q0, diagonal)
            q_rows = pl.ds(pl.multiple_of(q_tile * tq + q0, qc), qc)
            s = lax.dot_general(keys[hh][:n_keys], q_ref[0, hh, q_rows, :], contract_last,
                                preferred_element_type=F32)
            if diagonal is not None and q0 < diagonal + n_keys - 1:
                key_pos = lax.broadcasted_iota(jnp.int32, s.shape, 0) + diagonal
                query_pos = lax.broadcasted_iota(jnp.int32, s.shape, 1) + q0
                s = jnp.where(key_pos <= query_pos, s, NEG_INF)
            next_s[hh, q0 // qc, :n_keys, :] = s
            next_max[hh, q0 // qc] = jnp.max(s, axis=0, keepdims=True)

        def absorb_chain(i):
            hh, q0 = chains[i]
            c = q0 // qc
            n_keys = visible_keys(q0, absorbed_diagonal)
            m_prev, acc_prev = state[i]
            m_new = jnp.maximum(m_prev, max_scr[hh, c])
            p = jnp.exp2(s_scr[hh, c, :n_keys, :] - m_new).astype(BF16)
            acc_scr[hh, c] = jnp.exp2(m_prev - m_new) * acc_prev + _bdot(vt_aug[hh][:, :n_keys], p)
            m_scr[hh, c] = m_new

        if pieces:
            score_piece(*pieces[0])
        for i in range(max(len(pieces) - 1, len(chains))):
            if i + 1 < len(pieces):
                score_piece(*pieces[i + 1])
            if i < len(chains):
                absorb_chain(i)

    buf_a, buf_b = (sa_scr, max_a_scr), (sb_scr, max_b_scr)

    reset()
    block(scored=(0, *buf_a, 0, 0))
    block(scored=(1, *buf_b, tk, 0), absorbed=(0, *buf_a, 0))
    block(scored=(2, *buf_a, 0, 1), absorbed=(1, *buf_b, tk))
    finish(0)

    def query_tile(i, carry):
        reset()
        block(scored=(2 * i + 1, *buf_b, tk, i), absorbed=(2 * i, *buf_a, 0))
        block(scored=(0, *buf_a, None, i), absorbed=(2 * i + 1, *buf_b, tk))

        def pair(j):
            block(scored=(2 * j + 1, *buf_b, None, i), absorbed=(2 * j, *buf_a, None))
            block(scored=(2 * j + 2, *buf_a, None, i), absorbed=(2 * j + 1, *buf_b, None))

        def two_pairs(j, inner):
            pair(2 * j)
            pair(2 * j + 1)
            return inner

        lax.fori_loop(0, (i - 1) // 2, two_pairs, 0)

        @pl.when((i - 1) % 2 == 1)
        def _odd_pair():
            pair(i - 2)

        block(scored=(2 * i - 1, *buf_b, None, i), absorbed=(2 * i - 2, *buf_a, None))
        nxt = jnp.minimum(i + 1, n_query_tiles - 1)
        block(scored=(2 * nxt, *buf_a, 0, nxt), absorbed=(2 * i - 1, *buf_b, None))
        finish(i)
        return carry

    lax.fori_loop(1, n_query_tiles, query_tile, 0)


def _mla_attention(q, k, vt):
    batch, _, seq, _ = q.shape
    tq, tk, qc = MLA_QUERY_TILE, MLA_KEY_TILE, MLA_QUERY_CHUNK
    assert tq == 2 * tk and seq % tq == 0 and seq // tq >= 2
    return pl.pallas_call(
        _mla_kernel,
        grid=(batch, M_PAIRS),
        in_specs=[
            pl.BlockSpec((1, 2, seq, M_HEAD_PAD), lambda b, p: (b, p, 0, 0)),
            pl.BlockSpec((1, 2, seq, M_HEAD_PAD), lambda b, p: (b, p, 0, 0)),
            pl.BlockSpec((1, 1, seq // tk, LANES, tk), lambda b, p: (b, p, 0, 0, 0)),
        ],
        out_specs=pl.BlockSpec((1, seq, LANES), lambda b, p: (b, 0, p)),
        out_shape=jax.ShapeDtypeStruct((batch, seq, M_WIDTH), BF16),
        scratch_shapes=[
            pltpu.VMEM((2, tq // qc, 1, qc), F32),
            pltpu.VMEM((2, tq // qc, LANES, qc), F32),
            pltpu.VMEM((2, tq // qc, tk, qc), F32),
            pltpu.VMEM((2, tq // qc, tk, qc), F32),
            pltpu.VMEM((2, tq // qc, 1, qc), F32),
            pltpu.VMEM((2, tq // qc, 1, qc), F32),
        ],
        compiler_params=_params(2),
        name="mla_attention",
    )(q, k, vt)


def _t5_bucket_table(dilation, n_back):
    blk = BAND_BLOCK
    sub_dist = (np.arange(blk)[:, None] + blk) - np.arange(2 * blk)[None, :]
    dist = np.clip(sub_dist, 0, n_back) * dilation
    max_exact = REL_BUCKETS // 2
    d = np.maximum(dist, 1).astype(np.float32)
    ratio = np.log(d / np.float32(max_exact)) / np.float32(math.log(REL_MAX_DIST / max_exact))
    log_b = max_exact + (ratio * np.float32(REL_BUCKETS - max_exact)).astype(np.int32)
    log_b = np.minimum(log_b, REL_BUCKETS - 1)
    return np.where(dist < max_exact, dist, log_b).astype(np.int32)


def _rows(start, count, stride):
    return pl.ds(start, count) if stride == 1 else pl.ds(start, count, stride=stride)


def _dilated_kernel(rb_ref, bucket_ref, q_ref, kc_ref, kp_ref, vc_ref, vp_ref, o_ref,
                    bias_scr, acc_scr, m_scr, regroup_scr, out_scr):
    blk = BAND_BLOCK
    sup = SUPER_BLOCK
    grp = DILATED_REGROUP
    sub = sup // grp
    pair = pl.program_id(2)
    first_step = (pl.program_id(0) == 0) & (pl.program_id(1) == 0) & (pair == 0)

    @pl.when(first_step)
    def _build_bias():
        row = lax.broadcasted_iota(jnp.int32, (blk, 2 * blk), 0)
        col = lax.broadcasted_iota(jnp.int32, (blk, 2 * blk), 1)
        sub_dist = row + blk - col
        in_band = (sub_dist >= 0) & (sub_dist <= blk)
        for g in range(len(DILATED_PATTERNS)):
            bucket = bucket_ref[g]
            for hd in range(A_HEADS):
                bias = jnp.zeros((blk, 2 * blk), F32)
                for bk in range(REL_BUCKETS):
                    bias = jnp.where(bucket == bk, rb_ref[hd, bk] * LOG2_E, bias)
                bias_scr[g, hd] = jnp.where(in_band, bias, NEG_INF).astype(BF16)

    sources = (q_ref, kc_ref, kp_ref, vc_ref, vp_ref)
    for idx, ref in enumerate(sources):
        for r in range(grp):
            regroup_scr[idx, r * sub:(r + 1) * sub, :] = ref[0, 0, _rows(r, sub, grp), :]

    first_valid_col = jnp.where(pl.program_id(1) > 0, 0, blk)
    col = lax.broadcasted_iota(jnp.int32, (blk, 2 * blk), 1)
    lane = lax.broadcasted_iota(jnp.int32, (blk, LANES), 1)
    lane2 = lax.broadcasted_iota(jnp.int32, (2 * blk, LANES), 1)
    contract_last = (((1,), (1,)), ((), ()))
    n_pat = len(DILATED_PATTERNS)

    for g, (_, dil) in enumerate(DILATED_PATTERNS):
        regrouped = dil % grp == 0
        step = dil // grp if regrouped else dil
        for res in range(dil):
            base = (res % grp) * sub + res // grp if regrouped else res
            prev_base = base + (sub if regrouped else sup) - blk * step
            if regrouped:
                load = lambda idx, r: regroup_scr[idx, r, :].astype(BF16)
            else:
                load = lambda idx, r: sources[idx][0, 0, r, :].astype(BF16)
            prev_rows = _rows(prev_base, blk, step)
            k_prev, v_prev = load(2, prev_rows), load(4, prev_rows)
            for n in range(sup // (blk * dil)):
                rows = _rows(base + blk * step * n, blk, step)
                q, k_cur, v_cur = load(0, rows), load(1, rows), load(3, rows)
                k2 = jnp.concatenate([k_prev, k_cur], axis=0)
                v2 = jnp.concatenate([v_prev, v_cur], axis=0)
                k_prev, v_prev = k_cur, v_cur
                for hh in range(2):
                    in_head = (lane < A_HEAD_DIM) if hh == 0 else (lane >= A_HEAD_DIM)
                    in_head2 = (lane2 < A_HEAD_DIM) if hh == 0 else (lane2 >= A_HEAD_DIM)
                    s = lax.dot_general(jnp.where(in_head, q, jnp.zeros_like(q)), k2, contract_last,
                                        preferred_element_type=F32)
                    bias = bias_scr[g, 2 * pair + hh]
                    if n == 0:
                        bias = jnp.where(col >= first_valid_col, bias, jnp.full_like(bias, NEG_INF))
                    s = s.astype(BF16) + bias
                    m_blk = jnp.max(s, axis=-1, keepdims=True)
                    p = jnp.exp2(s - m_blk)
                    acc_scr[g, hh, rows, :] = _bdot(p, jnp.where(in_head2, v2, jnp.ones_like(v2)))
                    m_scr[g, hh, rows, :] = jnp.broadcast_to(m_blk.astype(F32), (blk, LANES))

    lane_sub = lax.broadcasted_iota(jnp.int32, (sub, LANES), 1)
    for r in range(grp):
        chunk = [pl.ds(r * sub, sub) if dil % grp == 0 else _rows(r, sub, grp) for _, dil in DILATED_PATTERNS]
        halves = []
        for hh in range(2):
            maxes = [m_scr[g, hh, chunk[g], :] for g in range(n_pat)]
            top = maxes[0]
            for g in range(1, n_pat):
                top = jnp.maximum(top, maxes[g])
            total = jnp.exp2(maxes[0] - top) * acc_scr[0, hh, chunk[0], :]
            for g in range(1, n_pat):
                total = total + jnp.exp2(maxes[g] - top) * acc_scr[g, hh, chunk[g], :]
            halves.append(total * (1.0 / pltpu.roll(total, A_HEAD_DIM, 1)))
        out_scr[_rows(r, sub, grp), :] = jnp.where(lane_sub < A_HEAD_DIM, halves[0], halves[1])
    o_ref[0] = out_scr[...].astype(BF16)


def _dilated_attention(a_qkv, rel_bias):
    batch, _, seq, _ = a_qkv.shape
    blk = BAND_BLOCK
    sup = SUPER_BLOCK
    n_pat = len(DILATED_PATTERNS)
    assert all(w // d == blk for w, d in DILATED_PATTERNS), "band of exactly one block behind the query"
    assert seq % sup == 0
    bucket = jnp.asarray(np.stack([_t5_bucket_table(d, w // d) for w, d in DILATED_PATTERNS]))

    def part(which, prev):
        def index(b, s, p):
            return (b, which * A_PAIRS + p, jnp.maximum(s - 1, 0) if prev else s, 0)
        return pl.BlockSpec((1, 1, sup, LANES), index)

    return pl.pallas_call(
        _dilated_kernel,
        grid=(batch, seq // sup, A_PAIRS),
        in_specs=[
            pl.BlockSpec(memory_space=pltpu.SMEM),
            pl.BlockSpec((n_pat, blk, 2 * blk), lambda b, s, p: (0, 0, 0)),
            part(0, False), part(1, False), part(1, True), part(2, False), part(2, True),
        ],
        out_specs=pl.BlockSpec((1, sup, LANES), lambda b, s, p: (b, s, p)),
        out_shape=jax.ShapeDtypeStruct((batch, seq, A_WIDTH), BF16),
        scratch_shapes=[
            pltpu.VMEM((n_pat, A_HEADS, blk, 2 * blk), BF16),
            pltpu.VMEM((n_pat, 2, sup, LANES), F32),
            pltpu.VMEM((n_pat, 2, sup, LANES), F32),
            pltpu.VMEM((5, sup, LANES), F32),
            pltpu.VMEM((sup, LANES), F32),
        ],
        compiler_params=_params(3),
        name="dilated_attention",
    )(rel_bias, bucket, a_qkv, a_qkv, a_qkv, a_qkv, a_qkv)


def _output_kernel(x_ref, mod_ref, oa_ref, ob_ref, gate_ref, wa_ref, wb_ref, wo_ref,
                   g_ref, gf_ref, wg_ref, wu_ref, wd_ref, out_ref):
    rows = x_ref.shape[1] // 2
    halves = [slice(0, rows), slice(rows, 2 * rows)]

    mixed = []
    for r in halves:
        y_a = _bdot(oa_ref[0, r, :], wa_ref[...])
        y_b = _bdot(ob_ref[0, r, :], wb_ref[...])
        gates = gate_ref[0, r, :].astype(F32)
        merged = _sigmoid(gates[:, :D_MODEL]) * y_a + _sigmoid(gates[:, D_MODEL:]) * y_b
        mixed.append(_bdot(merged.astype(BF16), wo_ref[...]))

    xs, acts = [], []
    for r, mix in zip(halves, mixed):
        x = x_ref[0, r, :] + mod_ref[0, 2:3, :] * mix
        hb = ((_rms(x) * g_ref[...]) * (1.0 + mod_ref[0, 4:5, :]) + mod_ref[0, 3:4, :]).astype(BF16)
        gate = _bdot(hb, wg_ref[...])
        up = _bdot(hb, wu_ref[...])
        xs.append(x)
        acts.append((gate * _sigmoid(gate) * up).astype(BF16))

    for r, x, act in zip(halves, xs, acts):
        y = x + mod_ref[0, 5:6, :] * _bdot(act, wd_ref[...])
        out_ref[0, r, :] = _rms(y) * gf_ref[...]


def _output_stage(x, mod, o_a, o_b, gates, w_up_a, w_up_b, w_o, g_ffn, g_final, w_gate, w_up, w_down):
    batch, seq, _ = x.shape
    tm = ROW_TILE
    row3 = lambda b, i: (b, i, 0)
    half = pl.BlockSpec((1, tm, A_WIDTH), row3)
    return pl.pallas_call(
        _output_kernel,
        grid=(batch, seq // tm),
        in_specs=[
            pl.BlockSpec((1, tm, D_MODEL), row3),
            pl.BlockSpec((1, N_MOD, D_MODEL), lambda b, i: (b, 0, 0)),
            half, half,
            pl.BlockSpec((1, tm, 2 * D_MODEL), row3),
            _resident((A_WIDTH, D_MODEL)), _resident((M_WIDTH, D_MODEL)), _resident((D_MODEL, D_MODEL)),
            _resident((1, D_MODEL)), _resident((1, D_MODEL)),
            _resident((D_MODEL, D_FF)), _resident((D_MODEL, D_FF)), _resident((D_FF, D_MODEL)),
        ],
        out_specs=pl.BlockSpec((1, tm, D_MODEL), row3),
        out_shape=jax.ShapeDtypeStruct((batch, seq, D_MODEL), F32),
        compiler_params=_params(2),
        name="output_stage",
    )(x, mod, o_a, o_b, gates,
      w_up_a.astype(BF16), w_up_b.astype(BF16), w_o.astype(BF16),
      g_ffn.reshape(1, D_MODEL), g_final.reshape(1, D_MODEL),
      w_gate.astype(BF16), w_up.astype(BF16), w_down.astype(BF16))


def kernel(x, c, positions, rel_bias, w_ada, b_ada, g_mix, w_in, g_q_lora, w_uq, g_kv_lora, w_ukv,
           w_up_a, w_up_b, w_o, g_ffn, w_gate, w_up, w_down, g_final):
    assert w_ada.shape[0] == 1, "single-layer trunk"
    mod = _modulation(c, w_ada[0], b_ada[0])
    a_qkv, gates, q, k, vt = _input_stage(x, mod, g_mix[0], positions, w_in[0], g_q_lora[0], w_uq[0],
                                         g_kv_lora[0], w_ukv[0])
    o_b = _mla_attention(q, k, vt)
    o_a = _dilated_attention(a_qkv, rel_bias)
    return _output_stage(x, mod, o_a, o_b, gates, w_up_a[0], w_up_b[0], w_o[0],
                         g_ffn[0], g_final, w_gate[0], w_up[0], w_down[0])
```

```python
import math

import jax
import jax.numpy as jnp
import numpy as np
from jax import lax
from jax.experimental import pallas as pl
from jax.experimental.pallas import tpu as pltpu

D_MODEL = 1024
A_HEADS = 8
A_HEAD_DIM = 64
A_WIDTH = A_HEADS * A_HEAD_DIM
DILATED_PATTERNS = ((128, 1), (512, 4), (2048, 16))
BAND_BLOCK = 128
REL_BUCKETS = 32
REL_MAX_DIST = 2048
M_HEADS = 8
M_NOPE = 64
M_ROPE = 32
M_V = 64
M_Q_LORA = 768
M_KV_LORA = 256
M_WIDTH = M_HEADS * M_V
ROPE_THETA = 10000.0
D_FF = -(-8 * D_MODEL // (3 * 256)) * 256
N_MOD = 6
EPS = 1e-6
NEG_INF = -1e30

LANES = 128
SUBLANES = 8
V7X_VMEM_BYTES = 64 * 1024 * 1024
VMEM_RESERVED_BYTES = 8 * 1024 * 1024
VMEM_LIMIT_BYTES = V7X_VMEM_BYTES - VMEM_RESERVED_BYTES

M_HEAD_PAD = LANES
M_PAIRS = M_HEADS // 2
ROPE_HALF = M_ROPE // 2
ROPE_LO = M_NOPE
ROPE_MID = M_NOPE + ROPE_HALF
ROPE_HI = M_NOPE + M_ROPE

A_PAIRS = A_HEADS // 2
A_TILES = 3 * A_PAIRS
SUPER_BLOCK = BAND_BLOCK * max(d for _, d in DILATED_PATTERNS)
DILATED_REGROUP = 4
LOG2_E = math.log2(math.e)

ROW_TILE = 512
MOD_COLUMN_BLOCK = D_MODEL
MLA_KEY_TILE = 512
MLA_QUERY_TILE = 2 * MLA_KEY_TILE
MLA_QUERY_CHUNK = 256

F32 = jnp.float32
BF16 = jnp.bfloat16


def _params(n_axes):
    return pltpu.CompilerParams(
        dimension_semantics=("arbitrary",) * n_axes,
        vmem_limit_bytes=VMEM_LIMIT_BYTES,
    )


def _resident(shape):
    zeros = (0,) * len(shape)
    return pl.BlockSpec(shape, lambda *_: zeros, pipeline_mode=pl.Buffered(1))


def _bdot(a, b):
    return jnp.dot(a, b, preferred_element_type=F32)


def _bdot_t(a, b_t):
    return lax.dot_general(a, b_t, (((1,), (1,)), ((), ())), preferred_element_type=F32)


def _rms(x):
    return x * lax.rsqrt(jnp.mean(x * x, axis=-1, keepdims=True) + EPS)


def _sigmoid(x):
    return 1.0 / (1.0 + jnp.exp(-x))


def _mod_kernel(ct_ref, w_ref, b_ref, o_ref):
    c = ct_ref[...]
    cond = c * _sigmoid(c)
    w = w_ref[...]
    o_ref[...] = jnp.zeros(o_ref.shape, F32)
    for b in range(c.shape[1]):
        o_ref[b:b + 1, :] = jnp.sum(cond[:, b:b + 1] * w, axis=0, keepdims=True) + b_ref[...]


def _modulation(c, w_ada, b_ada):
    batch = c.shape[0]
    rows = -(-batch // SUBLANES) * SUBLANES
    cols = MOD_COLUMN_BLOCK
    out = pl.pallas_call(
        _mod_kernel,
        grid=(N_MOD * D_MODEL // cols,),
        in_specs=[
            pl.BlockSpec((D_MODEL, batch), lambda j: (0, 0)),
            pl.BlockSpec((D_MODEL, cols), lambda j: (0, j)),
            pl.BlockSpec((1, cols), lambda j: (0, j)),
        ],
        out_specs=pl.BlockSpec((rows, cols), lambda j: (0, j)),
        out_shape=jax.ShapeDtypeStruct((rows, N_MOD * D_MODEL), F32),
        compiler_params=_params(1),
        name="adaln_mod",
    )(c.T, w_ada, b_ada.reshape(1, N_MOD * D_MODEL))
    return out[:batch].reshape(batch, N_MOD, D_MODEL)


def _rope_lanes(x, cos, signed_sin):
    lane = lax.broadcasted_iota(jnp.int32, x.shape, 1)
    partner = jnp.where(lane < ROPE_MID, pltpu.roll(x, LANES - ROPE_HALF, 1), pltpu.roll(x, ROPE_HALF, 1))
    return x * cos + partner * signed_sin


def _input_kernel(x_ref, mod_ref, g_ref, pos_ref, freq_ref, wa_ref, wg_ref, wcq_ref, wckv_ref, wkr_ref,
                  gq_ref, wuq_ref, gkv_ref, wuk_ref, wuv_ref,
                  a_ref, gate_ref, q_ref, k_ref, v_ref):
    shift = mod_ref[0, 0:1, :]
    scale = mod_ref[0, 1:2, :]
    half = x_ref.shape[1] // 2
    hb_halves = [((_rms(x_ref[0, r, :]) * g_ref[...]) * (1.0 + scale) + shift).astype(BF16)
                 for r in (slice(0, half), slice(half, 2 * half))]

    c_q = jnp.concatenate([_bdot_t(hb_half, wcq_ref[...]) for hb_half in hb_halves], axis=0)
    hb = jnp.concatenate(hb_halves, axis=0)
    c_kv = _bdot_t(hb, wckv_ref[...])
    k_r = _bdot_t(hb, wkr_ref[...])

    ang = freq_ref[...] * pos_ref[0, 0].astype(F32)
    cos_r, sin_r = jnp.cos(ang), jnp.sin(ang)
    rows = ang.shape[1]
    cos = jnp.concatenate(
        [jnp.ones((ROPE_LO, rows), F32), cos_r, jnp.ones((LANES - ROPE_HI, rows), F32)], axis=0).T
    signed_sin = jnp.concatenate(
        [jnp.zeros((ROPE_LO, rows), F32), -sin_r[:ROPE_HALF], sin_r[ROPE_HALF:],
         jnp.zeros((LANES - ROPE_HI, rows), F32)], axis=0).T

    q_all = _bdot((_rms(c_q) * gq_ref[...]).astype(BF16), wuq_ref[...])
    q_scale = (M_NOPE + M_ROPE) ** -0.5 * LOG2_E
    for hd in range(M_HEADS):
        q_h = q_all[:, hd * M_HEAD_PAD:(hd + 1) * M_HEAD_PAD]
        q_ref[0, hd] = (_rope_lanes(q_h, cos, signed_sin) * q_scale).astype(BF16)

    c_kv = (_rms(c_kv) * gkv_ref[...]).astype(BF16)
    k_rope = _rope_lanes(k_r, cos, signed_sin)
    k_all = _bdot(c_kv, wuk_ref[...])
    for hd in range(M_HEADS):
        k_ref[0, hd] = (k_all[:, hd * M_HEAD_PAD:(hd + 1) * M_HEAD_PAD] + k_rope).astype(BF16)
    v_all = _bdot(c_kv, wuv_ref[...])
    for pr in range(M_PAIRS):
        v_ref[0, pr, 0] = v_all[:, pr * LANES:(pr + 1) * LANES].T.astype(BF16)

    a_all = _bdot_t(hb, wa_ref[...])
    for j in range(A_TILES):
        tile = a_all[:, j * LANES:(j + 1) * LANES]
        a_ref[0, j] = tile * (A_HEAD_DIM ** -0.5 * LOG2_E) if j < A_PAIRS else tile
    gate_ref[0] = _bdot_t(hb, wg_ref[...]).astype(BF16)


def _input_stage(x, mod, g_mix, positions, w_in, g_q_lora, w_uq, g_kv_lora, w_ukv):
    batch, seq, _ = x.shape
    tm = MLA_KEY_TILE
    s0 = 3 * A_WIDTH
    s1 = s0 + M_Q_LORA
    s2 = s1 + M_KV_LORA
    s3 = s2 + M_ROPE
    assert s0 % M_Q_LORA == 0 and s1 % M_KV_LORA == 0
    w_all = w_in.T.astype(BF16)
    w_kr = jnp.pad(w_all[s2:s3], ((ROPE_LO, LANES - ROPE_HI), (0, 0)))
    w_g = w_all[s3:]

    def columns(width, start):
        return pl.BlockSpec((width, D_MODEL), lambda *_: (start // width, 0), pipeline_mode=pl.Buffered(1))
    w_uq_p = jnp.pad(w_uq, ((0, 0), (0, 0), (0, M_HEAD_PAD - M_NOPE - M_ROPE)))
    w_uq_p = w_uq_p.reshape(M_Q_LORA, M_HEADS * M_HEAD_PAD).astype(BF16)
    w_uk_p = jnp.pad(w_ukv[:, :, :M_NOPE], ((0, 0), (0, 0), (0, M_HEAD_PAD - M_NOPE)))
    w_uk_p = w_uk_p.reshape(M_KV_LORA, M_HEADS * M_HEAD_PAD).astype(BF16)
    w_uv = w_ukv[:, :, M_NOPE:].reshape(M_KV_LORA, M_WIDTH).astype(BF16)

    freqs = ROPE_THETA ** (-jnp.arange(ROPE_HALF, dtype=F32) / ROPE_HALF)
    freq_col = jnp.concatenate([freqs, freqs]).reshape(M_ROPE, 1)

    row3 = lambda b, i: (b, i, 0)
    head4 = lambda b, i: (b, 0, i, 0)
    return pl.pallas_call(
        _input_kernel,
        grid=(batch, seq // tm),
        in_specs=[
            pl.BlockSpec((1, tm, D_MODEL), row3),
            pl.BlockSpec((1, N_MOD, D_MODEL), lambda b, i: (b, 0, 0)),
            _resident((1, D_MODEL)),
            pl.BlockSpec((1, 1, 1, tm), lambda b, i: (b, i, 0, 0)),
            _resident((M_ROPE, 1)),
            columns(s0, 0), _resident(w_g.shape), columns(M_Q_LORA, s0), columns(M_KV_LORA, s1),
            _resident(w_kr.shape),
            _resident((1, M_Q_LORA)), _resident(w_uq_p.shape),
            _resident((1, M_KV_LORA)), _resident(w_uk_p.shape), _resident(w_uv.shape),
        ],
        out_specs=[
            pl.BlockSpec((1, A_TILES, tm, LANES), head4),
            pl.BlockSpec((1, tm, 2 * D_MODEL), row3),
            pl.BlockSpec((1, M_HEADS, tm, M_HEAD_PAD), head4),
            pl.BlockSpec((1, M_HEADS, tm, M_HEAD_PAD), head4),
            pl.BlockSpec((1, M_PAIRS, 1, LANES, tm), lambda b, i: (b, 0, i, 0, 0)),
        ],
        out_shape=[
            jax.ShapeDtypeStruct((batch, A_TILES, seq, LANES), F32),
            jax.ShapeDtypeStruct((batch, seq, 2 * D_MODEL), BF16),
            jax.ShapeDtypeStruct((batch, M_HEADS, seq, M_HEAD_PAD), BF16),
            jax.ShapeDtypeStruct((batch, M_HEADS, seq, M_HEAD_PAD), BF16),
            jax.ShapeDtypeStruct((batch, M_PAIRS, seq // tm, LANES, tm), BF16),
        ],
        compiler_params=_params(2),
        name="input_stage",
    )(x, mod, g_mix.reshape(1, D_MODEL), positions.reshape(batch, seq // tm, 1, tm), freq_col,
      w_all, w_g, w_all, w_all, w_kr, g_q_lora.reshape(1, M_Q_LORA), w_uq_p,
      g_kv_lora.reshape(1, M_KV_LORA), w_uk_p, w_uv)


def _mla_kernel(q_ref, k_ref, vt_ref, o_ref, m_scr, acc_scr, sa_scr, sb_scr, max_a_scr, max_b_scr):
    tq, tk = MLA_QUERY_TILE, MLA_KEY_TILE
    qc = MLA_QUERY_CHUNK
    n_query_tiles = q_ref.shape[2] // tq
    contract_last = (((1,), (1,)), ((), ()))
    v_row = lax.broadcasted_iota(jnp.int32, (LANES, tk), 0)
    own_rows = [v_row < M_V, v_row >= M_V]
    out_row = lax.broadcasted_iota(jnp.int32, (LANES, qc), 0)


    def reset():
        m_scr[...] = jnp.full(m_scr.shape, NEG_INF, F32)
        acc_scr[...] = jnp.zeros(acc_scr.shape, F32)

    def finish(q_tile):
        for c in range(tq // qc):
            o0 = acc_scr[0, c] * (1.0 / acc_scr[0, c, M_V:M_V + 1, :])
            o1 = acc_scr[1, c] * (1.0 / acc_scr[1, c, 0:1, :])
            rows = pl.ds(pl.multiple_of(q_tile * tq + c * qc, qc), qc)
            o_ref[0, rows, :] = jnp.where(out_row < M_V, o0, o1).T.astype(BF16)

    def block(scored=None, absorbed=None):
        def visible_keys(q0, diagonal):
            return qc if diagonal is not None and q0 == diagonal else tk

        pieces, chains = [], []
        if scored is not None:
            next_tile, next_s, next_max, diagonal, q_tile = scored
            start = pl.multiple_of(next_tile * tk, tk)
            keys = [k_ref[0, hh, pl.ds(start, tk), :] for hh in range(2)]
            pieces = [(hh, q0) for hh in range(2) for q0 in range(diagonal or 0, tq, qc)]
        if absorbed is not None:
            tile, s_scr, max_scr, absorbed_diagonal = absorbed
            vt = vt_ref[0, 0, tile]
            vt_aug = [jnp.where(own, vt, jnp.ones_like(vt)) for own in own_rows]
            chains = [(hh, q0) for hh in range(2) for q0 in range(absorbed_diagonal or 0, tq, qc)]
            state = [(m_scr[hh, q0 // qc], acc_scr[hh, q0 // qc]) for hh, q0 in chains]

        def score_piece(hh, q0):
            n_keys = visible_keys(q0, diagonal)
            q_rows = pl.ds(pl.multiple_of(q_tile * tq + q0, qc), qc)
            s = lax.dot_general(keys[hh][:n_keys], q_ref[0, hh, q_rows, :], contract_last,
                                preferred_element_type=F32)
            if diagonal is not None and q0 < diagonal + n_keys - 1:
                key_pos = lax.broadcasted_iota(jnp.int32, s.shape, 0) + diagonal
                query_pos = lax.broadcasted_iota(jnp.int32, s.shape, 1) + q0
                s = jnp.where(key_pos <= query_pos, s, NEG_INF)
            next_s[hh, q0 // qc, :n_keys, :] = s
            next_max[hh, q0 // qc] = jnp.max(s, axis=0, keepdims=True)

        def absorb_chain(i):
            hh, q0 = chains[i]
            c = q0 // qc
            n_keys = visible_keys(q0, absorbed_diagonal)
            m_prev, acc_prev = state[i]
            m_new = jnp.maximum(m_prev, max_scr[hh, c])
            p = jnp.exp2(s_scr[hh, c, :n_keys, :] - m_new).astype(BF16)
            acc_scr[hh, c] = jnp.exp2(m_prev - m_new) * acc_prev + _bdot(vt_aug[hh][:, :n_keys], p)
            m_scr[hh, c] = m_new

        if pieces:
            score_piece(*pieces[0])
        for i in range(max(len(pieces) - 1, len(chains))):
            if i + 1 < len(pieces):
                score_piece(*pieces[i + 1])
            if i < len(chains):
                absorb_chain(i)

    buf_a, buf_b = (sa_scr, max_a_scr), (sb_scr, max_b_scr)

    reset()
    block(scored=(0, *buf_a, 0, 0))
    block(scored=(1, *buf_b, tk, 0), absorbed=(0, *buf_a, 0))
    block(scored=(2, *buf_a, 0, 1), absorbed=(1, *buf_b, tk))
    finish(0)

    def query_tile(i, carry):
        reset()
        block(scored=(2 * i + 1, *buf_b, tk, i), absorbed=(2 * i, *buf_a, 0))
        block(scored=(0, *buf_a, None, i), absorbed=(2 * i + 1, *buf_b, tk))

        def pair(j):
            block(scored=(2 * j + 1, *buf_b, None, i), absorbed=(2 * j, *buf_a, None))
            block(scored=(2 * j + 2, *buf_a, None, i), absorbed=(2 * j + 1, *buf_b, None))

        def two_pairs(j, inner):
            pair(2 * j)
            pair(2 * j + 1)
            return inner

        lax.fori_loop(0, (i - 1) // 2, two_pairs, 0)

        @pl.when((i - 1) % 2 == 1)
        def _odd_pair():
            pair(i - 2)

        block(scored=(2 * i - 1, *buf_b, None, i), absorbed=(2 * i - 2, *buf_a, None))
        nxt = jnp.minimum(i + 1, n_query_tiles - 1)
        block(scored=(2 * nxt, *buf_a, 0, nxt), absorbed=(2 * i - 1, *buf_b, None))
        finish(i)
        return carry

    lax.fori_loop(1, n_query_tiles, query_tile, 0)


def _mla_attention(q, k, vt):
    batch, _, seq, _ = q.shape
    tq, tk, qc = MLA_QUERY_TILE, MLA_KEY_TILE, MLA_QUERY_CHUNK
    assert tq == 2 * tk and seq % tq == 0 and seq // tq >= 2
    return pl.pallas_call(
        _mla_kernel,
        grid=(batch, M_PAIRS),
        in_specs=[
            pl.BlockSpec((1, 2, seq, M_HEAD_PAD), lambda b, p: (b, p, 0, 0)),
            pl.BlockSpec((1, 2, seq, M_HEAD_PAD), lambda b, p: (b, p, 0, 0)),
            pl.BlockSpec((1, 1, seq // tk, LANES, tk), lambda b, p: (b, p, 0, 0, 0)),
        ],
        out_specs=pl.BlockSpec((1, seq, LANES), lambda b, p: (b, 0, p)),
        out_shape=jax.ShapeDtypeStruct((batch, seq, M_WIDTH), BF16),
        scratch_shapes=[
            pltpu.VMEM((2, tq // qc, 1, qc), F32),
            pltpu.VMEM((2, tq // qc, LANES, qc), F32),
            pltpu.VMEM((2, tq // qc, tk, qc), F32),
            pltpu.VMEM((2, tq // qc, tk, qc), F32),
            pltpu.VMEM((2, tq // qc, 1, qc), F32),
            pltpu.VMEM((2, tq // qc, 1, qc), F32),
        ],
        compiler_params=_params(2),
        name="mla_attention",
    )(q, k, vt)


def _t5_bucket_table(dilation, n_back):
    blk = BAND_BLOCK
    sub_dist = (np.arange(blk)[:, None] + blk) - np.arange(2 * blk)[None, :]
    dist = np.clip(sub_dist, 0, n_back) * dilation
    max_exact = REL_BUCKETS // 2
    d = np.maximum(dist, 1).astype(np.float32)
    ratio = np.log(d / np.float32(max_exact)) / np.float32(math.log(REL_MAX_DIST / max_exact))
    log_b = max_exact + (ratio * np.float32(REL_BUCKETS - max_exact)).astype(np.int32)
    log_b = np.minimum(log_b, REL_BUCKETS - 1)
    return np.where(dist < max_exact, dist, log_b).astype(np.int32)


def _rows(start, count, stride):
    return pl.ds(start, count) if stride == 1 else pl.ds(start, count, stride=stride)


def _dilated_kernel(rb_ref, bucket_ref, q_ref, kc_ref, kp_ref, vc_ref, vp_ref, o_ref,
                    bias_scr, acc_scr, m_scr, regroup_scr, out_scr):
    blk = BAND_BLOCK
    sup = SUPER_BLOCK
    grp = DILATED_REGROUP
    sub = sup // grp
    pair = pl.program_id(2)
    first_step = (pl.program_id(0) == 0) & (pl.program_id(1) == 0) & (pair == 0)

    @pl.when(first_step)
    def _build_bias():
        row = lax.broadcasted_iota(jnp.int32, (blk, 2 * blk), 0)
        col = lax.broadcasted_iota(jnp.int32, (blk, 2 * blk), 1)
        sub_dist = row + blk - col
        in_band = (sub_dist >= 0) & (sub_dist <= blk)
        for g in range(len(DILATED_PATTERNS)):
            bucket = bucket_ref[g]
            for hd in range(A_HEADS):
                bias = jnp.zeros((blk, 2 * blk), F32)
                for bk in range(REL_BUCKETS):
                    bias = jnp.where(bucket == bk, rb_ref[hd, bk] * LOG2_E, bias)
                bias_scr[g, hd] = jnp.where(in_band, bias, NEG_INF).astype(BF16)

    sources = (q_ref, kc_ref, kp_ref, vc_ref, vp_ref)
    for idx, ref in enumerate(sources):
        for r in range(grp):
            regroup_scr[idx, r * sub:(r + 1) * sub, :] = ref[0, 0, _rows(r, sub, grp), :]

    first_valid_col = jnp.where(pl.program_id(1) > 0, 0, blk)
    col = lax.broadcasted_iota(jnp.int32, (blk, 2 * blk), 1)
    lane = lax.broadcasted_iota(jnp.int32, (blk, LANES), 1)
    lane2 = lax.broadcasted_iota(jnp.int32, (2 * blk, LANES), 1)
    contract_last = (((1,), (1,)), ((), ()))
    n_pat = len(DILATED_PATTERNS)

    for g, (_, dil) in enumerate(DILATED_PATTERNS):
        regrouped = dil % grp == 0
        step = dil // grp if regrouped else dil
        for res in range(dil):
            base = (res % grp) * sub + res // grp if regrouped else res
            prev_base = base + (sub if regrouped else sup) - blk * step
            if regrouped:
                load = lambda idx, r: regroup_scr[idx, r, :].astype(BF16)
            else:
                load = lambda idx, r: sources[idx][0, 0, r, :].astype(BF16)
            prev_rows = _rows(prev_base, blk, step)
            k_prev, v_prev = load(2, prev_rows), load(4, prev_rows)
            for n in range(sup // (blk * dil)):
                rows = _rows(base + blk * step * n, blk, step)
                q, k_cur, v_cur = load(0, rows), load(1, rows), load(3, rows)
                k2 = jnp.concatenate([k_prev, k_cur], axis=0)
                v2 = jnp.concatenate([v_prev, v_cur], axis=0)
                k_prev, v_prev = k_cur, v_cur
                for hh in range(2):
                    in_head = (lane < A_HEAD_DIM) if hh == 0 else (lane >= A_HEAD_DIM)
                    in_head2 = (lane2 < A_HEAD_DIM) if hh == 0 else (lane2 >= A_HEAD_DIM)
                    s = lax.dot_general(jnp.where(in_head, q, jnp.zeros_like(q)), k2, contract_last,
                                        preferred_element_type=F32)
                    bias = bias_scr[g, 2 * pair + hh]
                    if n == 0:
                        bias = jnp.where(col >= first_valid_col, bias, jnp.full_like(bias, NEG_INF))
                    s = s.astype(BF16) + bias
                    m_blk = jnp.max(s, axis=-1, keepdims=True)
                    p = jnp.exp2(s - m_blk)
                    acc_scr[g, hh, rows, :] = _bdot(p, jnp.where(in_head2, v2, jnp.ones_like(v2)))
                    m_scr[g, hh, rows, :] = jnp.broadcast_to(m_blk.astype(F32), (blk, LANES))

    lane_sub = lax.broadcasted_iota(jnp.int32, (sub, LANES), 1)
    for r in range(grp):
        chunk = [pl.ds(r * sub, sub) if dil % grp == 0 else _rows(r, sub, grp) for _, dil in DILATED_PATTERNS]
        halves = []
        for hh in range(2):
            maxes = [m_scr[g, hh, chunk[g], :] for g in range(n_pat)]
            top = maxes[0]
            for g in range(1, n_pat):
                top = jnp.maximum(top, maxes[g])
            total = jnp.exp2(maxes[0] - top) * acc_scr[0, hh, chunk[0], :]
            for g in range(1, n_pat):
                total = total + jnp.exp2(maxes[g] - top) * acc_scr[g, hh, chunk[g], :]
            halves.append(total * (1.0 / pltpu.roll(total, A_HEAD_DIM, 1)))
        out_scr[_rows(r, sub, grp), :] = jnp.where(lane_sub < A_HEAD_DIM, halves[0], halves[1])
    o_ref[0] = out_scr[...].astype(BF16)


def _dilated_attention(a_qkv, rel_bias):
    batch, _, seq, _ = a_qkv.shape
    blk = BAND_BLOCK
    sup = SUPER_BLOCK
    n_pat = len(DILATED_PATTERNS)
    assert all(w // d == blk for w, d in DILATED_PATTERNS), "band of exactly one block behind the query"
    assert seq % sup == 0
    bucket = jnp.asarray(np.stack([_t5_bucket_table(d, w // d) for w, d in DILATED_PATTERNS]))

    def part(which, prev):
        def index(b, s, p):
            return (b, which * A_PAIRS + p, jnp.maximum(s - 1, 0) if prev else s, 0)
        return pl.BlockSpec((1, 1, sup, LANES), index)

    return pl.pallas_call(
        _dilated_kernel,
        grid=(batch, seq // sup, A_PAIRS),
        in_specs=[
            pl.BlockSpec(memory_space=pltpu.SMEM),
            pl.BlockSpec((n_pat, blk, 2 * blk), lambda b, s, p: (0, 0, 0)),
            part(0, False), part(1, False), part(1, True), part(2, False), part(2, True),
        ],
        out_specs=pl.BlockSpec((1, sup, LANES), lambda b, s, p: (b, s, p)),
        out_shape=jax.ShapeDtypeStruct((batch, seq, A_WIDTH), BF16),
        scratch_shapes=[
            pltpu.VMEM((n_pat, A_HEADS, blk, 2 * blk), BF16),
            pltpu.VMEM((n_pat, 2, sup, LANES), F32),
            pltpu.VMEM((n_pat, 2, sup, LANES), F32),
            pltpu.VMEM((5, sup, LANES), F32),
            pltpu.VMEM((sup, LANES), F32),
        ],
        compiler_params=_params(3),
        name="dilated_attention",
    )(rel_bias, bucket, a_qkv, a_qkv, a_qkv, a_qkv, a_qkv)


def _output_kernel(x_ref, mod_ref, oa_ref, ob_ref, gate_ref, wa_ref, wb_ref, wo_ref,
                   g_ref, gf_ref, wg_ref, wu_ref, wd_ref, out_ref):
    rows = x_ref.shape[1] // 2
    halves = [slice(0, rows), slice(rows, 2 * rows)]

    mixed = []
    for r in halves:
        y_a = _bdot(oa_ref[0, r, :], wa_ref[...])
        y_b = _bdot(ob_ref[0, r, :], wb_ref[...])
        gates = gate_ref[0, r, :].astype(F32)
        merged = _sigmoid(gates[:, :D_MODEL]) * y_a + _sigmoid(gates[:, D_MODEL:]) * y_b
        mixed.append(_bdot(merged.astype(BF16), wo_ref[...]))

    xs, acts = [], []
    for r, mix in zip(halves, mixed):
        x = x_ref[0, r, :] + mod_ref[0, 2:3, :] * mix
        hb = ((_rms(x) * g_ref[...]) * (1.0 + mod_ref[0, 4:5, :]) + mod_ref[0, 3:4, :]).astype(BF16)
        gate = _bdot(hb, wg_ref[...])
        up = _bdot(hb, wu_ref[...])
        xs.append(x)
        acts.append((gate * _sigmoid(gate) * up).astype(BF16))

    for r, x, act in zip(halves, xs, acts):
        y = x + mod_ref[0, 5:6, :] * _bdot(act, wd_ref[...])
        out_ref[0, r, :] = _rms(y) * gf_ref[...]


def _output_stage(x, mod, o_a, o_b, gates, w_up_a, w_up_b, w_o, g_ffn, g_final, w_gate, w_up, w_down):
    batch, seq, _ = x.shape
    tm = ROW_TILE
    row3 = lambda b, i: (b, i, 0)
    half = pl.BlockSpec((1, tm, A_WIDTH), row3)
    return pl.pallas_call(
        _output_kernel,
        grid=(batch, seq // tm),
        in_specs=[
            pl.BlockSpec((1, tm, D_MODEL), row3),
            pl.BlockSpec((1, N_MOD, D_MODEL), lambda b, i: (b, 0, 0)),
            half, half,
            pl.BlockSpec((1, tm, 2 * D_MODEL), row3),
            _resident((A_WIDTH, D_MODEL)), _resident((M_WIDTH, D_MODEL)), _resident((D_MODEL, D_MODEL)),
            _resident((1, D_MODEL)), _resident((1, D_MODEL)),
            _resident((D_MODEL, D_FF)), _resident((D_MODEL, D_FF)), _resident((D_FF, D_MODEL)),
        ],
        out_specs=pl.BlockSpec((1, tm, D_MODEL), row3),
        out_shape=jax.ShapeDtypeStruct((batch, seq, D_MODEL), F32),
        compiler_params=_params(2),
        name="output_stage",
    )(x, mod, o_a, o_b, gates,
      w_up_a.astype(BF16), w_up_b.astype(BF16), w_o.astype(BF16),
      g_ffn.reshape(1, D_MODEL), g_final.reshape(1, D_MODEL),
      w_gate.astype(BF16), w_up.astype(BF16), w_down.astype(BF16))


def kernel(x, c, positions, rel_bias, w_ada, b_ada, g_mix, w_in, g_q_lora, w_uq, g_kv_lora, w_ukv,
           w_up_a, w_up_b, w_o, g_ffn, w_gate, w_up, w_down, g_final):
    assert w_ada.shape[0] == 1, "single-layer trunk"
    mod = _modulation(c, w_ada[0], b_ada[0])
    a_qkv, gates, q, k, vt = _input_stage(x, mod, g_mix[0], positions, w_in[0], g_q_lora[0], w_uq[0],
                                         g_kv_lora[0], w_ukv[0])
    o_b = _mla_attention(q, k, vt)
    o_a = _dilated_attention(a_qkv, rel_bias)
    return _output_stage(x, mod, o_a, o_b, gates, w_up_a[0], w_up_b[0], w_o[0],
                         g_ffn[0], g_final, w_gate[0], w_up[0], w_down[0])
```

```python
import functools
import math

import jax
import jax.numpy as jnp
import numpy as np
from jax import lax
from jax.experimental import pallas as pl
from jax.experimental.pallas import tpu as pltpu

D_MODEL = 1024
A_HEADS = 8
A_HEAD_DIM = 64
A_WIDTH = A_HEADS * A_HEAD_DIM
DILATED_PATTERNS = ((128, 1), (512, 4), (2048, 16))
BAND_BLOCK = 128
REL_BUCKETS = 32
REL_MAX_DIST = 2048
M_HEADS = 8
M_NOPE = 64
M_ROPE = 32
M_V = 64
M_Q_LORA = 768
M_KV_LORA = 256
M_WIDTH = M_HEADS * M_V
ROPE_THETA = 10000.0
D_FF = -(-8 * D_MODEL // (3 * 256)) * 256
N_MOD = 6
EPS = 1e-6
NEG_INF = -1e30

LANES = 128
SUBLANES = 8
V7X_VMEM_BYTES = 64 * 1024 * 1024
VMEM_RESERVED_BYTES = 8 * 1024 * 1024
VMEM_LIMIT_BYTES = V7X_VMEM_BYTES - VMEM_RESERVED_BYTES

M_HEAD_PAD = LANES
M_PAIRS = M_HEADS // 2
ROPE_HALF = M_ROPE // 2
ROPE_LO = M_NOPE
ROPE_MID = M_NOPE + ROPE_HALF
ROPE_HI = M_NOPE + M_ROPE

A_PAIRS = A_HEADS // 2
A_TILES = 3 * A_PAIRS
SUPER_BLOCK = BAND_BLOCK * max(d for _, d in DILATED_PATTERNS)
DILATED_REGROUP = 4
LOG2_E = math.log2(math.e)

ROW_TILE = 512
MOD_COLUMN_BLOCK = D_MODEL
MLA_KEY_TILE = 512
MLA_QUERY_TILE = 2 * MLA_KEY_TILE
MLA_QUERY_CHUNK = 256

F32 = jnp.float32
BF16 = jnp.bfloat16


def _params(n_axes):
    return pltpu.CompilerParams(
        dimension_semantics=("arbitrary",) * n_axes,
        vmem_limit_bytes=VMEM_LIMIT_BYTES,
    )


def _resident(shape):
    zeros = (0,) * len(shape)
    return pl.BlockSpec(shape, lambda *_: zeros, pipeline_mode=pl.Buffered(1))


def _bdot(a, b):
    return jnp.dot(a, b, preferred_element_type=F32)


def _bdot_t(a, b_t):
    return lax.dot_general(a, b_t, (((1,), (1,)), ((), ())), preferred_element_type=F32)


def _rms(x):
    return x * lax.rsqrt(jnp.mean(x * x, axis=-1, keepdims=True) + EPS)


def _sigmoid(x):
    return 1.0 / (1.0 + jnp.exp(-x))


def _mod_kernel(ct_ref, w_ref, b_ref, o_ref):
    c = ct_ref[...]
    cond = c * _sigmoid(c)
    w = w_ref[...]
    o_ref[...] = jnp.zeros(o_ref.shape, F32)
    for b in range(c.shape[1]):
        o_ref[b:b + 1, :] = jnp.sum(cond[:, b:b + 1] * w, axis=0, keepdims=True) + b_ref[...]


def _modulation(c, w_ada, b_ada):
    batch = c.shape[0]
    rows = -(-batch // SUBLANES) * SUBLANES
    cols = MOD_COLUMN_BLOCK
    out = pl.pallas_call(
        _mod_kernel,
        grid=(N_MOD * D_MODEL // cols,),
        in_specs=[
            pl.BlockSpec((D_MODEL, batch), lambda j: (0, 0)),
            pl.BlockSpec((D_MODEL, cols), lambda j: (0, j)),
            pl.BlockSpec((1, cols), lambda j: (0, j)),
        ],
        out_specs=pl.BlockSpec((rows, cols), lambda j: (0, j)),
        out_shape=jax.ShapeDtypeStruct((rows, N_MOD * D_MODEL), F32),
        compiler_params=_params(1),
        name="adaln_mod",
    )(c.T, w_ada, b_ada.reshape(1, N_MOD * D_MODEL))
    return out[:batch].reshape(batch, N_MOD, D_MODEL)


def _rope_lanes(x, cos, signed_sin):
    lane = lax.broadcasted_iota(jnp.int32, x.shape, 1)
    partner = jnp.where(lane < ROPE_MID, pltpu.roll(x, LANES - ROPE_HALF, 1), pltpu.roll(x, ROPE_HALF, 1))
    return x * cos + partner * signed_sin


def _input_kernel(x_ref, mod_ref, g_ref, pos_ref, freq_ref, wa_ref, wg_ref, wcq_ref, wckv_ref, wkr_ref,
                  gq_ref, wuq_ref, gkv_ref, wuk_ref, wuv_ref,
                  a_ref, gate_ref, q_ref, k_ref, v_ref):
    shift = mod_ref[0, 0:1, :]
    scale = mod_ref[0, 1:2, :]
    half = x_ref.shape[1] // 2
    hb_halves = [((_rms(x_ref[0, r, :]) * g_ref[...]) * (1.0 + scale) + shift).astype(BF16)
                 for r in (slice(0, half), slice(half, 2 * half))]

    c_q = jnp.concatenate([_bdot_t(hb_half, wcq_ref[...]) for hb_half in hb_halves], axis=0)
    hb = jnp.concatenate(hb_halves, axis=0)
    c_kv = _bdot_t(hb, wckv_ref[...])
    k_r = _bdot_t(hb, wkr_ref[...])

    ang = freq_ref[...] * pos_ref[0, 0].astype(F32)
    cos_r, sin_r = jnp.cos(ang), jnp.sin(ang)
    rows = ang.shape[1]
    cos = jnp.concatenate(
        [jnp.ones((ROPE_LO, rows), F32), cos_r, jnp.ones((LANES - ROPE_HI, rows), F32)], axis=0).T
    signed_sin = jnp.concatenate(
        [jnp.zeros((ROPE_LO, rows), F32), -sin_r[:ROPE_HALF], sin_r[ROPE_HALF:],
         jnp.zeros((LANES - ROPE_HI, rows), F32)], axis=0).T

    q_all = _bdot((_rms(c_q) * gq_ref[...]).astype(BF16), wuq_ref[...])
    q_scale = (M_NOPE + M_ROPE) ** -0.5 * LOG2_E
    for hd in range(M_HEADS):
        q_h = q_all[:, hd * M_HEAD_PAD:(hd + 1) * M_HEAD_PAD]
        q_ref[0, hd] = (_rope_lanes(q_h, cos, signed_sin) * q_scale).astype(BF16)

    c_kv = (_rms(c_kv) * gkv_ref[...]).astype(BF16)
    k_rope = _rope_lanes(k_r, cos, signed_sin)
    k_all = _bdot(c_kv, wuk_ref[...])
    for hd in range(M_HEADS):
        k_ref[0, hd] = (k_all[:, hd * M_HEAD_PAD:(hd + 1) * M_HEAD_PAD] + k_rope).astype(BF16)
    v_all = _bdot(c_kv, wuv_ref[...])
    for pr in range(M_PAIRS):
        v_ref[0, pr, 0] = v_all[:, pr * LANES:(pr + 1) * LANES].T.astype(BF16)

    a_all = _bdot_t(hb, wa_ref[...])
    for j in range(A_TILES):
        tile = a_all[:, j * LANES:(j + 1) * LANES]
        a_ref[0, j] = tile * (A_HEAD_DIM ** -0.5 * LOG2_E) if j < A_PAIRS else tile
    gate_ref[0] = _bdot_t(hb, wg_ref[...]).astype(BF16)


def _input_stage(x, mod, g_mix, positions, w_in, g_q_lora, w_uq, g_kv_lora, w_ukv):
    batch, seq, _ = x.shape
    tm = MLA_KEY_TILE
    s0 = 3 * A_WIDTH
    s1 = s0 + M_Q_LORA
    s2 = s1 + M_KV_LORA
    s3 = s2 + M_ROPE
    assert s0 % M_Q_LORA == 0 and s1 % M_KV_LORA == 0
    w_all = w_in.T.astype(BF16)
    w_kr = jnp.pad(w_all[s2:s3], ((ROPE_LO, LANES - ROPE_HI), (0, 0)))
    w_g = w_all[s3:]

    def columns(width, start):
        return pl.BlockSpec((width, D_MODEL), lambda *_: (start // width, 0), pipeline_mode=pl.Buffered(1))
    w_uq_p = jnp.pad(w_uq, ((0, 0), (0, 0), (0, M_HEAD_PAD - M_NOPE - M_ROPE)))
    w_uq_p = w_uq_p.reshape(M_Q_LORA, M_HEADS * M_HEAD_PAD).astype(BF16)
    w_uk_p = jnp.pad(w_ukv[:, :, :M_NOPE], ((0, 0), (0, 0), (0, M_HEAD_PAD - M_NOPE)))
    w_uk_p = w_uk_p.reshape(M_KV_LORA, M_HEADS * M_HEAD_PAD).astype(BF16)
    w_uv = w_ukv[:, :, M_NOPE:].reshape(M_KV_LORA, M_WIDTH).astype(BF16)

    freqs = ROPE_THETA ** (-jnp.arange(ROPE_HALF, dtype=F32) / ROPE_HALF)
    freq_col = jnp.concatenate([freqs, freqs]).reshape(M_ROPE, 1)

    row3 = lambda b, i: (b, i, 0)
    head4 = lambda b, i: (b, 0, i, 0)
    return pl.pallas_call(
        _input_kernel,
        grid=(batch, seq // tm),
        in_specs=[
            pl.BlockSpec((1, tm, D_MODEL), row3),
            pl.BlockSpec((1, N_MOD, D_MODEL), lambda b, i: (b, 0, 0)),
            _resident((1, D_MODEL)),
            pl.BlockSpec((1, 1, 1, tm), lambda b, i: (b, i, 0, 0)),
            _resident((M_ROPE, 1)),
            columns(s0, 0), _resident(w_g.shape), columns(M_Q_LORA, s0), columns(M_KV_LORA, s1),
            _resident(w_kr.shape),
            _resident((1, M_Q_LORA)), _resident(w_uq_p.shape),
            _resident((1, M_KV_LORA)), _resident(w_uk_p.shape), _resident(w_uv.shape),
        ],
        out_specs=[
            pl.BlockSpec((1, A_TILES, tm, LANES), head4),
            pl.BlockSpec((1, tm, 2 * D_MODEL), row3),
            pl.BlockSpec((1, M_HEADS, tm, M_HEAD_PAD), head4),
            pl.BlockSpec((1, M_HEADS, tm, M_HEAD_PAD), head4),
            pl.BlockSpec((1, M_PAIRS, 1, LANES, tm), lambda b, i: (b, 0, i, 0, 0)),
        ],
        out_shape=[
            jax.ShapeDtypeStruct((batch, A_TILES, seq, LANES), F32),
            jax.ShapeDtypeStruct((batch, seq, 2 * D_MODEL), BF16),
            jax.ShapeDtypeStruct((batch, M_HEADS, seq, M_HEAD_PAD), BF16),
            jax.ShapeDtypeStruct((batch, M_HEADS, seq, M_HEAD_PAD), BF16),
            jax.ShapeDtypeStruct((batch, M_PAIRS, seq // tm, LANES, tm), BF16),
        ],
        compiler_params=_params(2),
        name="input_stage",
    )(x, mod, g_mix.reshape(1, D_MODEL), positions.reshape(batch, seq // tm, 1, tm), freq_col,
      w_all, w_g, w_all, w_all, w_kr, g_q_lora.reshape(1, M_Q_LORA), w_uq_p,
      g_kv_lora.reshape(1, M_KV_LORA), w_uk_p, w_uv)


def _mla_kernel(q_ref, k_ref, vt_ref, o_ref, m_scr, acc_scr, sa_scr, sb_scr, max_a_scr, max_b_scr):
    tq, tk = MLA_QUERY_TILE, MLA_KEY_TILE
    qc = MLA_QUERY_CHUNK
    n_query_tiles = q_ref.shape[2] // tq
    contract_last = (((1,), (1,)), ((), ()))
    v_row = lax.broadcasted_iota(jnp.int32, (LANES, tk), 0)
    own_rows = [v_row < M_V, v_row >= M_V]
    out_row = lax.broadcasted_iota(jnp.int32, (LANES, qc), 0)


    def reset():
        m_scr[...] = jnp.full(m_scr.shape, NEG_INF, F32)
        acc_scr[...] = jnp.zeros(acc_scr.shape, F32)

    def finish(q_tile):
        for c in range(tq // qc):
            o0 = acc_scr[0, c] * (1.0 / acc_scr[0, c, M_V:M_V + 1, :])
            o1 = acc_scr[1, c] * (1.0 / acc_scr[1, c, 0:1, :])
            rows = pl.ds(pl.multiple_of(q_tile * tq + c * qc, qc), qc)
            o_ref[0, rows, :] = jnp.where(out_row < M_V, o0, o1).T.astype(BF16)

    def block(scored=None, absorbed=None):
        def visible_keys(q0, diagonal):
            return qc if diagonal is not None and q0 == diagonal else tk

        pieces, chains = [], []
        if scored is not None:
            next_tile, next_s, next_max, diagonal, q_tile = scored
            start = pl.multiple_of(next_tile * tk, tk)
            keys = [k_ref[0, hh, pl.ds(start, tk), :] for hh in range(2)]
            pieces = [(hh, q0) for hh in range(2) for q0 in range(diagonal or 0, tq, qc)]
        if absorbed is not None:
            tile, s_scr, max_scr, absorbed_diagonal = absorbed
            vt = vt_ref[0, 0, tile]
            vt_aug = [jnp.where(own, vt, jnp.ones_like(vt)) for own in own_rows]
            chains = [(hh, q0) for hh in range(2) for q0 in range(absorbed_diagonal or 0, tq, qc)]
            state = [(m_scr[hh, q0 // qc], acc_scr[hh, q0 // qc]) for hh, q0 in chains]

        def score_piece(hh, q0):
            n_keys = visible_keys(q0, diagonal)
            q_rows = pl.ds(pl.multiple_of(q_tile * tq + q0, qc), qc)
            s = lax.dot_general(keys[hh][:n_keys], q_ref[0, hh, q_rows, :], contract_last,
                                preferred_element_type=F32)
            if diagonal is not None and q0 < diagonal + n_keys - 1:
                key_pos = lax.broadcasted_iota(jnp.int32, s.shape, 0) + diagonal
                query_pos = lax.broadcasted_iota(jnp.int32, s.shape, 1) + q0
                s = jnp.where(key_pos <= query_pos, s, NEG_INF)
            next_s[hh, q0 // qc, :n_keys, :] = s
            next_max[hh, q0 // qc] = jnp.max(s, axis=0, keepdims=True)

        def absorb_chain(i):
            hh, q0 = chains[i]
            c = q0 // qc
            n_keys = visible_keys(q0, absorbed_diagonal)
            m_prev, acc_prev = state[i]
            m_new = jnp.maximum(m_prev, max_scr[hh, c])
            p = jnp.exp2(s_scr[hh, c, :n_keys, :] - m_new).astype(BF16)
            acc_scr[hh, c] = jnp.exp2(m_prev - m_new) * acc_prev + _bdot(vt_aug[hh][:, :n_keys], p)
            m_scr[hh, c] = m_new

        if pieces:
            score_piece(*pieces[0])
        for i in range(max(len(pieces) - 1, len(chains))):
            if i + 1 < len(pieces):
                score_piece(*pieces[i + 1])
            if i < len(chains):
                absorb_chain(i)

    buf_a, buf_b = (sa_scr, max_a_scr), (sb_scr, max_b_scr)

    reset()
    block(scored=(0, *buf_a, 0, 0))
    block(scored=(1, *buf_b, tk, 0), absorbed=(0, *buf_a, 0))
    block(scored=(2, *buf_a, 0, 1), absorbed=(1, *buf_b, tk))
    finish(0)

    def query_tile(i, carry):
        reset()
        block(scored=(2 * i + 1, *buf_b, tk, i), absorbed=(2 * i, *buf_a, 0))
        block(scored=(0, *buf_a, None, i), absorbed=(2 * i + 1, *buf_b, tk))

        def pair(j):
            block(scored=(2 * j + 1, *buf_b, None, i), absorbed=(2 * j, *buf_a, None))
            block(scored=(2 * j + 2, *buf_a, None, i), absorbed=(2 * j + 1, *buf_b, None))

        def two_pairs(j, inner):
            pair(2 * j)
            pair(2 * j + 1)
            return inner

        lax.fori_loop(0, (i - 1) // 2, two_pairs, 0)

        @pl.when((i - 1) % 2 == 1)
        def _odd_pair():
            pair(i - 2)

        block(scored=(2 * i - 1, *buf_b, None, i), absorbed=(2 * i - 2, *buf_a, None))
        nxt = jnp.minimum(i + 1, n_query_tiles - 1)
        block(scored=(2 * nxt, *buf_a, 0, nxt), absorbed=(2 * i - 1, *buf_b, None))
        finish(i)
        return carry

    lax.fori_loop(1, n_query_tiles, query_tile, 0)


def _mla_attention(q, k, vt):
    batch, _, seq, _ = q.shape
    tq, tk, qc = MLA_QUERY_TILE, MLA_KEY_TILE, MLA_QUERY_CHUNK
    assert tq == 2 * tk and seq % tq == 0 and seq // tq >= 2
    return pl.pallas_call(
        _mla_kernel,
        grid=(batch, M_PAIRS),
        in_specs=[
            pl.BlockSpec((1, 2, seq, M_HEAD_PAD), lambda b, p: (b, p, 0, 0)),
            pl.BlockSpec((1, 2, seq, M_HEAD_PAD), lambda b, p: (b, p, 0, 0)),
            pl.BlockSpec((1, 1, seq // tk, LANES, tk), lambda b, p: (b, p, 0, 0, 0)),
        ],
        out_specs=pl.BlockSpec((1, seq, LANES), lambda b, p: (b, 0, p)),
        out_shape=jax.ShapeDtypeStruct((batch, seq, M_WIDTH), BF16),
        scratch_shapes=[
            pltpu.VMEM((2, tq // qc, 1, qc), F32),
            pltpu.VMEM((2, tq // qc, LANES, qc), F32),
            pltpu.VMEM((2, tq // qc, tk, qc), F32),
            pltpu.VMEM((2, tq // qc, tk, qc), F32),
            pltpu.VMEM((2, tq // qc, 1, qc), F32),
            pltpu.VMEM((2, tq // qc, 1, qc), F32),
        ],
        compiler_params=_params(2),
        name="mla_attention",
    )(q, k, vt)


def _t5_bucket_table(dilation, n_back):
    blk = BAND_BLOCK
    sub_dist = (np.arange(blk)[:, None] + blk) - np.arange(2 * blk)[None, :]
    dist = np.clip(sub_dist, 0, n_back) * dilation
    max_exact = REL_BUCKETS // 2
    d = np.maximum(dist, 1).astype(np.float32)
    ratio = np.log(d / np.float32(max_exact)) / np.float32(math.log(REL_MAX_DIST / max_exact))
    log_b = max_exact + (ratio * np.float32(REL_BUCKETS - max_exact)).astype(np.int32)
    log_b = np.minimum(log_b, REL_BUCKETS - 1)
    return np.where(dist < max_exact, dist, log_b).astype(np.int32)


def _rows(start, count, stride):
    return pl.ds(start, count) if stride == 1 else pl.ds(start, count, stride=stride)


def _dilated_kernel(rb_ref, bucket_ref, q_ref, kc_ref, kp_ref, vc_ref, vp_ref, o_ref,
                    bias_scr, acc_a, m_a, acc_b, m_b, regroup_scr, numer_scr, denom_scr,
                    *, n_super, n_work):
    blk = BAND_BLOCK
    sup = SUPER_BLOCK
    grp = DILATED_REGROUP
    sub = sup // grp
    step_id = pl.program_id(0)
    work = jnp.minimum(step_id, n_work - 1)
    pair = work % A_PAIRS
    super_block = (work // A_PAIRS) % n_super

    @pl.when(step_id == 0)
    def _first_step():
        acc_b[...] = jnp.ones(acc_b.shape, F32)
        m_b[...] = jnp.zeros(m_b.shape, F32)

    @pl.when(step_id == 0)
    def _build_bias():
        row = lax.broadcasted_iota(jnp.int32, (blk, 2 * blk), 0)
        col = lax.broadcasted_iota(jnp.int32, (blk, 2 * blk), 1)
        sub_dist = row + blk - col
        in_band = (sub_dist >= 0) & (sub_dist <= blk)
        for g in range(len(DILATED_PATTERNS)):
            bucket = bucket_ref[g]
            for hd in range(A_HEADS):
                bias = jnp.zeros((blk, 2 * blk), F32)
                for bk in range(REL_BUCKETS):
                    bias = jnp.where(bucket == bk, rb_ref[hd, bk] * LOG2_E, bias)
                bias_scr[g, hd] = jnp.where(in_band, bias, NEG_INF).astype(BF16)

    sources = (q_ref, kc_ref, kp_ref, vc_ref, vp_ref)
    for idx, ref in enumerate(sources):
        for r in range(grp):
            regroup_scr[idx, r * sub:(r + 1) * sub, :] = ref[0, 0, _rows(r, sub, grp), :]

    first_valid_col = jnp.where(super_block > 0, 0, blk)
    col = lax.broadcasted_iota(jnp.int32, (blk, 2 * blk), 1)
    lane = lax.broadcasted_iota(jnp.int32, (blk, LANES), 1)
    lane2 = lax.broadcasted_iota(jnp.int32, (2 * blk, LANES), 1)
    contract_last = (((1,), (1,)), ((), ()))
    n_pat = len(DILATED_PATTERNS)
    n_chunks = sup // blk

    def combine_chunk(c, acc_old, m_old, anchor):
        half_word = jnp.uint32(16)
        bits = lax.shift_right_logical(
            lax.shift_right_logical(pltpu.bitcast(anchor, jnp.uint32), half_word), half_word)
        zero = pltpu.bitcast(bits, F32)
        reg_rows = pl.ds(c * blk, blk)
        tok_rows = _rows(c // (sub // blk) + grp * blk * (c % (sub // blk)), blk, grp)
        totals = []
        for hh in range(2):
            maxes, accs = [], []
            for g, (_, dil) in enumerate(DILATED_PATTERNS):
                g_rows = reg_rows if dil % grp == 0 else tok_rows
                maxes.append(m_old[g, hh, g_rows, :])
                accs.append(acc_old[g, hh, g_rows, :])
            maxes[0] = maxes[0] + zero
            top = maxes[0]
            for m_g in maxes[1:]:
                top = jnp.maximum(top, m_g)
            total = jnp.exp2(maxes[0] - top) * accs[0]
            for m_g, acc_g in zip(maxes[1:], accs[1:]):
                total = total + jnp.exp2(m_g - top) * acc_g
            totals.append(total)
        numer_scr[tok_rows, :] = jnp.where(lane < A_HEAD_DIM, totals[0], totals[1])
        denom_scr[tok_rows, :] = jnp.where(lane < A_HEAD_DIM, totals[1], totals[0])

    def blocks(acc_new, m_new, acc_old, m_old):
        n_blocks = sum(sup // blk for _ in DILATED_PATTERNS)
        done = 0
        combined = 0
        for g, (_, dil) in enumerate(DILATED_PATTERNS):
            regrouped = dil % grp == 0
            step = dil // grp if regrouped else dil
            for res in range(dil):
                base = (res % grp) * sub + res // grp if regrouped else res
                prev_base = base + (sub if regrouped else sup) - blk * step
                if regrouped:
                    load = lambda idx, r: regroup_scr[idx, r, :].astype(BF16)
                else:
                    load = lambda idx, r: sources[idx][0, 0, r, :].astype(BF16)
                prev_rows = _rows(prev_base, blk, step)
                k_prev, v_prev = load(2, prev_rows), load(4, prev_rows)
                for n in range(sup // (blk * dil)):
                    rows = _rows(base + blk * step * n, blk, step)
                    q, k_cur, v_cur = load(0, rows), load(1, rows), load(3, rows)
                    k2 = jnp.concatenate([k_prev, k_cur], axis=0)
                    v2 = jnp.concatenate([v_prev, v_cur], axis=0)
                    k_prev, v_prev = k_cur, v_cur
                    for hh in range(2):
                        in_head = (lane < A_HEAD_DIM) if hh == 0 else (lane >= A_HEAD_DIM)
                        in_head2 = (lane2 < A_HEAD_DIM) if hh == 0 else (lane2 >= A_HEAD_DIM)
                        s = lax.dot_general(jnp.where(in_head, q, jnp.zeros_like(q)), k2, contract_last,
                                            preferred_element_type=F32)
                        bias = bias_scr[g, 2 * pair + hh]
                        if n == 0:
                            bias = jnp.where(col >= first_valid_col, bias, jnp.full_like(bias, NEG_INF))
                        s = s.astype(BF16) + bias
                        m_blk = jnp.max(s, axis=-1, keepdims=True)
                        p = jnp.exp2(s - m_blk)
                        acc = _bdot(p, jnp.where(in_head2, v2, jnp.ones_like(v2)))
                        acc_new[g, hh, rows, :] = acc
                        m_new[g, hh, rows, :] = jnp.broadcast_to(m_blk.astype(F32), (blk, LANES))
                    done += 1
                    while combined * n_blocks < done * n_chunks:
                        combine_chunk(combined, acc_old, m_old, acc)
                        combined += 1
        assert combined == n_chunks

    @pl.when(step_id % 2 == 0)
    def _even():
        blocks(acc_a, m_a, acc_b, m_b)

    @pl.when(step_id % 2 == 1)
    def _odd():
        blocks(acc_b, m_b, acc_a, m_a)

    o_ref[0] = (numer_scr[...] * (1.0 / pltpu.roll(denom_scr[...], A_HEAD_DIM, 1))).astype(BF16)


def _dilated_attention(a_qkv, rel_bias):
    batch, _, seq, _ = a_qkv.shape
    blk = BAND_BLOCK
    sup = SUPER_BLOCK
    n_pat = len(DILATED_PATTERNS)
    assert all(w // d == blk for w, d in DILATED_PATTERNS), "band of exactly one block behind the query"
    assert seq % sup == 0
    bucket = jnp.asarray(np.stack([_t5_bucket_table(d, w // d) for w, d in DILATED_PATTERNS]))

    n_super = seq // sup
    n_work = batch * n_super * A_PAIRS

    def decode(work):
        return work // (n_super * A_PAIRS), (work // A_PAIRS) % n_super, work % A_PAIRS

    def part(which, prev):
        def index(t):
            b, s, p = decode(jnp.minimum(t, n_work - 1))
            return (b, which * A_PAIRS + p, jnp.maximum(s - 1, 0) if prev else s, 0)
        return pl.BlockSpec((1, 1, sup, LANES), index)

    def out_index(t):
        b, s, p = decode(jnp.maximum(t - 1, 0))
        return (b, s, p)

    results = pltpu.VMEM((n_pat, 2, sup, LANES), F32)
    return pl.pallas_call(
        functools.partial(_dilated_kernel, n_super=n_super, n_work=n_work),
        grid=(n_work + 1,),
        in_specs=[
            pl.BlockSpec(memory_space=pltpu.SMEM),
            pl.BlockSpec((n_pat, blk, 2 * blk), lambda t: (0, 0, 0)),
            part(0, False), part(1, False), part(1, True), part(2, False), part(2, True),
        ],
        out_specs=pl.BlockSpec((1, sup, LANES), out_index),
        out_shape=jax.ShapeDtypeStruct((batch, seq, A_WIDTH), BF16),
        scratch_shapes=[
            pltpu.VMEM((n_pat, A_HEADS, blk, 2 * blk), BF16),
            results, results, results, results,
            pltpu.VMEM((5, sup, LANES), F32),
            pltpu.VMEM((sup, LANES), F32),
            pltpu.VMEM((sup, LANES), F32),
        ],
        compiler_params=_params(1),
        name="dilated_attention",
    )(rel_bias, bucket, a_qkv, a_qkv, a_qkv, a_qkv, a_qkv)


def _output_kernel(x_ref, mod_ref, oa_ref, ob_ref, gate_ref, wa_ref, wb_ref, wo_ref,
                   g_ref, gf_ref, wg_ref, wu_ref, wd_ref, out_ref):
    rows = x_ref.shape[1] // 2
    halves = [slice(0, rows), slice(rows, 2 * rows)]

    mixed = []
    for r in halves:
        y_a = _bdot(oa_ref[0, r, :], wa_ref[...])
        y_b = _bdot(ob_ref[0, r, :], wb_ref[...])
        gates = gate_ref[0, r, :].astype(F32)
        merged = _sigmoid(gates[:, :D_MODEL]) * y_a + _sigmoid(gates[:, D_MODEL:]) * y_b
        mixed.append(_bdot(merged.astype(BF16), wo_ref[...]))

    xs, acts = [], []
    for r, mix in zip(halves, mixed):
        x = x_ref[0, r, :] + mod_ref[0, 2:3, :] * mix
        hb = ((_rms(x) * g_ref[...]) * (1.0 + mod_ref[0, 4:5, :]) + mod_ref[0, 3:4, :]).astype(BF16)
        gate = _bdot(hb, wg_ref[...])
        up = _bdot(hb, wu_ref[...])
        xs.append(x)
        acts.append((gate * _sigmoid(gate) * up).astype(BF16))

    for r, x, act in zip(halves, xs, acts):
        y = x + mod_ref[0, 5:6, :] * _bdot(act, wd_ref[...])
        out_ref[0, r, :] = _rms(y) * gf_ref[...]


def _output_stage(x, mod, o_a, o_b, gates, w_up_a, w_up_b, w_o, g_ffn, g_final, w_gate, w_up, w_down):
    batch, seq, _ = x.shape
    tm = ROW_TILE
    row3 = lambda b, i: (b, i, 0)
    half = pl.BlockSpec((1, tm, A_WIDTH), row3)
    return pl.pallas_call(
        _output_kernel,
        grid=(batch, seq // tm),
        in_specs=[
            pl.BlockSpec((1, tm, D_MODEL), row3),
            pl.BlockSpec((1, N_MOD, D_MODEL), lambda b, i: (b, 0, 0)),
            half, half,
            pl.BlockSpec((1, tm, 2 * D_MODEL), row3),
            _resident((A_WIDTH, D_MODEL)), _resident((M_WIDTH, D_MODEL)), _resident((D_MODEL, D_MODEL)),
            _resident((1, D_MODEL)), _resident((1, D_MODEL)),
            _resident((D_MODEL, D_FF)), _resident((D_MODEL, D_FF)), _resident((D_FF, D_MODEL)),
        ],
        out_specs=pl.BlockSpec((1, tm, D_MODEL), row3),
        out_shape=jax.ShapeDtypeStruct((batch, seq, D_MODEL), F32),
        compiler_params=_params(2),
        name="output_stage",
    )(x, mod, o_a, o_b, gates,
      w_up_a.astype(BF16), w_up_b.astype(BF16), w_o.astype(BF16),
      g_ffn.reshape(1, D_MODEL), g_final.reshape(1, D_MODEL),
      w_gate.astype(BF16), w_up.astype(BF16), w_down.astype(BF16))


def kernel(x, c, positions, rel_bias, w_ada, b_ada, g_mix, w_in, g_q_lora, w_uq, g_kv_lora, w_ukv,
           w_up_a, w_up_b, w_o, g_ffn, w_gate, w_up, w_down, g_final):
    assert w_ada.shape[0] == 1, "single-layer trunk"
    mod = _modulation(c, w_ada[0], b_ada[0])
    a_qkv, gates, q, k, vt = _input_stage(x, mod, g_mix[0], positions, w_in[0], g_q_lora[0], w_uq[0],
                                         g_kv_lora[0], w_ukv[0])
    o_b = _mla_attention(q, k, vt)
    o_a = _dilated_attention(a_qkv, rel_bias)
    return _output_stage(x, mod, o_a, o_b, gates, w_up_a[0], w_up_b[0], w_o[0],
                         g_ffn[0], g_final, w_gate[0], w_up[0], w_down[0])
```

```python
import math

import jax
import jax.numpy as jnp
import numpy as np
from jax import lax
from jax.experimental import pallas as pl
from jax.experimental.pallas import tpu as pltpu

D_MODEL = 1024
A_HEADS = 8
A_HEAD_DIM = 64
A_WIDTH = A_HEADS * A_HEAD_DIM
DILATED_PATTERNS = ((128, 1), (512, 4), (2048, 16))
BAND_BLOCK = 128
REL_BUCKETS = 32
REL_MAX_DIST = 2048
M_HEADS = 8
M_NOPE = 64
M_ROPE = 32
M_V = 64
M_Q_LORA = 768
M_KV_LORA = 256
M_WIDTH = M_HEADS * M_V
ROPE_THETA = 10000.0
D_FF = -(-8 * D_MODEL // (3 * 256)) * 256
N_MOD = 6
EPS = 1e-6
NEG_INF = -1e30

LANES = 128
SUBLANES = 8
V7X_VMEM_BYTES = 64 * 1024 * 1024
VMEM_RESERVED_BYTES = 8 * 1024 * 1024
VMEM_LIMIT_BYTES = V7X_VMEM_BYTES - VMEM_RESERVED_BYTES

M_HEAD_PAD = LANES
M_PAIRS = M_HEADS // 2
ROPE_HALF = M_ROPE // 2
ROPE_LO = M_NOPE
ROPE_MID = M_NOPE + ROPE_HALF
ROPE_HI = M_NOPE + M_ROPE

A_PAIRS = A_HEADS // 2
A_TILES = 3 * A_PAIRS
SUPER_BLOCK = BAND_BLOCK * max(d for _, d in DILATED_PATTERNS)
DILATED_REGROUP = 4
LOG2_E = math.log2(math.e)

ROW_TILE = 512
MOD_COLUMN_BLOCK = D_MODEL
MLA_KEY_TILE = 512
MLA_QUERY_TILE = 2 * MLA_KEY_TILE
MLA_QUERY_CHUNK = 256

F32 = jnp.float32
BF16 = jnp.bfloat16


def _params(n_axes):
    return pltpu.CompilerParams(
        dimension_semantics=("arbitrary",) * n_axes,
        vmem_limit_bytes=VMEM_LIMIT_BYTES,
    )


def _resident(shape):
    zeros = (0,) * len(shape)
    return pl.BlockSpec(shape, lambda *_: zeros, pipeline_mode=pl.Buffered(1))


def _bdot(a, b):
    return jnp.dot(a, b, preferred_element_type=F32)


def _bdot_t(a, b_t):
    return lax.dot_general(a, b_t, (((1,), (1,)), ((), ())), preferred_element_type=F32)


def _rms(x):
    return x * lax.rsqrt(jnp.mean(x * x, axis=-1, keepdims=True) + EPS)


def _sigmoid(x):
    return 1.0 / (1.0 + jnp.exp(-x))


def _mod_kernel(ct_ref, w_ref, b_ref, o_ref):
    c = ct_ref[...]
    cond = c * _sigmoid(c)
    w = w_ref[...]
    o_ref[...] = jnp.zeros(o_ref.shape, F32)
    for b in range(c.shape[1]):
        o_ref[b:b + 1, :] = jnp.sum(cond[:, b:b + 1] * w, axis=0, keepdims=True) + b_ref[...]


def _modulation(c, w_ada, b_ada):
    batch = c.shape[0]
    rows = -(-batch // SUBLANES) * SUBLANES
    cols = MOD_COLUMN_BLOCK
    out = pl.pallas_call(
        _mod_kernel,
        grid=(N_MOD * D_MODEL // cols,),
        in_specs=[
            pl.BlockSpec((D_MODEL, batch), lambda j: (0, 0)),
            pl.BlockSpec((D_MODEL, cols), lambda j: (0, j)),
            pl.BlockSpec((1, cols), lambda j: (0, j)),
        ],
        out_specs=pl.BlockSpec((rows, cols), lambda j: (0, j)),
        out_shape=jax.ShapeDtypeStruct((rows, N_MOD * D_MODEL), F32),
        compiler_params=_params(1),
        name="adaln_mod",
    )(c.T, w_ada, b_ada.reshape(1, N_MOD * D_MODEL))
    return out[:batch].reshape(batch, N_MOD, D_MODEL)


def _rope_lanes(x, cos, signed_sin):
    lane = lax.broadcasted_iota(jnp.int32, x.shape, 1)
    partner = jnp.where(lane < ROPE_MID, pltpu.roll(x, LANES - ROPE_HALF, 1), pltpu.roll(x, ROPE_HALF, 1))
    return x * cos + partner * signed_sin


def _input_kernel(x_ref, mod_ref, g_ref, pos_ref, freq_ref, wa_ref, wg_ref, wcq_ref, wckv_ref, wkr_ref,
                  gq_ref, wuq_ref, gkv_ref, wuk_ref, wuv_ref,
                  a_ref, gate_ref, q_ref, k_ref, v_ref):
    shift = mod_ref[0, 0:1, :]
    scale = mod_ref[0, 1:2, :]
    half = x_ref.shape[1] // 2
    hb_halves = [((_rms(x_ref[0, r, :]) * g_ref[...]) * (1.0 + scale) + shift).astype(BF16)
                 for r in (slice(0, half), slice(half, 2 * half))]

    c_q = jnp.concatenate([_bdot_t(hb_half, wcq_ref[...]) for hb_half in hb_halves], axis=0)
    hb = jnp.concatenate(hb_halves, axis=0)
    c_kv = _bdot_t(hb, wckv_ref[...])
    k_r = _bdot_t(hb, wkr_ref[...])

    ang = freq_ref[...] * pos_ref[0, 0].astype(F32)
    cos_r, sin_r = jnp.cos(ang), jnp.sin(ang)
    rows = ang.shape[1]
    cos = jnp.concatenate(
        [jnp.ones((ROPE_LO, rows), F32), cos_r, jnp.ones((LANES - ROPE_HI, rows), F32)], axis=0).T
    signed_sin = jnp.concatenate(
        [jnp.zeros((ROPE_LO, rows), F32), -sin_r[:ROPE_HALF], sin_r[ROPE_HALF:],
         jnp.zeros((LANES - ROPE_HI, rows), F32)], axis=0).T

    q_all = _bdot((_rms(c_q) * gq_ref[...]).astype(BF16), wuq_ref[...])
    q_scale = (M_NOPE + M_ROPE) ** -0.5 * LOG2_E
    for hd in range(M_HEADS):
        q_h = q_all[:, hd * M_HEAD_PAD:(hd + 1) * M_HEAD_PAD]
        q_ref[0, hd] = (_rope_lanes(q_h, cos, signed_sin) * q_scale).astype(BF16)

    c_kv = (_rms(c_kv) * gkv_ref[...]).astype(BF16)
    k_rope = _rope_lanes(k_r, cos, signed_sin)
    k_all = _bdot(c_kv, wuk_ref[...])
    for hd in range(M_HEADS):
        k_ref[0, hd] = (k_all[:, hd * M_HEAD_PAD:(hd + 1) * M_HEAD_PAD] + k_rope).astype(BF16)
    v_all = _bdot(c_kv, wuv_ref[...])
    for pr in range(M_PAIRS):
        v_ref[0, pr, 0] = v_all[:, pr * LANES:(pr + 1) * LANES].T.astype(BF16)

    a_all = _bdot_t(hb, wa_ref[...])
    for j in range(A_TILES):
        tile = a_all[:, j * LANES:(j + 1) * LANES]
        a_ref[0, j] = tile * (A_HEAD_DIM ** -0.5 * LOG2_E) if j < A_PAIRS else tile
    gate_ref[0] = _bdot_t(hb, wg_ref[...]).astype(BF16)


def _input_stage(x, mod, g_mix, positions, w_in, g_q_lora, w_uq, g_kv_lora, w_ukv):
    batch, seq, _ = x.shape
    tm = MLA_KEY_TILE
    s0 = 3 * A_WIDTH
    s1 = s0 + M_Q_LORA
    s2 = s1 + M_KV_LORA
    s3 = s2 + M_ROPE
    assert s0 % M_Q_LORA == 0 and s1 % M_KV_LORA == 0
    w_all = w_in.T.astype(BF16)
    w_kr = jnp.pad(w_all[s2:s3], ((ROPE_LO, LANES - ROPE_HI), (0, 0)))
    w_g = w_all[s3:]

    def columns(width, start):
        return pl.BlockSpec((width, D_MODEL), lambda *_: (start // width, 0), pipeline_mode=pl.Buffered(1))
    w_uq_p = jnp.pad(w_uq, ((0, 0), (0, 0), (0, M_HEAD_PAD - M_NOPE - M_ROPE)))
    w_uq_p = w_uq_p.reshape(M_Q_LORA, M_HEADS * M_HEAD_PAD).astype(BF16)
    w_uk_p = jnp.pad(w_ukv[:, :, :M_NOPE], ((0, 0), (0, 0), (0, M_HEAD_PAD - M_NOPE)))
    w_uk_p = w_uk_p.reshape(M_KV_LORA, M_HEADS * M_HEAD_PAD).astype(BF16)
    w_uv = w_ukv[:, :, M_NOPE:].reshape(M_KV_LORA, M_WIDTH).astype(BF16)

    freqs = ROPE_THETA ** (-jnp.arange(ROPE_HALF, dtype=F32) / ROPE_HALF)
    freq_col = jnp.concatenate([freqs, freqs]).reshape(M_ROPE, 1)

    row3 = lambda b, i: (b, i, 0)
    head4 = lambda b, i: (b, 0, i, 0)
    return pl.pallas_call(
        _input_kernel,
        grid=(batch, seq // tm),
        in_specs=[
            pl.BlockSpec((1, tm, D_MODEL), row3),
            pl.BlockSpec((1, N_MOD, D_MODEL), lambda b, i: (b, 0, 0)),
            _resident((1, D_MODEL)),
            pl.BlockSpec((1, 1, 1, tm), lambda b, i: (b, i, 0, 0)),
            _resident((M_ROPE, 1)),
            columns(s0, 0), _resident(w_g.shape), columns(M_Q_LORA, s0), columns(M_KV_LORA, s1),
            _resident(w_kr.shape),
            _resident((1, M_Q_LORA)), _resident(w_uq_p.shape),
            _resident((1, M_KV_LORA)), _resident(w_uk_p.shape), _resident(w_uv.shape),
        ],
        out_specs=[
            pl.BlockSpec((1, A_TILES, tm, LANES), head4),
            pl.BlockSpec((1, tm, 2 * D_MODEL), row3),
            pl.BlockSpec((1, M_HEADS, tm, M_HEAD_PAD), head4),
            pl.BlockSpec((1, M_HEADS, tm, M_HEAD_PAD), head4),
            pl.BlockSpec((1, M_PAIRS, 1, LANES, tm), lambda b, i: (b, 0, i, 0, 0)),
        ],
        out_shape=[
            jax.ShapeDtypeStruct((batch, A_TILES, seq, LANES), F32),
            jax.ShapeDtypeStruct((batch, seq, 2 * D_MODEL), BF16),
            jax.ShapeDtypeStruct((batch, M_HEADS, seq, M_HEAD_PAD), BF16),
            jax.ShapeDtypeStruct((batch, M_HEADS, seq, M_HEAD_PAD), BF16),
            jax.ShapeDtypeStruct((batch, M_PAIRS, seq // tm, LANES, tm), BF16),
        ],
        compiler_params=_params(2),
        name="input_stage",
    )(x, mod, g_mix.reshape(1, D_MODEL), positions.reshape(batch, seq // tm, 1, tm), freq_col,
      w_all, w_g, w_all, w_all, w_kr, g_q_lora.reshape(1, M_Q_LORA), w_uq_p,
      g_kv_lora.reshape(1, M_KV_LORA), w_uk_p, w_uv)


def _mla_kernel(q_ref, k_ref, vt_ref, o_ref, m_scr, acc_scr, sa_scr, sb_scr, max_a_scr, max_b_scr):
    tq, tk = MLA_QUERY_TILE, MLA_KEY_TILE
    qc = MLA_QUERY_CHUNK
    n_query_tiles = q_ref.shape[2] // tq
    contract_last = (((1,), (1,)), ((), ()))
    v_row = lax.broadcasted_iota(jnp.int32, (LANES, tk), 0)
    own_rows = [v_row < M_V, v_row >= M_V]
    out_row = lax.broadcasted_iota(jnp.int32, (LANES, qc), 0)


    def reset():
        m_scr[...] = jnp.full(m_scr.shape, NEG_INF, F32)
        acc_scr[...] = jnp.zeros(acc_scr.shape, F32)

    def finish(q_tile):
        for c in range(tq // qc):
            o0 = acc_scr[0, c] * (1.0 / acc_scr[0, c, M_V:M_V + 1, :])
            o1 = acc_scr[1, c] * (1.0 / acc_scr[1, c, 0:1, :])
            rows = pl.ds(pl.multiple_of(q_tile * tq + c * qc, qc), qc)
            o_ref[0, rows, :] = jnp.where(out_row < M_V, o0, o1).T.astype(BF16)

    def block(scored=None, absorbed=None):
        def visible_keys(q0, diagonal):
            return qc if diagonal is not None and q0 == diagonal else tk

        pieces, chains = [], []
        if scored is not None:
            next_tile, next_s, next_max, diagonal, q_tile = scored
            start = pl.multiple_of(next_tile * tk, tk)
            keys = [k_ref[0, hh, pl.ds(start, tk), :] for hh in range(2)]
            pieces = [(hh, q0) for hh in range(2) for q0 in range(diagonal or 0, tq, qc)]
        if absorbed is not None:
            tile, s_scr, max_scr, absorbed_diagonal = absorbed
            vt = vt_ref[0, 0, tile]
            vt_aug = [jnp.where(own, vt, jnp.ones_like(vt)) for own in own_rows]
            chains = [(hh, q0) for hh in range(2) for q0 in range(absorbed_diagonal or 0, tq, qc)]
            state = [(m_scr[hh, q0 // qc], acc_scr[hh, q0 // qc]) for hh, q0 in chains]

        def score_piece(hh, q0):
            n_keys = visible_keys(q0, diagonal)
            q_rows = pl.ds(pl.multiple_of(q_tile * tq + q0, qc), qc)
            s = lax.dot_general(keys[hh][:n_keys], q_ref[0, hh, q_rows, :], contract_last,
                                preferred_element_type=F32)
            if diagonal is not None and q0 < diagonal + n_keys - 1:
                key_pos = lax.broadcasted_iota(jnp.int32, s.shape, 0) + diagonal
                query_pos = lax.broadcasted_iota(jnp.int32, s.shape, 1) + q0
                s = jnp.where(key_pos <= query_pos, s, NEG_INF)
            next_s[hh, q0 // qc, :n_keys, :] = s
            next_max[hh, q0 // qc] = jnp.max(s, axis=0, keepdims=True)

        def absorb_chain(i):
            hh, q0 = chains[i]
            c = q0 // qc
            n_keys = visible_keys(q0, absorbed_diagonal)
            m_prev, acc_prev = state[i]
            m_new = jnp.maximum(m_prev, max_scr[hh, c])
            p = jnp.exp2(s_scr[hh, c, :n_keys, :] - m_new).astype(BF16)
            acc_scr[hh, c] = jnp.exp2(m_prev - m_new) * acc_prev + _bdot(vt_aug[hh][:, :n_keys], p)
            m_scr[hh, c] = m_new

        if pieces:
            score_piece(*pieces[0])
        for i in range(max(len(pieces) - 1, len(chains))):
            if i + 1 < len(pieces):
                score_piece(*pieces[i + 1])
            if i < len(chains):
                absorb_chain(i)

    buf_a, buf_b = (sa_scr, max_a_scr), (sb_scr, max_b_scr)

    reset()
    block(scored=(0, *buf_a, 0, 0))
    block(scored=(1, *buf_b, tk, 0), absorbed=(0, *buf_a, 0))
    block(scored=(2, *buf_a, 0, 1), absorbed=(1, *buf_b, tk))
    finish(0)

    def query_tile(i, carry):
        reset()
        block(scored=(2 * i + 1, *buf_b, tk, i), absorbed=(2 * i, *buf_a, 0))
        block(scored=(0, *buf_a, None, i), absorbed=(2 * i + 1, *buf_b, tk))

        def pair(j):
            block(scored=(2 * j + 1, *buf_b, None, i), absorbed=(2 * j, *buf_a, None))
            block(scored=(2 * j + 2, *buf_a, None, i), absorbed=(2 * j + 1, *buf_b, None))

        def two_pairs(j, inner):
            pair(2 * j)
            pair(2 * j + 1)
            return inner

        lax.fori_loop(0, (i - 1) // 2, two_pairs, 0)

        @pl.when((i - 1) % 2 == 1)
        def _odd_pair():
            pair(i - 2)

        block(scored=(2 * i - 1, *buf_b, None, i), absorbed=(2 * i - 2, *buf_a, None))
        nxt = jnp.minimum(i + 1, n_query_tiles - 1)
        block(scored=(2 * nxt, *buf_a, 0, nxt), absorbed=(2 * i - 1, *buf_b, None))
        finish(i)
        return carry

    lax.fori_loop(1, n_query_tiles, query_tile, 0)


def _mla_attention(q, k, vt):
    batch, _, seq, _ = q.shape
    tq, tk, qc = MLA_QUERY_TILE, MLA_KEY_TILE, MLA_QUERY_CHUNK
    assert tq == 2 * tk and seq % tq == 0 and seq // tq >= 2
    return pl.pallas_call(
        _mla_kernel,
        grid=(batch, M_PAIRS),
        in_specs=[
            pl.BlockSpec((1, 2, seq, M_HEAD_PAD), lambda b, p: (b, p, 0, 0)),
            pl.BlockSpec((1, 2, seq, M_HEAD_PAD), lambda b, p: (b, p, 0, 0)),
            pl.BlockSpec((1, 1, seq // tk, LANES, tk), lambda b, p: (b, p, 0, 0, 0)),
        ],
        out_specs=pl.BlockSpec((1, seq, LANES), lambda b, p: (b, 0, p)),
        out_shape=jax.ShapeDtypeStruct((batch, seq, M_WIDTH), BF16),
        scratch_shapes=[
            pltpu.VMEM((2, tq // qc, 1, qc), F32),
            pltpu.VMEM((2, tq // qc, LANES, qc), F32),
            pltpu.VMEM((2, tq // qc, tk, qc), F32),
            pltpu.VMEM((2, tq // qc, tk, qc), F32),
            pltpu.VMEM((2, tq // qc, 1, qc), F32),
            pltpu.VMEM((2, tq // qc, 1, qc), F32),
        ],
        compiler_params=_params(2),
        name="mla_attention",
    )(q, k, vt)


def _t5_bucket_table(dilation, n_back):
    blk = BAND_BLOCK
    sub_dist = (np.arange(blk)[:, None] + blk) - np.arange(2 * blk)[None, :]
    dist = np.clip(sub_dist, 0, n_back) * dilation
    max_exact = REL_BUCKETS // 2
    d = np.maximum(dist, 1).astype(np.float32)
    ratio = np.log(d / np.float32(max_exact)) / np.float32(math.log(REL_MAX_DIST / max_exact))
    log_b = max_exact + (ratio * np.float32(REL_BUCKETS - max_exact)).astype(np.int32)
    log_b = np.minimum(log_b, REL_BUCKETS - 1)
    return np.where(dist < max_exact, dist, log_b).astype(np.int32)


def _rows(start, count, stride):
    return pl.ds(start, count) if stride == 1 else pl.ds(start, count, stride=stride)


def _dilated_kernel(rb_ref, bucket_ref, q_ref, kc_ref, kp_ref, vc_ref, vp_ref, o_ref,
                    bias_scr, acc_scr, m_scr, q_scr, kv_scr, out_scr):
    blk = BAND_BLOCK
    sup = SUPER_BLOCK
    grp = DILATED_REGROUP
    sub = sup // grp
    pair = pl.program_id(1)
    super_block = pl.program_id(2)
    first_step = (pl.program_id(0) == 0) & (pair == 0) & (super_block == 0)

    @pl.when(first_step)
    def _build_bias():
        row = lax.broadcasted_iota(jnp.int32, (blk, 2 * blk), 0)
        col = lax.broadcasted_iota(jnp.int32, (blk, 2 * blk), 1)
        sub_dist = row + blk - col
        in_band = (sub_dist >= 0) & (sub_dist <= blk)
        for g in range(len(DILATED_PATTERNS)):
            bucket = bucket_ref[g]
            for hd in range(A_HEADS):
                bias = jnp.zeros((blk, 2 * blk), F32)
                for bk in range(REL_BUCKETS):
                    bias = jnp.where(bucket == bk, rb_ref[hd, bk] * LOG2_E, bias)
                bias_scr[g, hd] = jnp.where(in_band, bias, NEG_INF).astype(BF16)

    cur = super_block % 2
    prev = 1 - cur

    @pl.when(super_block == 0)
    def _no_previous():
        kv_scr[1] = jnp.zeros((2, sup, LANES), F32)

    for r in range(grp):
        q_scr[r * sub:(r + 1) * sub, :] = q_ref[0, 0, _rows(r, sub, grp), :]
        kv_scr[cur, 0, r * sub:(r + 1) * sub, :] = kc_ref[0, 0, _rows(r, sub, grp), :]
        kv_scr[cur, 1, r * sub:(r + 1) * sub, :] = vc_ref[0, 0, _rows(r, sub, grp), :]

    def load_regrouped(idx, r):
        if idx == 0:
            return q_scr[r, :].astype(BF16)
        return kv_scr[prev if idx in (2, 4) else cur, 0 if idx < 3 else 1, r, :].astype(BF16)

    def load_token_order(idx, r):
        if idx in (2, 4):
            return (kp_ref if idx == 2 else vp_ref)[0, 0].astype(BF16)
        return (q_ref, kc_ref, None, vc_ref)[idx][0, 0, r, :].astype(BF16)

    first_valid_col = jnp.where(super_block > 0, 0, blk)
    col = lax.broadcasted_iota(jnp.int32, (blk, 2 * blk), 1)
    lane = lax.broadcasted_iota(jnp.int32, (blk, LANES), 1)
    lane2 = lax.broadcasted_iota(jnp.int32, (2 * blk, LANES), 1)
    contract_last = (((1,), (1,)), ((), ()))
    n_pat = len(DILATED_PATTERNS)

    for g, (_, dil) in enumerate(DILATED_PATTERNS):
        regrouped = dil % grp == 0
        assert regrouped or dil == 1
        step = dil // grp if regrouped else dil
        for res in range(dil):
            base = (res % grp) * sub + res // grp if regrouped else res
            prev_base = base + (sub if regrouped else sup) - blk * step
            load = load_regrouped if regrouped else load_token_order
            prev_rows = _rows(prev_base, blk, step)
            k_prev, v_prev = load(2, prev_rows), load(4, prev_rows)
            for n in range(sup // (blk * dil)):
                rows = _rows(base + blk * step * n, blk, step)
                q, k_cur, v_cur = load(0, rows), load(1, rows), load(3, rows)
                k2 = jnp.concatenate([k_prev, k_cur], axis=0)
                v2 = jnp.concatenate([v_prev, v_cur], axis=0)
                k_prev, v_prev = k_cur, v_cur
                for hh in range(2):
                    in_head = (lane < A_HEAD_DIM) if hh == 0 else (lane >= A_HEAD_DIM)
                    in_head2 = (lane2 < A_HEAD_DIM) if hh == 0 else (lane2 >= A_HEAD_DIM)
                    s = lax.dot_general(jnp.where(in_head, q, jnp.zeros_like(q)), k2, contract_last,
                                        preferred_element_type=F32)
                    bias = bias_scr[g, 2 * pair + hh]
                    if n == 0:
                        bias = jnp.where(col >= first_valid_col, bias, jnp.full_like(bias, NEG_INF))
                    s = s.astype(BF16) + bias
                    m_blk = jnp.max(s, axis=-1, keepdims=True)
                    p = jnp.exp2(s - m_blk)
                    acc_scr[g, hh, rows, :] = _bdot(p, jnp.where(in_head2, v2, jnp.ones_like(v2)))
                    m_scr[g, hh, rows, :] = jnp.broadcast_to(m_blk.astype(F32), (blk, LANES))

    lane_sub = lax.broadcasted_iota(jnp.int32, (sub, LANES), 1)
    for r in range(grp):
        chunk = [pl.ds(r * sub, sub) if dil % grp == 0 else _rows(r, sub, grp) for _, dil in DILATED_PATTERNS]
        halves = []
        for hh in range(2):
            maxes = [m_scr[g, hh, chunk[g], :] for g in range(n_pat)]
            top = maxes[0]
            for g in range(1, n_pat):
                top = jnp.maximum(top, maxes[g])
            total = jnp.exp2(maxes[0] - top) * acc_scr[0, hh, chunk[0], :]
            for g in range(1, n_pat):
                total = total + jnp.exp2(maxes[g] - top) * acc_scr[g, hh, chunk[g], :]
            halves.append(total * (1.0 / pltpu.roll(total, A_HEAD_DIM, 1)))
        out_scr[_rows(r, sub, grp), :] = jnp.where(lane_sub < A_HEAD_DIM, halves[0], halves[1])
    o_ref[0] = out_scr[...].astype(BF16)


def _dilated_attention(a_qkv, rel_bias):
    batch, _, seq, _ = a_qkv.shape
    blk = BAND_BLOCK
    sup = SUPER_BLOCK
    n_pat = len(DILATED_PATTERNS)
    assert all(w // d == blk for w, d in DILATED_PATTERNS), "band of exactly one block behind the query"
    assert seq % sup == 0
    bucket = jnp.asarray(np.stack([_t5_bucket_table(d, w // d) for w, d in DILATED_PATTERNS]))

    def part(which, prev):
        if prev:
            return pl.BlockSpec(
                (1, 1, blk, LANES),
                lambda b, p, s: (b, which * A_PAIRS + p, jnp.maximum(s * (sup // blk) - 1, 0), 0))
        return pl.BlockSpec((1, 1, sup, LANES), lambda b, p, s: (b, which * A_PAIRS + p, s, 0))

    return pl.pallas_call(
        _dilated_kernel,
        grid=(batch, A_PAIRS, seq // sup),
        in_specs=[
            pl.BlockSpec(memory_space=pltpu.SMEM),
            pl.BlockSpec((n_pat, blk, 2 * blk), lambda b, p, s: (0, 0, 0)),
            part(0, False), part(1, False), part(1, True), part(2, False), part(2, True),
        ],
        out_specs=pl.BlockSpec((1, sup, LANES), lambda b, p, s: (b, s, p)),
        out_shape=jax.ShapeDtypeStruct((batch, seq, A_WIDTH), BF16),
        scratch_shapes=[
            pltpu.VMEM((n_pat, A_HEADS, blk, 2 * blk), BF16),
            pltpu.VMEM((n_pat, 2, sup, LANES), F32),
            pltpu.VMEM((n_pat, 2, sup, LANES), F32),
            pltpu.VMEM((sup, LANES), F32),
            pltpu.VMEM((2, 2, sup, LANES), F32),
            pltpu.VMEM((sup, LANES), F32),
        ],
        compiler_params=_params(3),
        name="dilated_attention",
    )(rel_bias, bucket, a_qkv, a_qkv, a_qkv, a_qkv, a_qkv)


def _output_kernel(x_ref, mod_ref, oa_ref, ob_ref, gate_ref, wa_ref, wb_ref, wo_ref,
                   g_ref, gf_ref, wg_ref, wu_ref, wd_ref, out_ref):
    rows = x_ref.shape[1] // 2
    halves = [slice(0, rows), slice(rows, 2 * rows)]

    mixed = []
    for r in halves:
        y_a = _bdot(oa_ref[0, r, :], wa_ref[...])
        y_b = _bdot(ob_ref[0, r, :], wb_ref[...])
        gates = gate_ref[0, r, :].astype(F32)
        merged = _sigmoid(gates[:, :D_MODEL]) * y_a + _sigmoid(gates[:, D_MODEL:]) * y_b
        mixed.append(_bdot(merged.astype(BF16), wo_ref[...]))

    xs, acts = [], []
    for r, mix in zip(halves, mixed):
        x = x_ref[0, r, :] + mod_ref[0, 2:3, :] * mix
        hb = ((_rms(x) * g_ref[...]) * (1.0 + mod_ref[0, 4:5, :]) + mod_ref[0, 3:4, :]).astype(BF16)
        gate = _bdot(hb, wg_ref[...])
        up = _bdot(hb, wu_ref[...])
        xs.append(x)
        acts.append((gate * _sigmoid(gate) * up).astype(BF16))

    for r, x, act in zip(halves, xs, acts):
        y = x + mod_ref[0, 5:6, :] * _bdot(act, wd_ref[...])
        out_ref[0, r, :] = _rms(y) * gf_ref[...]


def _output_stage(x, mod, o_a, o_b, gates, w_up_a, w_up_b, w_o, g_ffn, g_final, w_gate, w_up, w_down):
    batch, seq, _ = x.shape
    tm = ROW_TILE
    row3 = lambda b, i: (b, i, 0)
    half = pl.BlockSpec((1, tm, A_WIDTH), row3)
    return pl.pallas_call(
        _output_kernel,
        grid=(batch, seq // tm),
        in_specs=[
            pl.BlockSpec((1, tm, D_MODEL), row3),
            pl.BlockSpec((1, N_MOD, D_MODEL), lambda b, i: (b, 0, 0)),
            half, half,
            pl.BlockSpec((1, tm, 2 * D_MODEL), row3),
            _resident((A_WIDTH, D_MODEL)), _resident((M_WIDTH, D_MODEL)), _resident((D_MODEL, D_MODEL)),
            _resident((1, D_MODEL)), _resident((1, D_MODEL)),
            _resident((D_MODEL, D_FF)), _resident((D_MODEL, D_FF)), _resident((D_FF, D_MODEL)),
        ],
        out_specs=pl.BlockSpec((1, tm, D_MODEL), row3),
        out_shape=jax.ShapeDtypeStruct((batch, seq, D_MODEL), F32),
        compiler_params=_params(2),
        name="output_stage",
    )(x, mod, o_a, o_b, gates,
      w_up_a.astype(BF16), w_up_b.astype(BF16), w_o.astype(BF16),
      g_ffn.reshape(1, D_MODEL), g_final.reshape(1, D_MODEL),
      w_gate.astype(BF16), w_up.astype(BF16), w_down.astype(BF16))


def kernel(x, c, positions, rel_bias, w_ada, b_ada, g_mix, w_in, g_q_lora, w_uq, g_kv_lora, w_ukv,
           w_up_a, w_up_b, w_o, g_ffn, w_gate, w_up, w_down, g_final):
    assert w_ada.shape[0] == 1, "single-layer trunk"
    mod = _modulation(c, w_ada[0], b_ada[0])
    a_qkv, gates, q, k, vt = _input_stage(x, mod, g_mix[0], positions, w_in[0], g_q_lora[0], w_uq[0],
                                         g_kv_lora[0], w_ukv[0])
    o_b = _mla_attention(q, k, vt)
    o_a = _dilated_attention(a_qkv, rel_bias)
    return _output_stage(x, mod, o_a, o_b, gates, w_up_a[0], w_up_b[0], w_o[0],
                         g_ffn[0], g_final, w_gate[0], w_up[0], w_down[0])
```

```python
import math

import jax
import jax.numpy as jnp
import numpy as np
from jax import lax
from jax.experimental import pallas as pl
from jax.experimental.pallas import tpu as pltpu

D_MODEL = 1024
A_HEADS = 8
A_HEAD_DIM = 64
A_WIDTH = A_HEADS * A_HEAD_DIM
DILATED_PATTERNS = ((128, 1), (512, 4), (2048, 16))
BAND_BLOCK = 128
REL_BUCKETS = 32
REL_MAX_DIST = 2048
M_HEADS = 8
M_NOPE = 64
M_ROPE = 32
M_V = 64
M_Q_LORA = 768
M_KV_LORA = 256
M_WIDTH = M_HEADS * M_V
ROPE_THETA = 10000.0
D_FF = -(-8 * D_MODEL // (3 * 256)) * 256
N_MOD = 6
EPS = 1e-6
NEG_INF = -1e30

LANES = 128
SUBLANES = 8
V7X_VMEM_BYTES = 64 * 1024 * 1024
VMEM_RESERVED_BYTES = 8 * 1024 * 1024
VMEM_LIMIT_BYTES = V7X_VMEM_BYTES - VMEM_RESERVED_BYTES

M_HEAD_PAD = LANES
M_PAIRS = M_HEADS // 2
ROPE_HALF = M_ROPE // 2
ROPE_LO = M_NOPE
ROPE_MID = M_NOPE + ROPE_HALF
ROPE_HI = M_NOPE + M_ROPE

A_PAIRS = A_HEADS // 2
A_TILES = 3 * A_PAIRS
SUPER_BLOCK = BAND_BLOCK * max(d for _, d in DILATED_PATTERNS)
DILATED_REGROUP = 4
LOG2_E = math.log2(math.e)

ROW_TILE = 512
MOD_COLUMN_BLOCK = D_MODEL
MLA_KEY_TILE = 512
MLA_QUERY_TILE = 2 * MLA_KEY_TILE
MLA_QUERY_CHUNK = 256

F32 = jnp.float32
BF16 = jnp.bfloat16


def _params(n_axes):
    return pltpu.CompilerParams(
        dimension_semantics=("arbitrary",) * n_axes,
        vmem_limit_bytes=VMEM_LIMIT_BYTES,
    )


def _resident(shape):
    zeros = (0,) * len(shape)
    return pl.BlockSpec(shape, lambda *_: zeros, pipeline_mode=pl.Buffered(1))


def _bdot(a, b):
    return jnp.dot(a, b, preferred_element_type=F32)


def _bdot_t(a, b_t):
    return lax.dot_general(a, b_t, (((1,), (1,)), ((), ())), preferred_element_type=F32)


def _rms(x):
    return x * lax.rsqrt(jnp.mean(x * x, axis=-1, keepdims=True) + EPS)


def _sigmoid(x):
    return 1.0 / (1.0 + jnp.exp(-x))


def _mod_kernel(ct_ref, w_ref, b_ref, o_ref):
    c = ct_ref[...]
    cond = c * _sigmoid(c)
    w = w_ref[...]
    o_ref[...] = jnp.zeros(o_ref.shape, F32)
    for b in range(c.shape[1]):
        o_ref[b:b + 1, :] = jnp.sum(cond[:, b:b + 1] * w, axis=0, keepdims=True) + b_ref[...]


def _modulation(c, w_ada, b_ada):
    batch = c.shape[0]
    rows = -(-batch // SUBLANES) * SUBLANES
    cols = MOD_COLUMN_BLOCK
    out = pl.pallas_call(
        _mod_kernel,
        grid=(N_MOD * D_MODEL // cols,),
        in_specs=[
            pl.BlockSpec((D_MODEL, batch), lambda j: (0, 0)),
            pl.BlockSpec((D_MODEL, cols), lambda j: (0, j)),
            pl.BlockSpec((1, cols), lambda j: (0, j)),
        ],
        out_specs=pl.BlockSpec((rows, cols), lambda j: (0, j)),
        out_shape=jax.ShapeDtypeStruct((rows, N_MOD * D_MODEL), F32),
        compiler_params=_params(1),
        name="adaln_mod",
    )(c.T, w_ada, b_ada.reshape(1, N_MOD * D_MODEL))
    return out[:batch].reshape(batch, N_MOD, D_MODEL)


def _rope_lanes(x, cos, signed_sin):
    lane = lax.broadcasted_iota(jnp.int32, x.shape, 1)
    partner = jnp.where(lane < ROPE_MID, pltpu.roll(x, LANES - ROPE_HALF, 1), pltpu.roll(x, ROPE_HALF, 1))
    return x * cos + partner * signed_sin


def _w_in_groups():
    s0 = 3 * A_WIDTH
    s1 = s0 + M_Q_LORA
    s2 = s1 + M_KV_LORA
    s3 = s2 + M_ROPE
    return (("a", 0, s0), ("cq", s0, M_Q_LORA), ("ckv", s1, M_KV_LORA), ("kr", s2, M_ROPE),
            ("g", s3, 2 * D_MODEL))


def _input_kernel(x_ref, mod_ref, g_ref, pos_ref, freq_ref, w_ref,
                  gq_ref, wuq_ref, gkv_ref, wuk_ref, wuv_ref,
                  a_ref, gate_ref, q_ref, k_ref, v_ref,
                  wa_ref, wcq_ref, wckv_ref, wkr_ref, wg_ref):
    @pl.when((pl.program_id(0) == 0) & (pl.program_id(1) == 0))
    def _cast_weights():
        targets = {"a": wa_ref, "cq": wcq_ref, "ckv": wckv_ref, "kr": wkr_ref, "g": wg_ref}
        wkr_ref[...] = jnp.zeros(wkr_ref.shape, BF16)
        piece = 256
        for name, start, rows in _w_in_groups():
            offset = ROPE_LO if name == "kr" else 0
            for r in range(0, rows, piece):
                n = min(piece, rows - r)
                targets[name][offset + r:offset + r + n, :] = w_ref[start + r:start + r + n, :].astype(BF16)

    shift = mod_ref[0, 0:1, :]
    scale = mod_ref[0, 1:2, :]
    half = x_ref.shape[1] // 2
    hb_halves = [((_rms(x_ref[0, r, :]) * g_ref[...]) * (1.0 + scale) + shift).astype(BF16)
                 for r in (slice(0, half), slice(half, 2 * half))]

    c_q = jnp.concatenate([_bdot_t(hb_half, wcq_ref[...]) for hb_half in hb_halves], axis=0)
    hb = jnp.concatenate(hb_halves, axis=0)
    c_kv = _bdot_t(hb, wckv_ref[...])
    k_r = _bdot_t(hb, wkr_ref[...])

    ang = freq_ref[...] * pos_ref[0, 0].astype(F32)
    cos_r, sin_r = jnp.cos(ang), jnp.sin(ang)
    rows = ang.shape[1]
    cos = jnp.concatenate(
        [jnp.ones((ROPE_LO, rows), F32), cos_r, jnp.ones((LANES - ROPE_HI, rows), F32)], axis=0).T
    signed_sin = jnp.concatenate(
        [jnp.zeros((ROPE_LO, rows), F32), -sin_r[:ROPE_HALF], sin_r[ROPE_HALF:],
         jnp.zeros((LANES - ROPE_HI, rows), F32)], axis=0).T

    q_all = _bdot((_rms(c_q) * gq_ref[...]).astype(BF16), wuq_ref[...])
    q_scale = (M_NOPE + M_ROPE) ** -0.5 * LOG2_E
    for hd in range(M_HEADS):
        q_h = q_all[:, hd * M_HEAD_PAD:(hd + 1) * M_HEAD_PAD]
        q_ref[0, hd] = (_rope_lanes(q_h, cos, signed_sin) * q_scale).astype(BF16)

    c_kv = (_rms(c_kv) * gkv_ref[...]).astype(BF16)
    k_rope = _rope_lanes(k_r, cos, signed_sin)
    k_all = _bdot(c_kv, wuk_ref[...])
    for hd in range(M_HEADS):
        k_ref[0, hd] = (k_all[:, hd * M_HEAD_PAD:(hd + 1) * M_HEAD_PAD] + k_rope).astype(BF16)
    v_all = _bdot(c_kv, wuv_ref[...])
    for pr in range(M_PAIRS):
        v_ref[0, pr, 0] = v_all[:, pr * LANES:(pr + 1) * LANES].T.astype(BF16)

    a_all = _bdot_t(hb, wa_ref[...])
    for j in range(A_TILES):
        tile = a_all[:, j * LANES:(j + 1) * LANES]
        a_ref[0, j] = tile * (A_HEAD_DIM ** -0.5 * LOG2_E) if j < A_PAIRS else tile
    gate_ref[0] = _bdot_t(hb, wg_ref[...]).astype(BF16)


def _input_stage(x, mod, g_mix, positions, w_in, g_q_lora, w_uq, g_kv_lora, w_ukv):
    batch, seq, _ = x.shape
    tm = MLA_KEY_TILE
    w_t = w_in.T
    assert all(start % (2 * SUBLANES) == 0 for _, start, _ in _w_in_groups())
    group_rows = {name: rows for name, _, rows in _w_in_groups()}
    w_uq_p = jnp.pad(w_uq, ((0, 0), (0, 0), (0, M_HEAD_PAD - M_NOPE - M_ROPE)))
    w_uq_p = w_uq_p.reshape(M_Q_LORA, M_HEADS * M_HEAD_PAD).astype(BF16)
    w_uk_p = jnp.pad(w_ukv[:, :, :M_NOPE], ((0, 0), (0, 0), (0, M_HEAD_PAD - M_NOPE)))
    w_uk_p = w_uk_p.reshape(M_KV_LORA, M_HEADS * M_HEAD_PAD).astype(BF16)
    w_uv = w_ukv[:, :, M_NOPE:].reshape(M_KV_LORA, M_WIDTH).astype(BF16)

    freqs = ROPE_THETA ** (-jnp.arange(ROPE_HALF, dtype=F32) / ROPE_HALF)
    freq_col = jnp.concatenate([freqs, freqs]).reshape(M_ROPE, 1)

    row3 = lambda b, i: (b, i, 0)
    head4 = lambda b, i: (b, 0, i, 0)
    return pl.pallas_call(
        _input_kernel,
        grid=(batch, seq // tm),
        in_specs=[
            pl.BlockSpec((1, tm, D_MODEL), row3),
            pl.BlockSpec((1, N_MOD, D_MODEL), lambda b, i: (b, 0, 0)),
            _resident((1, D_MODEL)),
            pl.BlockSpec((1, 1, 1, tm), lambda b, i: (b, i, 0, 0)),
            _resident((M_ROPE, 1)),
            _resident(w_t.shape),
            _resident((1, M_Q_LORA)), _resident(w_uq_p.shape),
            _resident((1, M_KV_LORA)), _resident(w_uk_p.shape), _resident(w_uv.shape),
        ],
        out_specs=[
            pl.BlockSpec((1, A_TILES, tm, LANES), head4),
            pl.BlockSpec((1, tm, 2 * D_MODEL), row3),
            pl.BlockSpec((1, M_HEADS, tm, M_HEAD_PAD), head4),
            pl.BlockSpec((1, M_HEADS, tm, M_HEAD_PAD), head4),
            pl.BlockSpec((1, M_PAIRS, 1, LANES, tm), lambda b, i: (b, 0, i, 0, 0)),
        ],
        out_shape=[
            jax.ShapeDtypeStruct((batch, A_TILES, seq, LANES), F32),
            jax.ShapeDtypeStruct((batch, seq, 2 * D_MODEL), BF16),
            jax.ShapeDtypeStruct((batch, M_HEADS, seq, M_HEAD_PAD), BF16),
            jax.ShapeDtypeStruct((batch, M_HEADS, seq, M_HEAD_PAD), BF16),
            jax.ShapeDtypeStruct((batch, M_PAIRS, seq // tm, LANES, tm), BF16),
        ],
        scratch_shapes=[
            pltpu.VMEM((group_rows["a"], D_MODEL), BF16),
            pltpu.VMEM((group_rows["cq"], D_MODEL), BF16),
            pltpu.VMEM((group_rows["ckv"], D_MODEL), BF16),
            pltpu.VMEM((M_HEAD_PAD, D_MODEL), BF16),
            pltpu.VMEM((group_rows["g"], D_MODEL), BF16),
        ],
        compiler_params=_params(2),
        name="input_stage",
    )(x, mod, g_mix.reshape(1, D_MODEL), positions.reshape(batch, seq // tm, 1, tm), freq_col,
      w_t, g_q_lora.reshape(1, M_Q_LORA), w_uq_p,
      g_kv_lora.reshape(1, M_KV_LORA), w_uk_p, w_uv)


def _mla_kernel(q_ref, k_ref, vt_ref, o_ref, m_scr, acc_scr, sa_scr, sb_scr, max_a_scr, max_b_scr):
    tq, tk = MLA_QUERY_TILE, MLA_KEY_TILE
    qc = MLA_QUERY_CHUNK
    n_query_tiles = q_ref.shape[2] // tq
    contract_last = (((1,), (1,)), ((), ()))
    v_row = lax.broadcasted_iota(jnp.int32, (LANES, tk), 0)
    own_rows = [v_row < M_V, v_row >= M_V]
    out_row = lax.broadcasted_iota(jnp.int32, (LANES, qc), 0)


    def reset():
        m_scr[...] = jnp.full(m_scr.shape, NEG_INF, F32)
        acc_scr[...] = jnp.zeros(acc_scr.shape, F32)

    def finish(q_tile):
        for c in range(tq // qc):
            o0 = acc_scr[0, c] * (1.0 / acc_scr[0, c, M_V:M_V + 1, :])
            o1 = acc_scr[1, c] * (1.0 / acc_scr[1, c, 0:1, :])
            rows = pl.ds(pl.multiple_of(q_tile * tq + c * qc, qc), qc)
            o_ref[0, rows, :] = jnp.where(out_row < M_V, o0, o1).T.astype(BF16)

    def block(scored=None, absorbed=None):
        def visible_keys(q0, diagonal):
            return qc if diagonal is not None and q0 == diagonal else tk

        pieces, chains = [], []
        if scored is not None:
            next_tile, next_s, next_max, diagonal, q_tile = scored
            start = pl.multiple_of(next_tile * tk, tk)
            keys = [k_ref[0, hh, pl.ds(start, tk), :] for hh in range(2)]
            pieces = [(hh, q0) for hh in range(2) for q0 in range(diagonal or 0, tq, qc)]
        if absorbed is not None:
            tile, s_scr, max_scr, absorbed_diagonal = absorbed
            vt = vt_ref[0, 0, tile]
            vt_aug = [jnp.where(own, vt, jnp.ones_like(vt)) for own in own_rows]
            chains = [(hh, q0) for hh in range(2) for q0 in range(absorbed_diagonal or 0, tq, qc)]
            state = [(m_scr[hh, q0 // qc], acc_scr[hh, q0 // qc]) for hh, q0 in chains]

        def score_piece(hh, q0):
            n_keys = visible_keys(q0, diagonal)
            q_rows = pl.ds(pl.multiple_of(q_tile * tq + q0, qc), qc)
            s = lax.dot_general(keys[hh][:n_keys], q_ref[0, hh, q_rows, :], contract_last,
                                preferred_element_type=F32)
            if diagonal is not None and q0 < diagonal + n_keys - 1:
                key_pos = lax.broadcasted_iota(jnp.int32, s.shape, 0) + diagonal
                query_pos = lax.broadcasted_iota(jnp.int32, s.shape, 1) + q0
                s = jnp.where(key_pos <= query_pos, s, NEG_INF)
            next_s[hh, q0 // qc, :n_keys, :] = s
            next_max[hh, q0 // qc] = jnp.max(s, axis=0, keepdims=True)

        def absorb_chain(i):
            hh, q0 = chains[i]
            c = q0 // qc
            n_keys = visible_keys(q0, absorbed_diagonal)
            m_prev, acc_prev = state[i]
            m_new = jnp.maximum(m_prev, max_scr[hh, c])
            p = jnp.exp2(s_scr[hh, c, :n_keys, :] - m_new).astype(BF16)
            acc_scr[hh, c] = jnp.exp2(m_prev - m_new) * acc_prev + _bdot(vt_aug[hh][:, :n_keys], p)
            m_scr[hh, c] = m_new

        if pieces:
            score_piece(*pieces[0])
        for i in range(max(len(pieces) - 1, len(chains))):
            if i + 1 < len(pieces):
                score_piece(*pieces[i + 1])
            if i < len(chains):
                absorb_chain(i)

    buf_a, buf_b = (sa_scr, max_a_scr), (sb_scr, max_b_scr)

    reset()
    block(scored=(0, *buf_a, 0, 0))
    block(scored=(1, *buf_b, tk, 0), absorbed=(0, *buf_a, 0))
    block(scored=(2, *buf_a, 0, 1), absorbed=(1, *buf_b, tk))
    finish(0)

    def query_tile(i, carry):
        reset()
        block(scored=(2 * i + 1, *buf_b, tk, i), absorbed=(2 * i, *buf_a, 0))
        block(scored=(0, *buf_a, None, i), absorbed=(2 * i + 1, *buf_b, tk))

        def pair(j):
            block(scored=(2 * j + 1, *buf_b, None, i), absorbed=(2 * j, *buf_a, None))
            block(scored=(2 * j + 2, *buf_a, None, i), absorbed=(2 * j + 1, *buf_b, None))

        def two_pairs(j, inner):
            pair(2 * j)
            pair(2 * j + 1)
            return inner

        lax.fori_loop(0, (i - 1) // 2, two_pairs, 0)

        @pl.when((i - 1) % 2 == 1)
        def _odd_pair():
            pair(i - 2)

        block(scored=(2 * i - 1, *buf_b, None, i), absorbed=(2 * i - 2, *buf_a, None))
        nxt = jnp.minimum(i + 1, n_query_tiles - 1)
        block(scored=(2 * nxt, *buf_a, 0, nxt), absorbed=(2 * i - 1, *buf_b, None))
        finish(i)
        return carry

    lax.fori_loop(1, n_query_tiles, query_tile, 0)


def _mla_attention(q, k, vt):
    batch, _, seq, _ = q.shape
    tq, tk, qc = MLA_QUERY_TILE, MLA_KEY_TILE, MLA_QUERY_CHUNK
    assert tq == 2 * tk and seq % tq == 0 and seq // tq >= 2
    return pl.pallas_call(
        _mla_kernel,
        grid=(batch, M_PAIRS),
        in_specs=[
            pl.BlockSpec((1, 2, seq, M_HEAD_PAD), lambda b, p: (b, p, 0, 0)),
            pl.BlockSpec((1, 2, seq, M_HEAD_PAD), lambda b, p: (b, p, 0, 0)),
            pl.BlockSpec((1, 1, seq // tk, LANES, tk), lambda b, p: (b, p, 0, 0, 0)),
        ],
        out_specs=pl.BlockSpec((1, seq, LANES), lambda b, p: (b, 0, p)),
        out_shape=jax.ShapeDtypeStruct((batch, seq, M_WIDTH), BF16),
        scratch_shapes=[
            pltpu.VMEM((2, tq // qc, 1, qc), F32),
            pltpu.VMEM((2, tq // qc, LANES, qc), F32),
            pltpu.VMEM((2, tq // qc, tk, qc), F32),
            pltpu.VMEM((2, tq // qc, tk, qc), F32),
            pltpu.VMEM((2, tq // qc, 1, qc), F32),
            pltpu.VMEM((2, tq // qc, 1, qc), F32),
        ],
        compiler_params=_params(2),
        name="mla_attention",
    )(q, k, vt)


def _t5_bucket_table(dilation, n_back):
    blk = BAND_BLOCK
    sub_dist = (np.arange(blk)[:, None] + blk) - np.arange(2 * blk)[None, :]
    dist = np.clip(sub_dist, 0, n_back) * dilation
    max_exact = REL_BUCKETS // 2
    d = np.maximum(dist, 1).astype(np.float32)
    ratio = np.log(d / np.float32(max_exact)) / np.float32(math.log(REL_MAX_DIST / max_exact))
    log_b = max_exact + (ratio * np.float32(REL_BUCKETS - max_exact)).astype(np.int32)
    log_b = np.minimum(log_b, REL_BUCKETS - 1)
    return np.where(dist < max_exact, dist, log_b).astype(np.int32)


def _rows(start, count, stride):
    return pl.ds(start, count) if stride == 1 else pl.ds(start, count, stride=stride)


def _dilated_kernel(rb_ref, bucket_ref, q_ref, kc_ref, kp_ref, vc_ref, vp_ref, o_ref,
                    bias_scr, acc_scr, m_scr, q_scr, kv_scr, out_scr):
    blk = BAND_BLOCK
    sup = SUPER_BLOCK
    grp = DILATED_REGROUP
    sub = sup // grp
    pair = pl.program_id(1)
    super_block = pl.program_id(2)
    first_step = (pl.program_id(0) == 0) & (pair == 0) & (super_block == 0)

    @pl.when(first_step)
    def _build_bias():
        row = lax.broadcasted_iota(jnp.int32, (blk, 2 * blk), 0)
        col = lax.broadcasted_iota(jnp.int32, (blk, 2 * blk), 1)
        sub_dist = row + blk - col
        in_band = (sub_dist >= 0) & (sub_dist <= blk)
        for g in range(len(DILATED_PATTERNS)):
            bucket = bucket_ref[g]
            for hd in range(A_HEADS):
                bias = jnp.zeros((blk, 2 * blk), F32)
                for bk in range(REL_BUCKETS):
                    bias = jnp.where(bucket == bk, rb_ref[hd, bk] * LOG2_E, bias)
                bias_scr[g, hd] = jnp.where(in_band, bias, NEG_INF).astype(BF16)

    cur = super_block % 2
    prev = 1 - cur

    @pl.when(super_block == 0)
    def _no_previous():
        kv_scr[1] = jnp.zeros((2, sup, LANES), F32)

    for r in range(grp):
        q_scr[r * sub:(r + 1) * sub, :] = q_ref[0, 0, _rows(r, sub, grp), :]
        kv_scr[cur, 0, r * sub:(r + 1) * sub, :] = kc_ref[0, 0, _rows(r, sub, grp), :]
        kv_scr[cur, 1, r * sub:(r + 1) * sub, :] = vc_ref[0, 0, _rows(r, sub, grp), :]

    def load_regrouped(idx, r):
        if idx == 0:
            return q_scr[r, :].astype(BF16)
        return kv_scr[prev if idx in (2, 4) else cur, 0 if idx < 3 else 1, r, :].astype(BF16)

    def load_token_order(idx, r):
        if idx in (2, 4):
            return (kp_ref if idx == 2 else vp_ref)[0, 0].astype(BF16)
        return (q_ref, kc_ref, None, vc_ref)[idx][0, 0, r, :].astype(BF16)

    first_valid_col = jnp.where(super_block > 0, 0, blk)
    col = lax.broadcasted_iota(jnp.int32, (blk, 2 * blk), 1)
    lane = lax.broadcasted_iota(jnp.int32, (blk, LANES), 1)
    lane2 = lax.broadcasted_iota(jnp.int32, (2 * blk, LANES), 1)
    contract_last = (((1,), (1,)), ((), ()))
    n_pat = len(DILATED_PATTERNS)

    for g, (_, dil) in enumerate(DILATED_PATTERNS):
        regrouped = dil % grp == 0
        assert regrouped or dil == 1
        step = dil // grp if regrouped else dil
        for res in range(dil):
            base = (res % grp) * sub + res // grp if regrouped else res
            prev_base = base + (sub if regrouped else sup) - blk * step
            load = load_regrouped if regrouped else load_token_order
            prev_rows = _rows(prev_base, blk, step)
            k_prev, v_prev = load(2, prev_rows), load(4, prev_rows)
            for n in range(sup // (blk * dil)):
                rows = _rows(base + blk * step * n, blk, step)
                q, k_cur, v_cur = load(0, rows), load(1, rows), load(3, rows)
                k2 = jnp.concatenate([k_prev, k_cur], axis=0)
                v2 = jnp.concatenate([v_prev, v_cur], axis=0)
                k_prev, v_prev = k_cur, v_cur
                for hh in range(2):
                    in_head = (lane < A_HEAD_DIM) if hh == 0 else (lane >= A_HEAD_DIM)
                    in_head2 = (lane2 < A_HEAD_DIM) if hh == 0 else (lane2 >= A_HEAD_DIM)
                    s = lax.dot_general(jnp.where(in_head, q, jnp.zeros_like(q)), k2, contract_last,
                                        preferred_element_type=F32)
                    bias = bias_scr[g, 2 * pair + hh]
                    if n == 0:
                        bias = jnp.where(col >= first_valid_col, bias, jnp.full_like(bias, NEG_INF))
                    s = s.astype(BF16) + bias
                    m_blk = jnp.max(s, axis=-1, keepdims=True)
                    p = jnp.exp2(s - m_blk)
                    acc_scr[g, hh, rows, :] = _bdot(p, jnp.where(in_head2, v2, jnp.ones_like(v2)))
                    m_scr[g, hh, rows, :] = jnp.broadcast_to(m_blk.astype(F32), (blk, LANES))

    lane_sub = lax.broadcasted_iota(jnp.int32, (sub, LANES), 1)
    for r in range(grp):
        chunk = [pl.ds(r * sub, sub) if dil % grp == 0 else _rows(r, sub, grp) for _, dil in DILATED_PATTERNS]
        halves = []
        for hh in range(2):
            maxes = [m_scr[g, hh, chunk[g], :] for g in range(n_pat)]
            top = maxes[0]
            for g in range(1, n_pat):
                top = jnp.maximum(top, maxes[g])
            total = jnp.exp2(maxes[0] - top) * acc_scr[0, hh, chunk[0], :]
            for g in range(1, n_pat):
                total = total + jnp.exp2(maxes[g] - top) * acc_scr[g, hh, chunk[g], :]
            halves.append(total * (1.0 / pltpu.roll(total, A_HEAD_DIM, 1)))
        out_scr[_rows(r, sub, grp), :] = jnp.where(lane_sub < A_HEAD_DIM, halves[0], halves[1])
    o_ref[0] = out_scr[...].astype(BF16)


def _dilated_attention(a_qkv, rel_bias):
    batch, _, seq, _ = a_qkv.shape
    blk = BAND_BLOCK
    sup = SUPER_BLOCK
    n_pat = len(DILATED_PATTERNS)
    assert all(w // d == blk for w, d in DILATED_PATTERNS), "band of exactly one block behind the query"
    assert seq % sup == 0
    bucket = jnp.asarray(np.stack([_t5_bucket_table(d, w // d) for w, d in DILATED_PATTERNS]))

    def part(which, prev):
        if prev:
            return pl.BlockSpec(
                (1, 1, blk, LANES),
                lambda b, p, s: (b, which * A_PAIRS + p, jnp.maximum(s * (sup // blk) - 1, 0), 0))
        return pl.BlockSpec((1, 1, sup, LANES), lambda b, p, s: (b, which * A_PAIRS + p, s, 0))

    return pl.pallas_call(
        _dilated_kernel,
        grid=(batch, A_PAIRS, seq // sup),
        in_specs=[
            pl.BlockSpec(memory_space=pltpu.SMEM),
            pl.BlockSpec((n_pat, blk, 2 * blk), lambda b, p, s: (0, 0, 0)),
            part(0, False), part(1, False), part(1, True), part(2, False), part(2, True),
        ],
        out_specs=pl.BlockSpec((1, sup, LANES), lambda b, p, s: (b, s, p)),
        out_shape=jax.ShapeDtypeStruct((batch, seq, A_WIDTH), BF16),
        scratch_shapes=[
            pltpu.VMEM((n_pat, A_HEADS, blk, 2 * blk), BF16),
            pltpu.VMEM((n_pat, 2, sup, LANES), F32),
            pltpu.VMEM((n_pat, 2, sup, LANES), F32),
            pltpu.VMEM((sup, LANES), F32),
            pltpu.VMEM((2, 2, sup, LANES), F32),
            pltpu.VMEM((sup, LANES), F32),
        ],
        compiler_params=_params(3),
        name="dilated_attention",
    )(rel_bias, bucket, a_qkv, a_qkv, a_qkv, a_qkv, a_qkv)


def _output_kernel(x_ref, mod_ref, oa_ref, ob_ref, gate_ref, wa_ref, wb_ref, wo_ref,
                   g_ref, gf_ref, wg_ref, wu_ref, wd_ref, out_ref):
    rows = x_ref.shape[1] // 2
    halves = [slice(0, rows), slice(rows, 2 * rows)]

    mixed = []
    for r in halves:
        y_a = _bdot(oa_ref[0, r, :], wa_ref[...])
        y_b = _bdot(ob_ref[0, r, :], wb_ref[...])
        gates = gate_ref[0, r, :].astype(F32)
        merged = _sigmoid(gates[:, :D_MODEL]) * y_a + _sigmoid(gates[:, D_MODEL:]) * y_b
        mixed.append(_bdot(merged.astype(BF16), wo_ref[...]))

    xs, acts = [], []
    for r, mix in zip(halves, mixed):
        x = x_ref[0, r, :] + mod_ref[0, 2:3, :] * mix
        hb = ((_rms(x) * g_ref[...]) * (1.0 + mod_ref[0, 4:5, :]) + mod_ref[0, 3:4, :]).astype(BF16)
        gate = _bdot(hb, wg_ref[...])
        up = _bdot(hb, wu_ref[...])
        xs.append(x)
        acts.append((gate * _sigmoid(gate) * up).astype(BF16))

    for r, x, act in zip(halves, xs, acts):
        y = x + mod_ref[0, 5:6, :] * _bdot(act, wd_ref[...])
        out_ref[0, r, :] = _rms(y) * gf_ref[...]


def _output_stage(x, mod, o_a, o_b, gates, w_up_a, w_up_b, w_o, g_ffn, g_final, w_gate, w_up, w_down):
    batch, seq, _ = x.shape
    tm = ROW_TILE
    row3 = lambda b, i: (b, i, 0)
    half = pl.BlockSpec((1, tm, A_WIDTH), row3)
    return pl.pallas_call(
        _output_kernel,
        grid=(batch, seq // tm),
        in_specs=[
            pl.BlockSpec((1, tm, D_MODEL), row3),
            pl.BlockSpec((1, N_MOD, D_MODEL), lambda b, i: (b, 0, 0)),
            half, half,
            pl.BlockSpec((1, tm, 2 * D_MODEL), row3),
            _resident((A_WIDTH, D_MODEL)), _resident((M_WIDTH, D_MODEL)), _resident((D_MODEL, D_MODEL)),
            _resident((1, D_MODEL)), _resident((1, D_MODEL)),
            _resident((D_MODEL, D_FF)), _resident((D_MODEL, D_FF)), _resident((D_FF, D_MODEL)),
        ],
        out_specs=pl.BlockSpec((1, tm, D_MODEL), row3),
        out_shape=jax.ShapeDtypeStruct((batch, seq, D_MODEL), F32),
        compiler_params=_params(2),
        name="output_stage",
    )(x, mod, o_a, o_b, gates,
      w_up_a.astype(BF16), w_up_b.astype(BF16), w_o.astype(BF16),
      g_ffn.reshape(1, D_MODEL), g_final.reshape(1, D_MODEL),
      w_gate.astype(BF16), w_up.astype(BF16), w_down.astype(BF16))


def kernel(x, c, positions, rel_bias, w_ada, b_ada, g_mix, w_in, g_q_lora, w_uq, g_kv_lora, w_ukv,
           w_up_a, w_up_b, w_o, g_ffn, w_gate, w_up, w_down, g_final):
    assert w_ada.shape[0] == 1, "single-layer trunk"
    mod = _modulation(c, w_ada[0], b_ada[0])
    a_qkv, gates, q, k, vt = _input_stage(x, mod, g_mix[0], positions, w_in[0], g_q_lora[0], w_uq[0],
                                         g_kv_lora[0], w_ukv[0])
    o_b = _mla_attention(q, k, vt)
    o_a = _dilated_attention(a_qkv, rel_bias)
    return _output_stage(x, mod, o_a, o_b, gates, w_up_a[0], w_up_b[0], w_o[0],
                         g_ffn[0], g_final, w_gate[0], w_up[0], w_down[0])
```

```python
import math

import jax
import jax.numpy as jnp
import numpy as np
from jax import lax
from jax.experimental import pallas as pl
from jax.experimental.pallas import tpu as pltpu

D_MODEL = 1024
A_HEADS = 8
A_HEAD_DIM = 64
A_WIDTH = A_HEADS * A_HEAD_DIM
DILATED_PATTERNS = ((128, 1), (512, 4), (2048, 16))
BAND_BLOCK = 128
REL_BUCKETS = 32
REL_MAX_DIST = 2048
M_HEADS = 8
M_NOPE = 64
M_ROPE = 32
M_V = 64
M_Q_LORA = 768
M_KV_LORA = 256
M_WIDTH = M_HEADS * M_V
ROPE_THETA = 10000.0
D_FF = -(-8 * D_MODEL // (3 * 256)) * 256
N_MOD = 6
EPS = 1e-6
NEG_INF = -1e30

LANES = 128
SUBLANES = 8
V7X_VMEM_BYTES = 64 * 1024 * 1024
VMEM_RESERVED_BYTES = 8 * 1024 * 1024
VMEM_LIMIT_BYTES = V7X_VMEM_BYTES - VMEM_RESERVED_BYTES

M_HEAD_PAD = LANES
M_PAIRS = M_HEADS // 2
ROPE_HALF = M_ROPE // 2
ROPE_LO = M_NOPE
ROPE_MID = M_NOPE + ROPE_HALF
ROPE_HI = M_NOPE + M_ROPE

A_PAIRS = A_HEADS // 2
A_TILES = 3 * A_PAIRS
SUPER_BLOCK = BAND_BLOCK * max(d for _, d in DILATED_PATTERNS)
DILATED_REGROUP = 4
LOG2_E = math.log2(math.e)

ROW_TILE = 512
MOD_COLUMN_BLOCK = D_MODEL
MLA_KEY_TILE = 512
MLA_QUERY_TILE = 2 * MLA_KEY_TILE
MLA_QUERY_CHUNK = 256

F32 = jnp.float32
BF16 = jnp.bfloat16


def _params(n_axes):
    return pltpu.CompilerParams(
        dimension_semantics=("arbitrary",) * n_axes,
        vmem_limit_bytes=VMEM_LIMIT_BYTES,
    )


def _resident(shape):
    zeros = (0,) * len(shape)
    return pl.BlockSpec(shape, lambda *_: zeros, pipeline_mode=pl.Buffered(1))


def _bdot(a, b):
    return jnp.dot(a, b, preferred_element_type=F32)


def _bdot_t(a, b_t):
    return lax.dot_general(a, b_t, (((1,), (1,)), ((), ())), preferred_element_type=F32)


def _rms(x):
    return x * lax.rsqrt(jnp.mean(x * x, axis=-1, keepdims=True) + EPS)


def _sigmoid(x):
    return 1.0 / (1.0 + jnp.exp(-x))


def _mod_kernel(ct_ref, w_ref, b_ref, o_ref):
    c = ct_ref[...]
    cond = c * _sigmoid(c)
    w = w_ref[...]
    o_ref[...] = jnp.zeros(o_ref.shape, F32)
    for b in range(c.shape[1]):
        o_ref[b:b + 1, :] = jnp.sum(cond[:, b:b + 1] * w, axis=0, keepdims=True) + b_ref[...]


def _modulation(c, w_ada, b_ada):
    batch = c.shape[0]
    rows = -(-batch // SUBLANES) * SUBLANES
    cols = MOD_COLUMN_BLOCK
    out = pl.pallas_call(
        _mod_kernel,
        grid=(N_MOD * D_MODEL // cols,),
        in_specs=[
            pl.BlockSpec((D_MODEL, batch), lambda j: (0, 0)),
            pl.BlockSpec((D_MODEL, cols), lambda j: (0, j)),
            pl.BlockSpec((1, cols), lambda j: (0, j)),
        ],
        out_specs=pl.BlockSpec((rows, cols), lambda j: (0, j)),
        out_shape=jax.ShapeDtypeStruct((rows, N_MOD * D_MODEL), F32),
        compiler_params=_params(1),
        name="adaln_mod",
    )(c.T, w_ada, b_ada.reshape(1, N_MOD * D_MODEL))
    return out[:batch].reshape(batch, N_MOD, D_MODEL)


def _rope_lanes(x, cos, signed_sin):
    lane = lax.broadcasted_iota(jnp.int32, x.shape, 1)
    partner = jnp.where(lane < ROPE_MID, pltpu.roll(x, LANES - ROPE_HALF, 1), pltpu.roll(x, ROPE_HALF, 1))
    return x * cos + partner * signed_sin


def _w_in_groups():
    s0 = 3 * A_WIDTH
    s1 = s0 + M_Q_LORA
    s2 = s1 + M_KV_LORA
    s3 = s2 + M_ROPE
    return (("a", 0, s0), ("cq", s0, M_Q_LORA), ("ckv", s1, M_KV_LORA), ("kr", s2, M_ROPE),
            ("g", s3, 2 * D_MODEL))


def _input_kernel(x_ref, mod_ref, g_ref, pos_ref, freq_ref, w_ref,
                  gq_ref, wuq_ref, gkv_ref, wuk_ref, wuv_ref,
                  a_ref, gate_ref, q_ref, k_ref, v_ref,
                  wa_ref, wcq_ref, wckv_ref, wkr_ref, wg_ref):
    @pl.when((pl.program_id(0) == 0) & (pl.program_id(1) == 0))
    def _cast_weights():
        targets = {"a": wa_ref, "cq": wcq_ref, "ckv": wckv_ref, "kr": wkr_ref, "g": wg_ref}
        wkr_ref[...] = jnp.zeros(wkr_ref.shape, BF16)
        piece = 256
        for name, start, rows in _w_in_groups():
            offset = ROPE_LO if name == "kr" else 0
            for r in range(0, rows, piece):
                n = min(piece, rows - r)
                targets[name][offset + r:offset + r + n, :] = w_ref[start + r:start + r + n, :].astype(BF16)

    shift = mod_ref[0, 0:1, :]
    scale = mod_ref[0, 1:2, :]
    half = x_ref.shape[1] // 2
    hb_halves = [((_rms(x_ref[0, r, :]) * g_ref[...]) * (1.0 + scale) + shift).astype(BF16)
                 for r in (slice(0, half), slice(half, 2 * half))]

    c_q = jnp.concatenate([_bdot_t(hb_half, wcq_ref[...]) for hb_half in hb_halves], axis=0)
    hb = jnp.concatenate(hb_halves, axis=0)
    c_kv = _bdot_t(hb, wckv_ref[...])
    k_r = _bdot_t(hb, wkr_ref[...])

    ang = freq_ref[...] * pos_ref[0, 0].astype(F32)
    cos_r, sin_r = jnp.cos(ang), jnp.sin(ang)
    rows = ang.shape[1]
    cos = jnp.concatenate(
        [jnp.ones((ROPE_LO, rows), F32), cos_r, jnp.ones((LANES - ROPE_HI, rows), F32)], axis=0).T
    signed_sin = jnp.concatenate(
        [jnp.zeros((ROPE_LO, rows), F32), -sin_r[:ROPE_HALF], sin_r[ROPE_HALF:],
         jnp.zeros((LANES - ROPE_HI, rows), F32)], axis=0).T

    q_all = _bdot((_rms(c_q) * gq_ref[...]).astype(BF16), wuq_ref[...])
    q_scale = (M_NOPE + M_ROPE) ** -0.5 * LOG2_E
    for hd in range(M_HEADS):
        q_h = q_all[:, hd * M_HEAD_PAD:(hd + 1) * M_HEAD_PAD]
        q_ref[0, hd] = (_rope_lanes(q_h, cos, signed_sin) * q_scale).astype(BF16)

    c_kv = (_rms(c_kv) * gkv_ref[...]).astype(BF16)
    k_rope = _rope_lanes(k_r, cos, signed_sin)
    k_all = _bdot(c_kv, wuk_ref[...])
    for hd in range(M_HEADS):
        k_ref[0, hd] = (k_all[:, hd * M_HEAD_PAD:(hd + 1) * M_HEAD_PAD] + k_rope).astype(BF16)
    v_all = _bdot(c_kv, wuv_ref[...])
    for pr in range(M_PAIRS):
        v_ref[0, pr, 0] = v_all[:, pr * LANES:(pr + 1) * LANES].T.astype(BF16)

    a_all = _bdot_t(hb, wa_ref[...])
    for j in range(A_TILES):
        tile = a_all[:, j * LANES:(j + 1) * LANES]
        a_ref[0, j] = tile * (A_HEAD_DIM ** -0.5 * LOG2_E) if j < A_PAIRS else tile
    gate_ref[0] = _bdot_t(hb, wg_ref[...]).astype(BF16)


def _input_stage(x, mod, g_mix, positions, w_in, g_q_lora, w_uq, g_kv_lora, w_ukv):
    batch, seq, _ = x.shape
    tm = MLA_KEY_TILE
    w_t = w_in.T
    assert all(start % (2 * SUBLANES) == 0 for _, start, _ in _w_in_groups())
    group_rows = {name: rows for name, _, rows in _w_in_groups()}
    w_uq_p = jnp.pad(w_uq, ((0, 0), (0, 0), (0, M_HEAD_PAD - M_NOPE - M_ROPE)))
    w_uq_p = w_uq_p.reshape(M_Q_LORA, M_HEADS * M_HEAD_PAD).astype(BF16)
    w_uk_p = jnp.pad(w_ukv[:, :, :M_NOPE], ((0, 0), (0, 0), (0, M_HEAD_PAD - M_NOPE)))
    w_uk_p = w_uk_p.reshape(M_KV_LORA, M_HEADS * M_HEAD_PAD).astype(BF16)
    w_uv = w_ukv[:, :, M_NOPE:].reshape(M_KV_LORA, M_WIDTH).astype(BF16)

    freqs = ROPE_THETA ** (-jnp.arange(ROPE_HALF, dtype=F32) / ROPE_HALF)
    freq_col = jnp.concatenate([freqs, freqs]).reshape(M_ROPE, 1)

    row3 = lambda b, i: (b, i, 0)
    head4 = lambda b, i: (b, 0, i, 0)
    return pl.pallas_call(
        _input_kernel,
        grid=(batch, seq // tm),
        in_specs=[
            pl.BlockSpec((1, tm, D_MODEL), row3),
            pl.BlockSpec((1, N_MOD, D_MODEL), lambda b, i: (b, 0, 0)),
            _resident((1, D_MODEL)),
            pl.BlockSpec((1, 1, 1, tm), lambda b, i: (b, i, 0, 0)),
            _resident((M_ROPE, 1)),
            _resident(w_t.shape),
            _resident((1, M_Q_LORA)), _resident(w_uq_p.shape),
            _resident((1, M_KV_LORA)), _resident(w_uk_p.shape), _resident(w_uv.shape),
        ],
        out_specs=[
            pl.BlockSpec((1, A_TILES, tm, LANES), head4),
            pl.BlockSpec((1, tm, 2 * D_MODEL), row3),
            pl.BlockSpec((1, M_HEADS, tm, M_HEAD_PAD), head4),
            pl.BlockSpec((1, M_HEADS, tm, M_HEAD_PAD), head4),
            pl.BlockSpec((1, M_PAIRS, 1, LANES, tm), lambda b, i: (b, 0, i, 0, 0)),
        ],
        out_shape=[
            jax.ShapeDtypeStruct((batch, A_TILES, seq, LANES), F32),
            jax.ShapeDtypeStruct((batch, seq, 2 * D_MODEL), BF16),
            jax.ShapeDtypeStruct((batch, M_HEADS, seq, M_HEAD_PAD), BF16),
            jax.ShapeDtypeStruct((batch, M_HEADS, seq, M_HEAD_PAD), BF16),
            jax.ShapeDtypeStruct((batch, M_PAIRS, seq // tm, LANES, tm), BF16),
        ],
        scratch_shapes=[
            pltpu.VMEM((group_rows["a"], D_MODEL), BF16),
            pltpu.VMEM((group_rows["cq"], D_MODEL), BF16),
            pltpu.VMEM((group_rows["ckv"], D_MODEL), BF16),
            pltpu.VMEM((M_HEAD_PAD, D_MODEL), BF16),
            pltpu.VMEM((group_rows["g"], D_MODEL), BF16),
        ],
        compiler_params=_params(2),
        name="input_stage",
    )(x, mod, g_mix.reshape(1, D_MODEL), positions.reshape(batch, seq // tm, 1, tm), freq_col,
      w_t, g_q_lora.reshape(1, M_Q_LORA), w_uq_p,
      g_kv_lora.reshape(1, M_KV_LORA), w_uk_p, w_uv)


def _mla_kernel(q_ref, k_ref, vt_ref, o_ref, m_scr, acc_scr, sa_scr, sb_scr, max_a_scr, max_b_scr):
    tq, tk = MLA_QUERY_TILE, MLA_KEY_TILE
    qc = MLA_QUERY_CHUNK
    n_query_tiles = q_ref.shape[2] // tq
    contract_last = (((1,), (1,)), ((), ()))
    v_row = lax.broadcasted_iota(jnp.int32, (LANES, tk), 0)
    own_rows = [v_row < M_V, v_row >= M_V]
    out_row = lax.broadcasted_iota(jnp.int32, (LANES, qc), 0)


    def reset():
        m_scr[...] = jnp.full(m_scr.shape, NEG_INF, F32)
        acc_scr[...] = jnp.zeros(acc_scr.shape, F32)

    def finish(q_tile):
        for c in range(tq // qc):
            o0 = acc_scr[0, c] * (1.0 / acc_scr[0, c, M_V:M_V + 1, :])
            o1 = acc_scr[1, c] * (1.0 / acc_scr[1, c, 0:1, :])
            rows = pl.ds(pl.multiple_of(q_tile * tq + c * qc, qc), qc)
            o_ref[0, rows, :] = jnp.where(out_row < M_V, o0, o1).T.astype(BF16)

    def block(scored=None, absorbed=None):
        def visible_keys(q0, diagonal):
            return qc if diagonal is not None and q0 == diagonal else tk

        pieces, chains = [], []
        if scored is not None:
            next_tile, next_s, next_max, diagonal, q_tile = scored
            start = pl.multiple_of(next_tile * tk, tk)
            keys = [k_ref[0, hh, pl.ds(start, tk), :] for hh in range(2)]
            pieces = [(hh, q0) for hh in range(2) for q0 in range(diagonal or 0, tq, qc)]
        if absorbed is not None:
            tile, s_scr, max_scr, absorbed_diagonal = absorbed
            vt = vt_ref[0, 0, tile]
            vt_aug = [jnp.where(own, vt, jnp.ones_like(vt)) for own in own_rows]
            chains = [(hh, q0) for hh in range(2) for q0 in range(absorbed_diagonal or 0, tq, qc)]
            state = [(m_scr[hh, q0 // qc], acc_scr[hh, q0 // qc]) for hh, q0 in chains]

        def score_piece(hh, q0):
            n_keys = visible_keys(q0, diagonal)
            q_rows = pl.ds(pl.multiple_of(q_tile * tq + q0, qc), qc)
            s = lax.dot_general(keys[hh][:n_keys], q_ref[0, hh, q_rows, :], contract_last,
                                preferred_element_type=F32)
            if diagonal is not None and q0 < diagonal + n_keys - 1:
                key_pos = lax.broadcasted_iota(jnp.int32, s.shape, 0) + diagonal
                query_pos = lax.broadcasted_iota(jnp.int32, s.shape, 1) + q0
                s = jnp.where(key_pos <= query_pos, s, NEG_INF)
            next_s[hh, q0 // qc, :n_keys, :] = s
            next_max[hh, q0 // qc] = jnp.max(s, axis=0, keepdims=True)

        def absorb_chain(i):
            hh, q0 = chains[i]
            c = q0 // qc
            n_keys = visible_keys(q0, absorbed_diagonal)
            m_prev, acc_prev = state[i]
            m_new = jnp.maximum(m_prev, max_scr[hh, c])
            p = jnp.exp2(s_scr[hh, c, :n_keys, :] - m_new).astype(BF16)
            acc_scr[hh, c] = jnp.exp2(m_prev - m_new) * acc_prev + _bdot(vt_aug[hh][:, :n_keys], p)
            m_scr[hh, c] = m_new

        if pieces:
            score_piece(*pieces[0])
        for i in range(max(len(pieces) - 1, len(chains))):
            if i + 1 < len(pieces):
                score_piece(*pieces[i + 1])
            if i < len(chains):
                absorb_chain(i)

    buf_a, buf_b = (sa_scr, max_a_scr), (sb_scr, max_b_scr)

    reset()
    block(scored=(0, *buf_a, 0, 0))
    block(scored=(1, *buf_b, tk, 0), absorbed=(0, *buf_a, 0))
    block(scored=(2, *buf_a, 0, 1), absorbed=(1, *buf_b, tk))
    finish(0)

    def query_tile(i, carry):
        reset()
        block(scored=(2 * i + 1, *buf_b, tk, i), absorbed=(2 * i, *buf_a, 0))
        block(scored=(0, *buf_a, None, i), absorbed=(2 * i + 1, *buf_b, tk))

        def pair(j):
            block(scored=(2 * j + 1, *buf_b, None, i), absorbed=(2 * j, *buf_a, None))
            block(scored=(2 * j + 2, *buf_a, None, i), absorbed=(2 * j + 1, *buf_b, None))

        def two_pairs(j, inner):
            pair(2 * j)
            pair(2 * j + 1)
            return inner

        lax.fori_loop(0, (i - 1) // 2, two_pairs, 0)

        @pl.when((i - 1) % 2 == 1)
        def _odd_pair():
            pair(i - 2)

        block(scored=(2 * i - 1, *buf_b, None, i), absorbed=(2 * i - 2, *buf_a, None))
        nxt = jnp.minimum(i + 1, n_query_tiles - 1)
        block(scored=(2 * nxt, *buf_a, 0, nxt), absorbed=(2 * i - 1, *buf_b, None))
        finish(i)
        return carry

    lax.fori_loop(1, n_query_tiles, query_tile, 0)


def _mla_attention(q, k, vt):
    batch, _, seq, _ = q.shape
    tq, tk, qc = MLA_QUERY_TILE, MLA_KEY_TILE, MLA_QUERY_CHUNK
    assert tq == 2 * tk and seq % tq == 0 and seq // tq >= 2
    return pl.pallas_call(
        _mla_kernel,
        grid=(batch, M_PAIRS),
        in_specs=[
            pl.BlockSpec((1, 2, seq, M_HEAD_PAD), lambda b, p: (b, p, 0, 0)),
            pl.BlockSpec((1, 2, seq, M_HEAD_PAD), lambda b, p: (b, p, 0, 0)),
            pl.BlockSpec((1, 1, seq // tk, LANES, tk), lambda b, p: (b, p, 0, 0, 0)),
        ],
        out_specs=pl.BlockSpec((1, seq, LANES), lambda b, p: (b, 0, p)),
        out_shape=jax.ShapeDtypeStruct((batch, seq, M_WIDTH), BF16),
        scratch_shapes=[
            pltpu.VMEM((2, tq // qc, 1, qc), F32),
            pltpu.VMEM((2, tq // qc, LANES, qc), F32),
            pltpu.VMEM((2, tq // qc, tk, qc), F32),
            pltpu.VMEM((2, tq // qc, tk, qc), F32),
            pltpu.VMEM((2, tq // qc, 1, qc), F32),
            pltpu.VMEM((2, tq // qc, 1, qc), F32),
        ],
        compiler_params=_params(2),
        name="mla_attention",
    )(q, k, vt)


def _t5_bucket_table(dilation, n_back):
    blk = BAND_BLOCK
    sub_dist = (np.arange(blk)[:, None] + blk) - np.arange(2 * blk)[None, :]
    dist = np.clip(sub_dist, 0, n_back) * dilation
    max_exact = REL_BUCKETS // 2
    d = np.maximum(dist, 1).astype(np.float32)
    ratio = np.log(d / np.float32(max_exact)) / np.float32(math.log(REL_MAX_DIST / max_exact))
    log_b = max_exact + (ratio * np.float32(REL_BUCKETS - max_exact)).astype(np.int32)
    log_b = np.minimum(log_b, REL_BUCKETS - 1)
    return np.where(dist < max_exact, dist, log_b).astype(np.int32)


def _rows(start, count, stride):
    return pl.ds(start, count) if stride == 1 else pl.ds(start, count, stride=stride)


def _dilated_kernel(rb_ref, bucket_ref, q_ref, kc_ref, kp_ref, vc_ref, vp_ref, o_ref,
                    bias_scr, acc_scr, m_scr, q_scr, kv_scr, out_scr):
    blk = BAND_BLOCK
    sup = SUPER_BLOCK
    grp = DILATED_REGROUP
    sub = sup // grp
    pair = pl.program_id(1)
    super_block = pl.program_id(2)
    first_step = (pl.program_id(0) == 0) & (pair == 0) & (super_block == 0)

    @pl.when(first_step)
    def _build_bias():
        row = lax.broadcasted_iota(jnp.int32, (blk, 2 * blk), 0)
        col = lax.broadcasted_iota(jnp.int32, (blk, 2 * blk), 1)
        sub_dist = row + blk - col
        in_band = (sub_dist >= 0) & (sub_dist <= blk)
        for g in range(len(DILATED_PATTERNS)):
            bucket = bucket_ref[g]
            for hd in range(A_HEADS):
                bias = jnp.zeros((blk, 2 * blk), F32)
                for bk in range(REL_BUCKETS):
                    bias = jnp.where(bucket == bk, rb_ref[hd, bk] * LOG2_E, bias)
                bias_scr[g, hd] = jnp.where(in_band, bias, NEG_INF).astype(BF16)

    cur = super_block % 2
    prev = 1 - cur

    @pl.when(super_block == 0)
    def _no_previous():
        kv_scr[1] = jnp.zeros((2, sup, LANES), F32)

    for r in range(grp):
        q_scr[r * sub:(r + 1) * sub, :] = q_ref[0, 0, _rows(r, sub, grp), :]
        kv_scr[cur, 0, r * sub:(r + 1) * sub, :] = kc_ref[0, 0, _rows(r, sub, grp), :]
        kv_scr[cur, 1, r * sub:(r + 1) * sub, :] = vc_ref[0, 0, _rows(r, sub, grp), :]

    def load_regrouped(idx, r):
        if idx == 0:
            return q_scr[r, :].astype(BF16)
        return kv_scr[prev if idx in (2, 4) else cur, 0 if idx < 3 else 1, r, :].astype(BF16)

    def load_token_order(idx, r):
        if idx in (2, 4):
            return (kp_ref if idx == 2 else vp_ref)[0, 0].astype(BF16)
        return (q_ref, kc_ref, None, vc_ref)[idx][0, 0, r, :].astype(BF16)

    first_valid_col = jnp.where(super_block > 0, 0, blk)
    col = lax.broadcasted_iota(jnp.int32, (blk, 2 * blk), 1)
    lane = lax.broadcasted_iota(jnp.int32, (blk, LANES), 1)
    lane2 = lax.broadcasted_iota(jnp.int32, (2 * blk, LANES), 1)
    contract_last = (((1,), (1,)), ((), ()))
    n_pat = len(DILATED_PATTERNS)

    for g, (_, dil) in enumerate(DILATED_PATTERNS):
        regrouped = dil % grp == 0
        assert regrouped or dil == 1
        step = dil // grp if regrouped else dil
        for res in range(dil):
            base = (res % grp) * sub + res // grp if regrouped else res
            prev_base = base + (sub if regrouped else sup) - blk * step
            load = load_regrouped if regrouped else load_token_order
            prev_rows = _rows(prev_base, blk, step)
            k_prev, v_prev = load(2, prev_rows), load(4, prev_rows)
            for n in range(sup // (blk * dil)):
                rows = _rows(base + blk * step * n, blk, step)
                q, k_cur, v_cur = load(0, rows), load(1, rows), load(3, rows)
                k2 = jnp.concatenate([k_prev, k_cur], axis=0)
                v2 = jnp.concatenate([v_prev, v_cur], axis=0)
                k_prev, v_prev = k_cur, v_cur
                for hh in range(2):
                    in_head = (lane < A_HEAD_DIM) if hh == 0 else (lane >= A_HEAD_DIM)
                    in_head2 = (lane2 < A_HEAD_DIM) if hh == 0 else (lane2 >= A_HEAD_DIM)
                    s = lax.dot_general(jnp.where(in_head, q, jnp.zeros_like(q)), k2, contract_last,
                                        preferred_element_type=F32)
                    bias = bias_scr[g, 2 * pair + hh]
                    if n == 0:
                        bias = jnp.where(col >= first_valid_col, bias, jnp.full_like(bias, NEG_INF))
                    s = s.astype(BF16) + bias
                    m_blk = jnp.max(s, axis=-1, keepdims=True)
                    p = jnp.exp2(s - m_blk)
                    acc_scr[g, hh, rows, :] = _bdot(p, jnp.where(in_head2, v2, jnp.ones_like(v2)))
                    m_scr[g, hh, rows, :] = jnp.broadcast_to(m_blk.astype(F32), (blk, LANES))

    lane_sub = lax.broadcasted_iota(jnp.int32, (sub, LANES), 1)
    for r in range(grp):
        chunk = [pl.ds(r * sub, sub) if dil % grp == 0 else _rows(r, sub, grp) for _, dil in DILATED_PATTERNS]
        halves = []
        for hh in range(2):
            maxes = [m_scr[g, hh, chunk[g], :] for g in range(n_pat)]
            top = maxes[0]
            for g in range(1, n_pat):
                top = jnp.maximum(top, maxes[g])
            total = jnp.exp2(maxes[0] - top) * acc_scr[0, hh, chunk[0], :]
            for g in range(1, n_pat):
                total = total + jnp.exp2(maxes[g] - top) * acc_scr[g, hh, chunk[g], :]
            halves.append(total * (1.0 / pltpu.roll(total, A_HEAD_DIM, 1)))
        out_scr[_rows(r, sub, grp), :] = jnp.where(lane_sub < A_HEAD_DIM, halves[0], halves[1])
    o_ref[0] = out_scr[...].astype(BF16)


def _dilated_attention(a_qkv, rel_bias):
    batch, _, seq, _ = a_qkv.shape
    blk = BAND_BLOCK
    sup = SUPER_BLOCK
    n_pat = len(DILATED_PATTERNS)
    assert all(w // d == blk for w, d in DILATED_PATTERNS), "band of exactly one block behind the query"
    assert seq % sup == 0
    bucket = jnp.asarray(np.stack([_t5_bucket_table(d, w // d) for w, d in DILATED_PATTERNS]))

    def part(which, prev):
        if prev:
            return pl.BlockSpec(
                (1, 1, blk, LANES),
                lambda b, p, s: (b, which * A_PAIRS + p, jnp.maximum(s * (sup // blk) - 1, 0), 0))
        return pl.BlockSpec((1, 1, sup, LANES), lambda b, p, s: (b, which * A_PAIRS + p, s, 0))

    return pl.pallas_call(
        _dilated_kernel,
        grid=(batch, A_PAIRS, seq // sup),
        in_specs=[
            pl.BlockSpec(memory_space=pltpu.SMEM),
            pl.BlockSpec((n_pat, blk, 2 * blk), lambda b, p, s: (0, 0, 0)),
            part(0, False), part(1, False), part(1, True), part(2, False), part(2, True),
        ],
        out_specs=pl.BlockSpec((1, sup, LANES), lambda b, p, s: (b, s, p)),
        out_shape=jax.ShapeDtypeStruct((batch, seq, A_WIDTH), BF16),
        scratch_shapes=[
            pltpu.VMEM((n_pat, A_HEADS, blk, 2 * blk), BF16),
            pltpu.VMEM((n_pat, 2, sup, LANES), F32),
            pltpu.VMEM((n_pat, 2, sup, LANES), F32),
            pltpu.VMEM((sup, LANES), F32),
            pltpu.VMEM((2, 2, sup, LANES), F32),
            pltpu.VMEM((sup, LANES), F32),
        ],
        compiler_params=_params(3),
        name="dilated_attention",
    )(rel_bias, bucket, a_qkv, a_qkv, a_qkv, a_qkv, a_qkv)


def _output_kernel(x_ref, mod_ref, oa_ref, ob_ref, gate_ref, wa_ref, wb_ref, wo_ref,
                   g_ref, gf_ref, wg_ref, wu_ref, wd_f32_ref, out_ref, wd_ref):
    @pl.when((pl.program_id(0) == 0) & (pl.program_id(1) == 0))
    def _cast_down_projection():
        piece = 256
        for r in range(0, D_FF, piece):
            wd_ref[r:r + piece, :] = wd_f32_ref[r:r + piece, :].astype(BF16)

    rows = x_ref.shape[1] // 2
    halves = [slice(0, rows), slice(rows, 2 * rows)]

    mixed = []
    for r in halves:
        y_a = _bdot(oa_ref[0, r, :], wa_ref[...])
        y_b = _bdot(ob_ref[0, r, :], wb_ref[...])
        gates = gate_ref[0, r, :].astype(F32)
        merged = _sigmoid(gates[:, :D_MODEL]) * y_a + _sigmoid(gates[:, D_MODEL:]) * y_b
        mixed.append(_bdot(merged.astype(BF16), wo_ref[...]))

    xs, acts = [], []
    for r, mix in zip(halves, mixed):
        x = x_ref[0, r, :] + mod_ref[0, 2:3, :] * mix
        hb = ((_rms(x) * g_ref[...]) * (1.0 + mod_ref[0, 4:5, :]) + mod_ref[0, 3:4, :]).astype(BF16)
        gate = _bdot(hb, wg_ref[...])
        up = _bdot(hb, wu_ref[...])
        xs.append(x)
        acts.append((gate * _sigmoid(gate) * up).astype(BF16))

    for r, x, act in zip(halves, xs, acts):
        y = x + mod_ref[0, 5:6, :] * _bdot(act, wd_ref[...])
        out_ref[0, r, :] = _rms(y) * gf_ref[...]


def _output_stage(x, mod, o_a, o_b, gates, w_up_a, w_up_b, w_o, g_ffn, g_final, w_gate, w_up, w_down):
    batch, seq, _ = x.shape
    tm = ROW_TILE
    row3 = lambda b, i: (b, i, 0)
    half = pl.BlockSpec((1, tm, A_WIDTH), row3)
    return pl.pallas_call(
        _output_kernel,
        grid=(batch, seq // tm),
        in_specs=[
            pl.BlockSpec((1, tm, D_MODEL), row3),
            pl.BlockSpec((1, N_MOD, D_MODEL), lambda b, i: (b, 0, 0)),
            half, half,
            pl.BlockSpec((1, tm, 2 * D_MODEL), row3),
            _resident((A_WIDTH, D_MODEL)), _resident((M_WIDTH, D_MODEL)), _resident((D_MODEL, D_MODEL)),
            _resident((1, D_MODEL)), _resident((1, D_MODEL)),
            _resident((D_MODEL, D_FF)), _resident((D_MODEL, D_FF)), _resident((D_FF, D_MODEL)),
        ],
        out_specs=pl.BlockSpec((1, tm, D_MODEL), row3),
        out_shape=jax.ShapeDtypeStruct((batch, seq, D_MODEL), F32),
        scratch_shapes=[pltpu.VMEM((D_FF, D_MODEL), BF16)],
        compiler_params=_params(2),
        name="output_stage",
    )(x, mod, o_a, o_b, gates,
      w_up_a.astype(BF16), w_up_b.astype(BF16), w_o.astype(BF16),
      g_ffn.reshape(1, D_MODEL), g_final.reshape(1, D_MODEL),
      w_gate.astype(BF16), w_up.astype(BF16), w_down)


def kernel(x, c, positions, rel_bias, w_ada, b_ada, g_mix, w_in, g_q_lora, w_uq, g_kv_lora, w_ukv,
           w_up_a, w_up_b, w_o, g_ffn, w_gate, w_up, w_down, g_final):
    assert w_ada.shape[0] == 1, "single-layer trunk"
    mod = _modulation(c, w_ada[0], b_ada[0])
    a_qkv, gates, q, k, vt = _input_stage(x, mod, g_mix[0], positions, w_in[0], g_q_lora[0], w_uq[0],
                                         g_kv_lora[0], w_ukv[0])
    o_b = _mla_attention(q, k, vt)
    o_a = _dilated_attention(a_qkv, rel_bias)
    return _output_stage(x, mod, o_a, o_b, gates, w_up_a[0], w_up_b[0], w_o[0],
                         g_ffn[0], g_final, w_gate[0], w_up[0], w_down[0])
```
